```python
import math
import jax, jax.numpy as jnp
from jax import lax
import numpy as np

D_MODEL = 1024
BATCH = 8
SEQ = 8192
DEPTH = 1

CONV_CH = D_MODEL // 2
CONV_WIDTH = 31
MLA_HEADS = 8
QK_NOPE = D_MODEL // 16
QK_ROPE = D_MODEL // 32
V_DIM = D_MODEL // 16
Q_LORA = 3 * D_MODEL // 8
KV_LORA = D_MODEL // 4
N_BRANCH = 2
IN_COLS = 2 * CONV_CH + Q_LORA + KV_LORA + QK_ROPE + N_BRANCH * D_MODEL
MEM_LEN = 256
X_HEADS = 4
X_HEAD_DIM = D_MODEL // 8
D_FF = 4 * D_MODEL
Q_BLOCK = 128
ROPE_THETA = 10000.0
EPS = 1e-6

kernel_name = "hybrid_conformer_mla_gated_block"


def rms_norm(x, g):
    xf = x.astype(jnp.float32)
    y = xf * lax.rsqrt(jnp.mean(xf * xf, axis=-1, keepdims=True) + EPS)
    return (y * g.astype(jnp.float32)).astype(x.dtype)


def layer_norm(x, g, b):
    xf = x.astype(jnp.float32)
    mu = jnp.mean(xf, axis=-1, keepdims=True)
    var = jnp.mean(jnp.square(xf - mu), axis=-1, keepdims=True)
    y = (xf - mu) * lax.rsqrt(var + EPS)
    return (y * g.astype(jnp.float32) + b.astype(jnp.float32)).astype(x.dtype)


def rope_tables(positions):
    half = QK_ROPE // 2
    inv_freq = ROPE_THETA ** (-jnp.arange(half, dtype=jnp.float32) / half)
    ang = positions.astype(jnp.float32)[..., None] * inv_freq
    return jnp.cos(ang), jnp.sin(ang)


def apply_rope(t, cos, sin):
    half = t.shape[-1] // 2
    t1, t2 = t[..., :half], t[..., half:]
    c, s = cos.astype(t.dtype), sin.astype(t.dtype)
    return jnp.concatenate([t1 * c - t2 * s, t2 * c + t1 * s], axis=-1)


def conformer_conv(conv_in, conv_w, conv_b, ln_g, ln_b, w_conv_out):
    a, gt = jnp.split(conv_in, 2, axis=-1)
    z = a * jax.nn.sigmoid(gt)
    rhs = conv_w.astype(z.dtype).reshape(CONV_WIDTH, 1, CONV_CH)
    z = lax.conv_general_dilated(
        z, rhs, window_strides=(1,), padding=[(CONV_WIDTH - 1, 0)],
        dimension_numbers=("NWC", "WIO", "NWC"), feature_group_count=CONV_CH)
    z = z + conv_b
    z = layer_norm(z, ln_g, ln_b)
    z = jax.nn.silu(z)
    return z @ w_conv_out


def mla_attention(c_q, c_kv, k_rope_raw, cos, sin, q_norm_g, w_uq, kv_norm_g, w_ukv, w_mla_out):
    B, S, _ = c_q.shape
    q = rms_norm(c_q, q_norm_g) @ w_uq
    q = q.reshape(B, S, MLA_HEADS, QK_NOPE + QK_ROPE)
    q_nope, q_rope = q[..., :QK_NOPE], q[..., QK_NOPE:]
    q_rope = apply_rope(q_rope, cos[:, :, None, :], sin[:, :, None, :])
    kv = rms_norm(c_kv, kv_norm_g) @ w_ukv
    kv = kv.reshape(B, S, MLA_HEADS, QK_NOPE + V_DIM)
    k_nope, v = kv[..., :QK_NOPE], kv[..., QK_NOPE:]
    k_rope = apply_rope(k_rope_raw, cos, sin)

    scale = (QK_NOPE + QK_ROPE) ** -0.5
    n_blk = S // Q_BLOCK
    qn = (q_nope * scale).reshape(B, n_blk, Q_BLOCK, MLA_HEADS, QK_NOPE).transpose(1, 0, 2, 3, 4)
    qr = (q_rope * scale).reshape(B, n_blk, Q_BLOCK, MLA_HEADS, QK_ROPE).transpose(1, 0, 2, 3, 4)
    key_idx = jnp.arange(S)
    neg = jnp.finfo(jnp.float32).min

    def attend(args):
        qn_b, qr_b, blk = args
        s = jnp.einsum("bqhd,bkhd->bhqk", qn_b, k_nope, preferred_element_type=jnp.float32)
        s = s + jnp.einsum("bqhr,bkr->bhqk", qr_b, k_rope, preferred_element_type=jnp.float32)
        q_idx = blk * Q_BLOCK + jnp.arange(Q_BLOCK)
        mask = key_idx[None, :] <= q_idx[:, None]
        s = jnp.where(mask[None, None], s, neg)
        p = jax.nn.softmax(s, axis=-1).astype(v.dtype)
        return jnp.einsum("bhqk,bkhd->bqhd", p, v)

    o = lax.map(attend, (qn, qr, jnp.arange(n_blk)))
    o = o.transpose(1, 0, 2, 3, 4).reshape(B, S, MLA_HEADS * V_DIM)
    return o @ w_mla_out


def memory_cross_attention(u, mem_n, w_xq, w_xkv, w_xo):
    B, S, _ = u.shape
    q = (u @ w_xq).reshape(B, S, X_HEADS, X_HEAD_DIM) * (X_HEAD_DIM ** -0.5)
    kv = (mem_n @ w_xkv).reshape(B, MEM_LEN, 2, X_HEADS, X_HEAD_DIM)
    k, v = kv[:, :, 0], kv[:, :, 1]
    s = jnp.einsum("bqhd,bkhd->bhqk", q, k, preferred_element_type=jnp.float32)
    p = jax.nn.softmax(s, axis=-1).astype(v.dtype)
    o = jnp.einsum("bhqk,bkhd->bqhd", p, v).reshape(B, S, X_HEADS * X_HEAD_DIM)
    return o @ w_xo


def _fwd_setup_inputs(seed: int = 0) -> dict:
    key = jax.random.key(seed)
    ks = jax.random.split(key, 32)
    f32 = jnp.float32

    def w(k, shape, fan_in):
        return jax.random.normal(k, shape, f32) * (fan_in ** -0.5)

    def gain(k, shape):
        return 1.0 + 0.02 * jax.random.normal(k, shape, f32)

    L = DEPTH
    x = jax.random.normal(ks[0], (BATCH, SEQ, D_MODEL), f32)
    mem = jax.random.normal(ks[1], (BATCH, MEM_LEN, D_MODEL), f32)
    offsets = jax.random.randint(ks[2], (BATCH, 1), 0, 4096, dtype=jnp.int32)
    positions = offsets + jnp.arange(SEQ, dtype=jnp.int32)[None, :]
    return {
        "x": x,
        "mem": mem,
        "positions": positions,
        "norm_mix_g": gain(ks[3], (L, D_MODEL)),
        "w_in": w(ks[4], (L, D_MODEL, IN_COLS), D_MODEL),
        "conv_w": w(ks[5], (L, CONV_WIDTH, CONV_CH), CONV_WIDTH),
        "conv_b": 0.02 * jax.random.normal(ks[6], (L, CONV_CH), f32),
        "conv_ln_g": gain(ks[7], (L, CONV_CH)),
        "conv_ln_b": 0.02 * jax.random.normal(ks[8], (L, CONV_CH), f32),
        "w_conv_out": w(ks[9], (L, CONV_CH, D_MODEL), CONV_CH),
        "q_norm_g": gain(ks[10], (L, Q_LORA)),
        "w_uq": w(ks[11], (L, Q_LORA, MLA_HEADS * (QK_NOPE + QK_ROPE)), Q_LORA),
        "kv_norm_g": gain(ks[12], (L, KV_LORA)),
        "w_ukv": w(ks[13], (L, KV_LORA, MLA_HEADS * (QK_NOPE + V_DIM)), KV_LORA),
        "w_mla_out": w(ks[14], (L, MLA_HEADS * V_DIM, D_MODEL), MLA_HEADS * V_DIM),
        "w_out": w(ks[15], (L, D_MODEL, D_MODEL), D_MODEL),
        "norm_xattn_g": gain(ks[16], (L, D_MODEL)),
        "norm_mem_g": gain(ks[17], (L, D_MODEL)),
        "w_xq": w(ks[18], (L, D_MODEL, X_HEADS * X_HEAD_DIM), D_MODEL),
        "w_xkv": w(ks[19], (L, D_MODEL, 2 * X_HEADS * X_HEAD_DIM), D_MODEL),
        "w_xo": w(ks[20], (L, X_HEADS * X_HEAD_DIM, D_MODEL), X_HEADS * X_HEAD_DIM),
        "norm_mlp_g": gain(ks[21], (L, D_MODEL)),
        "w_mlp1": w(ks[22], (L, D_MODEL, D_FF), D_MODEL),
        "w_mlp2": w(ks[23], (L, D_FF, D_MODEL), D_FF),
        "final_norm_g": gain(ks[24], (D_MODEL,)),
    }


def _fwd_reference(x, mem, positions, norm_mix_g, w_in, conv_w, conv_b, conv_ln_g, conv_ln_b,
              w_conv_out, q_norm_g, w_uq, kv_norm_g, w_ukv, w_mla_out, w_out,
              norm_xattn_g, norm_mem_g, w_xq, w_xkv, w_xo, norm_mlp_g, w_mlp1, w_mlp2,
              final_norm_g):
    cos, sin = rope_tables(positions)
    B, S, _ = x.shape
    cut = np.cumsum([2 * CONV_CH, Q_LORA, KV_LORA, QK_ROPE]).tolist()
    h = x
    for l in range(DEPTH):
        u = rms_norm(h, norm_mix_g[l])
        proj = u @ w_in[l]
        conv_in, c_q, c_kv, k_rope_raw, gate_logits = jnp.split(proj, cut, axis=-1)
        conv_out = conformer_conv(conv_in, conv_w[l], conv_b[l], conv_ln_g[l], conv_ln_b[l],
                                  w_conv_out[l])
        mla_out = mla_attention(c_q, c_kv, k_rope_raw, cos, sin, q_norm_g[l], w_uq[l],
                                kv_norm_g[l], w_ukv[l], w_mla_out[l])
        gates = jax.nn.sigmoid(gate_logits).reshape(B, S, N_BRANCH, D_MODEL)
        merged = gates[:, :, 0] * conv_out + gates[:, :, 1] * mla_out
        h = h + merged @ w_out[l]
        u = rms_norm(h, norm_xattn_g[l])
        mem_n = rms_norm(mem, norm_mem_g[l])
        h = h + memory_cross_attention(u, mem_n, w_xq[l], w_xkv[l], w_xo[l])
        u = rms_norm(h, norm_mlp_g[l])
        h = h + jnp.square(jax.nn.relu(u @ w_mlp1[l])) @ w_mlp2[l]
    return rms_norm(h, final_norm_g)


import jax as _jax
import jax.numpy as _jnp

TWIN_FORMAT = 'train_step'
FWD_PARAMS = ['x', 'mem', 'positions', 'norm_mix_g', 'w_in', 'conv_w', 'conv_b', 'conv_ln_g', 'conv_ln_b', 'w_conv_out', 'q_norm_g', 'w_uq', 'kv_norm_g', 'w_ukv', 'w_mla_out', 'w_out', 'norm_xattn_g', 'norm_mem_g', 'w_xq', 'w_xkv', 'w_xo', 'norm_mlp_g', 'w_mlp1', 'w_mlp2', 'final_norm_g']
TWIN_WEIGHTS = ['norm_mix_g', 'w_in', 'conv_w', 'conv_b', 'conv_ln_g', 'conv_ln_b', 'w_conv_out', 'q_norm_g', 'w_uq', 'kv_norm_g', 'w_ukv', 'w_mla_out', 'w_out', 'norm_xattn_g', 'norm_mem_g', 'w_xq', 'w_xkv', 'w_xo', 'norm_mlp_g', 'w_mlp1', 'w_mlp2', 'final_norm_g']
TWIN_DIFF_INPUT = 'x'
TWIN_INPUTS = ['x', 'mem', 'positions', 'norm_mix_g', 'w_in', 'conv_w', 'conv_b', 'conv_ln_g', 'conv_ln_b', 'w_conv_out', 'q_norm_g', 'w_uq', 'kv_norm_g', 'w_ukv', 'w_mla_out', 'w_out', 'norm_xattn_g', 'norm_mem_g', 'w_xq', 'w_xkv', 'w_xo', 'norm_mlp_g', 'w_mlp1', 'w_mlp2', 'final_norm_g', 'loss_target', 'm_norm_mix_g', 'm_w_in', 'm_conv_w', 'm_conv_b', 'm_conv_ln_g', 'm_conv_ln_b', 'm_w_conv_out', 'm_q_norm_g', 'm_w_uq', 'm_kv_norm_g', 'm_w_ukv', 'm_w_mla_out', 'm_w_out', 'm_norm_xattn_g', 'm_norm_mem_g', 'm_w_xq', 'm_w_xkv', 'm_w_xo', 'm_norm_mlp_g', 'm_w_mlp1', 'm_w_mlp2', 'm_final_norm_g', 'v_norm_mix_g', 'v_w_in', 'v_conv_w', 'v_conv_b', 'v_conv_ln_g', 'v_conv_ln_b', 'v_w_conv_out', 'v_q_norm_g', 'v_w_uq', 'v_kv_norm_g', 'v_w_ukv', 'v_w_mla_out', 'v_w_out', 'v_norm_xattn_g', 'v_norm_mem_g', 'v_w_xq', 'v_w_xkv', 'v_w_xo', 'v_norm_mlp_g', 'v_w_mlp1', 'v_w_mlp2', 'v_final_norm_g']
TWIN_OUTPUTS = ['loss', 'grad_x', 'grad_norm_mix_g', 'grad_w_in', 'grad_conv_w', 'grad_conv_b', 'grad_conv_ln_g', 'grad_conv_ln_b', 'grad_w_conv_out', 'grad_q_norm_g', 'grad_w_uq', 'grad_kv_norm_g', 'grad_w_ukv', 'grad_w_mla_out', 'grad_w_out', 'grad_norm_xattn_g', 'grad_norm_mem_g', 'grad_w_xq', 'grad_w_xkv', 'grad_w_xo', 'grad_norm_mlp_g', 'grad_w_mlp1', 'grad_w_mlp2', 'grad_final_norm_g', 'delta_norm_mix_g', 'delta_w_in', 'delta_conv_w', 'delta_conv_b', 'delta_conv_ln_g', 'delta_conv_ln_b', 'delta_w_conv_out', 'delta_q_norm_g', 'delta_w_uq', 'delta_kv_norm_g', 'delta_w_ukv', 'delta_w_mla_out', 'delta_w_out', 'delta_norm_xattn_g', 'delta_norm_mem_g', 'delta_w_xq', 'delta_w_xkv', 'delta_w_xo', 'delta_norm_mlp_g', 'delta_w_mlp1', 'delta_w_mlp2', 'delta_final_norm_g', 'new_m_norm_mix_g', 'new_m_w_in', 'new_m_conv_w', 'new_m_conv_b', 'new_m_conv_ln_g', 'new_m_conv_ln_b', 'new_m_w_conv_out', 'new_m_q_norm_g', 'new_m_w_uq', 'new_m_kv_norm_g', 'new_m_w_ukv', 'new_m_w_mla_out', 'new_m_w_out', 'new_m_norm_xattn_g', 'new_m_norm_mem_g', 'new_m_w_xq', 'new_m_w_xkv', 'new_m_w_xo', 'new_m_norm_mlp_g', 'new_m_w_mlp1', 'new_m_w_mlp2', 'new_m_final_norm_g', 'new_v_norm_mix_g', 'new_v_w_in', 'new_v_conv_w', 'new_v_conv_b', 'new_v_conv_ln_g', 'new_v_conv_ln_b', 'new_v_w_conv_out', 'new_v_q_norm_g', 'new_v_w_uq', 'new_v_kv_norm_g', 'new_v_w_ukv', 'new_v_w_mla_out', 'new_v_w_out', 'new_v_norm_xattn_g', 'new_v_norm_mem_g', 'new_v_w_xq', 'new_v_w_xkv', 'new_v_w_xo', 'new_v_norm_mlp_g', 'new_v_w_mlp1', 'new_v_w_mlp2', 'new_v_final_norm_g']
TWIN_LEAF_KINDS = {'loss': 'loss', 'grad_x': 'grad_x', 'grad_norm_mix_g': 'grad_w', 'grad_w_in': 'grad_w', 'grad_conv_w': 'grad_w', 'grad_conv_b': 'grad_w', 'grad_conv_ln_g': 'grad_w', 'grad_conv_ln_b': 'grad_w', 'grad_w_conv_out': 'grad_w', 'grad_q_norm_g': 'grad_w', 'grad_w_uq': 'grad_w', 'grad_kv_norm_g': 'grad_w', 'grad_w_ukv': 'grad_w', 'grad_w_mla_out': 'grad_w', 'grad_w_out': 'grad_w', 'grad_norm_xattn_g': 'grad_w', 'grad_norm_mem_g': 'grad_w', 'grad_w_xq': 'grad_w', 'grad_w_xkv': 'grad_w', 'grad_w_xo': 'grad_w', 'grad_norm_mlp_g': 'grad_w', 'grad_w_mlp1': 'grad_w', 'grad_w_mlp2': 'grad_w', 'grad_final_norm_g': 'grad_w', 'delta_norm_mix_g': 'delta_w', 'delta_w_in': 'delta_w', 'delta_conv_w': 'delta_w', 'delta_conv_b': 'delta_w', 'delta_conv_ln_g': 'delta_w', 'delta_conv_ln_b': 'delta_w', 'delta_w_conv_out': 'delta_w', 'delta_q_norm_g': 'delta_w', 'delta_w_uq': 'delta_w', 'delta_kv_norm_g': 'delta_w', 'delta_w_ukv': 'delta_w', 'delta_w_mla_out': 'delta_w', 'delta_w_out': 'delta_w', 'delta_norm_xattn_g': 'delta_w', 'delta_norm_mem_g': 'delta_w', 'delta_w_xq': 'delta_w', 'delta_w_xkv': 'delta_w', 'delta_w_xo': 'delta_w', 'delta_norm_mlp_g': 'delta_w', 'delta_w_mlp1': 'delta_w', 'delta_w_mlp2': 'delta_w', 'delta_final_norm_g': 'delta_w', 'new_m_norm_mix_g': 'new_m', 'new_m_w_in': 'new_m', 'new_m_conv_w': 'new_m', 'new_m_conv_b': 'new_m', 'new_m_conv_ln_g': 'new_m', 'new_m_conv_ln_b': 'new_m', 'new_m_w_conv_out': 'new_m', 'new_m_q_norm_g': 'new_m', 'new_m_w_uq': 'new_m', 'new_m_kv_norm_g': 'new_m', 'new_m_w_ukv': 'new_m', 'new_m_w_mla_out': 'new_m', 'new_m_w_out': 'new_m', 'new_m_norm_xattn_g': 'new_m', 'new_m_norm_mem_g': 'new_m', 'new_m_w_xq': 'new_m', 'new_m_w_xkv': 'new_m', 'new_m_w_xo': 'new_m', 'new_m_norm_mlp_g': 'new_m', 'new_m_w_mlp1': 'new_m', 'new_m_w_mlp2': 'new_m', 'new_m_final_norm_g': 'new_m', 'new_v_norm_mix_g': 'new_v', 'new_v_w_in': 'new_v', 'new_v_conv_w': 'new_v', 'new_v_conv_b': 'new_v', 'new_v_conv_ln_g': 'new_v', 'new_v_conv_ln_b': 'new_v', 'new_v_w_conv_out': 'new_v', 'new_v_q_norm_g': 'new_v', 'new_v_w_uq': 'new_v', 'new_v_kv_norm_g': 'new_v', 'new_v_w_ukv': 'new_v', 'new_v_w_mla_out': 'new_v', 'new_v_w_out': 'new_v', 'new_v_norm_xattn_g': 'new_v', 'new_v_norm_mem_g': 'new_v', 'new_v_w_xq': 'new_v', 'new_v_w_xkv': 'new_v', 'new_v_w_xo': 'new_v', 'new_v_norm_mlp_g': 'new_v', 'new_v_w_mlp1': 'new_v', 'new_v_w_mlp2': 'new_v', 'new_v_final_norm_g': 'new_v'}


def _forward(args):
    return _fwd_reference(*[args[k] for k in FWD_PARAMS])


def _output_shape():
    def fwd():
        inp = _fwd_setup_inputs(0)
        return _fwd_reference(*[inp[k] for k in FWD_PARAMS])
    out = _jax.eval_shape(fwd)
    return out.shape, out.dtype

N_MICROBATCH = 1
ADAM_LR = 0.001
ADAM_B1 = 0.9
ADAM_B2 = 0.999
ADAM_EPS = 1e-08
ADAM_WD = 0.01
ADAM_STEP = 10
PER_EXAMPLE_BATCH_AXIS = {'x': 0, 'mem': 0, 'positions': 0, 'loss_target': 0}
SHARED_INPUTS = []
_WEIGHT_DTYPES = {'norm_mix_g': _jnp.float32, 'w_in': _jnp.float32, 'conv_w': _jnp.float32, 'conv_b': _jnp.float32, 'conv_ln_g': _jnp.float32, 'conv_ln_b': _jnp.float32, 'w_conv_out': _jnp.float32, 'q_norm_g': _jnp.float32, 'w_uq': _jnp.float32, 'kv_norm_g': _jnp.float32, 'w_ukv': _jnp.float32, 'w_mla_out': _jnp.float32, 'w_out': _jnp.float32, 'norm_xattn_g': _jnp.float32, 'norm_mem_g': _jnp.float32, 'w_xq': _jnp.float32, 'w_xkv': _jnp.float32, 'w_xo': _jnp.float32, 'norm_mlp_g': _jnp.float32, 'w_mlp1': _jnp.float32, 'w_mlp2': _jnp.float32, 'final_norm_g': _jnp.float32}
MOMENT_SCALE = {'norm_mix_g': 1.106838e-01, 'w_in': 5.742791e-02, 'conv_w': 1.252517e-01, 'conv_b': 3.133176e-01, 'conv_ln_g': 1.850377e-01, 'conv_ln_b': 1.933002e-01, 'w_conv_out': 9.180695e-02, 'q_norm_g': 4.217368e-02, 'w_uq': 2.949026e-02, 'kv_norm_g': 7.798760e-02, 'w_ukv': 3.706131e-02, 'w_mla_out': 3.031935e-02, 'w_out': 8.955207e-02, 'norm_xattn_g': 2.728358e-02, 'norm_mem_g': 4.044418e-02, 'w_xq': 3.863234e-02, 'w_xkv': 3.933391e-02, 'w_xo': 2.859160e-02, 'norm_mlp_g': 2.315502e-01, 'w_mlp1': 1.106790e-01, 'w_mlp2': 2.347617e-01, 'final_norm_g': 6.468820e+01}


def _to_microbatches(a, axis):
    t = _jnp.moveaxis(a, axis, 0)
    t = t.reshape((N_MICROBATCH, t.shape[0] // N_MICROBATCH) + t.shape[1:])
    return _jnp.moveaxis(t, 1, axis + 1)


def setup_inputs(seed: int = 0) -> dict:
    inp = _fwd_setup_inputs(seed)
    key = _jax.random.fold_in(_jax.random.key(seed), 7919)
    shape, _ = _output_shape()
    out = dict(inp)
    out["loss_target"] = _jax.random.normal(_jax.random.fold_in(key, 0), shape, _jnp.float32)
    for i, name in enumerate(TWIN_WEIGHTS):
        w = inp[name].astype(_jnp.float32)
        if MOMENT_SCALE is None:
            s = _jnp.sqrt(_jnp.mean(_jnp.square(w)) + 1e-30)
        else:
            s = MOMENT_SCALE[name]
        km, kv = _jax.random.split(_jax.random.fold_in(key, i + 1))
        out[name] = w
        out["m_" + name] = s * _jax.random.normal(km, w.shape, _jnp.float32)
        out["v_" + name] = (s * s) * _jax.random.uniform(kv, w.shape, _jnp.float32, 0.5, 1.5)
    if N_MICROBATCH > 1:
        for name, axis in PER_EXAMPLE_BATCH_AXIS.items():
            out[name] = _to_microbatches(out[name], axis)
    return {'x': out['x'], 'mem': out['mem'], 'positions': out['positions'], 'norm_mix_g': out['norm_mix_g'], 'w_in': out['w_in'], 'conv_w': out['conv_w'], 'conv_b': out['conv_b'], 'conv_ln_g': out['conv_ln_g'], 'conv_ln_b': out['conv_ln_b'], 'w_conv_out': out['w_conv_out'], 'q_norm_g': out['q_norm_g'], 'w_uq': out['w_uq'], 'kv_norm_g': out['kv_norm_g'], 'w_ukv': out['w_ukv'], 'w_mla_out': out['w_mla_out'], 'w_out': out['w_out'], 'norm_xattn_g': out['norm_xattn_g'], 'norm_mem_g': out['norm_mem_g'], 'w_xq': out['w_xq'], 'w_xkv': out['w_xkv'], 'w_xo': out['w_xo'], 'norm_mlp_g': out['norm_mlp_g'], 'w_mlp1': out['w_mlp1'], 'w_mlp2': out['w_mlp2'], 'final_norm_g': out['final_norm_g'], 'loss_target': out['loss_target'], 'm_norm_mix_g': out['m_norm_mix_g'], 'm_w_in': out['m_w_in'], 'm_conv_w': out['m_conv_w'], 'm_conv_b': out['m_conv_b'], 'm_conv_ln_g': out['m_conv_ln_g'], 'm_conv_ln_b': out['m_conv_ln_b'], 'm_w_conv_out': out['m_w_conv_out'], 'm_q_norm_g': out['m_q_norm_g'], 'm_w_uq': out['m_w_uq'], 'm_kv_norm_g': out['m_kv_norm_g'], 'm_w_ukv': out['m_w_ukv'], 'm_w_mla_out': out['m_w_mla_out'], 'm_w_out': out['m_w_out'], 'm_norm_xattn_g': out['m_norm_xattn_g'], 'm_norm_mem_g': out['m_norm_mem_g'], 'm_w_xq': out['m_w_xq'], 'm_w_xkv': out['m_w_xkv'], 'm_w_xo': out['m_w_xo'], 'm_norm_mlp_g': out['m_norm_mlp_g'], 'm_w_mlp1': out['m_w_mlp1'], 'm_w_mlp2': out['m_w_mlp2'], 'm_final_norm_g': out['m_final_norm_g'], 'v_norm_mix_g': out['v_norm_mix_g'], 'v_w_in': out['v_w_in'], 'v_conv_w': out['v_conv_w'], 'v_conv_b': out['v_conv_b'], 'v_conv_ln_g': out['v_conv_ln_g'], 'v_conv_ln_b': out['v_conv_ln_b'], 'v_w_conv_out': out['v_w_conv_out'], 'v_q_norm_g': out['v_q_norm_g'], 'v_w_uq': out['v_w_uq'], 'v_kv_norm_g': out['v_kv_norm_g'], 'v_w_ukv': out['v_w_ukv'], 'v_w_mla_out': out['v_w_mla_out'], 'v_w_out': out['v_w_out'], 'v_norm_xattn_g': out['v_norm_xattn_g'], 'v_norm_mem_g': out['v_norm_mem_g'], 'v_w_xq': out['v_w_xq'], 'v_w_xkv': out['v_w_xkv'], 'v_w_xo': out['v_w_xo'], 'v_norm_mlp_g': out['v_norm_mlp_g'], 'v_w_mlp1': out['v_w_mlp1'], 'v_w_mlp2': out['v_w_mlp2'], 'v_final_norm_g': out['v_final_norm_g']}


def _loss(weights, diff, rest, loss_target):
    with _jax.named_scope("forward"):
        args = {**rest, TWIN_DIFF_INPUT: diff, **{k: w.astype(_WEIGHT_DTYPES[k]) for k, w in weights.items()}}
        y = _forward(args)
    with _jax.named_scope("loss_head"):
        err = _jnp.square(y.astype(_jnp.float32) - loss_target)
        return 0.5 * _jnp.sum(_jnp.mean(err, axis=-1)) if err.ndim else 0.5 * err


def _adamw(w, g, m, v):
    m = ADAM_B1 * m + (1.0 - ADAM_B1) * g
    v = ADAM_B2 * v + (1.0 - ADAM_B2) * _jnp.square(g)
    m_hat = m / (1.0 - ADAM_B1 ** ADAM_STEP)
    v_hat = v / (1.0 - ADAM_B2 ** ADAM_STEP)
    delta = -ADAM_LR * (m_hat / (_jnp.sqrt(v_hat) + ADAM_EPS) + ADAM_WD * w)
    return delta, m, v


def reference(x, mem, positions, norm_mix_g, w_in, conv_w, conv_b, conv_ln_g, conv_ln_b, w_conv_out, q_norm_g, w_uq, kv_norm_g, w_ukv, w_mla_out, w_out, norm_xattn_g, norm_mem_g, w_xq, w_xkv, w_xo, norm_mlp_g, w_mlp1, w_mlp2, final_norm_g, loss_target, m_norm_mix_g, m_w_in, m_conv_w, m_conv_b, m_conv_ln_g, m_conv_ln_b, m_w_conv_out, m_q_norm_g, m_w_uq, m_kv_norm_g, m_w_ukv, m_w_mla_out, m_w_out, m_norm_xattn_g, m_norm_mem_g, m_w_xq, m_w_xkv, m_w_xo, m_norm_mlp_g, m_w_mlp1, m_w_mlp2, m_final_norm_g, v_norm_mix_g, v_w_in, v_conv_w, v_conv_b, v_conv_ln_g, v_conv_ln_b, v_w_conv_out, v_q_norm_g, v_w_uq, v_kv_norm_g, v_w_ukv, v_w_mla_out, v_w_out, v_norm_xattn_g, v_norm_mem_g, v_w_xq, v_w_xkv, v_w_xo, v_norm_mlp_g, v_w_mlp1, v_w_mlp2, v_final_norm_g):
    given = dict(x=x, mem=mem, positions=positions, norm_mix_g=norm_mix_g, w_in=w_in, conv_w=conv_w, conv_b=conv_b, conv_ln_g=conv_ln_g, conv_ln_b=conv_ln_b, w_conv_out=w_conv_out, q_norm_g=q_norm_g, w_uq=w_uq, kv_norm_g=kv_norm_g, w_ukv=w_ukv, w_mla_out=w_mla_out, w_out=w_out, norm_xattn_g=norm_xattn_g, norm_mem_g=norm_mem_g, w_xq=w_xq, w_xkv=w_xkv, w_xo=w_xo, norm_mlp_g=norm_mlp_g, w_mlp1=w_mlp1, w_mlp2=w_mlp2, final_norm_g=final_norm_g, loss_target=loss_target, m_norm_mix_g=m_norm_mix_g, m_w_in=m_w_in, m_conv_w=m_conv_w, m_conv_b=m_conv_b, m_conv_ln_g=m_conv_ln_g, m_conv_ln_b=m_conv_ln_b, m_w_conv_out=m_w_conv_out, m_q_norm_g=m_q_norm_g, m_w_uq=m_w_uq, m_kv_norm_g=m_kv_norm_g, m_w_ukv=m_w_ukv, m_w_mla_out=m_w_mla_out, m_w_out=m_w_out, m_norm_xattn_g=m_norm_xattn_g, m_norm_mem_g=m_norm_mem_g, m_w_xq=m_w_xq, m_w_xkv=m_w_xkv, m_w_xo=m_w_xo, m_norm_mlp_g=m_norm_mlp_g, m_w_mlp1=m_w_mlp1, m_w_mlp2=m_w_mlp2, m_final_norm_g=m_final_norm_g, v_norm_mix_g=v_norm_mix_g, v_w_in=v_w_in, v_conv_w=v_conv_w, v_conv_b=v_conv_b, v_conv_ln_g=v_conv_ln_g, v_conv_ln_b=v_conv_ln_b, v_w_conv_out=v_w_conv_out, v_q_norm_g=v_q_norm_g, v_w_uq=v_w_uq, v_kv_norm_g=v_kv_norm_g, v_w_ukv=v_w_ukv, v_w_mla_out=v_w_mla_out, v_w_out=v_w_out, v_norm_xattn_g=v_norm_xattn_g, v_norm_mem_g=v_norm_mem_g, v_w_xq=v_w_xq, v_w_xkv=v_w_xkv, v_w_xo=v_w_xo, v_norm_mlp_g=v_norm_mlp_g, v_w_mlp1=v_w_mlp1, v_w_mlp2=v_w_mlp2, v_final_norm_g=v_final_norm_g)
    weights = {n: given[n] for n in TWIN_WEIGHTS}
    shared = {n: given[n] for n in SHARED_INPUTS}
    per_example = {n: given[n] for n in ['x', 'mem', 'positions']}
    grad_fn = _jax.value_and_grad(_loss, argnums=(0, 1))

    def one_microbatch(ex, loss_target):
        ex = dict(ex)
        diff = ex.pop(TWIN_DIFF_INPUT)
        return grad_fn(weights, diff, {**shared, **ex}, loss_target)

    if N_MICROBATCH == 1:
        loss, (grad_w, grad_x) = one_microbatch(per_example, given["loss_target"])
    else:
        def body(carry, xs):
            loss_sum, grad_sum = carry
            l_k, (gw_k, gx_k) = one_microbatch(xs[0], xs[1])
            with _jax.named_scope("update"):
                return (loss_sum + l_k, _jax.tree.map(_jnp.add, grad_sum, gw_k)), gx_k

        init = (_jnp.zeros((), _jnp.float32), _jax.tree.map(_jnp.zeros_like, weights))
        (loss, grad_w), grad_x = _jax.lax.scan(body, init, (per_example, given["loss_target"]))
    with _jax.named_scope("update"):
        delta_w, new_m, new_v = {}, {}, {}
        for n in TWIN_WEIGHTS:
            delta_w[n], new_m[n], new_v[n] = _adamw(weights[n], grad_w[n], given["m_" + n], given["v_" + n])
    return (loss, grad_x, *[grad_w[n] for n in TWIN_WEIGHTS], *[delta_w[n] for n in TWIN_WEIGHTS],
            *[new_m[n] for n in TWIN_WEIGHTS], *[new_v[n] for n in TWIN_WEIGHTS])
```

```python
import functools

import jax
import jax.numpy as jnp
import numpy as np
from jax import lax
from jax.experimental import pallas as pl
from jax.experimental.pallas import tpu as pltpu

F32, BF16 = jnp.float32, jnp.bfloat16
MESH = pl.DeviceIdType.MESH

N_DEV = 8
D = 1024
CC = D // 2
CW = 31
HALO = 32
NH = 8
NOPE, ROPE, VD = D // 16, D // 32, D // 16
QL, KL = 3 * D // 8, D // 4
HP = 128
XH, XD = 4, D // 8
DFF = 4 * D
EPS = 1e-6
THETA = 10000.0
MLA_SCALE = float((NOPE + ROPE) ** -0.5)
X_SCALE = float(XD ** -0.5)
NEG = -1e30
P_A, P_G, P_Q, P_KV, P_KR, P_GL, P_END = 0, CC, 2 * CC, 2 * CC + QL, 2 * CC + QL + KL, 2 * CC + QL + KL + HP, 2 * CC + QL + KL + HP + 2 * D

ADAM_LR, ADAM_B1, ADAM_B2, ADAM_EPS, ADAM_WD, ADAM_STEP = 0.001, 0.9, 0.999, 1e-08, 0.01, 10

ROW_TILE = 256
ATT_TILE = 512
DW_TILE = 512
PACK_TILE = 1536
VMEM_LIMIT = 56 * 1024 * 1024


def _mm(a, w):
    return jnp.dot(a.astype(BF16), w, preferred_element_type=F32)


def _mm_nt(a, w):
    return lax.dot_general(a.astype(BF16), w, (((1,), (1,)), ((), ())), preferred_element_type=F32)


def _rms(x, g):
    r = lax.rsqrt(jnp.mean(x * x, axis=-1, keepdims=True) + EPS)
    xh = x * r
    return xh * g, xh, r


def _rms_bwd(dy, xh, r, g):
    dxh = dy * g
    dx = r * (dxh - xh * jnp.mean(dxh * xh, axis=-1, keepdims=True))
    return dx, jnp.sum(dy * xh, axis=0, keepdims=True)


def _sig(x):
    return 1.0 / (1.0 + jnp.exp(-x))


def _row_call(name, fn, n, rows, consts, outs, accs=(), scratch=()):
    def row_spec(shape, mode):
        r = shape[-2] // n
        if mode == "cur":
            f = lambda i: i
        elif mode == "prev":
            f = lambda i: jnp.maximum(i - 1, 0)
        else:
            f = lambda i: jnp.minimum(i + 1, n - 1)
        if len(shape) == 2:
            return pl.BlockSpec((r, shape[1]), lambda i: (f(i), 0))
        return pl.BlockSpec((shape[0], r, shape[2]), lambda i: (0, f(i), 0))

    def whole_spec(shape, single):
        nd = len(shape)
        if single:
            return pl.BlockSpec(shape, lambda i: (0,) * nd, pipeline_mode=pl.Buffered(1))
        return pl.BlockSpec(shape, lambda i: (0,) * nd)

    nr, nc, no, na = len(rows), len(consts), len(outs), len(accs)

    def body(*refs):
        i = pl.program_id(0)
        row_refs, const_refs = refs[:nr], refs[nr:nr + nc]
        out_refs, acc_refs = refs[nr + nc:nr + nc + no], refs[nr + nc + no:nr + nc + no + na]
        if na:
            @pl.when(i == 0)
            def _():
                for a in acc_refs:
                    a[...] = jnp.zeros(a.shape, a.dtype)
        fn(i, row_refs, const_refs, out_refs, acc_refs, refs[nr + nc + no + na:])

    res = pl.pallas_call(
        body, name=name, grid=(n,),
        in_specs=[row_spec(a.shape, m) for a, m in rows] + [whole_spec(c.shape, True) for c in consts],
        out_specs=[row_spec(s, "cur") for s, _ in outs] + [whole_spec(s, False) for s, _ in accs],
        out_shape=[jax.ShapeDtypeStruct(s, d) for s, d in list(outs) + list(accs)],
        scratch_shapes=list(scratch),
        compiler_params=pltpu.CompilerParams(dimension_semantics=("arbitrary",), vmem_limit_bytes=VMEM_LIMIT),
    )(*[a for a, _ in rows], *consts)
    return list(res)


def _in_proj(x, g_mix, wp, n):
    t = x.shape[0]

    def fn(i, rows, consts, outs, accs, scr):
        g, w = consts
        u, _, _ = _rms(rows[0][...], g[...])
        ub = u.astype(BF16)
        outs[0][...] = ub
        for k, (lo, hi) in enumerate(((P_A, P_G), (P_G, P_Q), (P_Q, P_KV), (P_KV, P_KR), (P_KR, P_GL), (P_GL, P_END))):
            outs[1 + k][...] = _mm(ub, w[:, lo:hi])

    outs = [((t, D), BF16), ((t, CC), F32), ((t, CC), F32), ((t, QL), F32), ((t, KL), F32), ((t, HP), F32), ((t, 2 * D), F32)]
    return _row_call("in_proj", fn, n, [(x, "cur")], [g_mix, wp], outs)


def _fill_glu_window(i, a, gt, ap, gtp, zz):
    r = a.shape[0]
    zp = ap[r - HALO:, :] * _sig(gtp[r - HALO:, :])
    zz[0:HALO, :] = jnp.where(i > 0, zp, 0.0)
    zz[HALO:, :] = a[...] * _sig(gt[...])


def _conv_branch(a, gt, cw, cb, lng, lnb, wco, n):
    t = a.shape[0]
    r = t // n

    def fn(i, rows, consts, outs, accs, scr):
        w, b, lg, lb, wo = consts
        zz = scr[0]
        _fill_glu_window(i, rows[0], rows[1], rows[2], rows[3], zz)
        for c in range(CC // HP):
            ln = slice(c * HP, (c + 1) * HP)
            acc = jnp.zeros((r, HP), F32)
            for j in range(CW):
                acc = acc + w[j:j + 1, ln] * zz[HALO - (CW - 1) + j:HALO - (CW - 1) + j + r, ln]
            outs[0][:, ln] = acc + b[:, ln]
        zc = outs[0][...]
        mu = jnp.mean(zc, axis=-1, keepdims=True)
        dlt = zc - mu
        rs = lax.rsqrt(jnp.mean(dlt * dlt, axis=-1, keepdims=True) + EPS)
        zn = dlt * rs * lg[...] + lb[...]
        outs[1][...] = _mm(zn * _sig(zn), wo[...])

    return _row_call("conv_branch", fn, n, [(a, "cur"), (gt, "cur"), (a, "prev"), (gt, "prev")],
                     [cw, cb, lng, lnb, wco], [((t, CC), F32), ((t, D), F32)],
                     scratch=[pltpu.VMEM((r + HALO, CC), F32)])


def _rope(v, c, sa, sb):
    return v * c + pltpu.roll(v, HP - ROPE // 2, 1) * sa + pltpu.roll(v, ROPE // 2, 1) * sb


def _rope_bwd(dv, c, sa, sb):
    return dv * c + pltpu.roll(dv * sa, ROPE // 2, 1) + pltpu.roll(dv * sb, HP - ROPE // 2, 1)


def _mla_prep(cq, ckv, krp, tc, tsa, tsb, gq, wuq, gkv, wk, wv, n):
    t = cq.shape[0]

    def fn(i, rows, consts, outs, accs, scr):
        g_q, w_q, g_kv, w_k, w_v = consts
        c, sa, sb = rows[3][...], rows[4][...], rows[5][...]
        cqn = _rms(rows[0][...], g_q[...])[0].astype(BF16)
        ckvn = _rms(rows[1][...], g_kv[...])[0].astype(BF16)
        outs[3][...] = cqn
        outs[4][...] = ckvn
        krr = _rope(rows[2][...], c, sa, sb)
        for h in range(NH):
            ln = slice(h * HP, (h + 1) * HP)
            outs[0][:, ln] = (_rope(_mm(cqn, w_q[:, ln]), c, sa, sb) * MLA_SCALE).astype(BF16)
            outs[1][:, ln] = (_mm(ckvn, w_k[:, ln]) + krr).astype(BF16)
        outs[2][...] = _mm(ckvn, w_v[...]).astype(BF16)

    outs = [((t, NH * HP), BF16)] * 3 + [((t, QL), BF16), ((t, KL), BF16)]
    return _row_call("mla_prep", fn, n, [(a, "cur") for a in (cq, ckv, krp, tc, tsa, tsb)], [gq, wuq, gkv, wk, wv], outs)


def _causal_mask(tq, transposed):
    row = lax.broadcasted_iota(jnp.int32, (tq, tq), 0)
    col = lax.broadcasted_iota(jnp.int32, (tq, tq), 1)
    return (row <= col) if transposed else (col <= row)


def _flash_fwd(q, k, v):
    t = q.shape[0]
    tq = min(ATT_TILE, t)
    nq = t // tq

    def body(q_ref, k_ref, v_ref, o_ref, lse_ref):
        i = pl.program_id(1)
        qb = q_ref[...]

        def step(j, carry, masked):
            m, l, acc = carry
            at = pl.ds(pl.multiple_of(j * tq, tq), tq)
            s = _mm_nt(qb, k_ref[at, :])
            if masked:
                s = jnp.where(_causal_mask(tq, False), s, NEG)
            m_new = jnp.maximum(m, jnp.max(s, axis=-1, keepdims=True))
            alpha = jnp.exp(m - m_new)
            p = jnp.exp(s - m_new)
            return m_new, alpha * l + jnp.sum(p, axis=-1, keepdims=True), alpha * acc + _mm(p, v_ref[at, :])

        init = (jnp.full((tq, 1), NEG, F32), jnp.zeros((tq, 1), F32), jnp.zeros((tq, HP), F32))
        carry = lax.fori_loop(0, i, lambda j, c: step(j, c, False), init)
        m, l, acc = step(i, carry, True)
        o_ref[...] = acc / l
        lse_ref[0] = m + jnp.log(l)

    return pl.pallas_call(
        body, name="flash_fwd", grid=(NH, nq),
        in_specs=[pl.BlockSpec((tq, HP), lambda h, i: (i, h)), pl.BlockSpec((t, HP), lambda h, i: (0, h)),
                  pl.BlockSpec((t, HP), lambda h, i: (0, h))],
        out_specs=[pl.BlockSpec((tq, HP), lambda h, i: (i, h)), pl.BlockSpec((1, tq, 1), lambda h, i: (h, i, 0))],
        out_shape=[jax.ShapeDtypeStruct((t, NH * HP), F32), jax.ShapeDtypeStruct((NH, t, 1), F32)],
        compiler_params=pltpu.CompilerParams(dimension_semantics=("arbitrary", "arbitrary"), vmem_limit_bytes=VMEM_LIMIT),
    )(q, k, v)


def _flash_dq(q, k, v, do, lse, delta):
    t = q.shape[0]
    tq = min(ATT_TILE, t)
    nq = t // tq

    def body(q_ref, k_ref, v_ref, do_ref, lse_ref, dl_ref, dq_ref):
        i = pl.program_id(1)
        qb, dob, lse_i, dl_i = q_ref[...], do_ref[...], lse_ref[0], dl_ref[0]

        def step(j, acc, masked):
            at = pl.ds(pl.multiple_of(j * tq, tq), tq)
            kb = k_ref[at, :]
            s = _mm_nt(qb, kb)
            if masked:
                s = jnp.where(_causal_mask(tq, False), s, NEG)
            p = jnp.exp(s - lse_i)
            ds = p * (_mm_nt(dob, v_ref[at, :]) - dl_i)
            return acc + _mm(ds, kb)

        acc = lax.fori_loop(0, i, lambda j, c: step(j, c, False), jnp.zeros((tq, HP), F32))
        dq_ref[...] = step(i, acc, True)

    blk = pl.BlockSpec((tq, HP), lambda h, i: (i, h))
    whole = pl.BlockSpec((t, HP), lambda h, i: (0, h))
    col = pl.BlockSpec((1, tq, 1), lambda h, i: (h, i, 0))
    return pl.pallas_call(
        body, name="flash_dq", grid=(NH, nq), in_specs=[blk, whole, whole, blk, col, col], out_specs=blk,
        out_shape=jax.ShapeDtypeStruct((t, NH * HP), F32),
        compiler_params=pltpu.CompilerParams(dimension_semantics=("arbitrary", "arbitrary"), vmem_limit_bytes=VMEM_LIMIT),
    )(q, k, v, do, lse, delta)


def _flash_dkv(q, k, v, do, lse_row, delta_row):
    t = q.shape[0]
    tq = min(ATT_TILE, t)
    nq = t // tq

    def body(q_ref, k_ref, v_ref, do_ref, lse_ref, dl_ref, dk_ref, dv_ref):
        j = pl.program_id(1)
        kb, vb = k_ref[...], v_ref[...]

        def step(i, carry, masked):
            dk, dv = carry
            at = pl.ds(pl.multiple_of(i * tq, tq), tq)
            qb, dob = q_ref[at, :], do_ref[at, :]
            st = _mm_nt(kb, qb)
            if masked:
                st = jnp.where(_causal_mask(tq, True), st, NEG)
            pt = jnp.exp(st - lse_ref[0, :, at])
            dst = pt * (_mm_nt(vb, dob) - dl_ref[0, :, at])
            return dk + _mm(dst, qb), dv + _mm(pt, dob)

        carry = step(j, (jnp.zeros((tq, HP), F32), jnp.zeros((tq, HP), F32)), True)
        dk, dv = lax.fori_loop(j + 1, nq, lambda i, c: step(i, c, False), carry)
        dk_ref[...] = dk
        dv_ref[...] = dv

    blk = pl.BlockSpec((tq, HP), lambda h, j: (j, h))
    whole = pl.BlockSpec((t, HP), lambda h, j: (0, h))
    row = pl.BlockSpec((1, 1, t), lambda h, j: (h, 0, 0))
    return pl.pallas_call(
        body, name="flash_dkv", grid=(NH, nq), in_specs=[whole, blk, blk, whole, row, row], out_specs=[blk, blk],
        out_shape=[jax.ShapeDtypeStruct((t, NH * HP), F32)] * 2,
        compiler_params=pltpu.CompilerParams(dimension_semantics=("arbitrary", "arbitrary"), vmem_limit_bytes=VMEM_LIMIT),
    )(q, k, v, do, lse_row, delta_row)


def _mem_kv(mem, g_mem, wxkv):
    m = mem.shape[0]

    def fn(i, rows, consts, outs, accs, scr):
        mn = _rms(rows[0][...], consts[0][...])[0].astype(BF16)
        outs[0][...] = mn
        outs[1][...] = _mm(mn, consts[1][:, 0:XH * XD]).astype(BF16)
        outs[2][...] = _mm(mn, consts[1][:, XH * XD:]).astype(BF16)

    return _row_call("mem_kv", fn, 1, [(mem, "cur")], [g_mem, wxkv], [((m, D), BF16), ((m, XH * XD), BF16), ((m, XH * XD), BF16)])


def _merge_xattn(o, gl, conv_out, x, wmo, wo, g_x, wxq, kx, vx, wxo, n):
    t = x.shape[0]

    def fn(i, rows, consts, outs, accs, scr):
        w_mo, w_o, g, w_xq, k_x, v_x, w_xo = consts
        ob = rows[0][...].astype(BF16)
        outs[7][...] = ob
        mla = _mm(ob, w_mo[...])
        outs[0][...] = mla
        merged = (_sig(rows[1][:, 0:D]) * rows[2][...] + _sig(rows[1][:, D:]) * mla).astype(BF16)
        outs[1][...] = merged
        h1 = rows[3][...] + _mm(merged, w_o[...])
        outs[2][...] = h1
        u1 = _rms(h1, g[...])[0].astype(BF16)
        outs[3][...] = u1
        qx = (_mm(u1, w_xq[...]) * X_SCALE).astype(BF16)
        outs[4][...] = qx
        for h in range(XH):
            ln = slice(h * XD, (h + 1) * XD)
            s = _mm_nt(qx[:, ln], k_x[:, ln])
            e = jnp.exp(s - jnp.max(s, axis=-1, keepdims=True))
            p = e / jnp.sum(e, axis=-1, keepdims=True)
            outs[5][:, ln] = _mm(p, v_x[:, ln]).astype(BF16)
        outs[6][...] = h1 + _mm(outs[5][...], w_xo[...])

    outs = [((t, D), F32), ((t, D), BF16), ((t, D), F32), ((t, D), BF16), ((t, XH * XD), BF16), ((t, XH * XD), BF16),
            ((t, D), F32), ((t, NH * HP), BF16)]
    return _row_call("merge_xattn", fn, n, [(a, "cur") for a in (o, gl, conv_out, x)], [wmo, wo, g_x, wxq, kx, vx, wxo], outs)


def _mlp_loss(h2, target, g_mlp, w1, w2, g_fin, n):
    t = h2.shape[0]
    nck = DFF // D

    def fn(i, rows, consts, outs, accs, scr):
        g_m, w_1, w_2, g_f = consts
        h = rows[0][...]
        u2, xh2, r2 = _rms(h, g_m[...])
        ub = u2.astype(BF16)
        outs[0][...] = ub
        h3 = h
        a1 = []
        for c in range(nck):
            ck = slice(c * D, (c + 1) * D)
            a = _mm(ub, w_1[:, ck])
            a1.append(a)
            rl = jnp.maximum(a, 0.0)
            rb = (rl * rl).astype(BF16)
            outs[1][:, ck] = rb
            h3 = h3 + _mm(rb, w_2[ck, :])
        y, xh3, r3 = _rms(h3, g_f[...])
        err = y - rows[1][...]
        accs[0][...] += jnp.sum(jnp.sum(err * err, axis=1, keepdims=True), axis=0, keepdims=True) * (0.5 / D)
        dh3, dgf = _rms_bwd(err * (1.0 / D), xh3, r3, g_f[...])
        accs[1][...] += dgf
        db = dh3.astype(BF16)
        outs[3][...] = db
        du2 = jnp.zeros_like(h)
        for c in range(nck):
            ck = slice(c * D, (c + 1) * D)
            da = (_mm_nt(db, w_2[ck, :]) * (2.0 * jnp.maximum(a1[c], 0.0))).astype(BF16)
            outs[2][:, ck] = da
            du2 = du2 + _mm_nt(da, w_1[:, ck])
        dx2, dgm = _rms_bwd(du2, xh2, r2, g_m[...])
        accs[2][...] += dgm
        outs[4][...] = dh3 + dx2

    outs = [((t, D), BF16), ((t, DFF), BF16), ((t, DFF), BF16), ((t, D), BF16), ((t, D), F32)]
    accs = [((1, 1), F32), ((1, D), F32), ((1, D), F32)]
    return _row_call("mlp_loss", fn, n, [(h2, "cur"), (target, "cur")], [g_mlp, w1, w2, g_fin], outs, accs)


def _merge_xattn_bwd(dh2, h1, qx, gl, conv_out, mla, o, wxo, kx, vx, wxq, g_x, wo, wmo, n):
    t = dh2.shape[0]
    m = kx.shape[0]

    def fn(i, rows, consts, outs, accs, scr):
        w_xo, k_x, v_x, w_xq, g, w_o, w_mo = consts
        d2 = rows[0][...]
        d2b = d2.astype(BF16)
        outs[7][...] = d2b
        dox = _mm_nt(d2b, w_xo[...]).astype(BF16)
        q = rows[2][...]
        dq = []
        for h in range(XH):
            ln = slice(h * XD, (h + 1) * XD)
            qh, kh, vh, doh = q[:, ln], k_x[:, ln], v_x[:, ln], dox[:, ln]
            s = _mm_nt(qh, kh)
            e = jnp.exp(s - jnp.max(s, axis=-1, keepdims=True))
            p = e / jnp.sum(e, axis=-1, keepdims=True)
            dp = _mm_nt(doh, vh)
            ds = p * (dp - jnp.sum(p * dp, axis=-1, keepdims=True))
            dq.append(_mm(ds, kh) * X_SCALE)
            st = _mm_nt(kh, qh)
            et = jnp.exp(st - jnp.max(st, axis=0, keepdims=True))
            pt = et / jnp.sum(et, axis=0, keepdims=True)
            dpt = _mm_nt(vh, doh)
            dst = pt * (dpt - jnp.sum(pt * dpt, axis=0, keepdims=True))
            accs[0][:, ln] += _mm(dst, qh)
            accs[1][:, ln] += _mm(pt, doh)
        dqx = jnp.concatenate(dq, axis=1).astype(BF16)
        outs[0][...] = dqx
        _, xh1, r1 = _rms(rows[1][...], g[...])
        dx1, dg = _rms_bwd(_mm_nt(dqx, w_xq[...]), xh1, r1, g[...])
        accs[2][...] += dg
        d1 = d2 + dx1
        outs[1][...] = d1
        d1b = d1.astype(BF16)
        outs[8][...] = d1b
        dm = _mm_nt(d1b, w_o[...])
        g0, g1 = _sig(rows[3][:, 0:D]), _sig(rows[3][:, D:])
        outs[2][:, 0:D] = (dm * rows[4][...] * g0 * (1.0 - g0)).astype(BF16)
        outs[2][:, D:] = (dm * rows[5][...] * g1 * (1.0 - g1)).astype(BF16)
        outs[3][...] = (dm * g0).astype(BF16)
        dmla = (dm * g1).astype(BF16)
        outs[4][...] = dmla
        do = _mm_nt(dmla, w_mo[...])
        outs[5][...] = do.astype(BF16)
        prod = do * rows[6][...]
        for h in range(NH):
            outs[6][h] = jnp.sum(prod[:, h * HP:(h + 1) * HP], axis=-1, keepdims=True)

    outs = [((t, XH * XD), BF16), ((t, D), F32), ((t, 2 * D), BF16), ((t, D), BF16), ((t, D), BF16), ((t, NH * HP), BF16),
            ((NH, t, 1), F32), ((t, D), BF16), ((t, D), BF16)]
    accs = [((m, XH * XD), F32), ((m, XH * XD), F32), ((1, D), F32)]
    return _row_call("merge_xattn_bwd", fn, n, [(a, "cur") for a in (dh2, h1, qx, gl, conv_out, mla, o)],
                     [wxo, kx, vx, wxq, g_x, wo, wmo], outs, accs)


def _mla_prep_bwd(dq, dk, dv, cq, ckv, tc, tsa, tsb, gq, wuq, gkv, wk, wv, n):
    t = dq.shape[0]

    def fn(i, rows, consts, outs, accs, scr):
        g_q, w_q, g_kv, w_k, w_v = consts
        c, sa, sb = rows[5][...], rows[6][...], rows[7][...]
        dkr = jnp.zeros((rows[0].shape[0], HP), F32)
        for h in range(NH):
            ln = slice(h * HP, (h + 1) * HP)
            outs[0][:, ln] = _rope_bwd(rows[0][:, ln] * MLA_SCALE, c, sa, sb).astype(BF16)
            dkr = dkr + rows[1][:, ln]
        lane = lax.broadcasted_iota(jnp.int32, dkr.shape, 1)
        outs[5][...] = jnp.where((lane >= NOPE) & (lane < NOPE + ROPE), _rope_bwd(dkr, c, sa, sb), 0.0).astype(BF16)
        dkb, dvb = rows[1][...].astype(BF16), rows[2][...].astype(BF16)
        outs[1][...] = dkb
        outs[2][...] = dvb
        _, xq, rq = _rms(rows[3][...], g_q[...])
        dcq, dgq = _rms_bwd(_mm_nt(outs[0][...], w_q[...]), xq, rq, g_q[...])
        outs[3][...] = dcq.astype(BF16)
        accs[0][...] += dgq
        _, xk, rk = _rms(rows[4][...], g_kv[...])
        dckv, dgk = _rms_bwd(_mm_nt(dkb, w_k[...]) + _mm_nt(dvb, w_v[...]), xk, rk, g_kv[...])
        outs[4][...] = dckv.astype(BF16)
        accs[1][...] += dgk

    outs = [((t, NH * HP), BF16)] * 3 + [((t, QL), BF16), ((t, KL), BF16), ((t, HP), BF16)]
    return _row_call("mla_prep_bwd", fn, n, [(a, "cur") for a in (dq, dk, dv, cq, ckv, tc, tsa, tsb)],
                     [gq, wuq, gkv, wk, wv], outs, [((1, QL), F32), ((1, KL), F32)])


def _conv_out_bwd(dco, zc, wco, lng, lnb, n):
    t = zc.shape[0]

    def fn(i, rows, consts, outs, accs, scr):
        wo, lg, lb = consts
        z = rows[1][...]
        mu = jnp.mean(z, axis=-1, keepdims=True)
        dlt = z - mu
        rs = lax.rsqrt(jnp.mean(dlt * dlt, axis=-1, keepdims=True) + EPS)
        xh = dlt * rs
        zn = xh * lg[...] + lb[...]
        sg = _sig(zn)
        outs[0][...] = (zn * sg).astype(BF16)
        dzn = _mm_nt(rows[0][...], wo[...]) * (sg * (1.0 + zn * (1.0 - sg)))
        accs[0][...] += jnp.sum(dzn * xh, axis=0, keepdims=True)
        accs[1][...] += jnp.sum(dzn, axis=0, keepdims=True)
        dxh = dzn * lg[...]
        dzc = rs * (dxh - jnp.mean(dxh, axis=-1, keepdims=True) - xh * jnp.mean(dxh * xh, axis=-1, keepdims=True))
        outs[1][...] = dzc
        accs[2][...] += jnp.sum(dzc, axis=0, keepdims=True)

    return _row_call("conv_out_bwd", fn, n, [(dco, "cur"), (zc, "cur")], [wco, lng, lnb],
                     [((t, CC), BF16), ((t, CC), F32)], [((1, CC), F32)] * 3)


def _conv_glu_bwd(dzc, a, gt, cw, n):
    t = a.shape[0]
    r = t // n

    def fn(i, rows, consts, outs, accs, scr):
        w = consts[0]
        zz, dd = scr
        _fill_glu_window(i, rows[2], rows[3], rows[4], rows[5], zz)
        dd[0:r, :] = rows[0][...]
        dd[r:, :] = jnp.where(i < n - 1, rows[1][0:HALO, :], 0.0)
        for c in range(CC // HP):
            ln = slice(c * HP, (c + 1) * HP)
            dcur = dd[0:r, ln]
            acc = jnp.zeros((r, HP), F32)
            for j in range(CW):
                acc = acc + w[j:j + 1, ln] * dd[CW - 1 - j:CW - 1 - j + r, ln]
                accs[0][j:j + 1, ln] += jnp.sum(dcur * zz[HALO - (CW - 1) + j:HALO - (CW - 1) + j + r, ln], axis=0, keepdims=True)
            sg = _sig(rows[3][:, ln])
            outs[0][:, ln] = (acc * sg).astype(BF16)
            outs[0][:, CC + c * HP:CC + (c + 1) * HP] = (acc * rows[2][:, ln] * sg * (1.0 - sg)).astype(BF16)

    return _row_call("conv_glu_bwd", fn, n, [(dzc, "cur"), (dzc, "next"), (a, "cur"), (gt, "cur"), (a, "prev"), (gt, "prev")],
                     [cw], [((t, 2 * CC), BF16)], [((HALO, CC), F32)],
                     scratch=[pltpu.VMEM((r + HALO, CC), F32), pltpu.VMEM((r + HALO, CC), F32)])


def _in_proj_bwd(pieces, dh1, x, wp, g_mix, n):
    t = x.shape[0]
    offs = (P_A, P_Q, P_KV, P_KR, P_GL, P_END)

    def fn(i, rows, consts, outs, accs, scr):
        w, g = consts
        du = jnp.zeros((rows[0].shape[0], D), F32)
        for k in range(5):
            du = du + _mm_nt(rows[k][...], w[:, offs[k]:offs[k + 1]])
        _, xh, r = _rms(rows[6][...], g[...])
        dx, dg = _rms_bwd(du, xh, r, g[...])
        accs[0][...] += dg
        outs[0][...] = rows[5][...] + dx

    return _row_call("in_proj_bwd", fn, n, [(a, "cur") for a in list(pieces) + [dh1, x]], [wp, g_mix],
                     [((t, D), F32)], [((1, D), F32)])


def _mem_bwd(mem, dkx, dvx, g_mem, wxkv):
    m = mem.shape[0]

    def fn(i, rows, consts, outs, accs, scr):
        g, w = consts
        dkv = jnp.concatenate([rows[1][...], rows[2][...]], axis=1).astype(BF16)
        outs[0][...] = dkv
        _, xh, _ = _rms(rows[0][...], g[...])
        accs[0][...] += jnp.sum(_mm_nt(dkv, w[...]) * xh, axis=0, keepdims=True)

    return _row_call("mem_bwd", fn, 1, [(mem, "cur"), (dkx, "cur"), (dvx, "cur")], [g_mem, wxkv],
                     [((m, 2 * XH * XD), BF16)], [((1, D), F32)])


def _dw(name, xs, dy):
    t, k = xs.shape
    nn = dy.shape[1]
    tk, tn, tt = min(k, 1024), min(nn, 1024), min(t, DW_TILE)

    def body(x_ref, dy_ref, o_ref):
        @pl.when(pl.program_id(2) == 0)
        def _():
            o_ref[...] = jnp.zeros(o_ref.shape, F32)
        o_ref[...] += lax.dot_general(x_ref[...], dy_ref[...], (((0,), (0,)), ((), ())), preferred_element_type=F32)

    return pl.pallas_call(
        body, name=name, grid=(k // tk, nn // tn, t // tt),
        in_specs=[pl.BlockSpec((tt, tk), lambda a, b, c: (c, a)), pl.BlockSpec((tt, tn), lambda a, b, c: (c, b))],
        out_specs=pl.BlockSpec((tk, tn), lambda a, b, c: (a, b)),
        out_shape=jax.ShapeDtypeStruct((k, nn), F32),
        compiler_params=pltpu.CompilerParams(dimension_semantics=("arbitrary", "arbitrary", "arbitrary"), vmem_limit_bytes=VMEM_LIMIT),
    )(xs, dy)


def _coords():
    return lax.axis_index("x"), lax.axis_index("y"), lax.axis_index("c")


def _all_gather_packed(shard):
    rws = shard.shape[0]

    def body(x_ref, out_ref, send_sems, recv_sems, local_sem):
        x, y, c = _coords()
        me, sibling = (x, y, c), (x, y, 1 - c)
        chips = [(1 - x, y), (x, 1 - y), (1 - x, 1 - y)]

        def slot(px, py, pc):
            return out_ref.at[4 * px + 2 * py + pc]

        def copy(k, block, to, src=None):
            return pltpu.make_async_remote_copy(
                src_ref=slot(*block) if src is None else src, dst_ref=slot(*block),
                send_sem=send_sems.at[k], recv_sem=recv_sems.at[k], device_id=to, device_id_type=MESH)

        mine = pltpu.make_async_copy(x_ref, slot(*me), local_sem)
        mine.start()
        first = [copy(0, me, sibling, src=x_ref)] + [copy(1 + j, me, (*chip, c), src=x_ref) for j, chip in enumerate(chips)]
        for cp in first:
            cp.start()
        passed = [copy(4 + j, (*chip, c), sibling) for j, chip in enumerate(chips)]
        for j, chip in enumerate(chips):
            copy(1 + j, (*chip, c), me).wait_recv()
            passed[j].start()
        copy(0, sibling, me).wait_recv()
        for j, chip in enumerate(chips):
            copy(4 + j, (*chip, 1 - c), me).wait_recv()
        for cp in first + passed:
            cp.wait_send()
        mine.wait()

    return pl.pallas_call(
        body, name="all_gather_weights",
        out_shape=jax.ShapeDtypeStruct((N_DEV, rws, 128), shard.dtype),
        in_specs=[pl.BlockSpec(memory_space=pl.ANY)], out_specs=pl.BlockSpec(memory_space=pl.ANY),
        scratch_shapes=[pltpu.SemaphoreType.DMA((7,)), pltpu.SemaphoreType.DMA((7,)), pltpu.SemaphoreType.DMA],
    )(shard)


def _sibling_exchange(g_sib):
    def body(g_ref, out_ref, send_sem, recv_sem):
        x, y, c = _coords()
        cp = pltpu.make_async_remote_copy(src_ref=g_ref, dst_ref=out_ref, send_sem=send_sem, recv_sem=recv_sem,
                                          device_id=(x, y, 1 - c), device_id_type=MESH)
        cp.start()
        cp.wait()

    return pl.pallas_call(
        body, name="grad_sibling_exchange", out_shape=jax.ShapeDtypeStruct(g_sib.shape, g_sib.dtype),
        in_specs=[pl.BlockSpec(memory_space=pl.ANY)], out_specs=pl.BlockSpec(memory_space=pl.ANY),
        scratch_shapes=[pltpu.SemaphoreType.DMA, pltpu.SemaphoreType.DMA],
    )(g_sib)


def _chip_exchange(part):
    rws = part.shape[1]

    def body(p_ref, out_ref, send_sems, recv_sems):
        x, y, c = _coords()
        chips = [(1 - x, y), (x, 1 - y), (1 - x, 1 - y)]
        cps = [pltpu.make_async_remote_copy(src_ref=p_ref.at[2 * px + py], dst_ref=out_ref.at[j], send_sem=send_sems.at[j],
                                            recv_sem=recv_sems.at[j], device_id=(px, py, c), device_id_type=MESH)
               for j, (px, py) in enumerate(chips)]
        for cp in cps:
            cp.start()
        for cp in cps:
            cp.wait()

    return pl.pallas_call(
        body, name="grad_chip_exchange", out_shape=jax.ShapeDtypeStruct((3, rws, 128), part.dtype),
        in_specs=[pl.BlockSpec(memory_space=pl.ANY)], out_specs=pl.BlockSpec(memory_space=pl.ANY),
        scratch_shapes=[pltpu.SemaphoreType.DMA((3,)), pltpu.SemaphoreType.DMA((3,))],
    )(part)


def _adam(w, g, m, v):
    m = ADAM_B1 * m + (1.0 - ADAM_B1) * g
    v = ADAM_B2 * v + (1.0 - ADAM_B2) * (g * g)
    m_hat = m / (1.0 - ADAM_B1 ** ADAM_STEP)
    v_hat = v / (1.0 - ADAM_B2 ** ADAM_STEP)
    return -ADAM_LR * (m_hat / (jnp.sqrt(v_hat) + ADAM_EPS) + ADAM_WD * w), m, v


def _small_allreduce_adam(part, w, m, v):
    shape = part.shape

    def body(p_ref, w_ref, m_ref, v_ref, g_ref, d_ref, nm_ref, nv_ref, buf, send_sems, recv_sems):
        x, y, c = _coords()
        me = 4 * x + 2 * y + c
        buf[0] = p_ref[...]
        cps = []
        for j in range(1, N_DEV):
            jx, jy, jc = j >> 2, (j >> 1) & 1, j & 1
            peer = (1 - x if jx else x, 1 - y if jy else y, 1 - c if jc else c)
            cps.append(pltpu.make_async_remote_copy(src_ref=p_ref, dst_ref=buf.at[j], send_sem=send_sems.at[j - 1],
                                                    recv_sem=recv_sems.at[j - 1], device_id=peer, device_id_type=MESH))
        for cp in cps:
            cp.start()
        for cp in cps:
            cp.wait()
        g = buf[me]
        for d in range(1, N_DEV):
            g = g + buf[d ^ me]
        g_ref[...] = g
        d_ref[...], nm_ref[...], nv_ref[...] = _adam(w_ref[...], g, m_ref[...], v_ref[...])

    vm = pl.BlockSpec(memory_space=pltpu.VMEM)
    return pl.pallas_call(
        body, name="small_allreduce_adam", out_shape=[jax.ShapeDtypeStruct(shape, F32)] * 4,
        in_specs=[vm] * 4, out_specs=[vm] * 4,
        scratch_shapes=[pltpu.VMEM((N_DEV,) + shape, F32), pltpu.SemaphoreType.DMA((7,)), pltpu.SemaphoreType.DMA((7,))],
    )(part, w, m, v)


def _flat_call(name, fn, ins, n_out):
    rws = ins[0].shape[0]
    tile = min(PACK_TILE, rws)

    def body(*refs):
        res = fn(*[r[...] for r in refs[:len(ins)]])
        for r, val in zip(refs[len(ins):], res):
            r[...] = val

    spec = pl.BlockSpec((tile, 128), lambda i: (i, 0))
    return pl.pallas_call(
        body, name=name, grid=(rws // tile,), in_specs=[spec] * len(ins), out_specs=[spec] * n_out,
        out_shape=[jax.ShapeDtypeStruct((rws, 128), F32)] * n_out,
        compiler_params=pltpu.CompilerParams(dimension_semantics=("arbitrary",)),
    )(*ins)


BIG = ("w_in", "conv_w", "w_conv_out", "w_uq", "w_ukv", "w_mla_out", "w_out", "w_xq", "w_xkv", "w_xo", "w_mlp1", "w_mlp2")
ROW_SHARDED = ("w_out", "w_xq", "w_xkv", "w_mlp2")
SMALL = ("norm_mix_g", "conv_b", "conv_ln_g", "conv_ln_b", "q_norm_g", "kv_norm_g", "norm_xattn_g", "norm_mem_g", "norm_mlp_g", "final_norm_g")


def _seg(n):
    return -(-n // 128) * 128


def _pack_flat(parts, rows_multiple):
    lead = parts[0].shape[:-1]
    cols = []
    for p in parts:
        pad = _seg(p.shape[-1]) - p.shape[-1]
        cols.append(jnp.pad(p, [(0, 0)] * len(lead) + [(0, pad)]) if pad else p)
    flat = jnp.concatenate(cols, axis=-1)
    rws = flat.shape[-1] // 128
    pad_rows = -(-rws // rows_multiple) * rows_multiple - rws
    if pad_rows:
        flat = jnp.pad(flat, [(0, 0)] * len(lead) + [(0, pad_rows * 128)])
    return flat.reshape(lead + (-1, 128))


def _unpack_flat(packed, sizes):
    flat = packed.reshape(packed.shape[:-2] + (-1,))
    out, off = [], 0
    for s in sizes:
        out.append(flat[..., off:off + s])
        off += _seg(s)
    return out


def _to_full(name, stacked, shard_shape):
    k, nn = shard_shape
    a = stacked.reshape(N_DEV, k, nn)
    if name in ROW_SHARDED:
        return a.reshape(N_DEV * k, nn)
    return jnp.transpose(a, (1, 0, 2)).reshape(k, N_DEV * nn)


def _to_stacked(name, full, shard_shape):
    k, nn = shard_shape
    if name in ROW_SHARDED:
        return full.reshape(N_DEV, k * nn)
    return jnp.transpose(full.reshape(k, N_DEV, nn), (1, 0, 2)).reshape(N_DEV, k * nn)


def _head_pad(w, real, axis):
    shp = list(w.shape)
    shp[axis:axis + 1] = [NH, real]
    w = w.reshape(shp)
    pad = [(0, 0)] * w.ndim
    pad[axis + 1] = (0, HP - real)
    w = jnp.pad(w, pad)
    shp[axis:axis + 2] = [NH * HP]
    return w.reshape(shp)


def _head_unpad(w, lo, real, axis):
    shp = list(w.shape)
    shp[axis:axis + 1] = [NH, HP]
    w = lax.slice_in_dim(w.reshape(shp), lo, lo + real, axis=axis + 1)
    shp[axis:axis + 2] = [NH * real]
    return w.reshape(shp)


def kernel(x, mem, positions, norm_mix_g, w_in, conv_w, conv_b, conv_ln_g, conv_ln_b, w_conv_out, q_norm_g, w_uq, kv_norm_g, w_ukv, w_mla_out, w_out, norm_xattn_g, norm_mem_g, w_xq, w_xkv, w_xo, norm_mlp_g, w_mlp1, w_mlp2, final_norm_g, loss_target, m_norm_mix_g, m_w_in, m_conv_w, m_conv_b, m_conv_ln_g, m_conv_ln_b, m_w_conv_out, m_q_norm_g, m_w_uq, m_kv_norm_g, m_w_ukv, m_w_mla_out, m_w_out, m_norm_xattn_g, m_norm_mem_g, m_w_xq, m_w_xkv, m_w_xo, m_norm_mlp_g, m_w_mlp1, m_w_mlp2, m_final_norm_g, v_norm_mix_g, v_w_in, v_conv_w, v_conv_b, v_conv_ln_g, v_conv_ln_b, v_w_conv_out, v_q_norm_g, v_w_uq, v_kv_norm_g, v_w_ukv, v_w_mla_out, v_w_out, v_norm_xattn_g, v_norm_mem_g, v_w_xq, v_w_xkv, v_w_xo, v_norm_mlp_g, v_w_mlp1, v_w_mlp2, v_final_norm_g):
    args = dict(locals())
    t = x.shape[1]
    n = t // min(ROW_TILE, t)
    xs, mems, tgt = x[0], mem[0], loss_target[0]
    cidx = lax.axis_index("c")
    chip = 2 * lax.axis_index("x") + lax.axis_index("y")

    shard_shapes = {k: args[k].shape[1:] for k in BIG}
    sizes = [int(np.prod(shard_shapes[k])) for k in BIG]

    def pack_shards(prefix, dtype):
        return _pack_flat([args[prefix + k].reshape(-1).astype(dtype) for k in BIG], PACK_TILE)

    gathered = _all_gather_packed(pack_shards("", BF16))
    full = {k: _to_full(k, s, shard_shapes[k]) for k, s in zip(BIG, _unpack_flat(gathered, sizes))}

    wi = full["w_in"]
    kr_slot = jnp.pad(wi[:, P_KR:P_KR + ROPE], ((0, 0), (NOPE, HP - NOPE - ROPE)))
    wp = jnp.concatenate([wi[:, :P_KR], kr_slot, wi[:, P_KR + ROPE:]], axis=1)
    cw = jnp.pad(full["conv_w"].astype(F32), ((0, HALO - CW), (0, 0)))
    wuq = _head_pad(full["w_uq"], NOPE + ROPE, 1)
    ukv = full["w_ukv"].reshape(KL, NH, NOPE + VD)
    wk = _head_pad(ukv[:, :, :NOPE].reshape(KL, NH * NOPE), NOPE, 1)
    wv = _head_pad(ukv[:, :, NOPE:].reshape(KL, NH * VD), VD, 1)
    wmo = _head_pad(full["w_mla_out"], VD, 0)

    inv_freq = THETA ** (-jnp.arange(ROPE // 2, dtype=F32) / (ROPE // 2))
    ang = positions[0].astype(F32)[:, None] * inv_freq
    cs, sn, zr = jnp.cos(ang), jnp.sin(ang), jnp.zeros((t, ROPE // 2), F32)
    tail = jnp.zeros((t, HP - NOPE - ROPE), F32)
    tc = jnp.concatenate([jnp.ones((t, NOPE), F32), cs, cs, tail], axis=1)
    tsa = jnp.concatenate([jnp.zeros((t, NOPE), F32), -sn, zr, tail], axis=1)
    tsb = jnp.concatenate([jnp.zeros((t, NOPE), F32), zr, sn, tail], axis=1)

    u0, a, gt, cq, ckv, krp, gl = _in_proj(xs, norm_mix_g, wp, n)
    zc, conv_out = _conv_branch(a, gt, cw, conv_b, conv_ln_g, conv_ln_b, full["w_conv_out"], n)
    qh, kh, vh, cqn, ckvn = _mla_prep(cq, ckv, krp, tc, tsa, tsb, q_norm_g, wuq, kv_norm_g, wk, wv, n)
    o, lse = _flash_fwd(qh, kh, vh)
    memn, kx, vx = _mem_kv(mems, norm_mem_g, full["w_xkv"])
    mla, merged, h1, u1, qx, ox, h2, ob = _merge_xattn(o, gl, conv_out, xs, wmo, full["w_out"], norm_xattn_g, full["w_xq"],
                                                        kx, vx, full["w_xo"], n)
    gfin = final_norm_g.reshape(1, D)
    u2, rl2, da1, dh3b, dh2, loss_p, dg_fin, dg_mlp = _mlp_loss(h2, tgt, norm_mlp_g, full["w_mlp1"], full["w_mlp2"], gfin, n)

    (dqx, dh1, dgl, dco, dmla, dob, delta, dh2b, dh1b, dkx, dvx, dg_x) = _merge_xattn_bwd(
        dh2, h1, qx, gl, conv_out, mla, o, full["w_xo"], kx, vx, full["w_xq"], norm_xattn_g, full["w_out"], wmo, n)
    dq = _flash_dq(qh, kh, vh, dob, lse, delta)
    dk, dv = _flash_dkv(qh, kh, vh, dob, lse.reshape(NH, 1, t), delta.reshape(NH, 1, t))
    dqp, dkb, dvb, dcq, dckv, dkrp, dg_q, dg_kv = _mla_prep_bwd(dq, dk, dv, cq, ckv, tc, tsa, tsb, q_norm_g, wuq, kv_norm_g, wk, wv, n)
    zs, dzc, dg_lng, dg_lnb, dg_cb = _conv_out_bwd(dco, zc, full["w_conv_out"], conv_ln_g, conv_ln_b, n)
    dci, dcw = _conv_glu_bwd(dzc, a, gt, cw, n)
    pieces = (dci, dcq, dckv, dkrp, dgl)
    grad_x, dg_mix = _in_proj_bwd(pieces, dh1, xs, wp, norm_mix_g, n)
    dkv, dg_mem = _mem_bwd(mems, dkx, dvx, norm_mem_g, full["w_xkv"])

    dwp = [_dw("dw_in_%d" % k, u0, p) for k, p in enumerate(pieces)]
    gfull = {
        "w_in": jnp.concatenate([dwp[0], dwp[1], dwp[2], dwp[3][:, NOPE:NOPE + ROPE], dwp[4]], axis=1),
        "conv_w": dcw[:CW],
        "w_conv_out": _dw("dw_conv_out", zs, dco),
        "w_uq": _head_unpad(_dw("dw_uq", cqn, dqp), 0, NOPE + ROPE, 1),
        "w_mla_out": _head_unpad(_dw("dw_mla_out", ob, dmla), 0, VD, 0),
        "w_out": _dw("dw_out", merged, dh1b),
        "w_xq": _dw("dw_xq", u1, dqx),
        "w_xkv": _dw("dw_xkv", memn, dkv),
        "w_xo": _dw("dw_xo", ox, dh2b),
        "w_mlp1": _dw("dw_mlp1", u2, da1),
        "w_mlp2": _dw("dw_mlp2", rl2, dh3b),
    }
    gk = _head_unpad(_dw("dw_uk", ckvn, dkb), 0, NOPE, 1).reshape(KL, NH, NOPE)
    gv = _head_unpad(_dw("dw_uv", ckvn, dvb), 0, VD, 1).reshape(KL, NH, VD)
    gfull["w_ukv"] = jnp.concatenate([gk, gv], axis=2).reshape(KL, NH * (NOPE + VD))

    stacked = _pack_flat([_to_stacked(k, gfull[k], shard_shapes[k]) for k in BIG], PACK_TILE)
    rws = stacked.shape[1]
    by_core = stacked.reshape(4, 2, rws, 128)
    g_mine = lax.dynamic_index_in_dim(by_core, cidx, 1, keepdims=False)
    g_sib = lax.dynamic_index_in_dim(by_core, 1 - cidx, 1, keepdims=False)
    from_sib = _sibling_exchange(g_sib)
    part = _flat_call("grad_pair_sum", lambda p, q: (p + q,), [g_mine.reshape(4 * rws, 128), from_sib.reshape(4 * rws, 128)], 1)[0]
    part = part.reshape(4, rws, 128)
    others = _chip_exchange(part)
    own = lax.dynamic_index_in_dim(part, chip, 0, keepdims=False)

    def sum_adam(p0, p1, p2, p3, w, m, v):
        g = ((p0 + p1) + p2) + p3
        return (g,) + _adam(w, g, m, v)

    big_out = _flat_call("grad_sum_adam", sum_adam, [own, others[0], others[1], others[2], pack_shards("", F32),
                                                     pack_shards("m_", F32), pack_shards("v_", F32)], 4)

    small_g = {"norm_mix_g": dg_mix, "conv_b": dg_cb, "conv_ln_g": dg_lng, "conv_ln_b": dg_lnb, "q_norm_g": dg_q,
               "kv_norm_g": dg_kv, "norm_xattn_g": dg_x, "norm_mem_g": dg_mem, "norm_mlp_g": dg_mlp, "final_norm_g": dg_fin}
    small_sizes = [int(np.prod(args[k].shape)) for k in SMALL]

    def pack_small(vals):
        return _pack_flat([v.reshape(-1) for v in vals] + [jnp.zeros((128,), F32)], 8)

    sp = _pack_flat([small_g[k].reshape(-1) for k in SMALL] + [jnp.pad(loss_p.reshape(-1), (0, 127))], 8)
    small_out = _small_allreduce_adam(sp, pack_small([args[k] for k in SMALL]), pack_small([args["m_" + k] for k in SMALL]),
                                      pack_small([args["v_" + k] for k in SMALL]))

    big_un = [dict(zip(BIG, [s.reshape(args[k].shape) for k, s in zip(BIG, _unpack_flat(o_, sizes))])) for o_ in big_out]
    small_un = []
    for o_ in small_out:
        vals = _unpack_flat(o_, small_sizes + [128])
        small_un.append(dict(zip(SMALL, [s.reshape(args[k].shape) for k, s in zip(SMALL, vals)])))
    loss = _unpack_flat(small_out[0], small_sizes + [128])[-1][0]
    order = ("norm_mix_g", "w_in", "conv_w", "conv_b", "conv_ln_g", "conv_ln_b", "w_conv_out", "q_norm_g", "w_uq", "kv_norm_g",
             "w_ukv", "w_mla_out", "w_out", "norm_xattn_g", "norm_mem_g", "w_xq", "w_xkv", "w_xo", "norm_mlp_g", "w_mlp1",
             "w_mlp2", "final_norm_g")
    res = [loss, grad_x[None]]
    for kind in range(4):
        res += [big_un[kind][k] if k in BIG else small_un[kind][k] for k in order]
    return tuple(res)
```

```python
import functools

import jax
import jax.numpy as jnp
import numpy as np
from jax import lax
from jax.experimental import pallas as pl
from jax.experimental.pallas import tpu as pltpu

F32, BF16 = jnp.float32, jnp.bfloat16
MESH = pl.DeviceIdType.MESH

N_DEV = 8
D = 1024
CC = D // 2
CW = 31
HALO = 32
NH = 8
NOPE, ROPE, VD = D // 16, D // 32, D // 16
QL, KL = 3 * D // 8, D // 4
HP = 128
XH, XD = 4, D // 8
DFF = 4 * D
EPS = 1e-6
THETA = 10000.0
MLA_SCALE = float((NOPE + ROPE) ** -0.5)
X_SCALE = float(XD ** -0.5)
NEG = -1e30
P_A, P_G, P_Q, P_KV, P_KR, P_GL, P_END = 0, CC, 2 * CC, 2 * CC + QL, 2 * CC + QL + KL, 2 * CC + QL + KL + HP, 2 * CC + QL + KL + HP + 2 * D

ADAM_LR, ADAM_B1, ADAM_B2, ADAM_EPS, ADAM_WD, ADAM_STEP = 0.001, 0.9, 0.999, 1e-08, 0.01, 10

ROW_TILE = 256
ATT_TILE = 512
DW_TILE = 512
PACK_TILE = 1536
VMEM_LIMIT = 56 * 1024 * 1024


def _mm(a, w):
    return jnp.dot(a.astype(BF16), w, preferred_element_type=F32)


def _mm_nt(a, w):
    return lax.dot_general(a.astype(BF16), w, (((1,), (1,)), ((), ())), preferred_element_type=F32)


def _rms(x, g):
    r = lax.rsqrt(jnp.mean(x * x, axis=-1, keepdims=True) + EPS)
    xh = x * r
    return xh * g, xh, r


def _rms_bwd(dy, xh, r, g):
    dxh = dy * g
    dx = r * (dxh - xh * jnp.mean(dxh * xh, axis=-1, keepdims=True))
    return dx, jnp.sum(dy * xh, axis=0, keepdims=True)


def _sig(x):
    return 1.0 / (1.0 + jnp.exp(-x))


def _coords():
    return lax.axis_index("x"), lax.axis_index("y"), lax.axis_index("c")


def _exchange_ops(src_ref, gather, dst_ref, send_sems, recv_sems, local_sem):
    x, y, c = _coords()
    me = 4 * x + 2 * y + c
    ops = [pltpu.make_async_copy(src_ref if gather else src_ref.at[me], dst_ref.at[me], local_sem)]
    for j in range(1, N_DEV):
        px, py, pc = (1 - x if j & 4 else x), (1 - y if j & 2 else y), (1 - c if j & 1 else c)
        ops.append(pltpu.make_async_remote_copy(
            src_ref=src_ref if gather else src_ref.at[4 * px + 2 * py + pc], dst_ref=dst_ref.at[me],
            send_sem=send_sems.at[j - 1], recv_sem=recv_sems.at[j - 1], device_id=(px, py, pc), device_id_type=MESH))
    return ops


def _exchange_hook(first, last, gather, refs):
    @pl.when(first)
    def _():
        for op in _exchange_ops(refs[0], gather, *refs[1:]):
            op.start()

    @pl.when(last)
    def _():
        for op in _exchange_ops(refs[0], gather, *refs[1:]):
            op.wait()


def _exchange_shapes(exchange):
    if exchange is None:
        return [], [], [], []
    arr, gather = exchange
    shape = (N_DEV,) + arr.shape if gather else arr.shape
    any_spec = pl.BlockSpec(memory_space=pl.ANY)
    sems = [pltpu.SemaphoreType.DMA((N_DEV - 1,)), pltpu.SemaphoreType.DMA((N_DEV - 1,)), pltpu.SemaphoreType.DMA]
    return [any_spec], [any_spec], [jax.ShapeDtypeStruct(shape, arr.dtype)], sems


def _row_call(name, fn, n, rows, consts, outs, accs=(), scratch=(), exchange=None):
    def row_spec(shape, mode):
        r = shape[-2] // n
        if mode == "cur":
            f = lambda i: i
        elif mode == "prev":
            f = lambda i: jnp.maximum(i - 1, 0)
        else:
            f = lambda i: jnp.minimum(i + 1, n - 1)
        if len(shape) == 2:
            return pl.BlockSpec((r, shape[1]), lambda i: (f(i), 0))
        return pl.BlockSpec((shape[0], r, shape[2]), lambda i: (0, f(i), 0))

    def whole_spec(shape, single):
        nd = len(shape)
        if single:
            return pl.BlockSpec(shape, lambda i: (0,) * nd, pipeline_mode=pl.Buffered(1))
        return pl.BlockSpec(shape, lambda i: (0,) * nd)

    nr, nc, no, na, ns = len(rows), len(consts), len(outs), len(accs), len(scratch)
    x_in, x_out, x_shape, x_sems = _exchange_shapes(exchange)
    nx = len(x_in)

    def body(*refs):
        i = pl.program_id(0)
        row_refs, const_refs = refs[:nr], refs[nr:nr + nc]
        o0 = nr + nc + nx
        out_refs, acc_refs = refs[o0:o0 + no], refs[o0 + no:o0 + no + na]
        s0 = o0 + no + na + nx
        if nx:
            _exchange_hook(i == 0, i == n - 1, exchange[1], (refs[nr + nc], refs[s0 - 1]) + tuple(refs[s0 + ns:]))
        if na:
            @pl.when(i == 0)
            def _():
                for a in acc_refs:
                    a[...] = jnp.zeros(a.shape, a.dtype)
        fn(i, row_refs, const_refs, out_refs, acc_refs, refs[s0:s0 + ns])

    res = pl.pallas_call(
        body, name=name, grid=(n,),
        in_specs=[row_spec(a.shape, m) for a, m in rows] + [whole_spec(c.shape, True) for c in consts] + x_in,
        out_specs=[row_spec(s, "cur") for s, _ in outs] + [whole_spec(s, False) for s, _ in accs] + x_out,
        out_shape=[jax.ShapeDtypeStruct(s, d) for s, d in list(outs) + list(accs)] + x_shape,
        scratch_shapes=list(scratch) + x_sems,
        compiler_params=pltpu.CompilerParams(dimension_semantics=("arbitrary",), vmem_limit_bytes=VMEM_LIMIT),
    )(*[a for a, _ in rows], *consts, *([exchange[0]] if nx else []))
    return list(res)


def _in_proj(x, g_mix, wp, n):
    t = x.shape[0]

    def fn(i, rows, consts, outs, accs, scr):
        g, w = consts
        u, _, _ = _rms(rows[0][...], g[...])
        ub = u.astype(BF16)
        outs[0][...] = ub
        for k, (lo, hi) in enumerate(((P_A, P_G), (P_G, P_Q), (P_Q, P_KV), (P_KV, P_KR), (P_KR, P_GL), (P_GL, P_END))):
            outs[1 + k][...] = _mm(ub, w[:, lo:hi])

    outs = [((t, D), BF16), ((t, CC), F32), ((t, CC), F32), ((t, QL), F32), ((t, KL), F32), ((t, HP), F32), ((t, 2 * D), F32)]
    return _row_call("in_proj", fn, n, [(x, "cur")], [g_mix, wp], outs)


def _fill_glu_window(i, a, gt, ap, gtp, zz):
    r = a.shape[0]
    zp = ap[r - HALO:, :] * _sig(gtp[r - HALO:, :])
    zz[0:HALO, :] = jnp.where(i > 0, zp, 0.0)
    zz[HALO:, :] = a[...] * _sig(gt[...])


def _conv_branch(a, gt, cw, cb, lng, lnb, wco, n):
    t = a.shape[0]
    r = t // n

    def fn(i, rows, consts, outs, accs, scr):
        w, b, lg, lb, wo = consts
        zz = scr[0]
        _fill_glu_window(i, rows[0], rows[1], rows[2], rows[3], zz)
        for c in range(CC // HP):
            ln = slice(c * HP, (c + 1) * HP)
            acc = jnp.zeros((r, HP), F32)
            for j in range(CW):
                acc = acc + w[j:j + 1, ln] * zz[HALO - (CW - 1) + j:HALO - (CW - 1) + j + r, ln]
            outs[0][:, ln] = acc + b[:, ln]
        zc = outs[0][...]
        mu = jnp.mean(zc, axis=-1, keepdims=True)
        dlt = zc - mu
        rs = lax.rsqrt(jnp.mean(dlt * dlt, axis=-1, keepdims=True) + EPS)
        zn = dlt * rs * lg[...] + lb[...]
        outs[1][...] = _mm(zn * _sig(zn), wo[...])

    return _row_call("conv_branch", fn, n, [(a, "cur"), (gt, "cur"), (a, "prev"), (gt, "prev")],
                     [cw, cb, lng, lnb, wco], [((t, CC), F32), ((t, D), F32)],
                     scratch=[pltpu.VMEM((r + HALO, CC), F32)])


def _rope(v, c, sa, sb):
    return v * c + pltpu.roll(v, HP - ROPE // 2, 1) * sa + pltpu.roll(v, ROPE // 2, 1) * sb


def _rope_bwd(dv, c, sa, sb):
    return dv * c + pltpu.roll(dv * sa, ROPE // 2, 1) + pltpu.roll(dv * sb, HP - ROPE // 2, 1)


def _mla_prep(cq, ckv, krp, tc, tsa, tsb, gq, wuq, gkv, wk, wv, n):
    t = cq.shape[0]

    def fn(i, rows, consts, outs, accs, scr):
        g_q, w_q, g_kv, w_k, w_v = consts
        c, sa, sb = rows[3][...], rows[4][...], rows[5][...]
        cqn = _rms(rows[0][...], g_q[...])[0].astype(BF16)
        ckvn = _rms(rows[1][...], g_kv[...])[0].astype(BF16)
        outs[3][...] = cqn
        outs[4][...] = ckvn
        krr = _rope(rows[2][...], c, sa, sb)
        for h in range(NH):
            ln = slice(h * HP, (h + 1) * HP)
            outs[0][:, ln] = (_rope(_mm(cqn, w_q[:, ln]), c, sa, sb) * MLA_SCALE).astype(BF16)
            outs[1][:, ln] = (_mm(ckvn, w_k[:, ln]) + krr).astype(BF16)
        outs[2][...] = _mm(ckvn, w_v[...]).astype(BF16)

    outs = [((t, NH * HP), BF16)] * 3 + [((t, QL), BF16), ((t, KL), BF16)]
    return _row_call("mla_prep", fn, n, [(a, "cur") for a in (cq, ckv, krp, tc, tsa, tsb)], [gq, wuq, gkv, wk, wv], outs)


def _causal_mask(tq, transposed):
    row = lax.broadcasted_iota(jnp.int32, (tq, tq), 0)
    col = lax.broadcasted_iota(jnp.int32, (tq, tq), 1)
    return (row <= col) if transposed else (col <= row)


def _flash_hook(exchange, nq, refs):
    if exchange is not None:
        h, i = pl.program_id(0), pl.program_id(1)
        _exchange_hook((h == 0) & (i == 0), (h == NH - 1) & (i == nq - 1), exchange[1], refs)


def _flash_fwd(q, k, v, exchange=None):
    t = q.shape[0]
    tq = min(ATT_TILE, t)
    nq = t // tq
    x_in, x_out, x_shape, x_sems = _exchange_shapes(exchange)

    def body(q_ref, k_ref, v_ref, *rest):
        o_ref, lse_ref = rest[len(x_in):len(x_in) + 2]
        _flash_hook(exchange, nq, rest[:len(x_in)] + rest[len(x_in) + 2:])
        i = pl.program_id(1)
        qb = q_ref[...]

        def step(j, carry, masked):
            m, l, acc = carry
            at = pl.ds(pl.multiple_of(j * tq, tq), tq)
            s = _mm_nt(qb, k_ref[at, :])
            if masked:
                s = jnp.where(_causal_mask(tq, False), s, NEG)
            m_new = jnp.maximum(m, jnp.max(s, axis=-1, keepdims=True))
            alpha = jnp.exp(m - m_new)
            p = jnp.exp(s - m_new)
            return m_new, alpha * l + jnp.sum(p, axis=-1, keepdims=True), alpha * acc + _mm(p, v_ref[at, :])

        init = (jnp.full((tq, 1), NEG, F32), jnp.zeros((tq, 1), F32), jnp.zeros((tq, HP), F32))
        carry = lax.fori_loop(0, i, lambda j, c: step(j, c, False), init)
        m, l, acc = step(i, carry, True)
        o_ref[...] = acc / l
        lse_ref[0] = m + jnp.log(l)

    return pl.pallas_call(
        body, name="flash_fwd", grid=(NH, nq),
        in_specs=[pl.BlockSpec((tq, HP), lambda h, i: (i, h)), pl.BlockSpec((t, HP), lambda h, i: (0, h)),
                  pl.BlockSpec((t, HP), lambda h, i: (0, h))] + x_in,
        out_specs=[pl.BlockSpec((tq, HP), lambda h, i: (i, h)), pl.BlockSpec((1, tq, 1), lambda h, i: (h, i, 0))] + x_out,
        out_shape=[jax.ShapeDtypeStruct((t, NH * HP), F32), jax.ShapeDtypeStruct((NH, t, 1), F32)] + x_shape,
        scratch_shapes=x_sems,
        compiler_params=pltpu.CompilerParams(dimension_semantics=("arbitrary", "arbitrary"), vmem_limit_bytes=VMEM_LIMIT),
    )(q, k, v, *([exchange[0]] if x_in else []))


def _flash_dq(q, k, v, do, lse, delta, exchange=None):
    t = q.shape[0]
    tq = min(ATT_TILE, t)
    nq = t // tq
    x_in, x_out, x_shape, x_sems = _exchange_shapes(exchange)

    def body(q_ref, k_ref, v_ref, do_ref, lse_ref, dl_ref, *rest):
        dq_ref = rest[len(x_in)]
        _flash_hook(exchange, nq, rest[:len(x_in)] + rest[len(x_in) + 1:])
        i = pl.program_id(1)
        qb, dob, lse_i, dl_i = q_ref[...], do_ref[...], lse_ref[0], dl_ref[0]

        def step(j, acc, masked):
            at = pl.ds(pl.multiple_of(j * tq, tq), tq)
            kb = k_ref[at, :]
            s = _mm_nt(qb, kb)
            if masked:
                s = jnp.where(_causal_mask(tq, False), s, NEG)
            p = jnp.exp(s - lse_i)
            ds = p * (_mm_nt(dob, v_ref[at, :]) - dl_i)
            return acc + _mm(ds, kb)

        acc = lax.fori_loop(0, i, lambda j, c: step(j, c, False), jnp.zeros((tq, HP), F32))
        dq_ref[...] = step(i, acc, True)

    blk = pl.BlockSpec((tq, HP), lambda h, i: (i, h))
    whole = pl.BlockSpec((t, HP), lambda h, i: (0, h))
    col = pl.BlockSpec((1, tq, 1), lambda h, i: (h, i, 0))
    return pl.pallas_call(
        body, name="flash_dq", grid=(NH, nq), in_specs=[blk, whole, whole, blk, col, col] + x_in, out_specs=[blk] + x_out,
        out_shape=[jax.ShapeDtypeStruct((t, NH * HP), F32)] + x_shape, scratch_shapes=x_sems,
        compiler_params=pltpu.CompilerParams(dimension_semantics=("arbitrary", "arbitrary"), vmem_limit_bytes=VMEM_LIMIT),
    )(q, k, v, do, lse, delta, *([exchange[0]] if x_in else []))


def _flash_dkv(q, k, v, do, lse_row, delta_row, exchange=None):
    t = q.shape[0]
    tq = min(ATT_TILE, t)
    nq = t // tq
    x_in, x_out, x_shape, x_sems = _exchange_shapes(exchange)

    def body(q_ref, k_ref, v_ref, do_ref, lse_ref, dl_ref, *rest):
        dk_ref, dv_ref = rest[len(x_in):len(x_in) + 2]
        _flash_hook(exchange, nq, rest[:len(x_in)] + rest[len(x_in) + 2:])
        j = pl.program_id(1)
        kb, vb = k_ref[...], v_ref[...]

        def step(i, carry, masked):
            dk, dv = carry
            at = pl.ds(pl.multiple_of(i * tq, tq), tq)
            qb, dob = q_ref[at, :], do_ref[at, :]
            st = _mm_nt(kb, qb)
            if masked:
                st = jnp.where(_causal_mask(tq, True), st, NEG)
            pt = jnp.exp(st - lse_ref[0, :, at])
            dst = pt * (_mm_nt(vb, dob) - dl_ref[0, :, at])
            return dk + _mm(dst, qb), dv + _mm(pt, dob)

        carry = step(j, (jnp.zeros((tq, HP), F32), jnp.zeros((tq, HP), F32)), True)
        dk, dv = lax.fori_loop(j + 1, nq, lambda i, c: step(i, c, False), carry)
        dk_ref[...] = dk
        dv_ref[...] = dv

    blk = pl.BlockSpec((tq, HP), lambda h, j: (j, h))
    whole = pl.BlockSpec((t, HP), lambda h, j: (0, h))
    row = pl.BlockSpec((1, 1, t), lambda h, j: (h, 0, 0))
    return pl.pallas_call(
        body, name="flash_dkv", grid=(NH, nq), in_specs=[whole, blk, blk, whole, row, row] + x_in, out_specs=[blk, blk] + x_out,
        out_shape=[jax.ShapeDtypeStruct((t, NH * HP), F32)] * 2 + x_shape, scratch_shapes=x_sems,
        compiler_params=pltpu.CompilerParams(dimension_semantics=("arbitrary", "arbitrary"), vmem_limit_bytes=VMEM_LIMIT),
    )(q, k, v, do, lse_row, delta_row, *([exchange[0]] if x_in else []))


def _mem_kv(mem, g_mem, wxkv):
    m = mem.shape[0]

    def fn(i, rows, consts, outs, accs, scr):
        mn = _rms(rows[0][...], consts[0][...])[0].astype(BF16)
        outs[0][...] = mn
        outs[1][...] = _mm(mn, consts[1][:, 0:XH * XD]).astype(BF16)
        outs[2][...] = _mm(mn, consts[1][:, XH * XD:]).astype(BF16)

    return _row_call("mem_kv", fn, 1, [(mem, "cur")], [g_mem, wxkv], [((m, D), BF16), ((m, XH * XD), BF16), ((m, XH * XD), BF16)])


def _merge_xattn(o, gl, conv_out, x, wmo, wo, g_x, wxq, kx, vx, wxo, n):
    t = x.shape[0]

    def fn(i, rows, consts, outs, accs, scr):
        w_mo, w_o, g, w_xq, k_x, v_x, w_xo = consts
        ob = rows[0][...].astype(BF16)
        outs[7][...] = ob
        mla = _mm(ob, w_mo[...])
        outs[0][...] = mla
        merged = (_sig(rows[1][:, 0:D]) * rows[2][...] + _sig(rows[1][:, D:]) * mla).astype(BF16)
        outs[1][...] = merged
        h1 = rows[3][...] + _mm(merged, w_o[...])
        outs[2][...] = h1
        u1 = _rms(h1, g[...])[0].astype(BF16)
        outs[3][...] = u1
        qx = (_mm(u1, w_xq[...]) * X_SCALE).astype(BF16)
        outs[4][...] = qx
        for h in range(XH):
            ln = slice(h * XD, (h + 1) * XD)
            s = _mm_nt(qx[:, ln], k_x[:, ln])
            e = jnp.exp(s - jnp.max(s, axis=-1, keepdims=True))
            p = e / jnp.sum(e, axis=-1, keepdims=True)
            outs[5][:, ln] = _mm(p, v_x[:, ln]).astype(BF16)
        outs[6][...] = h1 + _mm(outs[5][...], w_xo[...])

    outs = [((t, D), F32), ((t, D), BF16), ((t, D), F32), ((t, D), BF16), ((t, XH * XD), BF16), ((t, XH * XD), BF16),
            ((t, D), F32), ((t, NH * HP), BF16)]
    return _row_call("merge_xattn", fn, n, [(a, "cur") for a in (o, gl, conv_out, x)], [wmo, wo, g_x, wxq, kx, vx, wxo], outs)


def _mlp_loss(h2, target, g_mlp, w1, w2, g_fin, n):
    t = h2.shape[0]
    nck = DFF // D

    def fn(i, rows, consts, outs, accs, scr):
        g_m, w_1, w_2, g_f = consts
        h = rows[0][...]
        u2, xh2, r2 = _rms(h, g_m[...])
        ub = u2.astype(BF16)
        outs[0][...] = ub
        h3 = h
        a1 = []
        for c in range(nck):
            ck = slice(c * D, (c + 1) * D)
            a = _mm(ub, w_1[:, ck])
            a1.append(a)
            rl = jnp.maximum(a, 0.0)
            rb = (rl * rl).astype(BF16)
            outs[1][:, ck] = rb
            h3 = h3 + _mm(rb, w_2[ck, :])
        y, xh3, r3 = _rms(h3, g_f[...])
        err = y - rows[1][...]
        accs[0][...] += jnp.sum(jnp.sum(err * err, axis=1, keepdims=True), axis=0, keepdims=True) * (0.5 / D)
        dh3, dgf = _rms_bwd(err * (1.0 / D), xh3, r3, g_f[...])
        accs[1][...] += dgf
        db = dh3.astype(BF16)
        outs[3][...] = db
        du2 = jnp.zeros_like(h)
        for c in range(nck):
            ck = slice(c * D, (c + 1) * D)
            da = (_mm_nt(db, w_2[ck, :]) * (2.0 * jnp.maximum(a1[c], 0.0))).astype(BF16)
            outs[2][:, ck] = da
            du2 = du2 + _mm_nt(da, w_1[:, ck])
        dx2, dgm = _rms_bwd(du2, xh2, r2, g_m[...])
        accs[2][...] += dgm
        outs[4][...] = dh3 + dx2

    outs = [((t, D), BF16), ((t, DFF), BF16), ((t, DFF), BF16), ((t, D), BF16), ((t, D), F32)]
    accs = [((1, 1), F32), ((1, D), F32), ((1, D), F32)]
    return _row_call("mlp_loss", fn, n, [(h2, "cur"), (target, "cur")], [g_mlp, w1, w2, g_fin], outs, accs)


def _merge_xattn_bwd(dh2, h1, qx, gl, conv_out, mla, o, wxo, kx, vx, wxq, g_x, wo, wmo, n):
    t = dh2.shape[0]
    m = kx.shape[0]

    def fn(i, rows, consts, outs, accs, scr):
        w_xo, k_x, v_x, w_xq, g, w_o, w_mo = consts
        d2 = rows[0][...]
        d2b = d2.astype(BF16)
        outs[7][...] = d2b
        dox = _mm_nt(d2b, w_xo[...]).astype(BF16)
        q = rows[2][...]
        dq = []
        for h in range(XH):
            ln = slice(h * XD, (h + 1) * XD)
            qh, kh, vh, doh = q[:, ln], k_x[:, ln], v_x[:, ln], dox[:, ln]
            s = _mm_nt(qh, kh)
            e = jnp.exp(s - jnp.max(s, axis=-1, keepdims=True))
            p = e / jnp.sum(e, axis=-1, keepdims=True)
            dp = _mm_nt(doh, vh)
            ds = p * (dp - jnp.sum(p * dp, axis=-1, keepdims=True))
            dq.append(_mm(ds, kh) * X_SCALE)
            st = _mm_nt(kh, qh)
            et = jnp.exp(st - jnp.max(st, axis=0, keepdims=True))
            pt = et / jnp.sum(et, axis=0, keepdims=True)
            dpt = _mm_nt(vh, doh)
            dst = pt * (dpt - jnp.sum(pt * dpt, axis=0, keepdims=True))
            accs[0][:, ln] += _mm(dst, qh)
            accs[1][:, ln] += _mm(pt, doh)
        dqx = jnp.concatenate(dq, axis=1).astype(BF16)
        outs[0][...] = dqx
        _, xh1, r1 = _rms(rows[1][...], g[...])
        dx1, dg = _rms_bwd(_mm_nt(dqx, w_xq[...]), xh1, r1, g[...])
        accs[2][...] += dg
        d1 = d2 + dx1
        outs[1][...] = d1
        d1b = d1.astype(BF16)
        outs[8][...] = d1b
        dm = _mm_nt(d1b, w_o[...])
        g0, g1 = _sig(rows[3][:, 0:D]), _sig(rows[3][:, D:])
        outs[2][:, 0:D] = (dm * rows[4][...] * g0 * (1.0 - g0)).astype(BF16)
        outs[2][:, D:] = (dm * rows[5][...] * g1 * (1.0 - g1)).astype(BF16)
        outs[3][...] = (dm * g0).astype(BF16)
        dmla = (dm * g1).astype(BF16)
        outs[4][...] = dmla
        do = _mm_nt(dmla, w_mo[...])
        outs[5][...] = do.astype(BF16)
        prod = do * rows[6][...]
        for h in range(NH):
            outs[6][h] = jnp.sum(prod[:, h * HP:(h + 1) * HP], axis=-1, keepdims=True)

    outs = [((t, XH * XD), BF16), ((t, D), F32), ((t, 2 * D), BF16), ((t, D), BF16), ((t, D), BF16), ((t, NH * HP), BF16),
            ((NH, t, 1), F32), ((t, D), BF16), ((t, D), BF16)]
    accs = [((m, XH * XD), F32), ((m, XH * XD), F32), ((1, D), F32)]
    return _row_call("merge_xattn_bwd", fn, n, [(a, "cur") for a in (dh2, h1, qx, gl, conv_out, mla, o)],
                     [wxo, kx, vx, wxq, g_x, wo, wmo], outs, accs)


def _mla_prep_bwd(dq, dk, dv, cq, ckv, tc, tsa, tsb, gq, wuq, gkv, wk, wv, n):
    t = dq.shape[0]

    def fn(i, rows, consts, outs, accs, scr):
        g_q, w_q, g_kv, w_k, w_v = consts
        c, sa, sb = rows[5][...], rows[6][...], rows[7][...]
        dkr = jnp.zeros((rows[0].shape[0], HP), F32)
        for h in range(NH):
            ln = slice(h * HP, (h + 1) * HP)
            outs[0][:, ln] = _rope_bwd(rows[0][:, ln] * MLA_SCALE, c, sa, sb).astype(BF16)
            dkr = dkr + rows[1][:, ln]
        lane = lax.broadcasted_iota(jnp.int32, dkr.shape, 1)
        outs[5][...] = jnp.where((lane >= NOPE) & (lane < NOPE + ROPE), _rope_bwd(dkr, c, sa, sb), 0.0).astype(BF16)
        dkb, dvb = rows[1][...].astype(BF16), rows[2][...].astype(BF16)
        outs[1][...] = dkb
        outs[2][...] = dvb
        _, xq, rq = _rms(rows[3][...], g_q[...])
        dcq, dgq = _rms_bwd(_mm_nt(outs[0][...], w_q[...]), xq, rq, g_q[...])
        outs[3][...] = dcq.astype(BF16)
        accs[0][...] += dgq
        _, xk, rk = _rms(rows[4][...], g_kv[...])
        dckv, dgk = _rms_bwd(_mm_nt(dkb, w_k[...]) + _mm_nt(dvb, w_v[...]), xk, rk, g_kv[...])
        outs[4][...] = dckv.astype(BF16)
        accs[1][...] += dgk

    outs = [((t, NH * HP), BF16)] * 3 + [((t, QL), BF16), ((t, KL), BF16), ((t, HP), BF16)]
    return _row_call("mla_prep_bwd", fn, n, [(a, "cur") for a in (dq, dk, dv, cq, ckv, tc, tsa, tsb)],
                     [gq, wuq, gkv, wk, wv], outs, [((1, QL), F32), ((1, KL), F32)])


def _conv_out_bwd(dco, zc, wco, lng, lnb, n):
    t = zc.shape[0]

    def fn(i, rows, consts, outs, accs, scr):
        wo, lg, lb = consts
        z = rows[1][...]
        mu = jnp.mean(z, axis=-1, keepdims=True)
        dlt = z - mu
        rs = lax.rsqrt(jnp.mean(dlt * dlt, axis=-1, keepdims=True) + EPS)
        xh = dlt * rs
        zn = xh * lg[...] + lb[...]
        sg = _sig(zn)
        outs[0][...] = (zn * sg).astype(BF16)
        dzn = _mm_nt(rows[0][...], wo[...]) * (sg * (1.0 + zn * (1.0 - sg)))
        accs[0][...] += jnp.sum(dzn * xh, axis=0, keepdims=True)
        accs[1][...] += jnp.sum(dzn, axis=0, keepdims=True)
        dxh = dzn * lg[...]
        dzc = rs * (dxh - jnp.mean(dxh, axis=-1, keepdims=True) - xh * jnp.mean(dxh * xh, axis=-1, keepdims=True))
        outs[1][...] = dzc
        accs[2][...] += jnp.sum(dzc, axis=0, keepdims=True)

    return _row_call("conv_out_bwd", fn, n, [(dco, "cur"), (zc, "cur")], [wco, lng, lnb],
                     [((t, CC), BF16), ((t, CC), F32)], [((1, CC), F32)] * 3)


def _conv_glu_bwd(dzc, a, gt, cw, n):
    t = a.shape[0]
    r = t // n

    def fn(i, rows, consts, outs, accs, scr):
        w = consts[0]
        zz, dd = scr
        _fill_glu_window(i, rows[2], rows[3], rows[4], rows[5], zz)
        dd[0:r, :] = rows[0][...]
        dd[r:, :] = jnp.where(i < n - 1, rows[1][0:HALO, :], 0.0)
        for c in range(CC // HP):
            ln = slice(c * HP, (c + 1) * HP)
            dcur = dd[0:r, ln]
            acc = jnp.zeros((r, HP), F32)
            for j in range(CW):
                acc = acc + w[j:j + 1, ln] * dd[CW - 1 - j:CW - 1 - j + r, ln]
                accs[0][j:j + 1, ln] += jnp.sum(dcur * zz[HALO - (CW - 1) + j:HALO - (CW - 1) + j + r, ln], axis=0, keepdims=True)
            sg = _sig(rows[3][:, ln])
            outs[0][:, ln] = (acc * sg).astype(BF16)
            outs[0][:, CC + c * HP:CC + (c + 1) * HP] = (acc * rows[2][:, ln] * sg * (1.0 - sg)).astype(BF16)

    return _row_call("conv_glu_bwd", fn, n, [(dzc, "cur"), (dzc, "next"), (a, "cur"), (gt, "cur"), (a, "prev"), (gt, "prev")],
                     [cw], [((t, 2 * CC), BF16)], [((HALO, CC), F32)],
                     scratch=[pltpu.VMEM((r + HALO, CC), F32), pltpu.VMEM((r + HALO, CC), F32)])


def _in_proj_bwd(pieces, dh1, x, wp, g_mix, n, exchange):
    t = x.shape[0]
    offs = (P_A, P_Q, P_KV, P_KR, P_GL, P_END)

    def fn(i, rows, consts, outs, accs, scr):
        w, g = consts
        du = jnp.zeros((rows[0].shape[0], D), F32)
        for k in range(5):
            du = du + _mm_nt(rows[k][...], w[:, offs[k]:offs[k + 1]])
        _, xh, r = _rms(rows[6][...], g[...])
        dx, dg = _rms_bwd(du, xh, r, g[...])
        accs[0][...] += dg
        outs[0][...] = rows[5][...] + dx

    return _row_call("in_proj_bwd", fn, n, [(a, "cur") for a in list(pieces) + [dh1, x]], [wp, g_mix],
                     [((t, D), F32)], [((1, D), F32)], exchange=exchange)


def _mem_bwd(mem, dkx, dvx, g_mem, wxkv):
    m = mem.shape[0]

    def fn(i, rows, consts, outs, accs, scr):
        g, w = consts
        dkv = jnp.concatenate([rows[1][...], rows[2][...]], axis=1).astype(BF16)
        outs[0][...] = dkv
        _, xh, _ = _rms(rows[0][...], g[...])
        accs[0][...] += jnp.sum(_mm_nt(dkv, w[...]) * xh, axis=0, keepdims=True)

    return _row_call("mem_bwd", fn, 1, [(mem, "cur"), (dkx, "cur"), (dvx, "cur")], [g_mem, wxkv],
                     [((m, 2 * XH * XD), BF16)], [((1, D), F32)])


def _dw(name, xs, dy):
    t, k = xs.shape
    nn = dy.shape[1]
    tk, tn, tt = min(k, 1024), min(nn, 1024), min(t, DW_TILE)

    def body(x_ref, dy_ref, o_ref):
        @pl.when(pl.program_id(2) == 0)
        def _():
            o_ref[...] = jnp.zeros(o_ref.shape, F32)
        o_ref[...] += lax.dot_general(x_ref[...], dy_ref[...], (((0,), (0,)), ((), ())), preferred_element_type=F32)

    return pl.pallas_call(
        body, name=name, grid=(k // tk, nn // tn, t // tt),
        in_specs=[pl.BlockSpec((tt, tk), lambda a, b, c: (c, a)), pl.BlockSpec((tt, tn), lambda a, b, c: (c, b))],
        out_specs=pl.BlockSpec((tk, tn), lambda a, b, c: (a, b)),
        out_shape=jax.ShapeDtypeStruct((k, nn), F32),
        compiler_params=pltpu.CompilerParams(dimension_semantics=("arbitrary", "arbitrary", "arbitrary"), vmem_limit_bytes=VMEM_LIMIT),
    )(xs, dy)


def _all_gather_packed(shard):
    rws = shard.shape[0]

    def body(x_ref, out_ref, send_sems, recv_sems, local_sem):
        x, y, c = _coords()
        me, sibling = (x, y, c), (x, y, 1 - c)
        chips = [(1 - x, y), (x, 1 - y), (1 - x, 1 - y)]

        def slot(px, py, pc):
            return out_ref.at[4 * px + 2 * py + pc]

        def copy(k, block, to, src=None):
            return pltpu.make_async_remote_copy(
                src_ref=slot(*block) if src is None else src, dst_ref=slot(*block),
                send_sem=send_sems.at[k], recv_sem=recv_sems.at[k], device_id=to, device_id_type=MESH)

        mine = pltpu.make_async_copy(x_ref, slot(*me), local_sem)
        mine.start()
        first = [copy(0, me, sibling, src=x_ref)] + [copy(1 + j, me, (*chip, c), src=x_ref) for j, chip in enumerate(chips)]
        for cp in first:
            cp.start()
        passed = [copy(4 + j, (*chip, c), sibling) for j, chip in enumerate(chips)]
        for j, chip in enumerate(chips):
            copy(1 + j, (*chip, c), me).wait_recv()
            passed[j].start()
        copy(0, sibling, me).wait_recv()
        for j, chip in enumerate(chips):
            copy(4 + j, (*chip, 1 - c), me).wait_recv()
        for cp in first + passed:
            cp.wait_send()
        mine.wait()

    return pl.pallas_call(
        body, name="all_gather_weights",
        out_shape=jax.ShapeDtypeStruct((N_DEV, rws, 128), shard.dtype),
        in_specs=[pl.BlockSpec(memory_space=pl.ANY)], out_specs=pl.BlockSpec(memory_space=pl.ANY),
        scratch_shapes=[pltpu.SemaphoreType.DMA((7,)), pltpu.SemaphoreType.DMA((7,)), pltpu.SemaphoreType.DMA],
    )(shard)


def _adam(w, g, m, v):
    m = ADAM_B1 * m + (1.0 - ADAM_B1) * g
    v = ADAM_B2 * v + (1.0 - ADAM_B2) * (g * g)
    m_hat = m / (1.0 - ADAM_B1 ** ADAM_STEP)
    v_hat = v / (1.0 - ADAM_B2 ** ADAM_STEP)
    return -ADAM_LR * (m_hat / (jnp.sqrt(v_hat) + ADAM_EPS) + ADAM_WD * w), m, v


def _small_allreduce_adam(part, w, m, v):
    shape = part.shape

    def body(p_ref, w_ref, m_ref, v_ref, g_ref, d_ref, nm_ref, nv_ref, buf, send_sems, recv_sems):
        x, y, c = _coords()
        me = 4 * x + 2 * y + c
        buf[0] = p_ref[...]
        cps = []
        for j in range(1, N_DEV):
            jx, jy, jc = j >> 2, (j >> 1) & 1, j & 1
            peer = (1 - x if jx else x, 1 - y if jy else y, 1 - c if jc else c)
            cps.append(pltpu.make_async_remote_copy(src_ref=p_ref, dst_ref=buf.at[j], send_sem=send_sems.at[j - 1],
                                                    recv_sem=recv_sems.at[j - 1], device_id=peer, device_id_type=MESH))
        for cp in cps:
            cp.start()
        for cp in cps:
            cp.wait()
        g = buf[me]
        for d in range(1, N_DEV):
            g = g + buf[d ^ me]
        g_ref[...] = g
        d_ref[...], nm_ref[...], nv_ref[...] = _adam(w_ref[...], g, m_ref[...], v_ref[...])

    vm = pl.BlockSpec(memory_space=pltpu.VMEM)
    return pl.pallas_call(
        body, name="small_allreduce_adam", out_shape=[jax.ShapeDtypeStruct(shape, F32)] * 4,
        in_specs=[vm] * 4, out_specs=[vm] * 4,
        scratch_shapes=[pltpu.VMEM((N_DEV,) + shape, F32), pltpu.SemaphoreType.DMA((7,)), pltpu.SemaphoreType.DMA((7,))],
    )(part, w, m, v)


def _sum_adam(name, parts, w, m, v):
    rws = w.shape[0]
    tile = max(d for d in range(16, PACK_TILE + 1, 16) if rws % d == 0)

    def body(p_ref, w_ref, m_ref, v_ref, g_ref, d_ref, nm_ref, nv_ref):
        g = p_ref[0]
        for d in range(1, N_DEV):
            g = g + p_ref[d]
        g_ref[...] = g
        d_ref[...], nm_ref[...], nv_ref[...] = _adam(w_ref[...], g, m_ref[...], v_ref[...])

    spec = pl.BlockSpec((tile, 128), lambda i: (i, 0))
    return pl.pallas_call(
        body, name=name, grid=(rws // tile,), in_specs=[pl.BlockSpec((N_DEV, tile, 128), lambda i: (0, i, 0))] + [spec] * 3,
        out_specs=[spec] * 4, out_shape=[jax.ShapeDtypeStruct((rws, 128), F32)] * 4,
        compiler_params=pltpu.CompilerParams(dimension_semantics=("arbitrary",), vmem_limit_bytes=VMEM_LIMIT),
    )(parts, w, m, v)


ROW_SHARDED = ("w_out", "w_xq", "w_xkv", "w_mlp2")
SMALL = ("norm_mix_g", "conv_b", "conv_ln_g", "conv_ln_b", "q_norm_g", "kv_norm_g", "norm_xattn_g", "norm_mem_g", "norm_mlp_g", "final_norm_g")
GATHER_FIRST = ("w_in", "conv_w", "w_conv_out", "w_uq", "w_ukv")
GATHER_LATE = ("w_mla_out", "w_out", "w_xq", "w_xkv", "w_xo", "w_mlp1", "w_mlp2")
REDUCE_MLP = ("w_mlp1", "w_mlp2")
REDUCE_MID = ("conv_w", "w_conv_out", "w_mla_out", "w_out", "w_xq", "w_xkv", "w_xo")
REDUCE_LAST = ("w_in", "w_uq", "w_ukv")
ROW_ALIGN = 16


def _padded(k, nn):
    return -(-k // ROW_ALIGN) * ROW_ALIGN, -(-nn // 128) * 128


def _pack_rows(a):
    k, nn = a.shape[-2:]
    kp, np_ = _padded(k, nn)
    if (kp, np_) != (k, nn):
        a = jnp.pad(a, [(0, 0)] * (a.ndim - 2) + [(0, kp - k), (0, np_ - nn)])
    return jnp.concatenate([a[..., b * 128:(b + 1) * 128] for b in range(np_ // 128)], axis=-2)


def _unpack_rows(p, k, nn):
    kp, np_ = _padded(k, nn)
    a = jnp.concatenate([p[..., b * kp:(b + 1) * kp, :] for b in range(np_ // 128)], axis=-1)
    return a[..., :k, :nn]


def _pack_group(arrays):
    return jnp.concatenate([_pack_rows(a) for a in arrays], axis=-2)


def _unpack_group(p, shapes):
    out, off = [], 0
    for k, nn in shapes:
        kp, np_ = _padded(k, nn)
        rws = kp * np_ // 128
        out.append(_unpack_rows(p[..., off:off + rws, :], k, nn))
        off += rws
    return out


def _to_full(name, stacked):
    _, k, nn = stacked.shape
    if name in ROW_SHARDED:
        return stacked.reshape(N_DEV * k, nn)
    return jnp.transpose(stacked, (1, 0, 2)).reshape(k, N_DEV * nn)


def _to_stacked(name, full, shard_shape):
    k, nn = shard_shape
    if name in ROW_SHARDED:
        return full.reshape(N_DEV, k, nn)
    return jnp.transpose(full.reshape(k, N_DEV, nn), (1, 0, 2))


def _pack_small(vals):
    flat = jnp.concatenate([v.reshape(-1) for v in vals])
    rws = flat.shape[0] // 128
    return jnp.pad(flat, (0, (-(-rws // 8) * 8 - rws) * 128)).reshape(-1, 128)


def _unpack_small(p, sizes):
    flat = p.reshape(-1)
    out, off = [], 0
    for s in sizes:
        out.append(flat[off:off + s])
        off += s
    return out


def _head_pad(w, real, axis):
    shp = list(w.shape)
    shp[axis:axis + 1] = [NH, real]
    w = w.reshape(shp)
    pad = [(0, 0)] * w.ndim
    pad[axis + 1] = (0, HP - real)
    w = jnp.pad(w, pad)
    shp[axis:axis + 2] = [NH * HP]
    return w.reshape(shp)


def _head_unpad(w, lo, real, axis):
    shp = list(w.shape)
    shp[axis:axis + 1] = [NH, HP]
    w = lax.slice_in_dim(w.reshape(shp), lo, lo + real, axis=axis + 1)
    shp[axis:axis + 2] = [NH * real]
    return w.reshape(shp)


def kernel(x, mem, positions, norm_mix_g, w_in, conv_w, conv_b, conv_ln_g, conv_ln_b, w_conv_out, q_norm_g, w_uq, kv_norm_g, w_ukv, w_mla_out, w_out, norm_xattn_g, norm_mem_g, w_xq, w_xkv, w_xo, norm_mlp_g, w_mlp1, w_mlp2, final_norm_g, loss_target, m_norm_mix_g, m_w_in, m_conv_w, m_conv_b, m_conv_ln_g, m_conv_ln_b, m_w_conv_out, m_q_norm_g, m_w_uq, m_kv_norm_g, m_w_ukv, m_w_mla_out, m_w_out, m_norm_xattn_g, m_norm_mem_g, m_w_xq, m_w_xkv, m_w_xo, m_norm_mlp_g, m_w_mlp1, m_w_mlp2, m_final_norm_g, v_norm_mix_g, v_w_in, v_conv_w, v_conv_b, v_conv_ln_g, v_conv_ln_b, v_w_conv_out, v_q_norm_g, v_w_uq, v_kv_norm_g, v_w_ukv, v_w_mla_out, v_w_out, v_norm_xattn_g, v_norm_mem_g, v_w_xq, v_w_xkv, v_w_xo, v_norm_mlp_g, v_w_mlp1, v_w_mlp2, v_final_norm_g):
    args = dict(locals())
    t = x.shape[1]
    n = t // min(ROW_TILE, t)
    xs, mems, tgt = x[0], mem[0], loss_target[0]

    def pack_shards(prefix, group, dtype):
        return _pack_group([args[prefix + k][0].astype(dtype) for k in group])

    def shapes(group):
        return [args[k].shape[1:] for k in group]

    def unpack_full(gathered, group):
        return {k: _to_full(k, s) for k, s in zip(group, _unpack_group(gathered, shapes(group)))}

    full = unpack_full(_all_gather_packed(pack_shards("", GATHER_FIRST, BF16)), GATHER_FIRST)

    wi = full["w_in"]
    kr_slot = jnp.pad(wi[:, P_KR:P_KR + ROPE], ((0, 0), (NOPE, HP - NOPE - ROPE)))
    wp = jnp.concatenate([wi[:, :P_KR], kr_slot, wi[:, P_KR + ROPE:]], axis=1)
    cw = jnp.pad(full["conv_w"].astype(F32), ((0, HALO - CW), (0, 0)))
    wuq = _head_pad(full["w_uq"], NOPE + ROPE, 1)
    ukv = full["w_ukv"].reshape(KL, NH, NOPE + VD)
    wk = _head_pad(ukv[:, :, :NOPE].reshape(KL, NH * NOPE), NOPE, 1)
    wv = _head_pad(ukv[:, :, NOPE:].reshape(KL, NH * VD), VD, 1)

    inv_freq = THETA ** (-jnp.arange(ROPE // 2, dtype=F32) / (ROPE // 2))
    ang = positions[0].astype(F32)[:, None] * inv_freq
    cs, sn, zr = jnp.cos(ang), jnp.sin(ang), jnp.zeros((t, ROPE // 2), F32)
    tail = jnp.zeros((t, HP - NOPE - ROPE), F32)
    tc = jnp.concatenate([jnp.ones((t, NOPE), F32), cs, cs, tail], axis=1)
    tsa = jnp.concatenate([jnp.zeros((t, NOPE), F32), -sn, zr, tail], axis=1)
    tsb = jnp.concatenate([jnp.zeros((t, NOPE), F32), zr, sn, tail], axis=1)

    u0, a, gt, cq, ckv, krp, gl = _in_proj(xs, norm_mix_g, wp, n)
    zc, conv_out = _conv_branch(a, gt, cw, conv_b, conv_ln_g, conv_ln_b, full["w_conv_out"], n)
    qh, kh, vh, cqn, ckvn = _mla_prep(cq, ckv, krp, tc, tsa, tsb, q_norm_g, wuq, kv_norm_g, wk, wv, n)
    o, lse, gathered = _flash_fwd(qh, kh, vh, exchange=(pack_shards("", GATHER_LATE, BF16), True))
    full.update(unpack_full(gathered, GATHER_LATE))
    wmo = _head_pad(full["w_mla_out"], VD, 0)
    memn, kx, vx = _mem_kv(mems, norm_mem_g, full["w_xkv"])
    mla, merged, h1, u1, qx, ox, h2, ob = _merge_xattn(o, gl, conv_out, xs, wmo, full["w_out"], norm_xattn_g, full["w_xq"],
                                                        kx, vx, full["w_xo"], n)
    gfin = final_norm_g.reshape(1, D)
    u2, rl2, da1, dh3b, dh2, loss_p, dg_fin, dg_mlp = _mlp_loss(h2, tgt, norm_mlp_g, full["w_mlp1"], full["w_mlp2"], gfin, n)

    (dqx, dh1, dgl, dco, dmla, dob, delta, dh2b, dh1b, dkx, dvx, dg_x) = _merge_xattn_bwd(
        dh2, h1, qx, gl, conv_out, mla, o, full["w_xo"], kx, vx, full["w_xq"], norm_xattn_g, full["w_out"], wmo, n)
    gfull = {"w_mlp1": _dw("dw_mlp1", u2, da1), "w_mlp2": _dw("dw_mlp2", rl2, dh3b)}

    def stacked(group):
        return _pack_group([_to_stacked(k, gfull[k], args[k].shape[1:]) for k in group])

    dq, parts_mlp = _flash_dq(qh, kh, vh, dob, lse, delta, exchange=(stacked(REDUCE_MLP), False))
    zs, dzc, dg_lng, dg_lnb, dg_cb = _conv_out_bwd(dco, zc, full["w_conv_out"], conv_ln_g, conv_ln_b, n)
    dci, dcw = _conv_glu_bwd(dzc, a, gt, cw, n)
    dkv, dg_mem = _mem_bwd(mems, dkx, dvx, norm_mem_g, full["w_xkv"])
    gfull.update({
        "conv_w": dcw[:CW],
        "w_conv_out": _dw("dw_conv_out", zs, dco),
        "w_mla_out": _head_unpad(_dw("dw_mla_out", ob, dmla), 0, VD, 0),
        "w_out": _dw("dw_out", merged, dh1b),
        "w_xq": _dw("dw_xq", u1, dqx),
        "w_xkv": _dw("dw_xkv", memn, dkv),
        "w_xo": _dw("dw_xo", ox, dh2b),
    })
    dk, dv, parts_mid = _flash_dkv(qh, kh, vh, dob, lse.reshape(NH, 1, t), delta.reshape(NH, 1, t),
                                   exchange=(stacked(REDUCE_MID), False))
    dqp, dkb, dvb, dcq, dckv, dkrp, dg_q, dg_kv = _mla_prep_bwd(dq, dk, dv, cq, ckv, tc, tsa, tsb, q_norm_g, wuq, kv_norm_g, wk, wv, n)
    pieces = (dci, dcq, dckv, dkrp, dgl)
    dwp = [_dw("dw_in_%d" % k, u0, p) for k, p in enumerate(pieces)]
    gfull["w_in"] = jnp.concatenate([dwp[0], dwp[1], dwp[2], dwp[3][:, NOPE:NOPE + ROPE], dwp[4]], axis=1)
    gfull["w_uq"] = _head_unpad(_dw("dw_uq", cqn, dqp), 0, NOPE + ROPE, 1)
    gk = _head_unpad(_dw("dw_uk", ckvn, dkb), 0, NOPE, 1).reshape(KL, NH, NOPE)
    gv = _head_unpad(_dw("dw_uv", ckvn, dvb), 0, VD, 1).reshape(KL, NH, VD)
    gfull["w_ukv"] = jnp.concatenate([gk, gv], axis=2).reshape(KL, NH * (NOPE + VD))
    grad_x, dg_mix, parts_last = _in_proj_bwd(pieces, dh1, xs, wp, norm_mix_g, n, (stacked(REDUCE_LAST), False))

    big = [{}, {}, {}, {}]
    for name, group, parts in (("adam_mlp", REDUCE_MLP, parts_mlp), ("adam_mid", REDUCE_MID, parts_mid),
                               ("adam_last", REDUCE_LAST, parts_last)):
        res = _sum_adam(name, parts, pack_shards("", group, F32), pack_shards("m_", group, F32), pack_shards("v_", group, F32))
        for kind in range(4):
            for k, val in zip(group, _unpack_group(res[kind], shapes(group))):
                big[kind][k] = val[None]

    small_g = {"norm_mix_g": dg_mix, "conv_b": dg_cb, "conv_ln_g": dg_lng, "conv_ln_b": dg_lnb, "q_norm_g": dg_q,
               "kv_norm_g": dg_kv, "norm_xattn_g": dg_x, "norm_mem_g": dg_mem, "norm_mlp_g": dg_mlp, "final_norm_g": dg_fin}
    small_sizes = [int(np.prod(args[k].shape)) for k in SMALL] + [128]
    zero_slot = jnp.zeros((128,), F32)
    small_out = _small_allreduce_adam(
        _pack_small([small_g[k] for k in SMALL] + [jnp.pad(loss_p.reshape(-1), (0, 127))]),
        _pack_small([args[k] for k in SMALL] + [zero_slot]), _pack_small([args["m_" + k] for k in SMALL] + [zero_slot]),
        _pack_small([args["v_" + k] for k in SMALL] + [zero_slot]))

    small = [dict(zip(SMALL, [s.reshape(args[k].shape) for k, s in zip(SMALL, _unpack_small(o_, small_sizes))])) for o_ in small_out]
    loss = _unpack_small(small_out[0], small_sizes)[-1][0]
    order = ("norm_mix_g", "w_in", "conv_w", "conv_b", "conv_ln_g", "conv_ln_b", "w_conv_out", "q_norm_g", "w_uq", "kv_norm_g",
             "w_ukv", "w_mla_out", "w_out", "norm_xattn_g", "norm_mem_g", "w_xq", "w_xkv", "w_xo", "norm_mlp_g", "w_mlp1",
             "w_mlp2", "final_norm_g")
    res = [loss, grad_x[None]]
    for kind in range(4):
        res += [big[kind][k] if k in big[kind] else small[kind][k] for k in order]
    return tuple(res)
```

```python
import functools

import jax
import jax.numpy as jnp
import numpy as np
from jax import lax
from jax.experimental import pallas as pl
from jax.experimental.pallas import tpu as pltpu

F32, BF16 = jnp.float32, jnp.bfloat16
MESH = pl.DeviceIdType.MESH

N_DEV = 8
D = 1024
CC = D // 2
CW = 31
HALO = 32
NH = 8
NOPE, ROPE, VD = D // 16, D // 32, D // 16
QL, KL = 3 * D // 8, D // 4
HP = 128
XH, XD = 4, D // 8
DFF = 4 * D
EPS = 1e-6
THETA = 10000.0
MLA_SCALE = float((NOPE + ROPE) ** -0.5)
X_SCALE = float(XD ** -0.5)
NEG = -1e30
P_A, P_G, P_Q, P_KV, P_KR, P_GL, P_END = 0, CC, 2 * CC, 2 * CC + QL, 2 * CC + QL + KL, 2 * CC + QL + KL + HP, 2 * CC + QL + KL + HP + 2 * D

ADAM_LR, ADAM_B1, ADAM_B2, ADAM_EPS, ADAM_WD, ADAM_STEP = 0.001, 0.9, 0.999, 1e-08, 0.01, 10

ROW_TILE = 256
ATT_TILE = 512
HG = 2
ATT_CHUNK = 32
DW_TILE = 512
PACK_TILE = 1536
VMEM_LIMIT = 56 * 1024 * 1024


def _mm(a, w):
    return jnp.dot(a.astype(BF16), w, preferred_element_type=F32)


def _mm_nt(a, w):
    return lax.dot_general(a.astype(BF16), w, (((1,), (1,)), ((), ())), preferred_element_type=F32)


def _rms(x, g):
    r = lax.rsqrt(jnp.mean(x * x, axis=-1, keepdims=True) + EPS)
    xh = x * r
    return xh * g, xh, r


def _rms_bwd(dy, xh, r, g):
    dxh = dy * g
    dx = r * (dxh - xh * jnp.mean(dxh * xh, axis=-1, keepdims=True))
    return dx, jnp.sum(dy * xh, axis=0, keepdims=True)


def _sig(x):
    return 1.0 / (1.0 + jnp.exp(-x))


def _coords():
    return lax.axis_index("x"), lax.axis_index("y"), lax.axis_index("c")


def _exchange_ops(src_ref, gather, dst_ref, send_sems, recv_sems, local_sem):
    x, y, c = _coords()
    me = 4 * x + 2 * y + c
    ops = [pltpu.make_async_copy(src_ref if gather else src_ref.at[me], dst_ref.at[me], local_sem)]
    for j in range(1, N_DEV):
        px, py, pc = (1 - x if j & 4 else x), (1 - y if j & 2 else y), (1 - c if j & 1 else c)
        ops.append(pltpu.make_async_remote_copy(
            src_ref=src_ref if gather else src_ref.at[4 * px + 2 * py + pc], dst_ref=dst_ref.at[me],
            send_sem=send_sems.at[j - 1], recv_sem=recv_sems.at[j - 1], device_id=(px, py, pc), device_id_type=MESH))
    return ops


def _exchange_hook(first, last, gather, refs):
    @pl.when(first)
    def _():
        for op in _exchange_ops(refs[0], gather, *refs[1:]):
            op.start()

    @pl.when(last)
    def _():
        for op in _exchange_ops(refs[0], gather, *refs[1:]):
            op.wait()


def _exchange_shapes(exchange):
    if exchange is None:
        return [], [], [], []
    arr, gather = exchange
    shape = (N_DEV,) + arr.shape if gather else arr.shape
    any_spec = pl.BlockSpec(memory_space=pl.ANY)
    sems = [pltpu.SemaphoreType.DMA((N_DEV - 1,)), pltpu.SemaphoreType.DMA((N_DEV - 1,)), pltpu.SemaphoreType.DMA]
    return [any_spec], [any_spec], [jax.ShapeDtypeStruct(shape, arr.dtype)], sems


def _row_call(name, fn, n, rows, consts, outs, accs=(), scratch=(), exchange=None):
    def row_spec(shape, mode):
        r = shape[-2] // n
        if mode == "cur":
            f = lambda i: i
        elif mode == "prev":
            f = lambda i: jnp.maximum(i - 1, 0)
        else:
            f = lambda i: jnp.minimum(i + 1, n - 1)
        if len(shape) == 2:
            return pl.BlockSpec((r, shape[1]), lambda i: (f(i), 0))
        return pl.BlockSpec((shape[0], r, shape[2]), lambda i: (0, f(i), 0))

    def whole_spec(shape, single):
        nd = len(shape)
        if single:
            return pl.BlockSpec(shape, lambda i: (0,) * nd, pipeline_mode=pl.Buffered(1))
        return pl.BlockSpec(shape, lambda i: (0,) * nd)

    nr, nc, no, na, ns = len(rows), len(consts), len(outs), len(accs), len(scratch)
    x_in, x_out, x_shape, x_sems = _exchange_shapes(exchange)
    nx = len(x_in)

    def body(*refs):
        i = pl.program_id(0)
        row_refs, const_refs = refs[:nr], refs[nr:nr + nc]
        o0 = nr + nc + nx
        out_refs, acc_refs = refs[o0:o0 + no], refs[o0 + no:o0 + no + na]
        s0 = o0 + no + na + nx
        if nx:
            _exchange_hook(i == 0, i == n - 1, exchange[1], (refs[nr + nc], refs[s0 - 1]) + tuple(refs[s0 + ns:]))
        if na:
            @pl.when(i == 0)
            def _():
                for a in acc_refs:
                    a[...] = jnp.zeros(a.shape, a.dtype)
        fn(i, row_refs, const_refs, out_refs, acc_refs, refs[s0:s0 + ns])

    res = pl.pallas_call(
        body, name=name, grid=(n,),
        in_specs=[row_spec(a.shape, m) for a, m in rows] + [whole_spec(c.shape, True) for c in consts] + x_in,
        out_specs=[row_spec(s, "cur") for s, _ in outs] + [whole_spec(s, False) for s, _ in accs] + x_out,
        out_shape=[jax.ShapeDtypeStruct(s, d) for s, d in list(outs) + list(accs)] + x_shape,
        scratch_shapes=list(scratch) + x_sems,
        compiler_params=pltpu.CompilerParams(dimension_semantics=("arbitrary",), vmem_limit_bytes=VMEM_LIMIT),
    )(*[a for a, _ in rows], *consts, *([exchange[0]] if nx else []))
    return list(res)


def _in_proj(x, g_mix, wp, n):
    t = x.shape[0]

    def fn(i, rows, consts, outs, accs, scr):
        g, w = consts
        u, _, _ = _rms(rows[0][...], g[...])
        ub = u.astype(BF16)
        outs[0][...] = ub
        for k, (lo, hi) in enumerate(((P_A, P_G), (P_G, P_Q), (P_Q, P_KV), (P_KV, P_KR), (P_KR, P_GL), (P_GL, P_END))):
            outs[1 + k][...] = _mm(ub, w[:, lo:hi])

    outs = [((t, D), BF16), ((t, CC), F32), ((t, CC), F32), ((t, QL), F32), ((t, KL), F32), ((t, HP), F32), ((t, 2 * D), F32)]
    return _row_call("in_proj", fn, n, [(x, "cur")], [g_mix, wp], outs)


def _fill_glu_window(i, a, gt, ap, gtp, zz):
    r = a.shape[0]
    zp = ap[r - HALO:, :] * _sig(gtp[r - HALO:, :])
    zz[0:HALO, :] = jnp.where(i > 0, zp, 0.0)
    zz[HALO:, :] = a[...] * _sig(gt[...])


def _conv_branch(a, gt, cw, cb, lng, lnb, wco, n):
    t = a.shape[0]
    r = t // n

    def fn(i, rows, consts, outs, accs, scr):
        w, b, lg, lb, wo = consts
        zz = scr[0]
        _fill_glu_window(i, rows[0], rows[1], rows[2], rows[3], zz)
        for c in range(CC // HP):
            ln = slice(c * HP, (c + 1) * HP)
            acc = jnp.zeros((r, HP), F32)
            for j in range(CW):
                acc = acc + w[j:j + 1, ln] * zz[HALO - (CW - 1) + j:HALO - (CW - 1) + j + r, ln]
            outs[0][:, ln] = acc + b[:, ln]
        zc = outs[0][...]
        mu = jnp.mean(zc, axis=-1, keepdims=True)
        dlt = zc - mu
        rs = lax.rsqrt(jnp.mean(dlt * dlt, axis=-1, keepdims=True) + EPS)
        zn = dlt * rs * lg[...] + lb[...]
        outs[1][...] = _mm(zn * _sig(zn), wo[...])

    return _row_call("conv_branch", fn, n, [(a, "cur"), (gt, "cur"), (a, "prev"), (gt, "prev")],
                     [cw, cb, lng, lnb, wco], [((t, CC), F32), ((t, D), F32)],
                     scratch=[pltpu.VMEM((r + HALO, CC), F32)])


def _rope(v, c, sa, sb):
    return v * c + pltpu.roll(v, HP - ROPE // 2, 1) * sa + pltpu.roll(v, ROPE // 2, 1) * sb


def _rope_bwd(dv, c, sa, sb):
    return dv * c + pltpu.roll(dv * sa, ROPE // 2, 1) + pltpu.roll(dv * sb, HP - ROPE // 2, 1)


def _mla_prep(cq, ckv, krp, tc, tsa, tsb, gq, wuq, gkv, wk, wv, n):
    t = cq.shape[0]

    def fn(i, rows, consts, outs, accs, scr):
        g_q, w_q, g_kv, w_k, w_v = consts
        c, sa, sb = rows[3][...], rows[4][...], rows[5][...]
        cqn = _rms(rows[0][...], g_q[...])[0].astype(BF16)
        ckvn = _rms(rows[1][...], g_kv[...])[0].astype(BF16)
        outs[3][...] = cqn
        outs[4][...] = ckvn
        krr = _rope(rows[2][...], c, sa, sb)
        for h in range(NH):
            ln = slice(h * HP, (h + 1) * HP)
            outs[0][:, ln] = (_rope(_mm(cqn, w_q[:, ln]), c, sa, sb) * MLA_SCALE).astype(BF16)
            outs[1][:, ln] = (_mm(ckvn, w_k[:, ln]) + krr).astype(BF16)
        vv = _mm(ckvn, w_v[...])
        lane = lax.broadcasted_iota(jnp.int32, vv.shape, 1)
        outs[2][...] = jnp.where((lane & (HP - 1)) == VD, 1.0, vv).astype(BF16)

    outs = [((t, NH * HP), BF16)] * 3 + [((t, QL), BF16), ((t, KL), BF16)]
    return _row_call("mla_prep", fn, n, [(a, "cur") for a in (cq, ckv, krp, tc, tsa, tsb)], [gq, wuq, gkv, wk, wv], outs)


def _chunk_mask(c, tq, transposed, rows=ATT_CHUNK):
    row = lax.broadcasted_iota(jnp.int32, (rows, tq), 0) + c * rows
    col = lax.broadcasted_iota(jnp.int32, (rows, tq), 1)
    return (row <= col) if transposed else (col <= row)


def _flash_hook(exchange, nh, nq, refs):
    if exchange is not None:
        h, i = pl.program_id(0), pl.program_id(1)
        _exchange_hook((h == 0) & (i == 0), (h == nh - 1) & (i == nq - 1), exchange[1], refs)


def _head_lanes(g):
    return slice(g * HP, (g + 1) * HP)


def _flash_fwd(q, k, v, exchange=None):
    t = q.shape[0]
    tq = min(ATT_TILE, t)
    nq = t // tq
    x_in, x_out, x_shape, x_sems = _exchange_shapes(exchange)

    def body(q_ref, k_ref, v_ref, *rest):
        o_ref, lse_ref = rest[len(x_in):len(x_in) + 2]
        _flash_hook(exchange, NH // HG, nq, rest[:len(x_in)] + rest[len(x_in) + 2:])
        i = pl.program_id(1)

        def step(j, carry, masked):
            at = pl.ds(pl.multiple_of(j * tq, tq), tq)
            out = []
            for g in range(HG):
                m, acc = carry[g]
                s = _mm_nt(q_ref[:, _head_lanes(g)], k_ref[at, _head_lanes(g)])
                if masked:
                    s = jnp.where(_chunk_mask(0, tq, False, tq), s, NEG)
                m_new = jnp.maximum(m, jnp.max(s, axis=-1, keepdims=True))
                out.append((m_new, jnp.exp(m - m_new) * acc + _mm(jnp.exp(s - m_new), v_ref[at, _head_lanes(g)])))
            return tuple(out)

        init = tuple((jnp.full((tq, 1), NEG, F32), jnp.zeros((tq, HP), F32)) for _ in range(HG))
        carry = lax.fori_loop(0, i, lambda j, c: step(j, c, False), init)
        for g, (m, acc) in enumerate(step(i, carry, True)):
            lane = lax.broadcasted_iota(jnp.int32, acc.shape, 1)
            l = jnp.sum(jnp.where(lane == VD, acc, 0.0), axis=-1, keepdims=True)
            o_ref[:, _head_lanes(g)] = acc / l
            lse_ref[g] = m + jnp.log(l)

    wide = HG * HP
    return pl.pallas_call(
        body, name="flash_fwd", grid=(NH // HG, nq),
        in_specs=[pl.BlockSpec((tq, wide), lambda h, i: (i, h)), pl.BlockSpec((t, wide), lambda h, i: (0, h)),
                  pl.BlockSpec((t, wide), lambda h, i: (0, h))] + x_in,
        out_specs=[pl.BlockSpec((tq, wide), lambda h, i: (i, h)), pl.BlockSpec((HG, tq, 1), lambda h, i: (h, i, 0))] + x_out,
        out_shape=[jax.ShapeDtypeStruct((t, NH * HP), F32), jax.ShapeDtypeStruct((NH, t, 1), F32)] + x_shape,
        scratch_shapes=x_sems,
        compiler_params=pltpu.CompilerParams(dimension_semantics=("arbitrary", "arbitrary"), vmem_limit_bytes=VMEM_LIMIT),
    )(q, k, v, *([exchange[0]] if x_in else []))


def _flash_dq(q, k, v, do, lse, delta, exchange=None):
    t = q.shape[0]
    tq = min(ATT_TILE, t)
    nq = t // tq
    x_in, x_out, x_shape, x_sems = _exchange_shapes(exchange)

    def body(q_ref, k_ref, v_ref, do_ref, lse_ref, dl_ref, *rest):
        dq_ref = rest[len(x_in)]
        s_scr, dp_scr, ds_scr = rest[len(rest) - 3:]
        _flash_hook(exchange, NH // HG, nq, rest[:len(x_in)] + rest[len(x_in) + 1:len(rest) - 3])
        i = pl.program_id(1)
        dq_ref[...] = jnp.zeros(dq_ref.shape, F32)

        def step(j, masked):
            at = pl.ds(pl.multiple_of(j * tq, tq), tq)
            for g in range(HG):
                s_scr[g] = _mm_nt(q_ref[:, _head_lanes(g)], k_ref[at, _head_lanes(g)])
                dp_scr[g] = _mm_nt(do_ref[:, _head_lanes(g)], v_ref[at, _head_lanes(g)])
            for g in range(HG):
                for c in range(tq // ATT_CHUNK):
                    rows = slice(c * ATT_CHUNK, (c + 1) * ATT_CHUNK)
                    s = s_scr[g, rows, :]
                    if masked:
                        s = jnp.where(_chunk_mask(c, tq, False), s, NEG)
                    p = jnp.exp(s - lse_ref[g, rows, :])
                    ds_scr[g, rows, :] = (p * (dp_scr[g, rows, :] - dl_ref[g, rows, :])).astype(BF16)
            for g in range(HG):
                dq_ref[:, _head_lanes(g)] += _mm(ds_scr[g], k_ref[at, _head_lanes(g)])

        lax.fori_loop(0, i, lambda j, c: (step(j, False), c)[1], 0)
        step(i, True)

    wide = HG * HP
    blk = pl.BlockSpec((tq, wide), lambda h, i: (i, h))
    whole = pl.BlockSpec((t, wide), lambda h, i: (0, h))
    col = pl.BlockSpec((HG, tq, 1), lambda h, i: (h, i, 0))
    x_sems = x_sems + [pltpu.VMEM((HG, tq, tq), F32), pltpu.VMEM((HG, tq, tq), F32), pltpu.VMEM((HG, tq, tq), BF16)]
    return pl.pallas_call(
        body, name="flash_dq", grid=(NH // HG, nq), in_specs=[blk, whole, whole, blk, col, col] + x_in, out_specs=[blk] + x_out,
        out_shape=[jax.ShapeDtypeStruct((t, NH * HP), F32)] + x_shape, scratch_shapes=x_sems,
        compiler_params=pltpu.CompilerParams(dimension_semantics=("arbitrary", "arbitrary"), vmem_limit_bytes=VMEM_LIMIT),
    )(q, k, v, do, lse, delta, *([exchange[0]] if x_in else []))


def _flash_dkv(q, k, v, do, lse_row, delta_row, exchange=None):
    t = q.shape[0]
    tq = min(ATT_TILE, t)
    nq = t // tq
    x_in, x_out, x_shape, x_sems = _exchange_shapes(exchange)

    def body(q_ref, k_ref, v_ref, do_ref, lse_ref, dl_ref, *rest):
        dk_ref, dv_ref = rest[len(x_in):len(x_in) + 2]
        st_scr, dpt_scr, pt_scr, dst_scr = rest[len(rest) - 4:]
        _flash_hook(exchange, NH // HG, nq, rest[:len(x_in)] + rest[len(x_in) + 2:len(rest) - 4])
        j = pl.program_id(1)
        dk_ref[...] = jnp.zeros(dk_ref.shape, F32)
        dv_ref[...] = jnp.zeros(dv_ref.shape, F32)

        def step(i, masked):
            at = pl.ds(pl.multiple_of(i * tq, tq), tq)
            for g in range(HG):
                st_scr[g] = _mm_nt(k_ref[:, _head_lanes(g)], q_ref[at, _head_lanes(g)])
                dpt_scr[g] = _mm_nt(v_ref[:, _head_lanes(g)], do_ref[at, _head_lanes(g)])
            for g in range(HG):
                lse_i, dl_i = lse_ref[g, :, at], dl_ref[g, :, at]
                for c in range(tq // ATT_CHUNK):
                    rows = slice(c * ATT_CHUNK, (c + 1) * ATT_CHUNK)
                    st = st_scr[g, rows, :]
                    if masked:
                        st = jnp.where(_chunk_mask(c, tq, True), st, NEG)
                    pt = jnp.exp(st - lse_i)
                    pt_scr[g, rows, :] = pt.astype(BF16)
                    dst_scr[g, rows, :] = (pt * (dpt_scr[g, rows, :] - dl_i)).astype(BF16)
            for g in range(HG):
                dv_ref[:, _head_lanes(g)] += _mm(pt_scr[g], do_ref[at, _head_lanes(g)])
                dk_ref[:, _head_lanes(g)] += _mm(dst_scr[g], q_ref[at, _head_lanes(g)])

        step(j, True)
        lax.fori_loop(j + 1, nq, lambda i, c: (step(i, False), c)[1], 0)

    wide = HG * HP
    blk = pl.BlockSpec((tq, wide), lambda h, j: (j, h))
    whole = pl.BlockSpec((t, wide), lambda h, j: (0, h))
    row = pl.BlockSpec((HG, 1, t), lambda h, j: (h, 0, 0))
    x_sems = x_sems + [pltpu.VMEM((HG, tq, tq), F32), pltpu.VMEM((HG, tq, tq), F32), pltpu.VMEM((HG, tq, tq), BF16),
                       pltpu.VMEM((HG, tq, tq), BF16)]
    return pl.pallas_call(
        body, name="flash_dkv", grid=(NH // HG, nq), in_specs=[whole, blk, blk, whole, row, row] + x_in, out_specs=[blk, blk] + x_out,
        out_shape=[jax.ShapeDtypeStruct((t, NH * HP), F32)] * 2 + x_shape, scratch_shapes=x_sems,
        compiler_params=pltpu.CompilerParams(dimension_semantics=("arbitrary", "arbitrary"), vmem_limit_bytes=VMEM_LIMIT),
    )(q, k, v, do, lse_row, delta_row, *([exchange[0]] if x_in else []))


def _mem_kv(mem, g_mem, wxkv):
    m = mem.shape[0]

    def fn(i, rows, consts, outs, accs, scr):
        mn = _rms(rows[0][...], consts[0][...])[0].astype(BF16)
        outs[0][...] = mn
        outs[1][...] = _mm(mn, consts[1][:, 0:XH * XD]).astype(BF16)
        outs[2][...] = _mm(mn, consts[1][:, XH * XD:]).astype(BF16)

    return _row_call("mem_kv", fn, 1, [(mem, "cur")], [g_mem, wxkv], [((m, D), BF16), ((m, XH * XD), BF16), ((m, XH * XD), BF16)])


def _merge_xattn(o, gl, conv_out, x, wmo, wo, g_x, wxq, kx, vx, wxo, n):
    t = x.shape[0]

    def fn(i, rows, consts, outs, accs, scr):
        w_mo, w_o, g, w_xq, k_x, v_x, w_xo = consts
        ob = rows[0][...].astype(BF16)
        outs[7][...] = ob
        mla = _mm(ob, w_mo[...])
        outs[0][...] = mla
        merged = (_sig(rows[1][:, 0:D]) * rows[2][...] + _sig(rows[1][:, D:]) * mla).astype(BF16)
        outs[1][...] = merged
        h1 = rows[3][...] + _mm(merged, w_o[...])
        outs[2][...] = h1
        u1 = _rms(h1, g[...])[0].astype(BF16)
        outs[3][...] = u1
        qx = (_mm(u1, w_xq[...]) * X_SCALE).astype(BF16)
        outs[4][...] = qx
        for h in range(XH):
            ln = slice(h * XD, (h + 1) * XD)
            s = _mm_nt(qx[:, ln], k_x[:, ln])
            e = jnp.exp(s - jnp.max(s, axis=-1, keepdims=True))
            p = e / jnp.sum(e, axis=-1, keepdims=True)
            outs[5][:, ln] = _mm(p, v_x[:, ln]).astype(BF16)
        outs[6][...] = h1 + _mm(outs[5][...], w_xo[...])

    outs = [((t, D), F32), ((t, D), BF16), ((t, D), F32), ((t, D), BF16), ((t, XH * XD), BF16), ((t, XH * XD), BF16),
            ((t, D), F32), ((t, NH * HP), BF16)]
    return _row_call("merge_xattn", fn, n, [(a, "cur") for a in (o, gl, conv_out, x)], [wmo, wo, g_x, wxq, kx, vx, wxo], outs)


def _mlp_loss(h2, target, g_mlp, w1, w2, g_fin, n):
    t = h2.shape[0]
    nck = DFF // D

    def fn(i, rows, consts, outs, accs, scr):
        g_m, w_1, w_2, g_f = consts
        h = rows[0][...]
        u2, xh2, r2 = _rms(h, g_m[...])
        ub = u2.astype(BF16)
        outs[0][...] = ub
        h3 = h
        a1 = []
        for c in range(nck):
            ck = slice(c * D, (c + 1) * D)
            a = _mm(ub, w_1[:, ck])
            a1.append(a)
            rl = jnp.maximum(a, 0.0)
            rb = (rl * rl).astype(BF16)
            outs[1][:, ck] = rb
            h3 = h3 + _mm(rb, w_2[ck, :])
        y, xh3, r3 = _rms(h3, g_f[...])
        err = y - rows[1][...]
        accs[0][...] += jnp.sum(jnp.sum(err * err, axis=1, keepdims=True), axis=0, keepdims=True) * (0.5 / D)
        dh3, dgf = _rms_bwd(err * (1.0 / D), xh3, r3, g_f[...])
        accs[1][...] += dgf
        db = dh3.astype(BF16)
        outs[3][...] = db
        du2 = jnp.zeros_like(h)
        for c in range(nck):
            ck = slice(c * D, (c + 1) * D)
            da = (_mm_nt(db, w_2[ck, :]) * (2.0 * jnp.maximum(a1[c], 0.0))).astype(BF16)
            outs[2][:, ck] = da
            du2 = du2 + _mm_nt(da, w_1[:, ck])
        dx2, dgm = _rms_bwd(du2, xh2, r2, g_m[...])
        accs[2][...] += dgm
        outs[4][...] = dh3 + dx2

    outs = [((t, D), BF16), ((t, DFF), BF16), ((t, DFF), BF16), ((t, D), BF16), ((t, D), F32)]
    accs = [((1, 1), F32), ((1, D), F32), ((1, D), F32)]
    return _row_call("mlp_loss", fn, n, [(h2, "cur"), (target, "cur")], [g_mlp, w1, w2, g_fin], outs, accs)


def _merge_xattn_bwd(dh2, h1, qx, gl, conv_out, mla, o, wxo, kx, vx, wxq, g_x, wo, wmo, n):
    t = dh2.shape[0]
    m = kx.shape[0]

    def fn(i, rows, consts, outs, accs, scr):
        w_xo, k_x, v_x, w_xq, g, w_o, w_mo = consts
        d2 = rows[0][...]
        d2b = d2.astype(BF16)
        outs[7][...] = d2b
        dox = _mm_nt(d2b, w_xo[...]).astype(BF16)
        q = rows[2][...]
        dq = []
        for h in range(XH):
            ln = slice(h * XD, (h + 1) * XD)
            qh, kh, vh, doh = q[:, ln], k_x[:, ln], v_x[:, ln], dox[:, ln]
            s = _mm_nt(qh, kh)
            e = jnp.exp(s - jnp.max(s, axis=-1, keepdims=True))
            p = e / jnp.sum(e, axis=-1, keepdims=True)
            dp = _mm_nt(doh, vh)
            ds = p * (dp - jnp.sum(p * dp, axis=-1, keepdims=True))
            dq.append(_mm(ds, kh) * X_SCALE)
            st = _mm_nt(kh, qh)
            et = jnp.exp(st - jnp.max(st, axis=0, keepdims=True))
            pt = et / jnp.sum(et, axis=0, keepdims=True)
            dpt = _mm_nt(vh, doh)
            dst = pt * (dpt - jnp.sum(pt * dpt, axis=0, keepdims=True))
            accs[0][:, ln] += _mm(dst, qh)
            accs[1][:, ln] += _mm(pt, doh)
        dqx = jnp.concatenate(dq, axis=1).astype(BF16)
        outs[0][...] = dqx
        _, xh1, r1 = _rms(rows[1][...], g[...])
        dx1, dg = _rms_bwd(_mm_nt(dqx, w_xq[...]), xh1, r1, g[...])
        accs[2][...] += dg
        d1 = d2 + dx1
        outs[1][...] = d1
        d1b = d1.astype(BF16)
        outs[8][...] = d1b
        dm = _mm_nt(d1b, w_o[...])
        g0, g1 = _sig(rows[3][:, 0:D]), _sig(rows[3][:, D:])
        outs[2][:, 0:D] = (dm * rows[4][...] * g0 * (1.0 - g0)).astype(BF16)
        outs[2][:, D:] = (dm * rows[5][...] * g1 * (1.0 - g1)).astype(BF16)
        outs[3][...] = (dm * g0).astype(BF16)
        dmla = (dm * g1).astype(BF16)
        outs[4][...] = dmla
        do = _mm_nt(dmla, w_mo[...])
        outs[5][...] = do.astype(BF16)
        prod = do * rows[6][...]
        for h in range(NH):
            outs[6][h] = jnp.sum(prod[:, h * HP:(h + 1) * HP], axis=-1, keepdims=True)

    outs = [((t, XH * XD), BF16), ((t, D), F32), ((t, 2 * D), BF16), ((t, D), BF16), ((t, D), BF16), ((t, NH * HP), BF16),
            ((NH, t, 1), F32), ((t, D), BF16), ((t, D), BF16)]
    accs = [((m, XH * XD), F32), ((m, XH * XD), F32), ((1, D), F32)]
    return _row_call("merge_xattn_bwd", fn, n, [(a, "cur") for a in (dh2, h1, qx, gl, conv_out, mla, o)],
                     [wxo, kx, vx, wxq, g_x, wo, wmo], outs, accs)


def _mla_prep_bwd(dq, dk, dv, cq, ckv, tc, tsa, tsb, gq, wuq, gkv, wk, wv, n):
    t = dq.shape[0]

    def fn(i, rows, consts, outs, accs, scr):
        g_q, w_q, g_kv, w_k, w_v = consts
        c, sa, sb = rows[5][...], rows[6][...], rows[7][...]
        dkr = jnp.zeros((rows[0].shape[0], HP), F32)
        for h in range(NH):
            ln = slice(h * HP, (h + 1) * HP)
            outs[0][:, ln] = _rope_bwd(rows[0][:, ln] * MLA_SCALE, c, sa, sb).astype(BF16)
            dkr = dkr + rows[1][:, ln]
        lane = lax.broadcasted_iota(jnp.int32, dkr.shape, 1)
        outs[5][...] = jnp.where((lane >= NOPE) & (lane < NOPE + ROPE), _rope_bwd(dkr, c, sa, sb), 0.0).astype(BF16)
        dkb, dvb = rows[1][...].astype(BF16), rows[2][...].astype(BF16)
        outs[1][...] = dkb
        outs[2][...] = dvb
        _, xq, rq = _rms(rows[3][...], g_q[...])
        dcq, dgq = _rms_bwd(_mm_nt(outs[0][...], w_q[...]), xq, rq, g_q[...])
        outs[3][...] = dcq.astype(BF16)
        accs[0][...] += dgq
        _, xk, rk = _rms(rows[4][...], g_kv[...])
        dckv, dgk = _rms_bwd(_mm_nt(dkb, w_k[...]) + _mm_nt(dvb, w_v[...]), xk, rk, g_kv[...])
        outs[4][...] = dckv.astype(BF16)
        accs[1][...] += dgk

    outs = [((t, NH * HP), BF16)] * 3 + [((t, QL), BF16), ((t, KL), BF16), ((t, HP), BF16)]
    return _row_call("mla_prep_bwd", fn, n, [(a, "cur") for a in (dq, dk, dv, cq, ckv, tc, tsa, tsb)],
                     [gq, wuq, gkv, wk, wv], outs, [((1, QL), F32), ((1, KL), F32)])


def _conv_out_bwd(dco, zc, wco, lng, lnb, n):
    t = zc.shape[0]

    def fn(i, rows, consts, outs, accs, scr):
        wo, lg, lb = consts
        z = rows[1][...]
        mu = jnp.mean(z, axis=-1, keepdims=True)
        dlt = z - mu
        rs = lax.rsqrt(jnp.mean(dlt * dlt, axis=-1, keepdims=True) + EPS)
        xh = dlt * rs
        zn = xh * lg[...] + lb[...]
        sg = _sig(zn)
        outs[0][...] = (zn * sg).astype(BF16)
        dzn = _mm_nt(rows[0][...], wo[...]) * (sg * (1.0 + zn * (1.0 - sg)))
        accs[0][...] += jnp.sum(dzn * xh, axis=0, keepdims=True)
        accs[1][...] += jnp.sum(dzn, axis=0, keepdims=True)
        dxh = dzn * lg[...]
        dzc = rs * (dxh - jnp.mean(dxh, axis=-1, keepdims=True) - xh * jnp.mean(dxh * xh, axis=-1, keepdims=True))
        outs[1][...] = dzc
        accs[2][...] += jnp.sum(dzc, axis=0, keepdims=True)

    return _row_call("conv_out_bwd", fn, n, [(dco, "cur"), (zc, "cur")], [wco, lng, lnb],
                     [((t, CC), BF16), ((t, CC), F32)], [((1, CC), F32)] * 3)


def _conv_glu_bwd(dzc, a, gt, cw, n):
    t = a.shape[0]
    r = t // n

    def fn(i, rows, consts, outs, accs, scr):
        w = consts[0]
        zz, dd = scr
        _fill_glu_window(i, rows[2], rows[3], rows[4], rows[5], zz)
        dd[0:r, :] = rows[0][...]
        dd[r:, :] = jnp.where(i < n - 1, rows[1][0:HALO, :], 0.0)
        for c in range(CC // HP):
            ln = slice(c * HP, (c + 1) * HP)
            dcur = dd[0:r, ln]
            acc = jnp.zeros((r, HP), F32)
            for j in range(CW):
                acc = acc + w[j:j + 1, ln] * dd[CW - 1 - j:CW - 1 - j + r, ln]
                accs[0][j:j + 1, ln] += jnp.sum(dcur * zz[HALO - (CW - 1) + j:HALO - (CW - 1) + j + r, ln], axis=0, keepdims=True)
            sg = _sig(rows[3][:, ln])
            outs[0][:, ln] = (acc * sg).astype(BF16)
            outs[0][:, CC + c * HP:CC + (c + 1) * HP] = (acc * rows[2][:, ln] * sg * (1.0 - sg)).astype(BF16)

    return _row_call("conv_glu_bwd", fn, n, [(dzc, "cur"), (dzc, "next"), (a, "cur"), (gt, "cur"), (a, "prev"), (gt, "prev")],
                     [cw], [((t, 2 * CC), BF16)], [((HALO, CC), F32)],
                     scratch=[pltpu.VMEM((r + HALO, CC), F32), pltpu.VMEM((r + HALO, CC), F32)])


def _in_proj_bwd(pieces, dh1, x, wp, g_mix, n, exchange):
    t = x.shape[0]
    offs = (P_A, P_Q, P_KV, P_KR, P_GL, P_END)

    def fn(i, rows, consts, outs, accs, scr):
        w, g = consts
        du = jnp.zeros((rows[0].shape[0], D), F32)
        for k in range(5):
            du = du + _mm_nt(rows[k][...], w[:, offs[k]:offs[k + 1]])
        _, xh, r = _rms(rows[6][...], g[...])
        dx, dg = _rms_bwd(du, xh, r, g[...])
        accs[0][...] += dg
        outs[0][...] = rows[5][...] + dx

    return _row_call("in_proj_bwd", fn, n, [(a, "cur") for a in list(pieces) + [dh1, x]], [wp, g_mix],
                     [((t, D), F32)], [((1, D), F32)], exchange=exchange)


def _mem_bwd(mem, dkx, dvx, g_mem, wxkv):
    m = mem.shape[0]

    def fn(i, rows, consts, outs, accs, scr):
        g, w = consts
        dkv = jnp.concatenate([rows[1][...], rows[2][...]], axis=1).astype(BF16)
        outs[0][...] = dkv
        _, xh, _ = _rms(rows[0][...], g[...])
        accs[0][...] += jnp.sum(_mm_nt(dkv, w[...]) * xh, axis=0, keepdims=True)

    return _row_call("mem_bwd", fn, 1, [(mem, "cur"), (dkx, "cur"), (dvx, "cur")], [g_mem, wxkv],
                     [((m, 2 * XH * XD), BF16)], [((1, D), F32)])


def _dw(name, xs, dy):
    t, k = xs.shape
    nn = dy.shape[1]
    tk, tn, tt = min(k, 1024), min(nn, 1024), min(t, DW_TILE)

    def body(x_ref, dy_ref, o_ref):
        @pl.when(pl.program_id(2) == 0)
        def _():
            o_ref[...] = jnp.zeros(o_ref.shape, F32)
        o_ref[...] += lax.dot_general(x_ref[...], dy_ref[...], (((0,), (0,)), ((), ())), preferred_element_type=F32)

    return pl.pallas_call(
        body, name=name, grid=(k // tk, nn // tn, t // tt),
        in_specs=[pl.BlockSpec((tt, tk), lambda a, b, c: (c, a)), pl.BlockSpec((tt, tn), lambda a, b, c: (c, b))],
        out_specs=pl.BlockSpec((tk, tn), lambda a, b, c: (a, b)),
        out_shape=jax.ShapeDtypeStruct((k, nn), F32),
        compiler_params=pltpu.CompilerParams(dimension_semantics=("arbitrary", "arbitrary", "arbitrary"), vmem_limit_bytes=VMEM_LIMIT),
    )(xs, dy)


def _all_gather_packed(shard):
    rws = shard.shape[0]

    def body(x_ref, out_ref, send_sems, recv_sems, local_sem):
        x, y, c = _coords()
        me, sibling = (x, y, c), (x, y, 1 - c)
        chips = [(1 - x, y), (x, 1 - y), (1 - x, 1 - y)]

        def slot(px, py, pc):
            return out_ref.at[4 * px + 2 * py + pc]

        def copy(k, block, to, src=None):
            return pltpu.make_async_remote_copy(
                src_ref=slot(*block) if src is None else src, dst_ref=slot(*block),
                send_sem=send_sems.at[k], recv_sem=recv_sems.at[k], device_id=to, device_id_type=MESH)

        mine = pltpu.make_async_copy(x_ref, slot(*me), local_sem)
        mine.start()
        first = [copy(0, me, sibling, src=x_ref)] + [copy(1 + j, me, (*chip, c), src=x_ref) for j, chip in enumerate(chips)]
        for cp in first:
            cp.start()
        passed = [copy(4 + j, (*chip, c), sibling) for j, chip in enumerate(chips)]
        for j, chip in enumerate(chips):
            copy(1 + j, (*chip, c), me).wait_recv()
            passed[j].start()
        copy(0, sibling, me).wait_recv()
        for j, chip in enumerate(chips):
            copy(4 + j, (*chip, 1 - c), me).wait_recv()
        for cp in first + passed:
            cp.wait_send()
        mine.wait()

    return pl.pallas_call(
        body, name="all_gather_weights",
        out_shape=jax.ShapeDtypeStruct((N_DEV, rws, 128), shard.dtype),
        in_specs=[pl.BlockSpec(memory_space=pl.ANY)], out_specs=pl.BlockSpec(memory_space=pl.ANY),
        scratch_shapes=[pltpu.SemaphoreType.DMA((7,)), pltpu.SemaphoreType.DMA((7,)), pltpu.SemaphoreType.DMA],
    )(shard)


def _adam(w, g, m, v):
    m = ADAM_B1 * m + (1.0 - ADAM_B1) * g
    v = ADAM_B2 * v + (1.0 - ADAM_B2) * (g * g)
    m_hat = m / (1.0 - ADAM_B1 ** ADAM_STEP)
    v_hat = v / (1.0 - ADAM_B2 ** ADAM_STEP)
    return -ADAM_LR * (m_hat / (jnp.sqrt(v_hat) + ADAM_EPS) + ADAM_WD * w), m, v


def _small_allreduce_adam(part, w, m, v):
    shape = part.shape

    def body(p_ref, w_ref, m_ref, v_ref, g_ref, d_ref, nm_ref, nv_ref, buf, send_sems, recv_sems):
        x, y, c = _coords()
        me = 4 * x + 2 * y + c
        buf[0] = p_ref[...]
        cps = []
        for j in range(1, N_DEV):
            jx, jy, jc = j >> 2, (j >> 1) & 1, j & 1
            peer = (1 - x if jx else x, 1 - y if jy else y, 1 - c if jc else c)
            cps.append(pltpu.make_async_remote_copy(src_ref=p_ref, dst_ref=buf.at[j], send_sem=send_sems.at[j - 1],
                                                    recv_sem=recv_sems.at[j - 1], device_id=peer, device_id_type=MESH))
        for cp in cps:
            cp.start()
        for cp in cps:
            cp.wait()
        g = buf[me]
        for d in range(1, N_DEV):
            g = g + buf[d ^ me]
        g_ref[...] = g
        d_ref[...], nm_ref[...], nv_ref[...] = _adam(w_ref[...], g, m_ref[...], v_ref[...])

    vm = pl.BlockSpec(memory_space=pltpu.VMEM)
    return pl.pallas_call(
        body, name="small_allreduce_adam", out_shape=[jax.ShapeDtypeStruct(shape, F32)] * 4,
        in_specs=[vm] * 4, out_specs=[vm] * 4,
        scratch_shapes=[pltpu.VMEM((N_DEV,) + shape, F32), pltpu.SemaphoreType.DMA((7,)), pltpu.SemaphoreType.DMA((7,))],
    )(part, w, m, v)


def _sum_adam(name, parts, w, m, v):
    rws = w.shape[0]
    tile = max(d for d in range(16, PACK_TILE + 1, 16) if rws % d == 0)

    def body(p_ref, w_ref, m_ref, v_ref, g_ref, d_ref, nm_ref, nv_ref):
        g = p_ref[0]
        for d in range(1, N_DEV):
            g = g + p_ref[d]
        g_ref[...] = g
        d_ref[...], nm_ref[...], nv_ref[...] = _adam(w_ref[...], g, m_ref[...], v_ref[...])

    spec = pl.BlockSpec((tile, 128), lambda i: (i, 0))
    return pl.pallas_call(
        body, name=name, grid=(rws // tile,), in_specs=[pl.BlockSpec((N_DEV, tile, 128), lambda i: (0, i, 0))] + [spec] * 3,
        out_specs=[spec] * 4, out_shape=[jax.ShapeDtypeStruct((rws, 128), F32)] * 4,
        compiler_params=pltpu.CompilerParams(dimension_semantics=("arbitrary",), vmem_limit_bytes=VMEM_LIMIT),
    )(parts, w, m, v)


ROW_SHARDED = ("w_out", "w_xq", "w_xkv", "w_mlp2")
SMALL = ("norm_mix_g", "conv_b", "conv_ln_g", "conv_ln_b", "q_norm_g", "kv_norm_g", "norm_xattn_g", "norm_mem_g", "norm_mlp_g", "final_norm_g")
GATHER_FIRST = ("w_in", "conv_w", "w_conv_out", "w_uq", "w_ukv")
GATHER_LATE = ("w_mla_out", "w_out", "w_xq", "w_xkv", "w_xo", "w_mlp1", "w_mlp2")
REDUCE_MLP = ("w_mlp1", "w_mlp2")
REDUCE_MID = ("conv_w", "w_conv_out", "w_mla_out", "w_out", "w_xq", "w_xkv", "w_xo")
REDUCE_LAST = ("w_in", "w_uq", "w_ukv")
ROW_ALIGN = 16


def _padded(k, nn):
    return -(-k // ROW_ALIGN) * ROW_ALIGN, -(-nn // 128) * 128


def _pack_rows(a):
    k, nn = a.shape[-2:]
    kp, np_ = _padded(k, nn)
    if (kp, np_) != (k, nn):
        a = jnp.pad(a, [(0, 0)] * (a.ndim - 2) + [(0, kp - k), (0, np_ - nn)])
    return jnp.concatenate([a[..., b * 128:(b + 1) * 128] for b in range(np_ // 128)], axis=-2)


def _unpack_rows(p, k, nn):
    kp, np_ = _padded(k, nn)
    a = jnp.concatenate([p[..., b * kp:(b + 1) * kp, :] for b in range(np_ // 128)], axis=-1)
    return a[..., :k, :nn]


def _pack_group(arrays):
    return jnp.concatenate([_pack_rows(a) for a in arrays], axis=-2)


def _unpack_group(p, shapes):
    out, off = [], 0
    for k, nn in shapes:
        kp, np_ = _padded(k, nn)
        rws = kp * np_ // 128
        out.append(_unpack_rows(p[..., off:off + rws, :], k, nn))
        off += rws
    return out


def _to_full(name, stacked):
    _, k, nn = stacked.shape
    if name in ROW_SHARDED:
        return stacked.reshape(N_DEV * k, nn)
    return jnp.transpose(stacked, (1, 0, 2)).reshape(k, N_DEV * nn)


def _to_stacked(name, full, shard_shape):
    k, nn = shard_shape
    if name in ROW_SHARDED:
        return full.reshape(N_DEV, k, nn)
    return jnp.transpose(full.reshape(k, N_DEV, nn), (1, 0, 2))


def _pack_small(vals):
    flat = jnp.concatenate([v.reshape(-1) for v in vals])
    rws = flat.shape[0] // 128
    return jnp.pad(flat, (0, (-(-rws // 8) * 8 - rws) * 128)).reshape(-1, 128)


def _unpack_small(p, sizes):
    flat = p.reshape(-1)
    out, off = [], 0
    for s in sizes:
        out.append(flat[off:off + s])
        off += s
    return out


def _head_pad(w, real, axis):
    shp = list(w.shape)
    shp[axis:axis + 1] = [NH, real]
    w = w.reshape(shp)
    pad = [(0, 0)] * w.ndim
    pad[axis + 1] = (0, HP - real)
    w = jnp.pad(w, pad)
    shp[axis:axis + 2] = [NH * HP]
    return w.reshape(shp)


def _head_unpad(w, lo, real, axis):
    shp = list(w.shape)
    shp[axis:axis + 1] = [NH, HP]
    w = lax.slice_in_dim(w.reshape(shp), lo, lo + real, axis=axis + 1)
    shp[axis:axis + 2] = [NH * real]
    return w.reshape(shp)


def kernel(x, mem, positions, norm_mix_g, w_in, conv_w, conv_b, conv_ln_g, conv_ln_b, w_conv_out, q_norm_g, w_uq, kv_norm_g, w_ukv, w_mla_out, w_out, norm_xattn_g, norm_mem_g, w_xq, w_xkv, w_xo, norm_mlp_g, w_mlp1, w_mlp2, final_norm_g, loss_target, m_norm_mix_g, m_w_in, m_conv_w, m_conv_b, m_conv_ln_g, m_conv_ln_b, m_w_conv_out, m_q_norm_g, m_w_uq, m_kv_norm_g, m_w_ukv, m_w_mla_out, m_w_out, m_norm_xattn_g, m_norm_mem_g, m_w_xq, m_w_xkv, m_w_xo, m_norm_mlp_g, m_w_mlp1, m_w_mlp2, m_final_norm_g, v_norm_mix_g, v_w_in, v_conv_w, v_conv_b, v_conv_ln_g, v_conv_ln_b, v_w_conv_out, v_q_norm_g, v_w_uq, v_kv_norm_g, v_w_ukv, v_w_mla_out, v_w_out, v_norm_xattn_g, v_norm_mem_g, v_w_xq, v_w_xkv, v_w_xo, v_norm_mlp_g, v_w_mlp1, v_w_mlp2, v_final_norm_g):
    args = dict(locals())
    t = x.shape[1]
    n = t // min(ROW_TILE, t)
    xs, mems, tgt = x[0], mem[0], loss_target[0]

    def pack_shards(prefix, group, dtype):
        return _pack_group([args[prefix + k][0].astype(dtype) for k in group])

    def shapes(group):
        return [args[k].shape[1:] for k in group]

    def unpack_full(gathered, group):
        return {k: _to_full(k, s) for k, s in zip(group, _unpack_group(gathered, shapes(group)))}

    full = unpack_full(_all_gather_packed(pack_shards("", GATHER_FIRST, BF16)), GATHER_FIRST)

    wi = full["w_in"]
    kr_slot = jnp.pad(wi[:, P_KR:P_KR + ROPE], ((0, 0), (NOPE, HP - NOPE - ROPE)))
    wp = jnp.concatenate([wi[:, :P_KR], kr_slot, wi[:, P_KR + ROPE:]], axis=1)
    cw = jnp.pad(full["conv_w"].astype(F32), ((0, HALO - CW), (0, 0)))
    wuq = _head_pad(full["w_uq"], NOPE + ROPE, 1)
    ukv = full["w_ukv"].reshape(KL, NH, NOPE + VD)
    wk = _head_pad(ukv[:, :, :NOPE].reshape(KL, NH * NOPE), NOPE, 1)
    wv = _head_pad(ukv[:, :, NOPE:].reshape(KL, NH * VD), VD, 1)

    inv_freq = THETA ** (-jnp.arange(ROPE // 2, dtype=F32) / (ROPE // 2))
    ang = positions[0].astype(F32)[:, None] * inv_freq
    cs, sn, zr = jnp.cos(ang), jnp.sin(ang), jnp.zeros((t, ROPE // 2), F32)
    tail = jnp.zeros((t, HP - NOPE - ROPE), F32)
    tc = jnp.concatenate([jnp.ones((t, NOPE), F32), cs, cs, tail], axis=1)
    tsa = jnp.concatenate([jnp.zeros((t, NOPE), F32), -sn, zr, tail], axis=1)
    tsb = jnp.concatenate([jnp.zeros((t, NOPE), F32), zr, sn, tail], axis=1)

    u0, a, gt, cq, ckv, krp, gl = _in_proj(xs, norm_mix_g, wp, n)
    zc, conv_out = _conv_branch(a, gt, cw, conv_b, conv_ln_g, conv_ln_b, full["w_conv_out"], n)
    qh, kh, vh, cqn, ckvn = _mla_prep(cq, ckv, krp, tc, tsa, tsb, q_norm_g, wuq, kv_norm_g, wk, wv, n)
    o, lse, gathered = _flash_fwd(qh, kh, vh, exchange=(pack_shards("", GATHER_LATE, BF16), True))
    full.update(unpack_full(gathered, GATHER_LATE))
    wmo = _head_pad(full["w_mla_out"], VD, 0)
    memn, kx, vx = _mem_kv(mems, norm_mem_g, full["w_xkv"])
    mla, merged, h1, u1, qx, ox, h2, ob = _merge_xattn(o, gl, conv_out, xs, wmo, full["w_out"], norm_xattn_g, full["w_xq"],
                                                        kx, vx, full["w_xo"], n)
    gfin = final_norm_g.reshape(1, D)
    u2, rl2, da1, dh3b, dh2, loss_p, dg_fin, dg_mlp = _mlp_loss(h2, tgt, norm_mlp_g, full["w_mlp1"], full["w_mlp2"], gfin, n)

    (dqx, dh1, dgl, dco, dmla, dob, delta, dh2b, dh1b, dkx, dvx, dg_x) = _merge_xattn_bwd(
        dh2, h1, qx, gl, conv_out, mla, o, full["w_xo"], kx, vx, full["w_xq"], norm_xattn_g, full["w_out"], wmo, n)
    gfull = {"w_mlp1": _dw("dw_mlp1", u2, da1), "w_mlp2": _dw("dw_mlp2", rl2, dh3b)}

    def stacked(group):
        return _pack_group([_to_stacked(k, gfull[k], args[k].shape[1:]) for k in group])

    dq, parts_mlp = _flash_dq(qh, kh, vh, dob, lse, delta, exchange=(stacked(REDUCE_MLP), False))
    zs, dzc, dg_lng, dg_lnb, dg_cb = _conv_out_bwd(dco, zc, full["w_conv_out"], conv_ln_g, conv_ln_b, n)
    dci, dcw = _conv_glu_bwd(dzc, a, gt, cw, n)
    dkv, dg_mem = _mem_bwd(mems, dkx, dvx, norm_mem_g, full["w_xkv"])
    gfull.update({
        "conv_w": dcw[:CW],
        "w_conv_out": _dw("dw_conv_out", zs, dco),
        "w_mla_out": _head_unpad(_dw("dw_mla_out", ob, dmla), 0, VD, 0),
        "w_out": _dw("dw_out", merged, dh1b),
        "w_xq": _dw("dw_xq", u1, dqx),
        "w_xkv": _dw("dw_xkv", memn, dkv),
        "w_xo": _dw("dw_xo", ox, dh2b),
    })
    dk, dv, parts_mid = _flash_dkv(qh, kh, vh, dob, lse.reshape(NH, 1, t), delta.reshape(NH, 1, t),
                                   exchange=(stacked(REDUCE_MID), False))
    dqp, dkb, dvb, dcq, dckv, dkrp, dg_q, dg_kv = _mla_prep_bwd(dq, dk, dv, cq, ckv, tc, tsa, tsb, q_norm_g, wuq, kv_norm_g, wk, wv, n)
    pieces = (dci, dcq, dckv, dkrp, dgl)
    dwp = [_dw("dw_in_%d" % k, u0, p) for k, p in enumerate(pieces)]
    gfull["w_in"] = jnp.concatenate([dwp[0], dwp[1], dwp[2], dwp[3][:, NOPE:NOPE + ROPE], dwp[4]], axis=1)
    gfull["w_uq"] = _head_unpad(_dw("dw_uq", cqn, dqp), 0, NOPE + ROPE, 1)
    gk = _head_unpad(_dw("dw_uk", ckvn, dkb), 0, NOPE, 1).reshape(KL, NH, NOPE)
    gv = _head_unpad(_dw("dw_uv", ckvn, dvb), 0, VD, 1).reshape(KL, NH, VD)
    gfull["w_ukv"] = jnp.concatenate([gk, gv], axis=2).reshape(KL, NH * (NOPE + VD))
    grad_x, dg_mix, parts_last = _in_proj_bwd(pieces, dh1, xs, wp, norm_mix_g, n, (stacked(REDUCE_LAST), False))

    big = [{}, {}, {}, {}]
    for name, group, parts in (("adam_mlp", REDUCE_MLP, parts_mlp), ("adam_mid", REDUCE_MID, parts_mid),
                               ("adam_last", REDUCE_LAST, parts_last)):
        res = _sum_adam(name, parts, pack_shards("", group, F32), pack_shards("m_", group, F32), pack_shards("v_", group, F32))
        for kind in range(4):
            for k, val in zip(group, _unpack_group(res[kind], shapes(group))):
                big[kind][k] = val[None]

    small_g = {"norm_mix_g": dg_mix, "conv_b": dg_cb, "conv_ln_g": dg_lng, "conv_ln_b": dg_lnb, "q_norm_g": dg_q,
               "kv_norm_g": dg_kv, "norm_xattn_g": dg_x, "norm_mem_g": dg_mem, "norm_mlp_g": dg_mlp, "final_norm_g": dg_fin}
    small_sizes = [int(np.prod(args[k].shape)) for k in SMALL] + [128]
    zero_slot = jnp.zeros((128,), F32)
    small_out = _small_allreduce_adam(
        _pack_small([small_g[k] for k in SMALL] + [jnp.pad(loss_p.reshape(-1), (0, 127))]),
        _pack_small([args[k] for k in SMALL] + [zero_slot]), _pack_small([args["m_" + k] for k in SMALL] + [zero_slot]),
        _pack_small([args["v_" + k] for k in SMALL] + [zero_slot]))

    small = [dict(zip(SMALL, [s.reshape(args[k].shape) for k, s in zip(SMALL, _unpack_small(o_, small_sizes))])) for o_ in small_out]
    loss = _unpack_small(small_out[0], small_sizes)[-1][0]
    order = ("norm_mix_g", "w_in", "conv_w", "conv_b", "conv_ln_g", "conv_ln_b", "w_conv_out", "q_norm_g", "w_uq", "kv_norm_g",
             "w_ukv", "w_mla_out", "w_out", "norm_xattn_g", "norm_mem_g", "w_xq", "w_xkv", "w_xo", "norm_mlp_g", "w_mlp1",
             "w_mlp2", "final_norm_g")
    res = [loss, grad_x[None]]
    for kind in range(4):
        res += [big[kind][k] if k in big[kind] else small[kind][k] for k in order]
    return tuple(res)
```

```python
import functools

import jax
import jax.numpy as jnp
import numpy as np
from jax import lax
from jax.experimental import pallas as pl
from jax.experimental.pallas import tpu as pltpu

F32, BF16 = jnp.float32, jnp.bfloat16
MESH = pl.DeviceIdType.MESH

N_DEV = 8
D = 1024
CC = D // 2
CW = 31
HALO = 32
NH = 8
NOPE, ROPE, VD = D // 16, D // 32, D // 16
QL, KL = 3 * D // 8, D // 4
HP = 128
XH, XD = 4, D // 8
DFF = 4 * D
EPS = 1e-6
THETA = 10000.0
MLA_SCALE = float((NOPE + ROPE) ** -0.5)
X_SCALE = float(XD ** -0.5)
NEG = -1e30
P_A, P_G, P_Q, P_KV, P_KR, P_GL, P_END = 0, CC, 2 * CC, 2 * CC + QL, 2 * CC + QL + KL, 2 * CC + QL + KL + HP, 2 * CC + QL + KL + HP + 2 * D

ADAM_LR, ADAM_B1, ADAM_B2, ADAM_EPS, ADAM_WD, ADAM_STEP = 0.001, 0.9, 0.999, 1e-08, 0.01, 10

ROW_TILE = 256
ATT_TILE = 512
HG = 2
ATT_CHUNK = 32
CONV_ROWS = 128
DW_TILE = 512
PACK_TILE = 1536
VMEM_LIMIT = 56 * 1024 * 1024


def _mm(a, w):
    return jnp.dot(a.astype(BF16), w, preferred_element_type=F32)


def _mm_nt(a, w):
    return lax.dot_general(a.astype(BF16), w, (((1,), (1,)), ((), ())), preferred_element_type=F32)


def _mm_tn(a, b):
    return lax.dot_general(a.astype(BF16), b.astype(BF16), (((0,), (0,)), ((), ())), preferred_element_type=F32)


def _rms(x, g):
    r = lax.rsqrt(jnp.mean(x * x, axis=-1, keepdims=True) + EPS)
    xh = x * r
    return xh * g, xh, r


def _rms_bwd(dy, xh, r, g):
    dxh = dy * g
    dx = r * (dxh - xh * jnp.mean(dxh * xh, axis=-1, keepdims=True))
    return dx, jnp.sum(dy * xh, axis=0, keepdims=True)


def _sig(x):
    return 1.0 / (1.0 + jnp.exp(-x))


def _coords():
    return lax.axis_index("x"), lax.axis_index("y"), lax.axis_index("c")


def _exchange_ops(src_ref, gather, dst_ref, send_sems, recv_sems, local_sem):
    x, y, c = _coords()
    me = 4 * x + 2 * y + c
    ops = [pltpu.make_async_copy(src_ref if gather else src_ref.at[me], dst_ref.at[me], local_sem)]
    for j in range(1, N_DEV):
        px, py, pc = (1 - x if j & 4 else x), (1 - y if j & 2 else y), (1 - c if j & 1 else c)
        ops.append(pltpu.make_async_remote_copy(
            src_ref=src_ref if gather else src_ref.at[4 * px + 2 * py + pc], dst_ref=dst_ref.at[me],
            send_sem=send_sems.at[j - 1], recv_sem=recv_sems.at[j - 1], device_id=(px, py, pc), device_id_type=MESH))
    return ops


def _exchange_hook(first, last, gather, refs):
    @pl.when(first)
    def _():
        for op in _exchange_ops(refs[0], gather, *refs[1:]):
            op.start()

    @pl.when(last)
    def _():
        for op in _exchange_ops(refs[0], gather, *refs[1:]):
            op.wait()


def _exchange_shapes(exchange):
    if exchange is None:
        return [], [], [], []
    arr, gather = exchange
    shape = (N_DEV,) + arr.shape if gather else arr.shape
    any_spec = pl.BlockSpec(memory_space=pl.ANY)
    sems = [pltpu.SemaphoreType.DMA((N_DEV - 1,)), pltpu.SemaphoreType.DMA((N_DEV - 1,)), pltpu.SemaphoreType.DMA]
    return [any_spec], [any_spec], [jax.ShapeDtypeStruct(shape, arr.dtype)], sems


def _row_call(name, fn, n, rows, consts, outs, accs=(), scratch=(), exchange=None):
    def row_spec(shape, mode):
        r = shape[-2] // n
        if mode == "cur":
            f = lambda i: i
        elif mode == "prev":
            f = lambda i: jnp.maximum(i - 1, 0)
        else:
            f = lambda i: jnp.minimum(i + 1, n - 1)
        if len(shape) == 2:
            return pl.BlockSpec((r, shape[1]), lambda i: (f(i), 0))
        return pl.BlockSpec((shape[0], r, shape[2]), lambda i: (0, f(i), 0))

    def whole_spec(shape, single):
        nd = len(shape)
        if single:
            return pl.BlockSpec(shape, lambda i: (0,) * nd, pipeline_mode=pl.Buffered(1))
        return pl.BlockSpec(shape, lambda i: (0,) * nd)

    nr, nc, no, na, ns = len(rows), len(consts), len(outs), len(accs), len(scratch)
    x_in, x_out, x_shape, x_sems = _exchange_shapes(exchange)
    nx = len(x_in)

    def body(*refs):
        i = pl.program_id(0)
        row_refs, const_refs = refs[:nr], refs[nr:nr + nc]
        o0 = nr + nc + nx
        out_refs, acc_refs = refs[o0:o0 + no], refs[o0 + no:o0 + no + na]
        s0 = o0 + no + na + nx
        if nx:
            _exchange_hook(i == 0, i == n - 1, exchange[1], (refs[nr + nc], refs[s0 - 1]) + tuple(refs[s0 + ns:]))
        if na:
            @pl.when(i == 0)
            def _():
                for a in acc_refs:
                    a[...] = jnp.zeros(a.shape, a.dtype)
        fn(i, row_refs, const_refs, out_refs, acc_refs, refs[s0:s0 + ns])

    res = pl.pallas_call(
        body, name=name, grid=(n,),
        in_specs=[row_spec(a.shape, m) for a, m in rows] + [whole_spec(c.shape, True) for c in consts] + x_in,
        out_specs=[row_spec(s, "cur") for s, _ in outs] + [whole_spec(s, False) for s, _ in accs] + x_out,
        out_shape=[jax.ShapeDtypeStruct(s, d) for s, d in list(outs) + list(accs)] + x_shape,
        scratch_shapes=list(scratch) + x_sems,
        compiler_params=pltpu.CompilerParams(dimension_semantics=("arbitrary",), vmem_limit_bytes=VMEM_LIMIT),
    )(*[a for a, _ in rows], *consts, *([exchange[0]] if nx else []))
    return list(res)


def _in_proj(x, g_mix, wp, n):
    t = x.shape[0]

    def fn(i, rows, consts, outs, accs, scr):
        g, w = consts
        u, _, _ = _rms(rows[0][...], g[...])
        ub = u.astype(BF16)
        outs[0][...] = ub
        for k, (lo, hi) in enumerate(((P_A, P_G), (P_G, P_Q), (P_Q, P_KV), (P_KV, P_KR), (P_KR, P_GL), (P_GL, P_END))):
            outs[1 + k][...] = _mm(ub, w[:, lo:hi])

    outs = [((t, D), BF16), ((t, CC), F32), ((t, CC), F32), ((t, QL), F32), ((t, KL), F32), ((t, HP), F32), ((t, 2 * D), F32)]
    return _row_call("in_proj", fn, n, [(x, "cur")], [g_mix, wp], outs)


def _fill_glu_window(i, a, gt, ap, gtp, zz):
    r = a.shape[0]
    zp = ap[r - HALO:, :] * _sig(gtp[r - HALO:, :])
    zz[0:HALO, :] = jnp.where(i > 0, zp, 0.0)
    zz[HALO:, :] = a[...] * _sig(gt[...])


def _shift_copies(buf, sh):
    rows = buf.shape[0]
    for s in range(8):
        sh[s, 0:rows - s, :] = buf[s:rows, :]


def _window(sh, o, base, rb, ln):
    return sh[o % 8, base + o - o % 8:base + o - o % 8 + rb, ln]


def _windows(buf, base, ln, rb, offsets):
    for s in range(8):
        group = [o for o in offsets if o % 8 == s]
        if group:
            shifted = buf[base + s:base + max(group) + rb, ln]
            for o in group:
                yield o, shifted[o - s:o - s + rb]


def _conv_branch(a, gt, cw, cb, lng, lnb, wco, n):
    t = a.shape[0]
    r = t // n

    def fn(i, rows, consts, outs, accs, scr):
        w, b, lg, lb, wo = consts
        zz, zsh = scr
        _fill_glu_window(i, rows[0], rows[1], rows[2], rows[3], zz)
        _shift_copies(zz, zsh)
        rb = min(CONV_ROWS, r)
        for c in range(CC // HP):
            ln = slice(c * HP, (c + 1) * HP)
            for base in range(0, r, rb):
                acc = jnp.zeros((rb, HP), F32)
                for j in range(CW):
                    acc = acc + w[j:j + 1, ln] * _window(zsh, HALO - (CW - 1) + j, base, rb, ln)
                outs[0][base:base + rb, ln] = acc + b[:, ln]
        zc = outs[0][...]
        mu = jnp.mean(zc, axis=-1, keepdims=True)
        dlt = zc - mu
        rs = lax.rsqrt(jnp.mean(dlt * dlt, axis=-1, keepdims=True) + EPS)
        zn = dlt * rs * lg[...] + lb[...]
        outs[1][...] = _mm(zn * _sig(zn), wo[...])

    return _row_call("conv_branch", fn, n, [(a, "cur"), (gt, "cur"), (a, "prev"), (gt, "prev")],
                     [cw, cb, lng, lnb, wco], [((t, CC), F32), ((t, D), F32)],
                     scratch=[pltpu.VMEM((r + HALO, CC), F32), pltpu.VMEM((8, r + HALO, CC), F32)])


def _rope(v, c, sa, sb):
    return v * c + pltpu.roll(v, HP - ROPE // 2, 1) * sa + pltpu.roll(v, ROPE // 2, 1) * sb


def _rope_bwd(dv, c, sa, sb):
    return dv * c + pltpu.roll(dv * sa, ROPE // 2, 1) + pltpu.roll(dv * sb, HP - ROPE // 2, 1)


def _mla_prep(cq, ckv, krp, tc, tsa, tsb, gq, wuq, gkv, wk, wv, n):
    t = cq.shape[0]

    def fn(i, rows, consts, outs, accs, scr):
        g_q, w_q, g_kv, w_k, w_v = consts
        c, sa, sb = rows[3][...], rows[4][...], rows[5][...]
        cqn = _rms(rows[0][...], g_q[...])[0].astype(BF16)
        ckvn = _rms(rows[1][...], g_kv[...])[0].astype(BF16)
        outs[3][...] = cqn
        outs[4][...] = ckvn
        krr = _rope(rows[2][...], c, sa, sb)
        for h in range(NH):
            ln = slice(h * HP, (h + 1) * HP)
            outs[0][:, ln] = (_rope(_mm(cqn, w_q[:, ln]), c, sa, sb) * MLA_SCALE).astype(BF16)
            outs[1][:, ln] = (_mm(ckvn, w_k[:, ln]) + krr).astype(BF16)
        vv = _mm(ckvn, w_v[...])
        lane = lax.broadcasted_iota(jnp.int32, vv.shape, 1)
        outs[2][...] = jnp.where((lane & (HP - 1)) == VD, 1.0, vv).astype(BF16)

    outs = [((t, NH * HP), BF16)] * 3 + [((t, QL), BF16), ((t, KL), BF16)]
    return _row_call("mla_prep", fn, n, [(a, "cur") for a in (cq, ckv, krp, tc, tsa, tsb)], [gq, wuq, gkv, wk, wv], outs)


def _chunk_mask(c, tq, transposed, rows=ATT_CHUNK):
    row = lax.broadcasted_iota(jnp.int32, (rows, tq), 0) + c * rows
    col = lax.broadcasted_iota(jnp.int32, (rows, tq), 1)
    return (row <= col) if transposed else (col <= row)


def _flash_hook(exchange, nh, nq, refs):
    if exchange is not None:
        h, i = pl.program_id(0), pl.program_id(1)
        _exchange_hook((h == 0) & (i == 0), (h == nh - 1) & (i == nq - 1), exchange[1], refs)


def _head_lanes(g):
    return slice(g * HP, (g + 1) * HP)


def _flash_fwd(q, k, v, exchange=None):
    t = q.shape[0]
    tq = min(ATT_TILE, t)
    nq = t // tq
    x_in, x_out, x_shape, x_sems = _exchange_shapes(exchange)

    def body(q_ref, k_ref, v_ref, *rest):
        o_ref, lse_ref = rest[len(x_in):len(x_in) + 2]
        _flash_hook(exchange, NH // HG, nq, rest[:len(x_in)] + rest[len(x_in) + 2:])
        i = pl.program_id(1)

        def step(j, carry, masked):
            at = pl.ds(pl.multiple_of(j * tq, tq), tq)
            out = []
            for g in range(HG):
                m, acc = carry[g]
                s = _mm_nt(q_ref[:, _head_lanes(g)], k_ref[at, _head_lanes(g)])
                if masked:
                    s = jnp.where(_chunk_mask(0, tq, False, tq), s, NEG)
                m_new = jnp.maximum(m, jnp.max(s, axis=-1, keepdims=True))
                out.append((m_new, jnp.exp(m - m_new) * acc + _mm(jnp.exp(s - m_new), v_ref[at, _head_lanes(g)])))
            return tuple(out)

        init = tuple((jnp.full((tq, 1), NEG, F32), jnp.zeros((tq, HP), F32)) for _ in range(HG))
        carry = lax.fori_loop(0, i, lambda j, c: step(j, c, False), init)
        for g, (m, acc) in enumerate(step(i, carry, True)):
            lane = lax.broadcasted_iota(jnp.int32, acc.shape, 1)
            l = jnp.sum(jnp.where(lane == VD, acc, 0.0), axis=-1, keepdims=True)
            o_ref[:, _head_lanes(g)] = acc / l
            lse_ref[g] = m + jnp.log(l)

    wide = HG * HP
    return pl.pallas_call(
        body, name="flash_fwd", grid=(NH // HG, nq),
        in_specs=[pl.BlockSpec((tq, wide), lambda h, i: (i, h)), pl.BlockSpec((t, wide), lambda h, i: (0, h)),
                  pl.BlockSpec((t, wide), lambda h, i: (0, h))] + x_in,
        out_specs=[pl.BlockSpec((tq, wide), lambda h, i: (i, h)), pl.BlockSpec((HG, tq, 1), lambda h, i: (h, i, 0))] + x_out,
        out_shape=[jax.ShapeDtypeStruct((t, NH * HP), F32), jax.ShapeDtypeStruct((NH, t, 1), F32)] + x_shape,
        scratch_shapes=x_sems,
        compiler_params=pltpu.CompilerParams(dimension_semantics=("arbitrary", "arbitrary"), vmem_limit_bytes=VMEM_LIMIT),
    )(q, k, v, *([exchange[0]] if x_in else []))


def _flash_bwd(q, k, v, do, lse_row, delta_row, exchange=None):
    t = q.shape[0]
    tq = min(ATT_TILE, t)
    nq = t // tq
    x_in, x_out, x_shape, x_sems = _exchange_shapes(exchange)

    def body(q_ref, k_ref, v_ref, do_ref, lse_ref, dl_ref, *rest):
        dk_ref, dv_ref, dq_ref = rest[len(x_in):len(x_in) + 3]
        st_scr, dpt_scr, pt_scr, dst_scr = rest[len(rest) - 4:]
        _flash_hook(exchange, NH // HG, nq, rest[:len(x_in)] + rest[len(x_in) + 3:len(rest) - 4])
        j = pl.program_id(1)
        dk_ref[...] = jnp.zeros(dk_ref.shape, F32)
        dv_ref[...] = jnp.zeros(dv_ref.shape, F32)

        @pl.when(j == 0)
        def _():
            dq_ref[...] = jnp.zeros(dq_ref.shape, F32)

        def step(i, masked):
            at = pl.ds(pl.multiple_of(i * tq, tq), tq)
            for g in range(HG):
                st_scr[g] = _mm_nt(k_ref[:, _head_lanes(g)], q_ref[at, _head_lanes(g)])
                dpt_scr[g] = _mm_nt(v_ref[:, _head_lanes(g)], do_ref[at, _head_lanes(g)])
            for g in range(HG):
                lse_i, dl_i = lse_ref[g, :, at], dl_ref[g, :, at]
                for c in range(tq // ATT_CHUNK):
                    rows = slice(c * ATT_CHUNK, (c + 1) * ATT_CHUNK)
                    st = st_scr[g, rows, :]
                    if masked:
                        st = jnp.where(_chunk_mask(c, tq, True), st, NEG)
                    pt = jnp.exp(st - lse_i)
                    pt_scr[g, rows, :] = pt.astype(BF16)
                    dst_scr[g, rows, :] = (pt * (dpt_scr[g, rows, :] - dl_i)).astype(BF16)
            for g in range(HG):
                dv_ref[:, _head_lanes(g)] += _mm(pt_scr[g], do_ref[at, _head_lanes(g)])
                dk_ref[:, _head_lanes(g)] += _mm(dst_scr[g], q_ref[at, _head_lanes(g)])
                dq_ref[at, _head_lanes(g)] += _mm_tn(dst_scr[g], k_ref[:, _head_lanes(g)])

        step(j, True)
        lax.fori_loop(j + 1, nq, lambda i, c: (step(i, False), c)[1], 0)

    wide = HG * HP
    blk = pl.BlockSpec((tq, wide), lambda h, j: (j, h))
    whole = pl.BlockSpec((t, wide), lambda h, j: (0, h))
    row = pl.BlockSpec((HG, 1, t), lambda h, j: (h, 0, 0))
    x_sems = x_sems + [pltpu.VMEM((HG, tq, tq), F32), pltpu.VMEM((HG, tq, tq), F32), pltpu.VMEM((HG, tq, tq), BF16),
                       pltpu.VMEM((HG, tq, tq), BF16)]
    return pl.pallas_call(
        body, name="flash_bwd", grid=(NH // HG, nq), in_specs=[whole, blk, blk, whole, row, row] + x_in,
        out_specs=[blk, blk, whole] + x_out, out_shape=[jax.ShapeDtypeStruct((t, NH * HP), F32)] * 3 + x_shape, scratch_shapes=x_sems,
        compiler_params=pltpu.CompilerParams(dimension_semantics=("arbitrary", "arbitrary"), vmem_limit_bytes=VMEM_LIMIT),
    )(q, k, v, do, lse_row, delta_row, *([exchange[0]] if x_in else []))


def _mem_kv(mem, g_mem, wxkv):
    m = mem.shape[0]

    def fn(i, rows, consts, outs, accs, scr):
        mn = _rms(rows[0][...], consts[0][...])[0].astype(BF16)
        outs[0][...] = mn
        outs[1][...] = _mm(mn, consts[1][:, 0:XH * XD]).astype(BF16)
        outs[2][...] = _mm(mn, consts[1][:, XH * XD:]).astype(BF16)

    return _row_call("mem_kv", fn, 1, [(mem, "cur")], [g_mem, wxkv], [((m, D), BF16), ((m, XH * XD), BF16), ((m, XH * XD), BF16)])


def _merge_xattn(o, gl, conv_out, x, wmo, wo, g_x, wxq, kx, vx, wxo, n):
    t = x.shape[0]

    def fn(i, rows, consts, outs, accs, scr):
        w_mo, w_o, g, w_xq, k_x, v_x, w_xo = consts
        ob = rows[0][...].astype(BF16)
        outs[7][...] = ob
        mla = _mm(ob, w_mo[...])
        outs[0][...] = mla
        merged = (_sig(rows[1][:, 0:D]) * rows[2][...] + _sig(rows[1][:, D:]) * mla).astype(BF16)
        outs[1][...] = merged
        h1 = rows[3][...] + _mm(merged, w_o[...])
        outs[2][...] = h1
        u1 = _rms(h1, g[...])[0].astype(BF16)
        outs[3][...] = u1
        qx = (_mm(u1, w_xq[...]) * X_SCALE).astype(BF16)
        outs[4][...] = qx
        for h in range(XH):
            ln = slice(h * XD, (h + 1) * XD)
            s = _mm_nt(qx[:, ln], k_x[:, ln])
            e = jnp.exp(s - jnp.max(s, axis=-1, keepdims=True))
            p = e / jnp.sum(e, axis=-1, keepdims=True)
            outs[5][:, ln] = _mm(p, v_x[:, ln]).astype(BF16)
        outs[6][...] = h1 + _mm(outs[5][...], w_xo[...])

    outs = [((t, D), F32), ((t, D), BF16), ((t, D), F32), ((t, D), BF16), ((t, XH * XD), BF16), ((t, XH * XD), BF16),
            ((t, D), F32), ((t, NH * HP), BF16)]
    return _row_call("merge_xattn", fn, n, [(a, "cur") for a in (o, gl, conv_out, x)], [wmo, wo, g_x, wxq, kx, vx, wxo], outs)


def _mlp_loss(h2, target, g_mlp, w1, w2, g_fin, n):
    t = h2.shape[0]
    nck = DFF // D

    def fn(i, rows, consts, outs, accs, scr):
        g_m, w_1, w_2, g_f = consts
        h = rows[0][...]
        u2, xh2, r2 = _rms(h, g_m[...])
        ub = u2.astype(BF16)
        outs[0][...] = ub
        h3 = h
        a1 = []
        for c in range(nck):
            ck = slice(c * D, (c + 1) * D)
            a = _mm(ub, w_1[:, ck])
            a1.append(a)
            rl = jnp.maximum(a, 0.0)
            rb = (rl * rl).astype(BF16)
            outs[1][:, ck] = rb
            h3 = h3 + _mm(rb, w_2[ck, :])
        y, xh3, r3 = _rms(h3, g_f[...])
        err = y - rows[1][...]
        accs[0][...] += jnp.sum(jnp.sum(err * err, axis=1, keepdims=True), axis=0, keepdims=True) * (0.5 / D)
        dh3, dgf = _rms_bwd(err * (1.0 / D), xh3, r3, g_f[...])
        accs[1][...] += dgf
        db = dh3.astype(BF16)
        outs[3][...] = db
        du2 = jnp.zeros_like(h)
        for c in range(nck):
            ck = slice(c * D, (c + 1) * D)
            da = (_mm_nt(db, w_2[ck, :]) * (2.0 * jnp.maximum(a1[c], 0.0))).astype(BF16)
            outs[2][:, ck] = da
            du2 = du2 + _mm_nt(da, w_1[:, ck])
        dx2, dgm = _rms_bwd(du2, xh2, r2, g_m[...])
        accs[2][...] += dgm
        outs[4][...] = dh3 + dx2

    outs = [((t, D), BF16), ((t, DFF), BF16), ((t, DFF), BF16), ((t, D), BF16), ((t, D), F32)]
    accs = [((1, 1), F32), ((1, D), F32), ((1, D), F32)]
    return _row_call("mlp_loss", fn, n, [(h2, "cur"), (target, "cur")], [g_mlp, w1, w2, g_fin], outs, accs)


def _merge_xattn_bwd(dh2, h1, qx, gl, conv_out, mla, o, wxo, kx, vx, wxq, g_x, wo, wmo, n):
    t = dh2.shape[0]
    m = kx.shape[0]

    def fn(i, rows, consts, outs, accs, scr):
        w_xo, k_x, v_x, w_xq, g, w_o, w_mo = consts
        d2 = rows[0][...]
        d2b = d2.astype(BF16)
        outs[7][...] = d2b
        dox = _mm_nt(d2b, w_xo[...]).astype(BF16)
        q = rows[2][...]
        dq = []
        for h in range(XH):
            ln = slice(h * XD, (h + 1) * XD)
            qh, kh, vh, doh = q[:, ln], k_x[:, ln], v_x[:, ln], dox[:, ln]
            s = _mm_nt(qh, kh)
            e = jnp.exp(s - jnp.max(s, axis=-1, keepdims=True))
            p = e / jnp.sum(e, axis=-1, keepdims=True)
            dp = _mm_nt(doh, vh)
            ds = p * (dp - jnp.sum(p * dp, axis=-1, keepdims=True))
            dq.append(_mm(ds, kh) * X_SCALE)
            accs[0][:, ln] += _mm_tn(ds, qh)
            accs[1][:, ln] += _mm_tn(p, doh)
        dqx = jnp.concatenate(dq, axis=1).astype(BF16)
        outs[0][...] = dqx
        _, xh1, r1 = _rms(rows[1][...], g[...])
        dx1, dg = _rms_bwd(_mm_nt(dqx, w_xq[...]), xh1, r1, g[...])
        accs[2][...] += dg
        d1 = d2 + dx1
        outs[1][...] = d1
        d1b = d1.astype(BF16)
        outs[8][...] = d1b
        dm = _mm_nt(d1b, w_o[...])
        g0, g1 = _sig(rows[3][:, 0:D]), _sig(rows[3][:, D:])
        outs[2][:, 0:D] = (dm * rows[4][...] * g0 * (1.0 - g0)).astype(BF16)
        outs[2][:, D:] = (dm * rows[5][...] * g1 * (1.0 - g1)).astype(BF16)
        outs[3][...] = (dm * g0).astype(BF16)
        dmla = (dm * g1).astype(BF16)
        outs[4][...] = dmla
        do = _mm_nt(dmla, w_mo[...])
        outs[5][...] = do.astype(BF16)
        prod = do * rows[6][...]
        for h in range(NH):
            outs[6][h] = jnp.sum(prod[:, h * HP:(h + 1) * HP], axis=-1, keepdims=True)

    outs = [((t, XH * XD), BF16), ((t, D), F32), ((t, 2 * D), BF16), ((t, D), BF16), ((t, D), BF16), ((t, NH * HP), BF16),
            ((NH, t, 1), F32), ((t, D), BF16), ((t, D), BF16)]
    accs = [((m, XH * XD), F32), ((m, XH * XD), F32), ((1, D), F32)]
    return _row_call("merge_xattn_bwd", fn, n, [(a, "cur") for a in (dh2, h1, qx, gl, conv_out, mla, o)],
                     [wxo, kx, vx, wxq, g_x, wo, wmo], outs, accs)


def _mla_prep_bwd(dq, dk, dv, cq, ckv, tc, tsa, tsb, gq, wuq, gkv, wk, wv, n):
    t = dq.shape[0]

    def fn(i, rows, consts, outs, accs, scr):
        g_q, w_q, g_kv, w_k, w_v = consts
        c, sa, sb = rows[5][...], rows[6][...], rows[7][...]
        dkr = jnp.zeros((rows[0].shape[0], HP), F32)
        for h in range(NH):
            ln = slice(h * HP, (h + 1) * HP)
            outs[0][:, ln] = _rope_bwd(rows[0][:, ln] * MLA_SCALE, c, sa, sb).astype(BF16)
            dkr = dkr + rows[1][:, ln]
        lane = lax.broadcasted_iota(jnp.int32, dkr.shape, 1)
        outs[5][...] = jnp.where((lane >= NOPE) & (lane < NOPE + ROPE), _rope_bwd(dkr, c, sa, sb), 0.0).astype(BF16)
        dkb, dvb = rows[1][...].astype(BF16), rows[2][...].astype(BF16)
        outs[1][...] = dkb
        outs[2][...] = dvb
        _, xq, rq = _rms(rows[3][...], g_q[...])
        dcq, dgq = _rms_bwd(_mm_nt(outs[0][...], w_q[...]), xq, rq, g_q[...])
        outs[3][...] = dcq.astype(BF16)
        accs[0][...] += dgq
        _, xk, rk = _rms(rows[4][...], g_kv[...])
        dckv, dgk = _rms_bwd(_mm_nt(dkb, w_k[...]) + _mm_nt(dvb, w_v[...]), xk, rk, g_kv[...])
        outs[4][...] = dckv.astype(BF16)
        accs[1][...] += dgk

    outs = [((t, NH * HP), BF16)] * 3 + [((t, QL), BF16), ((t, KL), BF16), ((t, HP), BF16)]
    return _row_call("mla_prep_bwd", fn, n, [(a, "cur") for a in (dq, dk, dv, cq, ckv, tc, tsa, tsb)],
                     [gq, wuq, gkv, wk, wv], outs, [((1, QL), F32), ((1, KL), F32)])


def _conv_out_bwd(dco, zc, wco, lng, lnb, n):
    t = zc.shape[0]

    def fn(i, rows, consts, outs, accs, scr):
        wo, lg, lb = consts
        z = rows[1][...]
        mu = jnp.mean(z, axis=-1, keepdims=True)
        dlt = z - mu
        rs = lax.rsqrt(jnp.mean(dlt * dlt, axis=-1, keepdims=True) + EPS)
        xh = dlt * rs
        zn = xh * lg[...] + lb[...]
        sg = _sig(zn)
        outs[0][...] = (zn * sg).astype(BF16)
        dzn = _mm_nt(rows[0][...], wo[...]) * (sg * (1.0 + zn * (1.0 - sg)))
        accs[0][...] += jnp.sum(dzn * xh, axis=0, keepdims=True)
        accs[1][...] += jnp.sum(dzn, axis=0, keepdims=True)
        dxh = dzn * lg[...]
        dzc = rs * (dxh - jnp.mean(dxh, axis=-1, keepdims=True) - xh * jnp.mean(dxh * xh, axis=-1, keepdims=True))
        outs[1][...] = dzc
        accs[2][...] += jnp.sum(dzc, axis=0, keepdims=True)

    return _row_call("conv_out_bwd", fn, n, [(dco, "cur"), (zc, "cur")], [wco, lng, lnb],
                     [((t, CC), BF16), ((t, CC), F32)], [((1, CC), F32)] * 3)


def _conv_glu_bwd(dzc, a, gt, cw, n):
    t = a.shape[0]
    r = t // n

    def fn(i, rows, consts, outs, accs, scr):
        w = consts[0]
        zz, dd = scr
        _fill_glu_window(i, rows[2], rows[3], rows[4], rows[5], zz)
        dd[0:r, :] = rows[0][...]
        dd[r:, :] = jnp.where(i < n - 1, rows[1][0:HALO, :], 0.0)
        rb = min(CONV_ROWS, r)
        for c in range(CC // HP):
            ln = slice(c * HP, (c + 1) * HP)
            for base in range(0, r, rb):
                here = slice(base, base + rb)
                dcur = dd[here, ln]
                for o, win in _windows(zz, base, ln, rb, [HALO - (CW - 1) + j for j in range(CW)]):
                    j = o - (HALO - (CW - 1))
                    accs[0][j:j + 1, ln] += jnp.sum(dcur * win, axis=0, keepdims=True)
                acc = jnp.zeros((rb, HP), F32)
                for o, win in _windows(dd, base, ln, rb, [CW - 1 - j for j in range(CW)]):
                    j = CW - 1 - o
                    acc = acc + w[j:j + 1, ln] * win
                sg = _sig(rows[3][here, ln])
                outs[0][here, ln] = (acc * sg).astype(BF16)
                outs[0][here, CC + c * HP:CC + (c + 1) * HP] = (acc * rows[2][here, ln] * sg * (1.0 - sg)).astype(BF16)

    return _row_call("conv_glu_bwd", fn, n, [(dzc, "cur"), (dzc, "next"), (a, "cur"), (gt, "cur"), (a, "prev"), (gt, "prev")],
                     [cw], [((t, 2 * CC), BF16)], [((HALO, CC), F32)],
                     scratch=[pltpu.VMEM((r + HALO, CC), F32), pltpu.VMEM((r + HALO, CC), F32)])


def _in_proj_bwd(pieces, dh1, x, wp, g_mix, n, exchange):
    t = x.shape[0]
    offs = (P_A, P_Q, P_KV, P_KR, P_GL, P_END)

    def fn(i, rows, consts, outs, accs, scr):
        w, g = consts
        du = jnp.zeros((rows[0].shape[0], D), F32)
        for k in range(5):
            du = du + _mm_nt(rows[k][...], w[:, offs[k]:offs[k + 1]])
        _, xh, r = _rms(rows[6][...], g[...])
        dx, dg = _rms_bwd(du, xh, r, g[...])
        accs[0][...] += dg
        outs[0][...] = rows[5][...] + dx

    return _row_call("in_proj_bwd", fn, n, [(a, "cur") for a in list(pieces) + [dh1, x]], [wp, g_mix],
                     [((t, D), F32)], [((1, D), F32)], exchange=exchange)


def _mem_bwd(mem, dkx, dvx, g_mem, wxkv):
    m = mem.shape[0]

    def fn(i, rows, consts, outs, accs, scr):
        g, w = consts
        dkv = jnp.concatenate([rows[1][...], rows[2][...]], axis=1).astype(BF16)
        outs[0][...] = dkv
        _, xh, _ = _rms(rows[0][...], g[...])
        accs[0][...] += jnp.sum(_mm_nt(dkv, w[...]) * xh, axis=0, keepdims=True)

    return _row_call("mem_bwd", fn, 1, [(mem, "cur"), (dkx, "cur"), (dvx, "cur")], [g_mem, wxkv],
                     [((m, 2 * XH * XD), BF16)], [((1, D), F32)])


def _dw(name, xs, dy):
    t, k = xs.shape
    nn = dy.shape[1]
    tk, tn, tt = min(k, 1024), min(nn, 1024), min(t, DW_TILE)

    def body(x_ref, dy_ref, o_ref):
        @pl.when(pl.program_id(2) == 0)
        def _():
            o_ref[...] = jnp.zeros(o_ref.shape, F32)
        o_ref[...] += lax.dot_general(x_ref[...], dy_ref[...], (((0,), (0,)), ((), ())), preferred_element_type=F32)

    return pl.pallas_call(
        body, name=name, grid=(k // tk, nn // tn, t // tt),
        in_specs=[pl.BlockSpec((tt, tk), lambda a, b, c: (c, a)), pl.BlockSpec((tt, tn), lambda a, b, c: (c, b))],
        out_specs=pl.BlockSpec((tk, tn), lambda a, b, c: (a, b)),
        out_shape=jax.ShapeDtypeStruct((k, nn), F32),
        compiler_params=pltpu.CompilerParams(dimension_semantics=("arbitrary", "arbitrary", "arbitrary"), vmem_limit_bytes=VMEM_LIMIT),
    )(xs, dy)


def _all_gather_packed(shard):
    rws = shard.shape[0]

    def body(x_ref, out_ref, send_sems, recv_sems, local_sem):
        x, y, c = _coords()
        me, sibling = (x, y, c), (x, y, 1 - c)
        chips = [(1 - x, y), (x, 1 - y), (1 - x, 1 - y)]

        def slot(px, py, pc):
            return out_ref.at[4 * px + 2 * py + pc]

        def copy(k, block, to, src=None):
            return pltpu.make_async_remote_copy(
                src_ref=slot(*block) if src is None else src, dst_ref=slot(*block),
                send_sem=send_sems.at[k], recv_sem=recv_sems.at[k], device_id=to, device_id_type=MESH)

        mine = pltpu.make_async_copy(x_ref, slot(*me), local_sem)
        mine.start()
        first = [copy(0, me, sibling, src=x_ref)] + [copy(1 + j, me, (*chip, c), src=x_ref) for j, chip in enumerate(chips)]
        for cp in first:
            cp.start()
        passed = [copy(4 + j, (*chip, c), sibling) for j, chip in enumerate(chips)]
        for j, chip in enumerate(chips):
            copy(1 + j, (*chip, c), me).wait_recv()
            passed[j].start()
        copy(0, sibling, me).wait_recv()
        for j, chip in enumerate(chips):
            copy(4 + j, (*chip, 1 - c), me).wait_recv()
        for cp in first + passed:
            cp.wait_send()
        mine.wait()

    return pl.pallas_call(
        body, name="all_gather_weights",
        out_shape=jax.ShapeDtypeStruct((N_DEV, rws, 128), shard.dtype),
        in_specs=[pl.BlockSpec(memory_space=pl.ANY)], out_specs=pl.BlockSpec(memory_space=pl.ANY),
        scratch_shapes=[pltpu.SemaphoreType.DMA((7,)), pltpu.SemaphoreType.DMA((7,)), pltpu.SemaphoreType.DMA],
    )(shard)


def _adam(w, g, m, v):
    m = ADAM_B1 * m + (1.0 - ADAM_B1) * g
    v = ADAM_B2 * v + (1.0 - ADAM_B2) * (g * g)
    m_hat = m / (1.0 - ADAM_B1 ** ADAM_STEP)
    v_hat = v / (1.0 - ADAM_B2 ** ADAM_STEP)
    return -ADAM_LR * (m_hat / (jnp.sqrt(v_hat) + ADAM_EPS) + ADAM_WD * w), m, v


def _small_allreduce_adam(part, w, m, v):
    shape = part.shape

    def body(p_ref, w_ref, m_ref, v_ref, g_ref, d_ref, nm_ref, nv_ref, buf, send_sems, recv_sems):
        x, y, c = _coords()
        me = 4 * x + 2 * y + c
        buf[0] = p_ref[...]
        cps = []
        for j in range(1, N_DEV):
            jx, jy, jc = j >> 2, (j >> 1) & 1, j & 1
            peer = (1 - x if jx else x, 1 - y if jy else y, 1 - c if jc else c)
            cps.append(pltpu.make_async_remote_copy(src_ref=p_ref, dst_ref=buf.at[j], send_sem=send_sems.at[j - 1],
                                                    recv_sem=recv_sems.at[j - 1], device_id=peer, device_id_type=MESH))
        for cp in cps:
            cp.start()
        for cp in cps:
            cp.wait()
        g = buf[me]
        for d in range(1, N_DEV):
            g = g + buf[d ^ me]
        g_ref[...] = g
        d_ref[...], nm_ref[...], nv_ref[...] = _adam(w_ref[...], g, m_ref[...], v_ref[...])

    vm = pl.BlockSpec(memory_space=pltpu.VMEM)
    return pl.pallas_call(
        body, name="small_allreduce_adam", out_shape=[jax.ShapeDtypeStruct(shape, F32)] * 4,
        in_specs=[vm] * 4, out_specs=[vm] * 4,
        scratch_shapes=[pltpu.VMEM((N_DEV,) + shape, F32), pltpu.SemaphoreType.DMA((7,)), pltpu.SemaphoreType.DMA((7,))],
    )(part, w, m, v)


def _sum_adam(name, parts, w, m, v):
    rws = w.shape[0]
    tile = max(d for d in range(16, PACK_TILE + 1, 16) if rws % d == 0)

    def body(p_ref, w_ref, m_ref, v_ref, g_ref, d_ref, nm_ref, nv_ref):
        g = p_ref[0]
        for d in range(1, N_DEV):
            g = g + p_ref[d]
        g_ref[...] = g
        d_ref[...], nm_ref[...], nv_ref[...] = _adam(w_ref[...], g, m_ref[...], v_ref[...])

    spec = pl.BlockSpec((tile, 128), lambda i: (i, 0))
    return pl.pallas_call(
        body, name=name, grid=(rws // tile,), in_specs=[pl.BlockSpec((N_DEV, tile, 128), lambda i: (0, i, 0))] + [spec] * 3,
        out_specs=[spec] * 4, out_shape=[jax.ShapeDtypeStruct((rws, 128), F32)] * 4,
        compiler_params=pltpu.CompilerParams(dimension_semantics=("arbitrary",), vmem_limit_bytes=VMEM_LIMIT),
    )(parts, w, m, v)


ROW_SHARDED = ("w_out", "w_xq", "w_xkv", "w_mlp2")
SMALL = ("norm_mix_g", "conv_b", "conv_ln_g", "conv_ln_b", "q_norm_g", "kv_norm_g", "norm_xattn_g", "norm_mem_g", "norm_mlp_g", "final_norm_g")
GATHER_FIRST = ("w_in", "conv_w", "w_conv_out", "w_uq", "w_ukv")
GATHER_LATE = ("w_mla_out", "w_out", "w_xq", "w_xkv", "w_xo", "w_mlp1", "w_mlp2")
REDUCE_EARLY = ("w_mlp1", "w_mlp2", "conv_w", "w_conv_out", "w_mla_out", "w_out", "w_xq", "w_xkv", "w_xo")
REDUCE_LAST = ("w_in", "w_uq", "w_ukv")
ROW_ALIGN = 16


def _padded(k, nn):
    return -(-k // ROW_ALIGN) * ROW_ALIGN, -(-nn // 128) * 128


def _pack_rows(a):
    k, nn = a.shape[-2:]
    kp, np_ = _padded(k, nn)
    if (kp, np_) != (k, nn):
        a = jnp.pad(a, [(0, 0)] * (a.ndim - 2) + [(0, kp - k), (0, np_ - nn)])
    return jnp.concatenate([a[..., b * 128:(b + 1) * 128] for b in range(np_ // 128)], axis=-2)


def _unpack_rows(p, k, nn):
    kp, np_ = _padded(k, nn)
    a = jnp.concatenate([p[..., b * kp:(b + 1) * kp, :] for b in range(np_ // 128)], axis=-1)
    return a[..., :k, :nn]


def _pack_group(arrays):
    return jnp.concatenate([_pack_rows(a) for a in arrays], axis=-2)


def _unpack_group(p, shapes):
    out, off = [], 0
    for k, nn in shapes:
        kp, np_ = _padded(k, nn)
        rws = kp * np_ // 128
        out.append(_unpack_rows(p[..., off:off + rws, :], k, nn))
        off += rws
    return out


def _to_full(name, stacked):
    _, k, nn = stacked.shape
    if name in ROW_SHARDED:
        return stacked.reshape(N_DEV * k, nn)
    return jnp.transpose(stacked, (1, 0, 2)).reshape(k, N_DEV * nn)


def _to_stacked(name, full, shard_shape):
    k, nn = shard_shape
    if name in ROW_SHARDED:
        return full.reshape(N_DEV, k, nn)
    return jnp.transpose(full.reshape(k, N_DEV, nn), (1, 0, 2))


def _pack_small(vals):
    flat = jnp.concatenate([v.reshape(-1) for v in vals])
    rws = flat.shape[0] // 128
    return jnp.pad(flat, (0, (-(-rws // 8) * 8 - rws) * 128)).reshape(-1, 128)


def _unpack_small(p, sizes):
    flat = p.reshape(-1)
    out, off = [], 0
    for s in sizes:
        out.append(flat[off:off + s])
        off += s
    return out


def _head_pad(w, real, axis):
    shp = list(w.shape)
    shp[axis:axis + 1] = [NH, real]
    w = w.reshape(shp)
    pad = [(0, 0)] * w.ndim
    pad[axis + 1] = (0, HP - real)
    w = jnp.pad(w, pad)
    shp[axis:axis + 2] = [NH * HP]
    return w.reshape(shp)


def _head_unpad(w, lo, real, axis):
    shp = list(w.shape)
    shp[axis:axis + 1] = [NH, HP]
    w = lax.slice_in_dim(w.reshape(shp), lo, lo + real, axis=axis + 1)
    shp[axis:axis + 2] = [NH * real]
    return w.reshape(shp)


def kernel(x, mem, positions, norm_mix_g, w_in, conv_w, conv_b, conv_ln_g, conv_ln_b, w_conv_out, q_norm_g, w_uq, kv_norm_g, w_ukv, w_mla_out, w_out, norm_xattn_g, norm_mem_g, w_xq, w_xkv, w_xo, norm_mlp_g, w_mlp1, w_mlp2, final_norm_g, loss_target, m_norm_mix_g, m_w_in, m_conv_w, m_conv_b, m_conv_ln_g, m_conv_ln_b, m_w_conv_out, m_q_norm_g, m_w_uq, m_kv_norm_g, m_w_ukv, m_w_mla_out, m_w_out, m_norm_xattn_g, m_norm_mem_g, m_w_xq, m_w_xkv, m_w_xo, m_norm_mlp_g, m_w_mlp1, m_w_mlp2, m_final_norm_g, v_norm_mix_g, v_w_in, v_conv_w, v_conv_b, v_conv_ln_g, v_conv_ln_b, v_w_conv_out, v_q_norm_g, v_w_uq, v_kv_norm_g, v_w_ukv, v_w_mla_out, v_w_out, v_norm_xattn_g, v_norm_mem_g, v_w_xq, v_w_xkv, v_w_xo, v_norm_mlp_g, v_w_mlp1, v_w_mlp2, v_final_norm_g):
    args = dict(locals())
    t = x.shape[1]
    n = t // min(ROW_TILE, t)
    xs, mems, tgt = x[0], mem[0], loss_target[0]

    def pack_shards(prefix, group, dtype):
        return _pack_group([args[prefix + k][0].astype(dtype) for k in group])

    def shapes(group):
        return [args[k].shape[1:] for k in group]

    def unpack_full(gathered, group):
        return {k: _to_full(k, s) for k, s in zip(group, _unpack_group(gathered, shapes(group)))}

    full = unpack_full(_all_gather_packed(pack_shards("", GATHER_FIRST, BF16)), GATHER_FIRST)

    wi = full["w_in"]
    kr_slot = jnp.pad(wi[:, P_KR:P_KR + ROPE], ((0, 0), (NOPE, HP - NOPE - ROPE)))
    wp = jnp.concatenate([wi[:, :P_KR], kr_slot, wi[:, P_KR + ROPE:]], axis=1)
    cw = jnp.pad(full["conv_w"].astype(F32), ((0, HALO - CW), (0, 0)))
    wuq = _head_pad(full["w_uq"], NOPE + ROPE, 1)
    ukv = full["w_ukv"].reshape(KL, NH, NOPE + VD)
    wk = _head_pad(ukv[:, :, :NOPE].reshape(KL, NH * NOPE), NOPE, 1)
    wv = _head_pad(ukv[:, :, NOPE:].reshape(KL, NH * VD), VD, 1)

    inv_freq = THETA ** (-jnp.arange(ROPE // 2, dtype=F32) / (ROPE // 2))
    ang = positions[0].astype(F32)[:, None] * inv_freq
    cs, sn, zr = jnp.cos(ang), jnp.sin(ang), jnp.zeros((t, ROPE // 2), F32)
    tail = jnp.zeros((t, HP - NOPE - ROPE), F32)
    tc = jnp.concatenate([jnp.ones((t, NOPE), F32), cs, cs, tail], axis=1)
    tsa = jnp.concatenate([jnp.zeros((t, NOPE), F32), -sn, zr, tail], axis=1)
    tsb = jnp.concatenate([jnp.zeros((t, NOPE), F32), zr, sn, tail], axis=1)

    u0, a, gt, cq, ckv, krp, gl = _in_proj(xs, norm_mix_g, wp, n)
    zc, conv_out = _conv_branch(a, gt, cw, conv_b, conv_ln_g, conv_ln_b, full["w_conv_out"], n)
    qh, kh, vh, cqn, ckvn = _mla_prep(cq, ckv, krp, tc, tsa, tsb, q_norm_g, wuq, kv_norm_g, wk, wv, n)
    o, lse, gathered = _flash_fwd(qh, kh, vh, exchange=(pack_shards("", GATHER_LATE, BF16), True))
    full.update(unpack_full(gathered, GATHER_LATE))
    wmo = _head_pad(full["w_mla_out"], VD, 0)
    memn, kx, vx = _mem_kv(mems, norm_mem_g, full["w_xkv"])
    mla, merged, h1, u1, qx, ox, h2, ob = _merge_xattn(o, gl, conv_out, xs, wmo, full["w_out"], norm_xattn_g, full["w_xq"],
                                                        kx, vx, full["w_xo"], n)
    gfin = final_norm_g.reshape(1, D)
    u2, rl2, da1, dh3b, dh2, loss_p, dg_fin, dg_mlp = _mlp_loss(h2, tgt, norm_mlp_g, full["w_mlp1"], full["w_mlp2"], gfin, n)

    (dqx, dh1, dgl, dco, dmla, dob, delta, dh2b, dh1b, dkx, dvx, dg_x) = _merge_xattn_bwd(
        dh2, h1, qx, gl, conv_out, mla, o, full["w_xo"], kx, vx, full["w_xq"], norm_xattn_g, full["w_out"], wmo, n)
    gfull = {"w_mlp1": _dw("dw_mlp1", u2, da1), "w_mlp2": _dw("dw_mlp2", rl2, dh3b)}

    def stacked(group):
        return _pack_group([_to_stacked(k, gfull[k], args[k].shape[1:]) for k in group])

    zs, dzc, dg_lng, dg_lnb, dg_cb = _conv_out_bwd(dco, zc, full["w_conv_out"], conv_ln_g, conv_ln_b, n)
    dci, dcw = _conv_glu_bwd(dzc, a, gt, cw, n)
    dkv, dg_mem = _mem_bwd(mems, dkx, dvx, norm_mem_g, full["w_xkv"])
    gfull.update({
        "conv_w": dcw[:CW],
        "w_conv_out": _dw("dw_conv_out", zs, dco),
        "w_mla_out": _head_unpad(_dw("dw_mla_out", ob, dmla), 0, VD, 0),
        "w_out": _dw("dw_out", merged, dh1b),
        "w_xq": _dw("dw_xq", u1, dqx),
        "w_xkv": _dw("dw_xkv", memn, dkv),
        "w_xo": _dw("dw_xo", ox, dh2b),
    })
    dk, dv, dq, parts_early = _flash_bwd(qh, kh, vh, dob, lse.reshape(NH, 1, t), delta.reshape(NH, 1, t),
                                         exchange=(stacked(REDUCE_EARLY), False))
    dqp, dkb, dvb, dcq, dckv, dkrp, dg_q, dg_kv = _mla_prep_bwd(dq, dk, dv, cq, ckv, tc, tsa, tsb, q_norm_g, wuq, kv_norm_g, wk, wv, n)
    pieces = (dci, dcq, dckv, dkrp, dgl)
    dwp = [_dw("dw_in_%d" % k, u0, p) for k, p in enumerate(pieces)]
    gfull["w_in"] = jnp.concatenate([dwp[0], dwp[1], dwp[2], dwp[3][:, NOPE:NOPE + ROPE], dwp[4]], axis=1)
    gfull["w_uq"] = _head_unpad(_dw("dw_uq", cqn, dqp), 0, NOPE + ROPE, 1)
    gk = _head_unpad(_dw("dw_uk", ckvn, dkb), 0, NOPE, 1).reshape(KL, NH, NOPE)
    gv = _head_unpad(_dw("dw_uv", ckvn, dvb), 0, VD, 1).reshape(KL, NH, VD)
    gfull["w_ukv"] = jnp.concatenate([gk, gv], axis=2).reshape(KL, NH * (NOPE + VD))
    grad_x, dg_mix, parts_last = _in_proj_bwd(pieces, dh1, xs, wp, norm_mix_g, n, (stacked(REDUCE_LAST), False))

    big = [{}, {}, {}, {}]
    for name, group, parts in (("adam_early", REDUCE_EARLY, parts_early), ("adam_last", REDUCE_LAST, parts_last)):
        res = _sum_adam(name, parts, pack_shards("", group, F32), pack_shards("m_", group, F32), pack_shards("v_", group, F32))
        for kind in range(4):
            for k, val in zip(group, _unpack_group(res[kind], shapes(group))):
                big[kind][k] = val[None]

    small_g = {"norm_mix_g": dg_mix, "conv_b": dg_cb, "conv_ln_g": dg_lng, "conv_ln_b": dg_lnb, "q_norm_g": dg_q,
               "kv_norm_g": dg_kv, "norm_xattn_g": dg_x, "norm_mem_g": dg_mem, "norm_mlp_g": dg_mlp, "final_norm_g": dg_fin}
    small_sizes = [int(np.prod(args[k].shape)) for k in SMALL] + [128]
    zero_slot = jnp.zeros((128,), F32)
    small_out = _small_allreduce_adam(
        _pack_small([small_g[k] for k in SMALL] + [jnp.pad(loss_p.reshape(-1), (0, 127))]),
        _pack_small([args[k] for k in SMALL] + [zero_slot]), _pack_small([args["m_" + k] for k in SMALL] + [zero_slot]),
        _pack_small([args["v_" + k] for k in SMALL] + [zero_slot]))

    small = [dict(zip(SMALL, [s.reshape(args[k].shape) for k, s in zip(SMALL, _unpack_small(o_, small_sizes))])) for o_ in small_out]
    loss = _unpack_small(small_out[0], small_sizes)[-1][0]
    order = ("norm_mix_g", "w_in", "conv_w", "conv_b", "conv_ln_g", "conv_ln_b", "w_conv_out", "q_norm_g", "w_uq", "kv_norm_g",
             "w_ukv", "w_mla_out", "w_out", "norm_xattn_g", "norm_mem_g", "w_xq", "w_xkv", "w_xo", "norm_mlp_g", "w_mlp1",
             "w_mlp2", "final_norm_g")
    res = [loss, grad_x[None]]
    for kind in range(4):
        res += [big[kind][k] if k in big[kind] else small[kind][k] for k in order]
    return tuple(res)
```

```python
import functools

import jax
import jax.numpy as jnp
import numpy as np
from jax import lax
from jax.experimental import pallas as pl
from jax.experimental.pallas import tpu as pltpu

F32, BF16 = jnp.float32, jnp.bfloat16
MESH = pl.DeviceIdType.MESH

N_DEV = 8
D = 1024
CC = D // 2
CW = 31
HALO = 32
NH = 8
NOPE, ROPE, VD = D // 16, D // 32, D // 16
QL, KL = 3 * D // 8, D // 4
HP = 128
XH, XD = 4, D // 8
DFF = 4 * D
EPS = 1e-6
THETA = 10000.0
MLA_SCALE = float((NOPE + ROPE) ** -0.5)
X_SCALE = float(XD ** -0.5)
NEG = -1e30
P_A, P_G, P_Q, P_KV, P_KR, P_GL, P_END = 0, CC, 2 * CC, 2 * CC + QL, 2 * CC + QL + KL, 2 * CC + QL + KL + HP, 2 * CC + QL + KL + HP + 2 * D

ADAM_LR, ADAM_B1, ADAM_B2, ADAM_EPS, ADAM_WD, ADAM_STEP = 0.001, 0.9, 0.999, 1e-08, 0.01, 10

ROW_TILE = 256
ATT_TILE = 512
HG = 2
HG_FWD = 4
ATT_CHUNK = 32
CONV_ROWS = 128
DW_TILE = 1024
PACK_TILE = 1536
VMEM_LIMIT = 56 * 1024 * 1024


def _mm(a, w):
    return jnp.dot(a.astype(BF16), w, preferred_element_type=F32)


def _mm_nt(a, w):
    return lax.dot_general(a.astype(BF16), w, (((1,), (1,)), ((), ())), preferred_element_type=F32)


def _mm_tn(a, b):
    return lax.dot_general(a.astype(BF16), b.astype(BF16), (((0,), (0,)), ((), ())), preferred_element_type=F32)


def _rms(x, g):
    r = lax.rsqrt(jnp.mean(x * x, axis=-1, keepdims=True) + EPS)
    xh = x * r
    return xh * g, xh, r


def _rms_bwd(dy, xh, r, g):
    dxh = dy * g
    dx = r * (dxh - xh * jnp.mean(dxh * xh, axis=-1, keepdims=True))
    return dx, jnp.sum(dy * xh, axis=0, keepdims=True)


def _sig(x):
    return 1.0 / (1.0 + jnp.exp(-x))


def _coords():
    return lax.axis_index("x"), lax.axis_index("y"), lax.axis_index("c")


def _exchange_ops(src_ref, gather, dst_ref, send_sems, recv_sems, local_sem):
    x, y, c = _coords()
    me = 4 * x + 2 * y + c
    ops = [pltpu.make_async_copy(src_ref if gather else src_ref.at[me], dst_ref.at[me], local_sem)]
    for j in range(1, N_DEV):
        px, py, pc = (1 - x if j & 4 else x), (1 - y if j & 2 else y), (1 - c if j & 1 else c)
        ops.append(pltpu.make_async_remote_copy(
            src_ref=src_ref if gather else src_ref.at[4 * px + 2 * py + pc], dst_ref=dst_ref.at[me],
            send_sem=send_sems.at[j - 1], recv_sem=recv_sems.at[j - 1], device_id=(px, py, pc), device_id_type=MESH))
    return ops


def _exchange_hook(first, last, gather, refs):
    @pl.when(first)
    def _():
        for op in _exchange_ops(refs[0], gather, *refs[1:]):
            op.start()

    @pl.when(last)
    def _():
        for op in _exchange_ops(refs[0], gather, *refs[1:]):
            op.wait()


def _exchange_shapes(exchange):
    if exchange is None:
        return [], [], [], []
    arr, gather = exchange
    shape = (N_DEV,) + arr.shape if gather else arr.shape
    any_spec = pl.BlockSpec(memory_space=pl.ANY)
    sems = [pltpu.SemaphoreType.DMA((N_DEV - 1,)), pltpu.SemaphoreType.DMA((N_DEV - 1,)), pltpu.SemaphoreType.DMA]
    return [any_spec], [any_spec], [jax.ShapeDtypeStruct(shape, arr.dtype)], sems


def _row_call(name, fn, n, rows, consts, outs, accs=(), scratch=(), exchange=None):
    def row_spec(shape, mode):
        r = shape[-2] // n
        if mode == "cur":
            f = lambda i: i
        elif mode == "prev":
            f = lambda i: jnp.maximum(i - 1, 0)
        else:
            f = lambda i: jnp.minimum(i + 1, n - 1)
        if len(shape) == 2:
            return pl.BlockSpec((r, shape[1]), lambda i: (f(i), 0))
        return pl.BlockSpec((shape[0], r, shape[2]), lambda i: (0, f(i), 0))

    def whole_spec(shape, single):
        nd = len(shape)
        if single:
            return pl.BlockSpec(shape, lambda i: (0,) * nd, pipeline_mode=pl.Buffered(1))
        return pl.BlockSpec(shape, lambda i: (0,) * nd)

    nr, nc, no, na, ns = len(rows), len(consts), len(outs), len(accs), len(scratch)
    x_in, x_out, x_shape, x_sems = _exchange_shapes(exchange)
    nx = len(x_in)

    def body(*refs):
        i = pl.program_id(0)
        row_refs, const_refs = refs[:nr], refs[nr:nr + nc]
        o0 = nr + nc + nx
        out_refs, acc_refs = refs[o0:o0 + no], refs[o0 + no:o0 + no + na]
        s0 = o0 + no + na + nx
        if nx:
            _exchange_hook(i == 0, i == n - 1, exchange[1], (refs[nr + nc], refs[s0 - 1]) + tuple(refs[s0 + ns:]))
        if na:
            @pl.when(i == 0)
            def _():
                for a in acc_refs:
                    a[...] = jnp.zeros(a.shape, a.dtype)
        fn(i, row_refs, const_refs, out_refs, acc_refs, refs[s0:s0 + ns])

    res = pl.pallas_call(
        body, name=name, grid=(n,),
        in_specs=[row_spec(a.shape, m) for a, m in rows] + [whole_spec(c.shape, True) for c in consts] + x_in,
        out_specs=[row_spec(s, "cur") for s, _ in outs] + [whole_spec(s, False) for s, _ in accs] + x_out,
        out_shape=[jax.ShapeDtypeStruct(s, d) for s, d in list(outs) + list(accs)] + x_shape,
        scratch_shapes=list(scratch) + x_sems,
        compiler_params=pltpu.CompilerParams(dimension_semantics=("arbitrary",), vmem_limit_bytes=VMEM_LIMIT),
    )(*[a for a, _ in rows], *consts, *([exchange[0]] if nx else []))
    return list(res)


def _in_proj(x, g_mix, wp, n):
    t = x.shape[0]

    def fn(i, rows, consts, outs, accs, scr):
        g, w = consts
        u, _, _ = _rms(rows[0][...], g[...])
        ub = u.astype(BF16)
        outs[0][...] = ub
        for k, (lo, hi) in enumerate(((P_A, P_G), (P_G, P_Q), (P_Q, P_KV), (P_KV, P_KR), (P_KR, P_GL), (P_GL, P_END))):
            outs[1 + k][...] = _mm(ub, w[:, lo:hi])

    outs = [((t, D), BF16), ((t, CC), F32), ((t, CC), F32), ((t, QL), F32), ((t, KL), F32), ((t, HP), F32), ((t, 2 * D), F32)]
    return _row_call("in_proj", fn, n, [(x, "cur")], [g_mix, wp], outs)


def _fill_glu_window(i, a, gt, ap, gtp, zz):
    r = a.shape[0]
    zp = ap[r - HALO:, :] * _sig(gtp[r - HALO:, :])
    zz[0:HALO, :] = jnp.where(i > 0, zp, 0.0)
    zz[HALO:, :] = a[...] * _sig(gt[...])


def _shift_copies(buf, sh):
    rows = buf.shape[0]
    for s in range(8):
        sh[s, 0:rows - s, :] = buf[s:rows, :]


def _window(sh, o, base, rb, ln):
    return sh[o % 8, base + o - o % 8:base + o - o % 8 + rb, ln]


def _windows(buf, base, ln, rb, offsets):
    for s in range(8):
        group = [o for o in offsets if o % 8 == s]
        if group:
            shifted = buf[base + s:base + max(group) + rb, ln]
            for o in group:
                yield o, shifted[o - s:o - s + rb]


def _conv_branch(a, gt, cw, cb, lng, lnb, wco, n):
    t = a.shape[0]
    r = t // n

    def fn(i, rows, consts, outs, accs, scr):
        w, b, lg, lb, wo = consts
        zz, zsh = scr
        _fill_glu_window(i, rows[0], rows[1], rows[2], rows[3], zz)
        _shift_copies(zz, zsh)
        rb = min(CONV_ROWS, r)
        for c in range(CC // HP):
            ln = slice(c * HP, (c + 1) * HP)
            for base in range(0, r, rb):
                acc = jnp.zeros((rb, HP), F32)
                for j in range(CW):
                    acc = acc + w[j:j + 1, ln] * _window(zsh, HALO - (CW - 1) + j, base, rb, ln)
                outs[0][base:base + rb, ln] = acc + b[:, ln]
        zc = outs[0][...]
        mu = jnp.mean(zc, axis=-1, keepdims=True)
        dlt = zc - mu
        rs = lax.rsqrt(jnp.mean(dlt * dlt, axis=-1, keepdims=True) + EPS)
        zn = dlt * rs * lg[...] + lb[...]
        outs[1][...] = _mm(zn * _sig(zn), wo[...])

    return _row_call("conv_branch", fn, n, [(a, "cur"), (gt, "cur"), (a, "prev"), (gt, "prev")],
                     [cw, cb, lng, lnb, wco], [((t, CC), F32), ((t, D), F32)],
                     scratch=[pltpu.VMEM((r + HALO, CC), F32), pltpu.VMEM((8, r + HALO, CC), F32)])


def _rope(v, c, sa, sb):
    return v * c + pltpu.roll(v, HP - ROPE // 2, 1) * sa + pltpu.roll(v, ROPE // 2, 1) * sb


def _rope_bwd(dv, c, sa, sb):
    return dv * c + pltpu.roll(dv * sa, ROPE // 2, 1) + pltpu.roll(dv * sb, HP - ROPE // 2, 1)


def _mla_prep(cq, ckv, krp, tc, tsa, tsb, gq, wuq, gkv, wk, wv, n):
    t = cq.shape[0]

    def fn(i, rows, consts, outs, accs, scr):
        g_q, w_q, g_kv, w_k, w_v = consts
        c, sa, sb = rows[3][...], rows[4][...], rows[5][...]
        cqn = _rms(rows[0][...], g_q[...])[0].astype(BF16)
        ckvn = _rms(rows[1][...], g_kv[...])[0].astype(BF16)
        outs[3][...] = cqn
        outs[4][...] = ckvn
        krr = _rope(rows[2][...], c, sa, sb)
        for h in range(NH):
            ln = slice(h * HP, (h + 1) * HP)
            outs[0][:, ln] = (_rope(_mm(cqn, w_q[:, ln]), c, sa, sb) * MLA_SCALE).astype(BF16)
            outs[1][:, ln] = (_mm(ckvn, w_k[:, ln]) + krr).astype(BF16)
        vv = _mm(ckvn, w_v[...])
        lane = lax.broadcasted_iota(jnp.int32, vv.shape, 1)
        outs[2][...] = jnp.where((lane & (HP - 1)) == VD, 1.0, vv).astype(BF16)

    outs = [((t, NH * HP), BF16)] * 3 + [((t, QL), BF16), ((t, KL), BF16)]
    return _row_call("mla_prep", fn, n, [(a, "cur") for a in (cq, ckv, krp, tc, tsa, tsb)], [gq, wuq, gkv, wk, wv], outs)


def _chunk_mask(c, tq, transposed, rows=ATT_CHUNK):
    row = lax.broadcasted_iota(jnp.int32, (rows, tq), 0) + c * rows
    col = lax.broadcasted_iota(jnp.int32, (rows, tq), 1)
    return (row <= col) if transposed else (col <= row)


def _flash_hook(exchange, nh, nq, refs):
    if exchange is not None:
        h, i = pl.program_id(0), pl.program_id(1)
        _exchange_hook((h == 0) & (i == 0), (h == nh - 1) & (i == nq - 1), exchange[1], refs)


def _head_lanes(g):
    return slice(g * HP, (g + 1) * HP)


def _flash_fwd(q, k, v, exchange=None):
    t = q.shape[0]
    tq = min(ATT_TILE, t)
    nq = t // tq
    x_in, x_out, x_shape, x_sems = _exchange_shapes(exchange)

    def body(q_ref, k_ref, v_ref, *rest):
        o_ref, lse_ref = rest[len(x_in):len(x_in) + 2]
        _flash_hook(exchange, NH // HG_FWD, nq, rest[:len(x_in)] + rest[len(x_in) + 2:])
        i = pl.program_id(1)

        def step(j, carry, masked):
            at = pl.ds(pl.multiple_of(j * tq, tq), tq)
            out = []
            for g in range(HG_FWD):
                m, acc = carry[g]
                s = _mm_nt(q_ref[:, _head_lanes(g)], k_ref[at, _head_lanes(g)])
                if masked:
                    s = jnp.where(_chunk_mask(0, tq, False, tq), s, NEG)
                m_new = jnp.maximum(m, jnp.max(s, axis=-1, keepdims=True))
                out.append((m_new, jnp.exp(m - m_new) * acc + _mm(jnp.exp(s - m_new), v_ref[at, _head_lanes(g)])))
            return tuple(out)

        init = tuple((jnp.full((tq, 1), NEG, F32), jnp.zeros((tq, HP), F32)) for _ in range(HG_FWD))
        carry = lax.fori_loop(0, i, lambda j, c: step(j, c, False), init)
        for g, (m, acc) in enumerate(step(i, carry, True)):
            lane = lax.broadcasted_iota(jnp.int32, acc.shape, 1)
            l = jnp.sum(jnp.where(lane == VD, acc, 0.0), axis=-1, keepdims=True)
            o_ref[:, _head_lanes(g)] = acc / l
            lse_ref[g] = m + jnp.log(l)

    wide = HG_FWD * HP
    return pl.pallas_call(
        body, name="flash_fwd", grid=(NH // HG_FWD, nq),
        in_specs=[pl.BlockSpec((tq, wide), lambda h, i: (i, h)), pl.BlockSpec((t, wide), lambda h, i: (0, h)),
                  pl.BlockSpec((t, wide), lambda h, i: (0, h))] + x_in,
        out_specs=[pl.BlockSpec((tq, wide), lambda h, i: (i, h)), pl.BlockSpec((HG_FWD, tq, 1), lambda h, i: (h, i, 0))] + x_out,
        out_shape=[jax.ShapeDtypeStruct((t, NH * HP), F32), jax.ShapeDtypeStruct((NH, t, 1), F32)] + x_shape,
        scratch_shapes=x_sems,
        compiler_params=pltpu.CompilerParams(dimension_semantics=("arbitrary", "arbitrary"), vmem_limit_bytes=VMEM_LIMIT),
    )(q, k, v, *([exchange[0]] if x_in else []))


def _flash_bwd(q, k, v, do, lse_row, delta_row, exchange=None):
    t = q.shape[0]
    tq = min(ATT_TILE, t)
    nq = t // tq
    x_in, x_out, x_shape, x_sems = _exchange_shapes(exchange)

    def body(q_ref, k_ref, v_ref, do_ref, lse_ref, dl_ref, *rest):
        dk_ref, dv_ref, dq_ref = rest[len(x_in):len(x_in) + 3]
        st_scr, dpt_scr, pt_scr, dst_scr = rest[len(rest) - 4:]
        _flash_hook(exchange, NH // HG, nq, rest[:len(x_in)] + rest[len(x_in) + 3:len(rest) - 4])
        j = pl.program_id(1)
        dk_ref[...] = jnp.zeros(dk_ref.shape, F32)
        dv_ref[...] = jnp.zeros(dv_ref.shape, F32)

        @pl.when(j == 0)
        def _():
            dq_ref[...] = jnp.zeros(dq_ref.shape, F32)

        def step(i, masked):
            at = pl.ds(pl.multiple_of(i * tq, tq), tq)
            for g in range(HG):
                st_scr[g] = _mm_nt(k_ref[:, _head_lanes(g)], q_ref[at, _head_lanes(g)])
                dpt_scr[g] = _mm_nt(v_ref[:, _head_lanes(g)], do_ref[at, _head_lanes(g)])
            for g in range(HG):
                lse_i, dl_i = lse_ref[g, :, at], dl_ref[g, :, at]
                for c in range(tq // ATT_CHUNK):
                    rows = slice(c * ATT_CHUNK, (c + 1) * ATT_CHUNK)
                    st = st_scr[g, rows, :]
                    if masked:
                        st = jnp.where(_chunk_mask(c, tq, True), st, NEG)
                    pt = jnp.exp(st - lse_i)
                    pt_scr[g, rows, :] = pt.astype(BF16)
                    dst_scr[g, rows, :] = (pt * (dpt_scr[g, rows, :] - dl_i)).astype(BF16)
            for g in range(HG):
                dv_ref[:, _head_lanes(g)] += _mm(pt_scr[g], do_ref[at, _head_lanes(g)])
                dk_ref[:, _head_lanes(g)] += _mm(dst_scr[g], q_ref[at, _head_lanes(g)])
                dq_ref[at, _head_lanes(g)] += _mm_tn(dst_scr[g], k_ref[:, _head_lanes(g)])

        step(j, True)
        lax.fori_loop(j + 1, nq, lambda i, c: (step(i, False), c)[1], 0)

    wide = HG * HP
    blk = pl.BlockSpec((tq, wide), lambda h, j: (j, h))
    whole = pl.BlockSpec((t, wide), lambda h, j: (0, h))
    row = pl.BlockSpec((HG, 1, t), lambda h, j: (h, 0, 0))
    x_sems = x_sems + [pltpu.VMEM((HG, tq, tq), F32), pltpu.VMEM((HG, tq, tq), F32), pltpu.VMEM((HG, tq, tq), BF16),
                       pltpu.VMEM((HG, tq, tq), BF16)]
    return pl.pallas_call(
        body, name="flash_bwd", grid=(NH // HG, nq), in_specs=[whole, blk, blk, whole, row, row] + x_in,
        out_specs=[blk, blk, whole] + x_out, out_shape=[jax.ShapeDtypeStruct((t, NH * HP), F32)] * 3 + x_shape, scratch_shapes=x_sems,
        compiler_params=pltpu.CompilerParams(dimension_semantics=("arbitrary", "arbitrary"), vmem_limit_bytes=VMEM_LIMIT),
    )(q, k, v, do, lse_row, delta_row, *([exchange[0]] if x_in else []))


def _mem_kv(mem, g_mem, wxkv):
    m = mem.shape[0]

    def fn(i, rows, consts, outs, accs, scr):
        mn = _rms(rows[0][...], consts[0][...])[0].astype(BF16)
        outs[0][...] = mn
        outs[1][...] = _mm(mn, consts[1][:, 0:XH * XD]).astype(BF16)
        outs[2][...] = _mm(mn, consts[1][:, XH * XD:]).astype(BF16)

    return _row_call("mem_kv", fn, 1, [(mem, "cur")], [g_mem, wxkv], [((m, D), BF16), ((m, XH * XD), BF16), ((m, XH * XD), BF16)])


def _merge_xattn(o, gl, conv_out, x, wmo, wo, g_x, wxq, kx, vx, wxo, n):
    t = x.shape[0]

    def fn(i, rows, consts, outs, accs, scr):
        w_mo, w_o, g, w_xq, k_x, v_x, w_xo = consts
        ob = rows[0][...].astype(BF16)
        outs[7][...] = ob
        mla = _mm(ob, w_mo[...])
        outs[0][...] = mla
        merged = (_sig(rows[1][:, 0:D]) * rows[2][...] + _sig(rows[1][:, D:]) * mla).astype(BF16)
        outs[1][...] = merged
        h1 = rows[3][...] + _mm(merged, w_o[...])
        outs[2][...] = h1
        u1 = _rms(h1, g[...])[0].astype(BF16)
        outs[3][...] = u1
        qx = (_mm(u1, w_xq[...]) * X_SCALE).astype(BF16)
        outs[4][...] = qx
        for h in range(XH):
            ln = slice(h * XD, (h + 1) * XD)
            s = _mm_nt(qx[:, ln], k_x[:, ln])
            e = jnp.exp(s - jnp.max(s, axis=-1, keepdims=True))
            p = e / jnp.sum(e, axis=-1, keepdims=True)
            outs[5][:, ln] = _mm(p, v_x[:, ln]).astype(BF16)
        outs[6][...] = h1 + _mm(outs[5][...], w_xo[...])

    outs = [((t, D), F32), ((t, D), BF16), ((t, D), F32), ((t, D), BF16), ((t, XH * XD), BF16), ((t, XH * XD), BF16),
            ((t, D), F32), ((t, NH * HP), BF16)]
    return _row_call("merge_xattn", fn, n, [(a, "cur") for a in (o, gl, conv_out, x)], [wmo, wo, g_x, wxq, kx, vx, wxo], outs)


def _mlp_loss(h2, target, g_mlp, w1, w2, g_fin, n):
    t = h2.shape[0]
    nck = DFF // D

    def fn(i, rows, consts, outs, accs, scr):
        g_m, w_1, w_2, g_f = consts
        h = rows[0][...]
        u2, xh2, r2 = _rms(h, g_m[...])
        ub = u2.astype(BF16)
        outs[0][...] = ub
        h3 = h
        a1 = []
        for c in range(nck):
            ck = slice(c * D, (c + 1) * D)
            a = _mm(ub, w_1[:, ck])
            a1.append(a)
            rl = jnp.maximum(a, 0.0)
            rb = (rl * rl).astype(BF16)
            outs[1][:, ck] = rb
            h3 = h3 + _mm(rb, w_2[ck, :])
        y, xh3, r3 = _rms(h3, g_f[...])
        err = y - rows[1][...]
        accs[0][...] += jnp.sum(jnp.sum(err * err, axis=1, keepdims=True), axis=0, keepdims=True) * (0.5 / D)
        dh3, dgf = _rms_bwd(err * (1.0 / D), xh3, r3, g_f[...])
        accs[1][...] += dgf
        db = dh3.astype(BF16)
        outs[3][...] = db
        du2 = jnp.zeros_like(h)
        for c in range(nck):
            ck = slice(c * D, (c + 1) * D)
            da = (_mm_nt(db, w_2[ck, :]) * (2.0 * jnp.maximum(a1[c], 0.0))).astype(BF16)
            outs[2][:, ck] = da
            du2 = du2 + _mm_nt(da, w_1[:, ck])
        dx2, dgm = _rms_bwd(du2, xh2, r2, g_m[...])
        accs[2][...] += dgm
        outs[4][...] = dh3 + dx2

    outs = [((t, D), BF16), ((t, DFF), BF16), ((t, DFF), BF16), ((t, D), BF16), ((t, D), F32)]
    accs = [((1, 1), F32), ((1, D), F32), ((1, D), F32)]
    return _row_call("mlp_loss", fn, n, [(h2, "cur"), (target, "cur")], [g_mlp, w1, w2, g_fin], outs, accs)


def _merge_xattn_bwd(dh2, h1, qx, gl, conv_out, mla, o, wxo, kx, vx, wxq, g_x, wo, wmo, n):
    t = dh2.shape[0]
    m = kx.shape[0]

    def fn(i, rows, consts, outs, accs, scr):
        w_xo, k_x, v_x, w_xq, g, w_o, w_mo = consts
        d2 = rows[0][...]
        d2b = d2.astype(BF16)
        outs[7][...] = d2b
        dox = _mm_nt(d2b, w_xo[...]).astype(BF16)
        q = rows[2][...]
        dq = []
        for h in range(XH):
            ln = slice(h * XD, (h + 1) * XD)
            qh, kh, vh, doh = q[:, ln], k_x[:, ln], v_x[:, ln], dox[:, ln]
            s = _mm_nt(qh, kh)
            e = jnp.exp(s - jnp.max(s, axis=-1, keepdims=True))
            p = e / jnp.sum(e, axis=-1, keepdims=True)
            dp = _mm_nt(doh, vh)
            ds = p * (dp - jnp.sum(p * dp, axis=-1, keepdims=True))
            dq.append(_mm(ds, kh) * X_SCALE)
            accs[0][:, ln] += _mm_tn(ds, qh)
            accs[1][:, ln] += _mm_tn(p, doh)
        dqx = jnp.concatenate(dq, axis=1).astype(BF16)
        outs[0][...] = dqx
        _, xh1, r1 = _rms(rows[1][...], g[...])
        dx1, dg = _rms_bwd(_mm_nt(dqx, w_xq[...]), xh1, r1, g[...])
        accs[2][...] += dg
        d1 = d2 + dx1
        outs[1][...] = d1
        d1b = d1.astype(BF16)
        outs[8][...] = d1b
        dm = _mm_nt(d1b, w_o[...])
        g0, g1 = _sig(rows[3][:, 0:D]), _sig(rows[3][:, D:])
        outs[2][:, 0:D] = (dm * rows[4][...] * g0 * (1.0 - g0)).astype(BF16)
        outs[2][:, D:] = (dm * rows[5][...] * g1 * (1.0 - g1)).astype(BF16)
        outs[3][...] = (dm * g0).astype(BF16)
        dmla = (dm * g1).astype(BF16)
        outs[4][...] = dmla
        do = _mm_nt(dmla, w_mo[...])
        outs[5][...] = do.astype(BF16)
        prod = do * rows[6][...]
        for h in range(NH):
            outs[6][h] = jnp.sum(prod[:, h * HP:(h + 1) * HP], axis=-1, keepdims=True)

    outs = [((t, XH * XD), BF16), ((t, D), F32), ((t, 2 * D), BF16), ((t, D), BF16), ((t, D), BF16), ((t, NH * HP), BF16),
            ((NH, t, 1), F32), ((t, D), BF16), ((t, D), BF16)]
    accs = [((m, XH * XD), F32), ((m, XH * XD), F32), ((1, D), F32)]
    return _row_call("merge_xattn_bwd", fn, n, [(a, "cur") for a in (dh2, h1, qx, gl, conv_out, mla, o)],
                     [wxo, kx, vx, wxq, g_x, wo, wmo], outs, accs)


def _mla_prep_bwd(dq, dk, dv, cq, ckv, tc, tsa, tsb, gq, wuq, gkv, wk, wv, n):
    t = dq.shape[0]

    def fn(i, rows, consts, outs, accs, scr):
        g_q, w_q, g_kv, w_k, w_v = consts
        c, sa, sb = rows[5][...], rows[6][...], rows[7][...]
        dkr = jnp.zeros((rows[0].shape[0], HP), F32)
        for h in range(NH):
            ln = slice(h * HP, (h + 1) * HP)
            outs[0][:, ln] = _rope_bwd(rows[0][:, ln] * MLA_SCALE, c, sa, sb).astype(BF16)
            dkr = dkr + rows[1][:, ln]
        lane = lax.broadcasted_iota(jnp.int32, dkr.shape, 1)
        outs[5][...] = jnp.where((lane >= NOPE) & (lane < NOPE + ROPE), _rope_bwd(dkr, c, sa, sb), 0.0).astype(BF16)
        dkb, dvb = rows[1][...].astype(BF16), rows[2][...].astype(BF16)
        outs[1][...] = dkb
        outs[2][...] = dvb
        _, xq, rq = _rms(rows[3][...], g_q[...])
        dcq, dgq = _rms_bwd(_mm_nt(outs[0][...], w_q[...]), xq, rq, g_q[...])
        outs[3][...] = dcq.astype(BF16)
        accs[0][...] += dgq
        _, xk, rk = _rms(rows[4][...], g_kv[...])
        dckv, dgk = _rms_bwd(_mm_nt(dkb, w_k[...]) + _mm_nt(dvb, w_v[...]), xk, rk, g_kv[...])
        outs[4][...] = dckv.astype(BF16)
        accs[1][...] += dgk

    outs = [((t, NH * HP), BF16)] * 3 + [((t, QL), BF16), ((t, KL), BF16), ((t, HP), BF16)]
    return _row_call("mla_prep_bwd", fn, n, [(a, "cur") for a in (dq, dk, dv, cq, ckv, tc, tsa, tsb)],
                     [gq, wuq, gkv, wk, wv], outs, [((1, QL), F32), ((1, KL), F32)])


def _conv_out_bwd(dco, zc, wco, lng, lnb, n):
    t = zc.shape[0]

    def fn(i, rows, consts, outs, accs, scr):
        wo, lg, lb = consts
        z = rows[1][...]
        mu = jnp.mean(z, axis=-1, keepdims=True)
        dlt = z - mu
        rs = lax.rsqrt(jnp.mean(dlt * dlt, axis=-1, keepdims=True) + EPS)
        xh = dlt * rs
        zn = xh * lg[...] + lb[...]
        sg = _sig(zn)
        outs[0][...] = (zn * sg).astype(BF16)
        dzn = _mm_nt(rows[0][...], wo[...]) * (sg * (1.0 + zn * (1.0 - sg)))
        accs[0][...] += jnp.sum(dzn * xh, axis=0, keepdims=True)
        accs[1][...] += jnp.sum(dzn, axis=0, keepdims=True)
        dxh = dzn * lg[...]
        dzc = rs * (dxh - jnp.mean(dxh, axis=-1, keepdims=True) - xh * jnp.mean(dxh * xh, axis=-1, keepdims=True))
        outs[1][...] = dzc
        accs[2][...] += jnp.sum(dzc, axis=0, keepdims=True)

    return _row_call("conv_out_bwd", fn, n, [(dco, "cur"), (zc, "cur")], [wco, lng, lnb],
                     [((t, CC), BF16), ((t, CC), F32)], [((1, CC), F32)] * 3)


def _conv_glu_bwd(dzc, a, gt, cw, n):
    t = a.shape[0]
    r = t // n

    def fn(i, rows, consts, outs, accs, scr):
        w = consts[0]
        zz, dd = scr
        _fill_glu_window(i, rows[2], rows[3], rows[4], rows[5], zz)
        dd[0:r, :] = rows[0][...]
        dd[r:, :] = jnp.where(i < n - 1, rows[1][0:HALO, :], 0.0)
        rb = min(CONV_ROWS, r)
        for c in range(CC // HP):
            ln = slice(c * HP, (c + 1) * HP)
            for base in range(0, r, rb):
                here = slice(base, base + rb)
                dcur = dd[here, ln]
                for o, win in _windows(zz, base, ln, rb, [HALO - (CW - 1) + j for j in range(CW)]):
                    j = o - (HALO - (CW - 1))
                    accs[0][j:j + 1, ln] += jnp.sum(dcur * win, axis=0, keepdims=True)
                acc = jnp.zeros((rb, HP), F32)
                for o, win in _windows(dd, base, ln, rb, [CW - 1 - j for j in range(CW)]):
                    j = CW - 1 - o
                    acc = acc + w[j:j + 1, ln] * win
                sg = _sig(rows[3][here, ln])
                outs[0][here, ln] = (acc * sg).astype(BF16)
                outs[0][here, CC + c * HP:CC + (c + 1) * HP] = (acc * rows[2][here, ln] * sg * (1.0 - sg)).astype(BF16)

    return _row_call("conv_glu_bwd", fn, n, [(dzc, "cur"), (dzc, "next"), (a, "cur"), (gt, "cur"), (a, "prev"), (gt, "prev")],
                     [cw], [((t, 2 * CC), BF16)], [((HALO, CC), F32)],
                     scratch=[pltpu.VMEM((r + HALO, CC), F32), pltpu.VMEM((r + HALO, CC), F32)])


def _in_proj_bwd(pieces, dh1, x, wp, g_mix, n, exchange):
    t = x.shape[0]
    offs = (P_A, P_Q, P_KV, P_KR, P_GL, P_END)

    def fn(i, rows, consts, outs, accs, scr):
        w, g = consts
        du = jnp.zeros((rows[0].shape[0], D), F32)
        for k in range(5):
            du = du + _mm_nt(rows[k][...], w[:, offs[k]:offs[k + 1]])
        _, xh, r = _rms(rows[6][...], g[...])
        dx, dg = _rms_bwd(du, xh, r, g[...])
        accs[0][...] += dg
        outs[0][...] = rows[5][...] + dx

    return _row_call("in_proj_bwd", fn, n, [(a, "cur") for a in list(pieces) + [dh1, x]], [wp, g_mix],
                     [((t, D), F32)], [((1, D), F32)], exchange=exchange)


def _mem_bwd(mem, dkx, dvx, g_mem, wxkv):
    m = mem.shape[0]

    def fn(i, rows, consts, outs, accs, scr):
        g, w = consts
        dkv = jnp.concatenate([rows[1][...], rows[2][...]], axis=1).astype(BF16)
        outs[0][...] = dkv
        _, xh, _ = _rms(rows[0][...], g[...])
        accs[0][...] += jnp.sum(_mm_nt(dkv, w[...]) * xh, axis=0, keepdims=True)

    return _row_call("mem_bwd", fn, 1, [(mem, "cur"), (dkx, "cur"), (dvx, "cur")], [g_mem, wxkv],
                     [((m, 2 * XH * XD), BF16)], [((1, D), F32)])


def _dw(name, xs, dy):
    t, k = xs.shape
    nn = dy.shape[1]
    tk, tn, tt = min(k, 1024), min(nn, 1024), min(t, DW_TILE)

    def body(x_ref, dy_ref, o_ref):
        @pl.when(pl.program_id(2) == 0)
        def _():
            o_ref[...] = jnp.zeros(o_ref.shape, F32)
        o_ref[...] += lax.dot_general(x_ref[...], dy_ref[...], (((0,), (0,)), ((), ())), preferred_element_type=F32)

    return pl.pallas_call(
        body, name=name, grid=(k // tk, nn // tn, t // tt),
        in_specs=[pl.BlockSpec((tt, tk), lambda a, b, c: (c, a)), pl.BlockSpec((tt, tn), lambda a, b, c: (c, b))],
        out_specs=pl.BlockSpec((tk, tn), lambda a, b, c: (a, b)),
        out_shape=jax.ShapeDtypeStruct((k, nn), F32),
        compiler_params=pltpu.CompilerParams(dimension_semantics=("arbitrary", "arbitrary", "arbitrary"), vmem_limit_bytes=VMEM_LIMIT),
    )(xs, dy)


def _all_gather_packed(shard):
    rws = shard.shape[0]

    def body(x_ref, out_ref, send_sems, recv_sems, local_sem):
        x, y, c = _coords()
        me, sibling = (x, y, c), (x, y, 1 - c)
        chips = [(1 - x, y), (x, 1 - y), (1 - x, 1 - y)]

        def slot(px, py, pc):
            return out_ref.at[4 * px + 2 * py + pc]

        def copy(k, block, to, src=None):
            return pltpu.make_async_remote_copy(
                src_ref=slot(*block) if src is None else src, dst_ref=slot(*block),
                send_sem=send_sems.at[k], recv_sem=recv_sems.at[k], device_id=to, device_id_type=MESH)

        mine = pltpu.make_async_copy(x_ref, slot(*me), local_sem)
        mine.start()
        first = [copy(0, me, sibling, src=x_ref)] + [copy(1 + j, me, (*chip, c), src=x_ref) for j, chip in enumerate(chips)]
        for cp in first:
            cp.start()
        passed = [copy(4 + j, (*chip, c), sibling) for j, chip in enumerate(chips)]
        for j, chip in enumerate(chips):
            copy(1 + j, (*chip, c), me).wait_recv()
            passed[j].start()
        copy(0, sibling, me).wait_recv()
        for j, chip in enumerate(chips):
            copy(4 + j, (*chip, 1 - c), me).wait_recv()
        for cp in first + passed:
            cp.wait_send()
        mine.wait()

    return pl.pallas_call(
        body, name="all_gather_weights",
        out_shape=jax.ShapeDtypeStruct((N_DEV, rws, 128), shard.dtype),
        in_specs=[pl.BlockSpec(memory_space=pl.ANY)], out_specs=pl.BlockSpec(memory_space=pl.ANY),
        scratch_shapes=[pltpu.SemaphoreType.DMA((7,)), pltpu.SemaphoreType.DMA((7,)), pltpu.SemaphoreType.DMA],
    )(shard)


def _adam(w, g, m, v):
    m = ADAM_B1 * m + (1.0 - ADAM_B1) * g
    v = ADAM_B2 * v + (1.0 - ADAM_B2) * (g * g)
    m_hat = m / (1.0 - ADAM_B1 ** ADAM_STEP)
    v_hat = v / (1.0 - ADAM_B2 ** ADAM_STEP)
    return -ADAM_LR * (m_hat / (jnp.sqrt(v_hat) + ADAM_EPS) + ADAM_WD * w), m, v


def _small_allreduce_adam(part, w, m, v):
    shape = part.shape

    def body(p_ref, w_ref, m_ref, v_ref, g_ref, d_ref, nm_ref, nv_ref, buf, send_sems, recv_sems):
        x, y, c = _coords()
        me = 4 * x + 2 * y + c
        buf[0] = p_ref[...]
        cps = []
        for j in range(1, N_DEV):
            jx, jy, jc = j >> 2, (j >> 1) & 1, j & 1
            peer = (1 - x if jx else x, 1 - y if jy else y, 1 - c if jc else c)
            cps.append(pltpu.make_async_remote_copy(src_ref=p_ref, dst_ref=buf.at[j], send_sem=send_sems.at[j - 1],
                                                    recv_sem=recv_sems.at[j - 1], device_id=peer, device_id_type=MESH))
        for cp in cps:
            cp.start()
        for cp in cps:
            cp.wait()
        g = buf[me]
        for d in range(1, N_DEV):
            g = g + buf[d ^ me]
        g_ref[...] = g
        d_ref[...], nm_ref[...], nv_ref[...] = _adam(w_ref[...], g, m_ref[...], v_ref[...])

    vm = pl.BlockSpec(memory_space=pltpu.VMEM)
    return pl.pallas_call(
        body, name="small_allreduce_adam", out_shape=[jax.ShapeDtypeStruct(shape, F32)] * 4,
        in_specs=[vm] * 4, out_specs=[vm] * 4,
        scratch_shapes=[pltpu.VMEM((N_DEV,) + shape, F32), pltpu.SemaphoreType.DMA((7,)), pltpu.SemaphoreType.DMA((7,))],
    )(part, w, m, v)


def _sum_adam(name, parts, w, m, v):
    rws = w.shape[0]
    tile = max(d for d in range(16, PACK_TILE + 1, 16) if rws % d == 0)

    def body(p_ref, w_ref, m_ref, v_ref, g_ref, d_ref, nm_ref, nv_ref):
        g = p_ref[0]
        for d in range(1, N_DEV):
            g = g + p_ref[d]
        g_ref[...] = g
        d_ref[...], nm_ref[...], nv_ref[...] = _adam(w_ref[...], g, m_ref[...], v_ref[...])

    spec = pl.BlockSpec((tile, 128), lambda i: (i, 0))
    return pl.pallas_call(
        body, name=name, grid=(rws // tile,), in_specs=[pl.BlockSpec((N_DEV, tile, 128), lambda i: (0, i, 0))] + [spec] * 3,
        out_specs=[spec] * 4, out_shape=[jax.ShapeDtypeStruct((rws, 128), F32)] * 4,
        compiler_params=pltpu.CompilerParams(dimension_semantics=("arbitrary",), vmem_limit_bytes=VMEM_LIMIT),
    )(parts, w, m, v)


ROW_SHARDED = ("w_out", "w_xq", "w_xkv", "w_mlp2")
SMALL = ("norm_mix_g", "conv_b", "conv_ln_g", "conv_ln_b", "q_norm_g", "kv_norm_g", "norm_xattn_g", "norm_mem_g", "norm_mlp_g", "final_norm_g")
GATHER_FIRST = ("w_in", "conv_w", "w_conv_out", "w_uq", "w_ukv")
GATHER_LATE = ("w_mla_out", "w_out", "w_xq", "w_xkv", "w_xo", "w_mlp1", "w_mlp2")
REDUCE_EARLY = ("w_mlp1", "w_mlp2", "conv_w", "w_conv_out", "w_mla_out", "w_out", "w_xq", "w_xkv", "w_xo")
REDUCE_LAST = ("w_in", "w_uq", "w_ukv")
ROW_ALIGN = 16


def _padded(k, nn):
    return -(-k // ROW_ALIGN) * ROW_ALIGN, -(-nn // 128) * 128


def _pack_rows(a):
    k, nn = a.shape[-2:]
    kp, np_ = _padded(k, nn)
    if (kp, np_) != (k, nn):
        a = jnp.pad(a, [(0, 0)] * (a.ndim - 2) + [(0, kp - k), (0, np_ - nn)])
    if np_ == 128:
        return a
    lead = a.shape[:-2]
    return jnp.swapaxes(a.reshape(lead + (kp, np_ // 128, 128)), -2, -3).reshape(lead + (kp * np_ // 128, 128))


def _unpack_rows(p, k, nn):
    kp, np_ = _padded(k, nn)
    if np_ != 128:
        lead = p.shape[:-2]
        p = jnp.swapaxes(p.reshape(lead + (np_ // 128, kp, 128)), -2, -3).reshape(lead + (kp, np_))
    return p[..., :k, :nn]


def _pack_group(arrays):
    return jnp.concatenate([_pack_rows(a) for a in arrays], axis=-2)


def _unpack_group(p, shapes):
    out, off = [], 0
    for k, nn in shapes:
        kp, np_ = _padded(k, nn)
        rws = kp * np_ // 128
        out.append(_unpack_rows(p[..., off:off + rws, :], k, nn))
        off += rws
    return out


def _to_full(name, stacked):
    _, k, nn = stacked.shape
    if name in ROW_SHARDED:
        return stacked.reshape(N_DEV * k, nn)
    return jnp.transpose(stacked, (1, 0, 2)).reshape(k, N_DEV * nn)


def _to_stacked(name, full, shard_shape):
    k, nn = shard_shape
    if name in ROW_SHARDED:
        return full.reshape(N_DEV, k, nn)
    return jnp.transpose(full.reshape(k, N_DEV, nn), (1, 0, 2))


def _pack_small(vals):
    flat = jnp.concatenate([v.reshape(-1) for v in vals])
    rws = flat.shape[0] // 128
    return jnp.pad(flat, (0, (-(-rws // 8) * 8 - rws) * 128)).reshape(-1, 128)


def _unpack_small(p, sizes):
    flat = p.reshape(-1)
    out, off = [], 0
    for s in sizes:
        out.append(flat[off:off + s])
        off += s
    return out


def _head_pad(w, real, axis):
    shp = list(w.shape)
    shp[axis:axis + 1] = [NH, real]
    w = w.reshape(shp)
    pad = [(0, 0)] * w.ndim
    pad[axis + 1] = (0, HP - real)
    w = jnp.pad(w, pad)
    shp[axis:axis + 2] = [NH * HP]
    return w.reshape(shp)


def _head_unpad(w, lo, real, axis):
    shp = list(w.shape)
    shp[axis:axis + 1] = [NH, HP]
    w = lax.slice_in_dim(w.reshape(shp), lo, lo + real, axis=axis + 1)
    shp[axis:axis + 2] = [NH * real]
    return w.reshape(shp)


def kernel(x, mem, positions, norm_mix_g, w_in, conv_w, conv_b, conv_ln_g, conv_ln_b, w_conv_out, q_norm_g, w_uq, kv_norm_g, w_ukv, w_mla_out, w_out, norm_xattn_g, norm_mem_g, w_xq, w_xkv, w_xo, norm_mlp_g, w_mlp1, w_mlp2, final_norm_g, loss_target, m_norm_mix_g, m_w_in, m_conv_w, m_conv_b, m_conv_ln_g, m_conv_ln_b, m_w_conv_out, m_q_norm_g, m_w_uq, m_kv_norm_g, m_w_ukv, m_w_mla_out, m_w_out, m_norm_xattn_g, m_norm_mem_g, m_w_xq, m_w_xkv, m_w_xo, m_norm_mlp_g, m_w_mlp1, m_w_mlp2, m_final_norm_g, v_norm_mix_g, v_w_in, v_conv_w, v_conv_b, v_conv_ln_g, v_conv_ln_b, v_w_conv_out, v_q_norm_g, v_w_uq, v_kv_norm_g, v_w_ukv, v_w_mla_out, v_w_out, v_norm_xattn_g, v_norm_mem_g, v_w_xq, v_w_xkv, v_w_xo, v_norm_mlp_g, v_w_mlp1, v_w_mlp2, v_final_norm_g):
    args = dict(locals())
    t = x.shape[1]
    n = t // min(ROW_TILE, t)
    xs, mems, tgt = x[0], mem[0], loss_target[0]

    def pack_shards(prefix, group, dtype):
        return _pack_group([args[prefix + k][0].astype(dtype) for k in group])

    def shapes(group):
        return [args[k].shape[1:] for k in group]

    def unpack_full(gathered, group):
        return {k: _to_full(k, s) for k, s in zip(group, _unpack_group(gathered, shapes(group)))}

    full = unpack_full(_all_gather_packed(pack_shards("", GATHER_FIRST, BF16)), GATHER_FIRST)

    wi = full["w_in"]
    kr_slot = jnp.pad(wi[:, P_KR:P_KR + ROPE], ((0, 0), (NOPE, HP - NOPE - ROPE)))
    wp = jnp.concatenate([wi[:, :P_KR], kr_slot, wi[:, P_KR + ROPE:]], axis=1)
    cw = jnp.pad(full["conv_w"].astype(F32), ((0, HALO - CW), (0, 0)))
    wuq = _head_pad(full["w_uq"], NOPE + ROPE, 1)
    ukv = full["w_ukv"].reshape(KL, NH, NOPE + VD)
    wk = _head_pad(ukv[:, :, :NOPE].reshape(KL, NH * NOPE), NOPE, 1)
    wv = _head_pad(ukv[:, :, NOPE:].reshape(KL, NH * VD), VD, 1)

    inv_freq = THETA ** (-jnp.arange(ROPE // 2, dtype=F32) / (ROPE // 2))
    ang = positions[0].astype(F32)[:, None] * inv_freq
    cs, sn, zr = jnp.cos(ang), jnp.sin(ang), jnp.zeros((t, ROPE // 2), F32)
    tail = jnp.zeros((t, HP - NOPE - ROPE), F32)
    tc = jnp.concatenate([jnp.ones((t, NOPE), F32), cs, cs, tail], axis=1)
    tsa = jnp.concatenate([jnp.zeros((t, NOPE), F32), -sn, zr, tail], axis=1)
    tsb = jnp.concatenate([jnp.zeros((t, NOPE), F32), zr, sn, tail], axis=1)

    u0, a, gt, cq, ckv, krp, gl = _in_proj(xs, norm_mix_g, wp, n)
    zc, conv_out = _conv_branch(a, gt, cw, conv_b, conv_ln_g, conv_ln_b, full["w_conv_out"], n)
    qh, kh, vh, cqn, ckvn = _mla_prep(cq, ckv, krp, tc, tsa, tsb, q_norm_g, wuq, kv_norm_g, wk, wv, n)
    o, lse, gathered = _flash_fwd(qh, kh, vh, exchange=(pack_shards("", GATHER_LATE, BF16), True))
    full.update(unpack_full(gathered, GATHER_LATE))
    wmo = _head_pad(full["w_mla_out"], VD, 0)
    memn, kx, vx = _mem_kv(mems, norm_mem_g, full["w_xkv"])
    mla, merged, h1, u1, qx, ox, h2, ob = _merge_xattn(o, gl, conv_out, xs, wmo, full["w_out"], norm_xattn_g, full["w_xq"],
                                                        kx, vx, full["w_xo"], n)
    gfin = final_norm_g.reshape(1, D)
    u2, rl2, da1, dh3b, dh2, loss_p, dg_fin, dg_mlp = _mlp_loss(h2, tgt, norm_mlp_g, full["w_mlp1"], full["w_mlp2"], gfin, n)

    (dqx, dh1, dgl, dco, dmla, dob, delta, dh2b, dh1b, dkx, dvx, dg_x) = _merge_xattn_bwd(
        dh2, h1, qx, gl, conv_out, mla, o, full["w_xo"], kx, vx, full["w_xq"], norm_xattn_g, full["w_out"], wmo, n)
    gfull = {"w_mlp1": _dw("dw_mlp1", u2, da1), "w_mlp2": _dw("dw_mlp2", rl2, dh3b)}

    def stacked(group):
        return _pack_group([_to_stacked(k, gfull[k], args[k].shape[1:]) for k in group])

    zs, dzc, dg_lng, dg_lnb, dg_cb = _conv_out_bwd(dco, zc, full["w_conv_out"], conv_ln_g, conv_ln_b, n)
    dci, dcw = _conv_glu_bwd(dzc, a, gt, cw, n)
    dkv, dg_mem = _mem_bwd(mems, dkx, dvx, norm_mem_g, full["w_xkv"])
    gfull.update({
        "conv_w": dcw[:CW],
        "w_conv_out": _dw("dw_conv_out", zs, dco),
        "w_mla_out": _head_unpad(_dw("dw_mla_out", ob, dmla), 0, VD, 0),
        "w_out": _dw("dw_out", merged, dh1b),
        "w_xq": _dw("dw_xq", u1, dqx),
        "w_xkv": _dw("dw_xkv", memn, dkv),
        "w_xo": _dw("dw_xo", ox, dh2b),
    })
    dk, dv, dq, parts_early = _flash_bwd(qh, kh, vh, dob, lse.reshape(NH, 1, t), delta.reshape(NH, 1, t),
                                         exchange=(stacked(REDUCE_EARLY), False))
    dqp, dkb, dvb, dcq, dckv, dkrp, dg_q, dg_kv = _mla_prep_bwd(dq, dk, dv, cq, ckv, tc, tsa, tsb, q_norm_g, wuq, kv_norm_g, wk, wv, n)
    pieces = (dci, dcq, dckv, dkrp, dgl)
    dwp = [_dw("dw_in_%d" % k, u0, p) for k, p in enumerate(pieces)]
    gfull["w_in"] = jnp.concatenate([dwp[0], dwp[1], dwp[2], dwp[3][:, NOPE:NOPE + ROPE], dwp[4]], axis=1)
    gfull["w_uq"] = _head_unpad(_dw("dw_uq", cqn, dqp), 0, NOPE + ROPE, 1)
    gk = _head_unpad(_dw("dw_uk", ckvn, dkb), 0, NOPE, 1).reshape(KL, NH, NOPE)
    gv = _head_unpad(_dw("dw_uv", ckvn, dvb), 0, VD, 1).reshape(KL, NH, VD)
    gfull["w_ukv"] = jnp.concatenate([gk, gv], axis=2).reshape(KL, NH * (NOPE + VD))
    grad_x, dg_mix, parts_last = _in_proj_bwd(pieces, dh1, xs, wp, norm_mix_g, n, (stacked(REDUCE_LAST), False))

    big = [{}, {}, {}, {}]
    for name, group, parts in (("adam_early", REDUCE_EARLY, parts_early), ("adam_last", REDUCE_LAST, parts_last)):
        res = _sum_adam(name, parts, pack_shards("", group, F32), pack_shards("m_", group, F32), pack_shards("v_", group, F32))
        for kind in range(4):
            for k, val in zip(group, _unpack_group(res[kind], shapes(group))):
                big[kind][k] = val[None]

    small_g = {"norm_mix_g": dg_mix, "conv_b": dg_cb, "conv_ln_g": dg_lng, "conv_ln_b": dg_lnb, "q_norm_g": dg_q,
               "kv_norm_g": dg_kv, "norm_xattn_g": dg_x, "norm_mem_g": dg_mem, "norm_mlp_g": dg_mlp, "final_norm_g": dg_fin}
    small_sizes = [int(np.prod(args[k].shape)) for k in SMALL] + [128]
    zero_slot = jnp.zeros((128,), F32)
    small_out = _small_allreduce_adam(
        _pack_small([small_g[k] for k in SMALL] + [jnp.pad(loss_p.reshape(-1), (0, 127))]),
        _pack_small([args[k] for k in SMALL] + [zero_slot]), _pack_small([args["m_" + k] for k in SMALL] + [zero_slot]),
        _pack_small([args["v_" + k] for k in SMALL] + [zero_slot]))

    small = [dict(zip(SMALL, [s.reshape(args[k].shape) for k, s in zip(SMALL, _unpack_small(o_, small_sizes))])) for o_ in small_out]
    loss = _unpack_small(small_out[0], small_sizes)[-1][0]
    order = ("norm_mix_g", "w_in", "conv_w", "conv_b", "conv_ln_g", "conv_ln_b", "w_conv_out", "q_norm_g", "w_uq", "kv_norm_g",
             "w_ukv", "w_mla_out", "w_out", "norm_xattn_g", "norm_mem_g", "w_xq", "w_xkv", "w_xo", "norm_mlp_g", "w_mlp1",
             "w_mlp2", "final_norm_g")
    res = [loss, grad_x[None]]
    for kind in range(4):
        res += [big[kind][k] if k in big[kind] else small[kind][k] for k in order]
    return tuple(res)
```

```python
import functools

import jax
import jax.numpy as jnp
import numpy as np
from jax import lax
from jax.experimental import pallas as pl
from jax.experimental.pallas import tpu as pltpu

F32, BF16 = jnp.float32, jnp.bfloat16
MESH = pl.DeviceIdType.MESH

N_DEV = 8
D = 1024
CC = D // 2
CW = 31
HALO = 32
NH = 8
NOPE, ROPE, VD = D // 16, D // 32, D // 16
QL, KL = 3 * D // 8, D // 4
HP = 128
XH, XD = 4, D // 8
DFF = 4 * D
EPS = 1e-6
THETA = 10000.0
MLA_SCALE = float((NOPE + ROPE) ** -0.5)
X_SCALE = float(XD ** -0.5)
NEG = -1e30
P_A, P_G, P_Q, P_KV, P_KR, P_GL, P_END = 0, CC, 2 * CC, 2 * CC + QL, 2 * CC + QL + KL, 2 * CC + QL + KL + HP, 2 * CC + QL + KL + HP + 2 * D

ADAM_LR, ADAM_B1, ADAM_B2, ADAM_EPS, ADAM_WD, ADAM_STEP = 0.001, 0.9, 0.999, 1e-08, 0.01, 10

ROW_TILE = 256
ATT_TILE = 512
HG = 2
HG_FWD = 4
ATT_CHUNK = 32
CONV_ROWS = 128
DW_TILE = 1024
PACK_TILE = 1536
VMEM_LIMIT = 56 * 1024 * 1024


def _mm(a, w):
    return jnp.dot(a.astype(BF16), w, preferred_element_type=F32)


def _mm_nt(a, w):
    return lax.dot_general(a.astype(BF16), w, (((1,), (1,)), ((), ())), preferred_element_type=F32)


def _mm_tn(a, b):
    return lax.dot_general(a.astype(BF16), b.astype(BF16), (((0,), (0,)), ((), ())), preferred_element_type=F32)


def _rms(x, g):
    r = lax.rsqrt(jnp.mean(x * x, axis=-1, keepdims=True) + EPS)
    xh = x * r
    return xh * g, xh, r


def _rms_bwd(dy, xh, r, g):
    dxh = dy * g
    dx = r * (dxh - xh * jnp.mean(dxh * xh, axis=-1, keepdims=True))
    return dx, jnp.sum(dy * xh, axis=0, keepdims=True)


def _sig(x):
    return 1.0 / (1.0 + jnp.exp(-x))


def _coords():
    return lax.axis_index("x"), lax.axis_index("y"), lax.axis_index("c")


def _exchange_ops(src_ref, gather, dst_ref, send_sems, recv_sems, local_sem):
    x, y, c = _coords()
    me = 4 * x + 2 * y + c
    ops = [pltpu.make_async_copy(src_ref if gather else src_ref.at[me], dst_ref.at[me], local_sem)]
    for j in range(1, N_DEV):
        px, py, pc = (1 - x if j & 4 else x), (1 - y if j & 2 else y), (1 - c if j & 1 else c)
        ops.append(pltpu.make_async_remote_copy(
            src_ref=src_ref if gather else src_ref.at[4 * px + 2 * py + pc], dst_ref=dst_ref.at[me],
            send_sem=send_sems.at[j - 1], recv_sem=recv_sems.at[j - 1], device_id=(px, py, pc), device_id_type=MESH))
    return ops


def _exchange_hook(first, last, gather, refs):
    @pl.when(first)
    def _():
        for op in _exchange_ops(refs[0], gather, *refs[1:]):
            op.start()

    @pl.when(last)
    def _():
        for op in _exchange_ops(refs[0], gather, *refs[1:]):
            op.wait()


def _exchange_shapes(exchange):
    if exchange is None:
        return [], [], [], []
    arr, gather = exchange
    shape = (N_DEV,) + arr.shape if gather else arr.shape
    any_spec = pl.BlockSpec(memory_space=pl.ANY)
    sems = [pltpu.SemaphoreType.DMA((N_DEV - 1,)), pltpu.SemaphoreType.DMA((N_DEV - 1,)), pltpu.SemaphoreType.DMA]
    return [any_spec], [any_spec], [jax.ShapeDtypeStruct(shape, arr.dtype)], sems


def _row_call(name, fn, n, rows, consts, outs, accs=(), scratch=(), exchange=None):
    def row_spec(shape, mode):
        r = shape[-2] // n
        if mode == "cur":
            f = lambda i: i
        elif mode == "prev":
            f = lambda i: jnp.maximum(i - 1, 0)
        else:
            f = lambda i: jnp.minimum(i + 1, n - 1)
        if len(shape) == 2:
            return pl.BlockSpec((r, shape[1]), lambda i: (f(i), 0))
        return pl.BlockSpec((shape[0], r, shape[2]), lambda i: (0, f(i), 0))

    def whole_spec(shape, single):
        nd = len(shape)
        if single:
            return pl.BlockSpec(shape, lambda i: (0,) * nd, pipeline_mode=pl.Buffered(1))
        return pl.BlockSpec(shape, lambda i: (0,) * nd)

    nr, nc, no, na, ns = len(rows), len(consts), len(outs), len(accs), len(scratch)
    x_in, x_out, x_shape, x_sems = _exchange_shapes(exchange)
    nx = len(x_in)

    def body(*refs):
        i = pl.program_id(0)
        row_refs, const_refs = refs[:nr], refs[nr:nr + nc]
        o0 = nr + nc + nx
        out_refs, acc_refs = refs[o0:o0 + no], refs[o0 + no:o0 + no + na]
        s0 = o0 + no + na + nx
        if nx:
            _exchange_hook(i == 0, i == n - 1, exchange[1], (refs[nr + nc], refs[s0 - 1]) + tuple(refs[s0 + ns:]))
        if na:
            @pl.when(i == 0)
            def _():
                for a in acc_refs:
                    a[...] = jnp.zeros(a.shape, a.dtype)
        fn(i, row_refs, const_refs, out_refs, acc_refs, refs[s0:s0 + ns])

    res = pl.pallas_call(
        body, name=name, grid=(n,),
        in_specs=[row_spec(a.shape, m) for a, m in rows] + [whole_spec(c.shape, True) for c in consts] + x_in,
        out_specs=[row_spec(s, "cur") for s, _ in outs] + [whole_spec(s, False) for s, _ in accs] + x_out,
        out_shape=[jax.ShapeDtypeStruct(s, d) for s, d in list(outs) + list(accs)] + x_shape,
        scratch_shapes=list(scratch) + x_sems,
        compiler_params=pltpu.CompilerParams(dimension_semantics=("arbitrary",), vmem_limit_bytes=VMEM_LIMIT),
    )(*[a for a, _ in rows], *consts, *([exchange[0]] if nx else []))
    return list(res)


def _in_proj(x, g_mix, wp, n):
    t = x.shape[0]

    def fn(i, rows, consts, outs, accs, scr):
        g, w = consts
        u, _, _ = _rms(rows[0][...], g[...])
        ub = u.astype(BF16)
        outs[0][...] = ub
        for k, (lo, hi) in enumerate(((P_A, P_G), (P_G, P_Q), (P_Q, P_KV), (P_KV, P_KR), (P_KR, P_GL), (P_GL, P_END))):
            outs[1 + k][...] = _mm(ub, w[:, lo:hi])

    outs = [((t, D), BF16), ((t, CC), F32), ((t, CC), F32), ((t, QL), F32), ((t, KL), F32), ((t, HP), F32), ((t, 2 * D), F32)]
    return _row_call("in_proj", fn, n, [(x, "cur")], [g_mix, wp], outs)


def _fill_glu_window(i, a, gt, ap, gtp, zz):
    r = a.shape[0]
    zp = ap[r - HALO:, :] * _sig(gtp[r - HALO:, :])
    zz[0:HALO, :] = jnp.where(i > 0, zp, 0.0)
    zz[HALO:, :] = a[...] * _sig(gt[...])


def _shift_copies(buf, sh):
    rows = buf.shape[0]
    for s in range(8):
        sh[s, 0:rows - s, :] = buf[s:rows, :]


def _window(sh, o, base, rb, ln):
    return sh[o % 8, base + o - o % 8:base + o - o % 8 + rb, ln]


def _windows(buf, base, ln, rb, offsets):
    for s in range(8):
        group = [o for o in offsets if o % 8 == s]
        if group:
            shifted = buf[base + s:base + max(group) + rb, ln]
            for o in group:
                yield o, shifted[o - s:o - s + rb]


def _conv_branch(a, gt, cw, cb, lng, lnb, wco, n):
    t = a.shape[0]
    r = t // n

    def fn(i, rows, consts, outs, accs, scr):
        w, b, lg, lb, wo = consts
        zz, zsh = scr
        _fill_glu_window(i, rows[0], rows[1], rows[2], rows[3], zz)
        _shift_copies(zz, zsh)
        rb = min(CONV_ROWS, r)
        for c in range(CC // HP):
            ln = slice(c * HP, (c + 1) * HP)
            for base in range(0, r, rb):
                acc = jnp.zeros((rb, HP), F32)
                for j in range(CW):
                    acc = acc + w[j:j + 1, ln] * _window(zsh, HALO - (CW - 1) + j, base, rb, ln)
                outs[0][base:base + rb, ln] = acc + b[:, ln]
        zc = outs[0][...]
        mu = jnp.mean(zc, axis=-1, keepdims=True)
        dlt = zc - mu
        rs = lax.rsqrt(jnp.mean(dlt * dlt, axis=-1, keepdims=True) + EPS)
        zn = dlt * rs * lg[...] + lb[...]
        outs[1][...] = _mm(zn * _sig(zn), wo[...])

    return _row_call("conv_branch", fn, n, [(a, "cur"), (gt, "cur"), (a, "prev"), (gt, "prev")],
                     [cw, cb, lng, lnb, wco], [((t, CC), F32), ((t, D), F32)],
                     scratch=[pltpu.VMEM((r + HALO, CC), F32), pltpu.VMEM((8, r + HALO, CC), F32)])


def _rope(v, c, sa, sb):
    return v * c + pltpu.roll(v, HP - ROPE // 2, 1) * sa + pltpu.roll(v, ROPE // 2, 1) * sb


def _rope_bwd(dv, c, sa, sb):
    return dv * c + pltpu.roll(dv * sa, ROPE // 2, 1) + pltpu.roll(dv * sb, HP - ROPE // 2, 1)


def _mla_prep(cq, ckv, krp, tc, tsa, tsb, gq, wuq, gkv, wk, wv, n):
    t = cq.shape[0]

    def fn(i, rows, consts, outs, accs, scr):
        g_q, w_q, g_kv, w_k, w_v = consts
        c, sa, sb = rows[3][...], rows[4][...], rows[5][...]
        cqn = _rms(rows[0][...], g_q[...])[0].astype(BF16)
        ckvn = _rms(rows[1][...], g_kv[...])[0].astype(BF16)
        outs[3][...] = cqn
        outs[4][...] = ckvn
        krr = _rope(rows[2][...], c, sa, sb)
        for h in range(NH):
            ln = slice(h * HP, (h + 1) * HP)
            outs[0][:, ln] = (_rope(_mm(cqn, w_q[:, ln]), c, sa, sb) * MLA_SCALE).astype(BF16)
            outs[1][:, ln] = (_mm(ckvn, w_k[:, ln]) + krr).astype(BF16)
        vv = _mm(ckvn, w_v[...])
        lane = lax.broadcasted_iota(jnp.int32, vv.shape, 1)
        outs[2][...] = jnp.where((lane & (HP - 1)) == VD, 1.0, vv).astype(BF16)

    outs = [((t, NH * HP), BF16)] * 3 + [((t, QL), BF16), ((t, KL), BF16)]
    return _row_call("mla_prep", fn, n, [(a, "cur") for a in (cq, ckv, krp, tc, tsa, tsb)], [gq, wuq, gkv, wk, wv], outs)


def _chunk_mask(c, tq, transposed, rows=ATT_CHUNK):
    row = lax.broadcasted_iota(jnp.int32, (rows, tq), 0) + c * rows
    col = lax.broadcasted_iota(jnp.int32, (rows, tq), 1)
    return (row <= col) if transposed else (col <= row)


def _flash_hook(exchange, nh, nq, refs):
    if exchange is not None:
        h, i = pl.program_id(0), pl.program_id(1)
        _exchange_hook((h == 0) & (i == 0), (h == nh - 1) & (i == nq - 1), exchange[1], refs)


def _head_lanes(g):
    return slice(g * HP, (g + 1) * HP)


def _flash_fwd(q, k, v, exchange=None):
    t = q.shape[0]
    tq = min(ATT_TILE, t)
    nq = t // tq
    x_in, x_out, x_shape, x_sems = _exchange_shapes(exchange)

    def body(q_ref, k_ref, v_ref, *rest):
        o_ref, lse_ref = rest[len(x_in):len(x_in) + 2]
        _flash_hook(exchange, NH // HG_FWD, nq, rest[:len(x_in)] + rest[len(x_in) + 2:])
        i = pl.program_id(1)

        def step(j, carry, masked):
            at = pl.ds(pl.multiple_of(j * tq, tq), tq)
            out = []
            for g in range(HG_FWD):
                m, acc = carry[g]
                s = _mm_nt(q_ref[:, _head_lanes(g)], k_ref[at, _head_lanes(g)])
                if masked:
                    s = jnp.where(_chunk_mask(0, tq, False, tq), s, NEG)
                m_new = jnp.maximum(m, jnp.max(s, axis=-1, keepdims=True))
                out.append((m_new, jnp.exp(m - m_new) * acc + _mm(jnp.exp(s - m_new), v_ref[at, _head_lanes(g)])))
            return tuple(out)

        init = tuple((jnp.full((tq, 1), NEG, F32), jnp.zeros((tq, HP), F32)) for _ in range(HG_FWD))
        carry = lax.fori_loop(0, i, lambda j, c: step(j, c, False), init)
        for g, (m, acc) in enumerate(step(i, carry, True)):
            lane = lax.broadcasted_iota(jnp.int32, acc.shape, 1)
            l = jnp.sum(jnp.where(lane == VD, acc, 0.0), axis=-1, keepdims=True)
            o_ref[:, _head_lanes(g)] = acc / l
            lse_ref[g] = m + jnp.log(l)

    wide = HG_FWD * HP
    return pl.pallas_call(
        body, name="flash_fwd", grid=(NH // HG_FWD, nq),
        in_specs=[pl.BlockSpec((tq, wide), lambda h, i: (i, h)), pl.BlockSpec((t, wide), lambda h, i: (0, h)),
                  pl.BlockSpec((t, wide), lambda h, i: (0, h))] + x_in,
        out_specs=[pl.BlockSpec((tq, wide), lambda h, i: (i, h)), pl.BlockSpec((HG_FWD, tq, 1), lambda h, i: (h, i, 0))] + x_out,
        out_shape=[jax.ShapeDtypeStruct((t, NH * HP), F32), jax.ShapeDtypeStruct((NH, t, 1), F32)] + x_shape,
        scratch_shapes=x_sems,
        compiler_params=pltpu.CompilerParams(dimension_semantics=("arbitrary", "arbitrary"), vmem_limit_bytes=VMEM_LIMIT),
    )(q, k, v, *([exchange[0]] if x_in else []))


def _flash_bwd(q, k, v, do, lse_row, delta_row, exchange=None):
    t = q.shape[0]
    tq = min(ATT_TILE, t)
    nq = t // tq
    x_in, x_out, x_shape, x_sems = _exchange_shapes(exchange)

    def body(q_ref, k_ref, v_ref, do_ref, lse_ref, dl_ref, *rest):
        dk_ref, dv_ref, dq_ref = rest[len(x_in):len(x_in) + 3]
        st_scr, dpt_scr, pt_scr, dst_scr = rest[len(rest) - 4:]
        _flash_hook(exchange, NH // HG, nq, rest[:len(x_in)] + rest[len(x_in) + 3:len(rest) - 4])
        j = pl.program_id(1)
        dk_ref[...] = jnp.zeros(dk_ref.shape, F32)
        dv_ref[...] = jnp.zeros(dv_ref.shape, F32)

        @pl.when(j == 0)
        def _():
            dq_ref[...] = jnp.zeros(dq_ref.shape, F32)

        def step(i, masked):
            at = pl.ds(pl.multiple_of(i * tq, tq), tq)
            for g in range(HG):
                st_scr[g] = _mm_nt(k_ref[:, _head_lanes(g)], q_ref[at, _head_lanes(g)])
                dpt_scr[g] = _mm_nt(v_ref[:, _head_lanes(g)], do_ref[at, _head_lanes(g)])
            for g in range(HG):
                lse_i, dl_i = lse_ref[g, :, at], dl_ref[g, :, at]
                for c in range(tq // ATT_CHUNK):
                    rows = slice(c * ATT_CHUNK, (c + 1) * ATT_CHUNK)
                    st = st_scr[g, rows, :]
                    if masked:
                        st = jnp.where(_chunk_mask(c, tq, True), st, NEG)
                    pt = jnp.exp(st - lse_i)
                    pt_scr[g, rows, :] = pt.astype(BF16)
                    dst_scr[g, rows, :] = (pt * (dpt_scr[g, rows, :] - dl_i)).astype(BF16)
            for g in range(HG):
                dv_ref[:, _head_lanes(g)] += _mm(pt_scr[g], do_ref[at, _head_lanes(g)])
                dk_ref[:, _head_lanes(g)] += _mm(dst_scr[g], q_ref[at, _head_lanes(g)])
                dq_ref[at, _head_lanes(g)] += _mm_tn(dst_scr[g], k_ref[:, _head_lanes(g)])

        step(j, True)
        lax.fori_loop(j + 1, nq, lambda i, c: (step(i, False), c)[1], 0)

    wide = HG * HP
    blk = pl.BlockSpec((tq, wide), lambda h, j: (j, h))
    whole = pl.BlockSpec((t, wide), lambda h, j: (0, h))
    row = pl.BlockSpec((HG, 1, t), lambda h, j: (h, 0, 0))
    x_sems = x_sems + [pltpu.VMEM((HG, tq, tq), F32), pltpu.VMEM((HG, tq, tq), F32), pltpu.VMEM((HG, tq, tq), BF16),
                       pltpu.VMEM((HG, tq, tq), BF16)]
    return pl.pallas_call(
        body, name="flash_bwd", grid=(NH // HG, nq), in_specs=[whole, blk, blk, whole, row, row] + x_in,
        out_specs=[blk, blk, whole] + x_out, out_shape=[jax.ShapeDtypeStruct((t, NH * HP), F32)] * 3 + x_shape, scratch_shapes=x_sems,
        compiler_params=pltpu.CompilerParams(dimension_semantics=("arbitrary", "arbitrary"), vmem_limit_bytes=VMEM_LIMIT),
    )(q, k, v, do, lse_row, delta_row, *([exchange[0]] if x_in else []))


def _mem_kv(mem, g_mem, wxkv):
    m = mem.shape[0]

    def fn(i, rows, consts, outs, accs, scr):
        mn = _rms(rows[0][...], consts[0][...])[0].astype(BF16)
        outs[0][...] = mn
        outs[1][...] = _mm(mn, consts[1][:, 0:XH * XD]).astype(BF16)
        outs[2][...] = _mm(mn, consts[1][:, XH * XD:]).astype(BF16)

    return _row_call("mem_kv", fn, 1, [(mem, "cur")], [g_mem, wxkv], [((m, D), BF16), ((m, XH * XD), BF16), ((m, XH * XD), BF16)])


def _merge_xattn(o, gl, conv_out, x, wmo, wo, g_x, wxq, kx, vx, wxo, n):
    t = x.shape[0]

    def fn(i, rows, consts, outs, accs, scr):
        w_mo, w_o, g, w_xq, k_x, v_x, w_xo = consts
        ob = rows[0][...].astype(BF16)
        outs[7][...] = ob
        mla = _mm(ob, w_mo[...])
        outs[0][...] = mla
        merged = (_sig(rows[1][:, 0:D]) * rows[2][...] + _sig(rows[1][:, D:]) * mla).astype(BF16)
        outs[1][...] = merged
        h1 = rows[3][...] + _mm(merged, w_o[...])
        outs[2][...] = h1
        u1 = _rms(h1, g[...])[0].astype(BF16)
        outs[3][...] = u1
        qx = (_mm(u1, w_xq[...]) * X_SCALE).astype(BF16)
        outs[4][...] = qx
        for h in range(XH):
            ln = slice(h * XD, (h + 1) * XD)
            s = _mm_nt(qx[:, ln], k_x[:, ln])
            e = jnp.exp(s - jnp.max(s, axis=-1, keepdims=True))
            p = e / jnp.sum(e, axis=-1, keepdims=True)
            outs[5][:, ln] = _mm(p, v_x[:, ln]).astype(BF16)
        outs[6][...] = h1 + _mm(outs[5][...], w_xo[...])

    outs = [((t, D), F32), ((t, D), BF16), ((t, D), F32), ((t, D), BF16), ((t, XH * XD), BF16), ((t, XH * XD), BF16),
            ((t, D), F32), ((t, NH * HP), BF16)]
    return _row_call("merge_xattn", fn, n, [(a, "cur") for a in (o, gl, conv_out, x)], [wmo, wo, g_x, wxq, kx, vx, wxo], outs)


def _mlp_loss(h2, target, g_mlp, w1, w2, g_fin, n):
    t = h2.shape[0]
    nck = DFF // D

    def fn(i, rows, consts, outs, accs, scr):
        g_m, w_1, w_2, g_f = consts
        h = rows[0][...]
        u2, xh2, r2 = _rms(h, g_m[...])
        ub = u2.astype(BF16)
        outs[0][...] = ub
        h3 = h
        a1 = []
        for c in range(nck):
            ck = slice(c * D, (c + 1) * D)
            a = _mm(ub, w_1[:, ck])
            a1.append(a)
            rl = jnp.maximum(a, 0.0)
            rb = (rl * rl).astype(BF16)
            outs[1][:, ck] = rb
            h3 = h3 + _mm(rb, w_2[ck, :])
        y, xh3, r3 = _rms(h3, g_f[...])
        err = y - rows[1][...]
        accs[0][...] += jnp.sum(jnp.sum(err * err, axis=1, keepdims=True), axis=0, keepdims=True) * (0.5 / D)
        dh3, dgf = _rms_bwd(err * (1.0 / D), xh3, r3, g_f[...])
        accs[1][...] += dgf
        db = dh3.astype(BF16)
        outs[3][...] = db
        du2 = jnp.zeros_like(h)
        for c in range(nck):
            ck = slice(c * D, (c + 1) * D)
            da = (_mm_nt(db, w_2[ck, :]) * (2.0 * jnp.maximum(a1[c], 0.0))).astype(BF16)
            outs[2][:, ck] = da
            du2 = du2 + _mm_nt(da, w_1[:, ck])
        dx2, dgm = _rms_bwd(du2, xh2, r2, g_m[...])
        accs[2][...] += dgm
        outs[4][...] = dh3 + dx2

    outs = [((t, D), BF16), ((t, DFF), BF16), ((t, DFF), BF16), ((t, D), BF16), ((t, D), F32)]
    accs = [((1, 1), F32), ((1, D), F32), ((1, D), F32)]
    return _row_call("mlp_loss", fn, n, [(h2, "cur"), (target, "cur")], [g_mlp, w1, w2, g_fin], outs, accs)


def _merge_xattn_bwd(dh2, h1, qx, gl, conv_out, mla, o, wxo, kx, vx, wxq, g_x, wo, wmo, n):
    t = dh2.shape[0]
    m = kx.shape[0]

    def fn(i, rows, consts, outs, accs, scr):
        w_xo, k_x, v_x, w_xq, g, w_o, w_mo = consts
        d2 = rows[0][...]
        d2b = d2.astype(BF16)
        outs[7][...] = d2b
        dox = _mm_nt(d2b, w_xo[...]).astype(BF16)
        q = rows[2][...]
        dq = []
        for h in range(XH):
            ln = slice(h * XD, (h + 1) * XD)
            qh, kh, vh, doh = q[:, ln], k_x[:, ln], v_x[:, ln], dox[:, ln]
            s = _mm_nt(qh, kh)
            e = jnp.exp(s - jnp.max(s, axis=-1, keepdims=True))
            p = e / jnp.sum(e, axis=-1, keepdims=True)
            dp = _mm_nt(doh, vh)
            ds = p * (dp - jnp.sum(p * dp, axis=-1, keepdims=True))
            dq.append(_mm(ds, kh) * X_SCALE)
            accs[0][:, ln] += _mm_tn(ds, qh)
            accs[1][:, ln] += _mm_tn(p, doh)
        dqx = jnp.concatenate(dq, axis=1).astype(BF16)
        outs[0][...] = dqx
        _, xh1, r1 = _rms(rows[1][...], g[...])
        dx1, dg = _rms_bwd(_mm_nt(dqx, w_xq[...]), xh1, r1, g[...])
        accs[2][...] += dg
        d1 = d2 + dx1
        outs[1][...] = d1
        d1b = d1.astype(BF16)
        outs[8][...] = d1b
        dm = _mm_nt(d1b, w_o[...])
        g0, g1 = _sig(rows[3][:, 0:D]), _sig(rows[3][:, D:])
        outs[2][:, 0:D] = (dm * rows[4][...] * g0 * (1.0 - g0)).astype(BF16)
        outs[2][:, D:] = (dm * rows[5][...] * g1 * (1.0 - g1)).astype(BF16)
        outs[3][...] = (dm * g0).astype(BF16)
        dmla = (dm * g1).astype(BF16)
        outs[4][...] = dmla
        do = _mm_nt(dmla, w_mo[...])
        outs[5][...] = do.astype(BF16)
        prod = do * rows[6][...]
        for h in range(NH):
            outs[6][h] = jnp.sum(prod[:, h * HP:(h + 1) * HP], axis=-1, keepdims=True)

    outs = [((t, XH * XD), BF16), ((t, D), F32), ((t, 2 * D), BF16), ((t, D), BF16), ((t, D), BF16), ((t, NH * HP), BF16),
            ((NH, t, 1), F32), ((t, D), BF16), ((t, D), BF16)]
    accs = [((m, XH * XD), F32), ((m, XH * XD), F32), ((1, D), F32)]
    return _row_call("merge_xattn_bwd", fn, n, [(a, "cur") for a in (dh2, h1, qx, gl, conv_out, mla, o)],
                     [wxo, kx, vx, wxq, g_x, wo, wmo], outs, accs)


def _mla_prep_bwd(dq, dk, dv, cq, ckv, tc, tsa, tsb, gq, wuq, gkv, wk, wv, n):
    t = dq.shape[0]

    def fn(i, rows, consts, outs, accs, scr):
        g_q, w_q, g_kv, w_k, w_v = consts
        c, sa, sb = rows[5][...], rows[6][...], rows[7][...]
        dkr = jnp.zeros((rows[0].shape[0], HP), F32)
        for h in range(NH):
            ln = slice(h * HP, (h + 1) * HP)
            outs[0][:, ln] = _rope_bwd(rows[0][:, ln] * MLA_SCALE, c, sa, sb).astype(BF16)
            dkr = dkr + rows[1][:, ln]
        lane = lax.broadcasted_iota(jnp.int32, dkr.shape, 1)
        outs[5][...] = jnp.where((lane >= NOPE) & (lane < NOPE + ROPE), _rope_bwd(dkr, c, sa, sb), 0.0).astype(BF16)
        dkb, dvb = rows[1][...].astype(BF16), rows[2][...].astype(BF16)
        outs[1][...] = dkb
        outs[2][...] = dvb
        _, xq, rq = _rms(rows[3][...], g_q[...])
        dcq, dgq = _rms_bwd(_mm_nt(outs[0][...], w_q[...]), xq, rq, g_q[...])
        outs[3][...] = dcq.astype(BF16)
        accs[0][...] += dgq
        _, xk, rk = _rms(rows[4][...], g_kv[...])
        dckv, dgk = _rms_bwd(_mm_nt(dkb, w_k[...]) + _mm_nt(dvb, w_v[...]), xk, rk, g_kv[...])
        outs[4][...] = dckv.astype(BF16)
        accs[1][...] += dgk

    outs = [((t, NH * HP), BF16)] * 3 + [((t, QL), BF16), ((t, KL), BF16), ((t, HP), BF16)]
    return _row_call("mla_prep_bwd", fn, n, [(a, "cur") for a in (dq, dk, dv, cq, ckv, tc, tsa, tsb)],
                     [gq, wuq, gkv, wk, wv], outs, [((1, QL), F32), ((1, KL), F32)])


def _conv_out_bwd(dco, zc, wco, lng, lnb, n):
    t = zc.shape[0]

    def fn(i, rows, consts, outs, accs, scr):
        wo, lg, lb = consts
        z = rows[1][...]
        mu = jnp.mean(z, axis=-1, keepdims=True)
        dlt = z - mu
        rs = lax.rsqrt(jnp.mean(dlt * dlt, axis=-1, keepdims=True) + EPS)
        xh = dlt * rs
        zn = xh * lg[...] + lb[...]
        sg = _sig(zn)
        outs[0][...] = (zn * sg).astype(BF16)
        dzn = _mm_nt(rows[0][...], wo[...]) * (sg * (1.0 + zn * (1.0 - sg)))
        accs[0][...] += jnp.sum(dzn * xh, axis=0, keepdims=True)
        accs[1][...] += jnp.sum(dzn, axis=0, keepdims=True)
        dxh = dzn * lg[...]
        dzc = rs * (dxh - jnp.mean(dxh, axis=-1, keepdims=True) - xh * jnp.mean(dxh * xh, axis=-1, keepdims=True))
        outs[1][...] = dzc
        accs[2][...] += jnp.sum(dzc, axis=0, keepdims=True)

    return _row_call("conv_out_bwd", fn, n, [(dco, "cur"), (zc, "cur")], [wco, lng, lnb],
                     [((t, CC), BF16), ((t, CC), F32)], [((1, CC), F32)] * 3)


def _conv_glu_bwd(dzc, a, gt, cw, n):
    t = a.shape[0]
    r = t // n

    def fn(i, rows, consts, outs, accs, scr):
        w = consts[0]
        zz, dd = scr
        _fill_glu_window(i, rows[2], rows[3], rows[4], rows[5], zz)
        dd[0:r, :] = rows[0][...]
        dd[r:, :] = jnp.where(i < n - 1, rows[1][0:HALO, :], 0.0)
        rb = min(CONV_ROWS, r)
        for c in range(CC // HP):
            ln = slice(c * HP, (c + 1) * HP)
            for base in range(0, r, rb):
                here = slice(base, base + rb)
                dcur = dd[here, ln]
                for o, win in _windows(zz, base, ln, rb, [HALO - (CW - 1) + j for j in range(CW)]):
                    j = o - (HALO - (CW - 1))
                    accs[0][j:j + 1, ln] += jnp.sum(dcur * win, axis=0, keepdims=True)
                acc = jnp.zeros((rb, HP), F32)
                for o, win in _windows(dd, base, ln, rb, [CW - 1 - j for j in range(CW)]):
                    j = CW - 1 - o
                    acc = acc + w[j:j + 1, ln] * win
                sg = _sig(rows[3][here, ln])
                outs[0][here, ln] = (acc * sg).astype(BF16)
                outs[0][here, CC + c * HP:CC + (c + 1) * HP] = (acc * rows[2][here, ln] * sg * (1.0 - sg)).astype(BF16)

    return _row_call("conv_glu_bwd", fn, n, [(dzc, "cur"), (dzc, "next"), (a, "cur"), (gt, "cur"), (a, "prev"), (gt, "prev")],
                     [cw], [((t, 2 * CC), BF16)], [((HALO, CC), F32)],
                     scratch=[pltpu.VMEM((r + HALO, CC), F32), pltpu.VMEM((r + HALO, CC), F32)])


def _in_proj_bwd(pieces, dh1, x, wp, g_mix, n, exchange):
    t = x.shape[0]
    offs = (P_A, P_Q, P_KV, P_KR, P_GL, P_END)

    def fn(i, rows, consts, outs, accs, scr):
        w, g = consts
        du = jnp.zeros((rows[0].shape[0], D), F32)
        for k in range(5):
            du = du + _mm_nt(rows[k][...], w[:, offs[k]:offs[k + 1]])
        _, xh, r = _rms(rows[6][...], g[...])
        dx, dg = _rms_bwd(du, xh, r, g[...])
        accs[0][...] += dg
        outs[0][...] = rows[5][...] + dx

    return _row_call("in_proj_bwd", fn, n, [(a, "cur") for a in list(pieces) + [dh1, x]], [wp, g_mix],
                     [((t, D), F32)], [((1, D), F32)], exchange=exchange)


def _mem_bwd(mem, dkx, dvx, g_mem, wxkv):
    m = mem.shape[0]

    def fn(i, rows, consts, outs, accs, scr):
        g, w = consts
        dkv = jnp.concatenate([rows[1][...], rows[2][...]], axis=1).astype(BF16)
        outs[0][...] = dkv
        _, xh, _ = _rms(rows[0][...], g[...])
        accs[0][...] += jnp.sum(_mm_nt(dkv, w[...]) * xh, axis=0, keepdims=True)

    return _row_call("mem_bwd", fn, 1, [(mem, "cur"), (dkx, "cur"), (dvx, "cur")], [g_mem, wxkv],
                     [((m, 2 * XH * XD), BF16)], [((1, D), F32)])


def _dw(name, xs, dy):
    t, k = xs.shape
    nn = dy.shape[1]
    tk, tn, tt = min(k, 1024), min(nn, 1024), min(t, DW_TILE)

    def body(x_ref, dy_ref, o_ref):
        @pl.when(pl.program_id(2) == 0)
        def _():
            o_ref[...] = jnp.zeros(o_ref.shape, F32)
        o_ref[...] += lax.dot_general(x_ref[...], dy_ref[...], (((0,), (0,)), ((), ())), preferred_element_type=F32)

    return pl.pallas_call(
        body, name=name, grid=(k // tk, nn // tn, t // tt),
        in_specs=[pl.BlockSpec((tt, tk), lambda a, b, c: (c, a)), pl.BlockSpec((tt, tn), lambda a, b, c: (c, b))],
        out_specs=pl.BlockSpec((tk, tn), lambda a, b, c: (a, b)),
        out_shape=jax.ShapeDtypeStruct((k, nn), F32),
        compiler_params=pltpu.CompilerParams(dimension_semantics=("arbitrary", "arbitrary", "arbitrary"), vmem_limit_bytes=VMEM_LIMIT),
    )(xs, dy)


def _all_gather_packed(shard):
    rws = shard.shape[0]

    def body(x_ref, out_ref, send_sems, recv_sems, local_sem):
        x, y, c = _coords()
        me, sibling = (x, y, c), (x, y, 1 - c)
        chips = [(1 - x, y), (x, 1 - y), (1 - x, 1 - y)]

        def slot(px, py, pc):
            return out_ref.at[4 * px + 2 * py + pc]

        def copy(k, block, to, src=None):
            return pltpu.make_async_remote_copy(
                src_ref=slot(*block) if src is None else src, dst_ref=slot(*block),
                send_sem=send_sems.at[k], recv_sem=recv_sems.at[k], device_id=to, device_id_type=MESH)

        mine = pltpu.make_async_copy(x_ref, slot(*me), local_sem)
        mine.start()
        first = [copy(0, me, sibling, src=x_ref)] + [copy(1 + j, me, (*chip, c), src=x_ref) for j, chip in enumerate(chips)]
        for cp in first:
            cp.start()
        passed = [copy(4 + j, (*chip, c), sibling) for j, chip in enumerate(chips)]
        for j, chip in enumerate(chips):
            copy(1 + j, (*chip, c), me).wait_recv()
            passed[j].start()
        copy(0, sibling, me).wait_recv()
        for j, chip in enumerate(chips):
            copy(4 + j, (*chip, 1 - c), me).wait_recv()
        for cp in first + passed:
            cp.wait_send()
        mine.wait()

    return pl.pallas_call(
        body, name="all_gather_weights",
        out_shape=jax.ShapeDtypeStruct((N_DEV, rws, 128), shard.dtype),
        in_specs=[pl.BlockSpec(memory_space=pl.ANY)], out_specs=pl.BlockSpec(memory_space=pl.ANY),
        scratch_shapes=[pltpu.SemaphoreType.DMA((7,)), pltpu.SemaphoreType.DMA((7,)), pltpu.SemaphoreType.DMA],
    )(shard)


def _adam(w, g, m, v):
    m = ADAM_B1 * m + (1.0 - ADAM_B1) * g
    v = ADAM_B2 * v + (1.0 - ADAM_B2) * (g * g)
    m_hat = m / (1.0 - ADAM_B1 ** ADAM_STEP)
    v_hat = v / (1.0 - ADAM_B2 ** ADAM_STEP)
    return -ADAM_LR * (m_hat / (jnp.sqrt(v_hat) + ADAM_EPS) + ADAM_WD * w), m, v


def _small_allreduce_adam(part, w, m, v):
    shape = part.shape

    def body(p_ref, w_ref, m_ref, v_ref, g_ref, d_ref, nm_ref, nv_ref, buf, send_sems, recv_sems):
        x, y, c = _coords()
        me = 4 * x + 2 * y + c
        buf[0] = p_ref[...]
        cps = []
        for j in range(1, N_DEV):
            jx, jy, jc = j >> 2, (j >> 1) & 1, j & 1
            peer = (1 - x if jx else x, 1 - y if jy else y, 1 - c if jc else c)
            cps.append(pltpu.make_async_remote_copy(src_ref=p_ref, dst_ref=buf.at[j], send_sem=send_sems.at[j - 1],
                                                    recv_sem=recv_sems.at[j - 1], device_id=peer, device_id_type=MESH))
        for cp in cps:
            cp.start()
        for cp in cps:
            cp.wait()
        g = buf[me]
        for d in range(1, N_DEV):
            g = g + buf[d ^ me]
        g_ref[...] = g
        d_ref[...], nm_ref[...], nv_ref[...] = _adam(w_ref[...], g, m_ref[...], v_ref[...])

    vm = pl.BlockSpec(memory_space=pltpu.VMEM)
    return pl.pallas_call(
        body, name="small_allreduce_adam", out_shape=[jax.ShapeDtypeStruct(shape, F32)] * 4,
        in_specs=[vm] * 4, out_specs=[vm] * 4,
        scratch_shapes=[pltpu.VMEM((N_DEV,) + shape, F32), pltpu.SemaphoreType.DMA((7,)), pltpu.SemaphoreType.DMA((7,))],
    )(part, w, m, v)


def _sum_parts(name, parts):
    rws = parts.shape[1]
    tile = max(d for d in range(16, PACK_TILE + 1, 16) if rws % d == 0)

    def body(p_ref, g_ref):
        g = p_ref[0]
        for d in range(1, N_DEV):
            g = g + p_ref[d]
        g_ref[...] = g

    return pl.pallas_call(
        body, name=name, grid=(rws // tile,), in_specs=[pl.BlockSpec((N_DEV, tile, 128), lambda i: (0, i, 0))],
        out_specs=pl.BlockSpec((tile, 128), lambda i: (i, 0)), out_shape=jax.ShapeDtypeStruct((rws, 128), F32),
        compiler_params=pltpu.CompilerParams(dimension_semantics=("arbitrary",), vmem_limit_bytes=VMEM_LIMIT),
    )(parts)


def _adam_call(name, w, g, m, v):
    def body(w_ref, g_ref, m_ref, v_ref, d_ref, nm_ref, nv_ref):
        d_ref[...], nm_ref[...], nv_ref[...] = _adam(w_ref[...], g_ref[...], m_ref[...], v_ref[...])

    vm = pl.BlockSpec(memory_space=pltpu.VMEM)
    return pl.pallas_call(
        body, name=name, in_specs=[vm] * 4, out_specs=[vm] * 3, out_shape=[jax.ShapeDtypeStruct(w.shape, F32)] * 3,
        compiler_params=pltpu.CompilerParams(vmem_limit_bytes=VMEM_LIMIT),
    )(w, g, m, v)


ROW_SHARDED = ("w_out", "w_xq", "w_xkv", "w_mlp2")
SMALL = ("norm_mix_g", "conv_b", "conv_ln_g", "conv_ln_b", "q_norm_g", "kv_norm_g", "norm_xattn_g", "norm_mem_g", "norm_mlp_g", "final_norm_g")
GATHER_FIRST = ("w_in", "conv_w", "w_conv_out", "w_uq", "w_ukv")
GATHER_LATE = ("w_mla_out", "w_out", "w_xq", "w_xkv", "w_xo", "w_mlp1", "w_mlp2")
REDUCE_EARLY = ("w_mlp1", "w_mlp2", "conv_w", "w_conv_out", "w_mla_out", "w_out", "w_xq", "w_xkv", "w_xo")
REDUCE_LAST = ("w_in", "w_uq", "w_ukv")
ROW_ALIGN = 16


def _padded(k, nn):
    return -(-k // ROW_ALIGN) * ROW_ALIGN, -(-nn // 128) * 128


def _pack_rows(a):
    k, nn = a.shape[-2:]
    kp, np_ = _padded(k, nn)
    if (kp, np_) != (k, nn):
        a = jnp.pad(a, [(0, 0)] * (a.ndim - 2) + [(0, kp - k), (0, np_ - nn)])
    if np_ == 128:
        return a
    lead = a.shape[:-2]
    return jnp.swapaxes(a.reshape(lead + (kp, np_ // 128, 128)), -2, -3).reshape(lead + (kp * np_ // 128, 128))


def _unpack_rows(p, k, nn):
    kp, np_ = _padded(k, nn)
    if np_ != 128:
        lead = p.shape[:-2]
        p = jnp.swapaxes(p.reshape(lead + (np_ // 128, kp, 128)), -2, -3).reshape(lead + (kp, np_))
    return p[..., :k, :nn]


def _pack_group(arrays):
    return jnp.concatenate([_pack_rows(a) for a in arrays], axis=-2)


def _unpack_group(p, shapes):
    out, off = [], 0
    for k, nn in shapes:
        kp, np_ = _padded(k, nn)
        rws = kp * np_ // 128
        out.append(_unpack_rows(p[..., off:off + rws, :], k, nn))
        off += rws
    return out


def _tile_shaped(k, nn):
    return k % ROW_ALIGN == 0 and nn % 128 == 0


def _packed_to_full(name, seg, shard_shape):
    k, nn = shard_shape
    if not _tile_shaped(k, nn):
        stacked = _unpack_rows(seg, k, nn)
        return jnp.transpose(stacked, (1, 0, 2)).reshape(k, N_DEV * nn)
    a = seg.reshape(N_DEV, nn // 128, k, 128)
    if name in ROW_SHARDED:
        return jnp.transpose(a, (0, 2, 1, 3)).reshape(N_DEV * k, nn)
    return jnp.transpose(a, (2, 0, 1, 3)).reshape(k, N_DEV * nn)


def _full_to_packed(name, full, shard_shape):
    k, nn = shard_shape
    if not _tile_shaped(k, nn):
        return _pack_rows(jnp.transpose(full.reshape(k, N_DEV, nn), (1, 0, 2)))
    if name in ROW_SHARDED:
        a = jnp.transpose(full.reshape(N_DEV, k, nn // 128, 128), (0, 2, 1, 3))
    else:
        a = jnp.transpose(full.reshape(k, N_DEV, nn // 128, 128), (1, 2, 0, 3))
    return a.reshape(N_DEV, k * nn // 128, 128)


def _pack_small(vals):
    flat = jnp.concatenate([v.reshape(-1) for v in vals])
    rws = flat.shape[0] // 128
    return jnp.pad(flat, (0, (-(-rws // 8) * 8 - rws) * 128)).reshape(-1, 128)


def _unpack_small(p, sizes):
    flat = p.reshape(-1)
    out, off = [], 0
    for s in sizes:
        out.append(flat[off:off + s])
        off += s
    return out


def _head_pad(w, real, axis):
    shp = list(w.shape)
    shp[axis:axis + 1] = [NH, real]
    w = w.reshape(shp)
    pad = [(0, 0)] * w.ndim
    pad[axis + 1] = (0, HP - real)
    w = jnp.pad(w, pad)
    shp[axis:axis + 2] = [NH * HP]
    return w.reshape(shp)


def _head_unpad(w, lo, real, axis):
    shp = list(w.shape)
    shp[axis:axis + 1] = [NH, HP]
    w = lax.slice_in_dim(w.reshape(shp), lo, lo + real, axis=axis + 1)
    shp[axis:axis + 2] = [NH * real]
    return w.reshape(shp)


def kernel(x, mem, positions, norm_mix_g, w_in, conv_w, conv_b, conv_ln_g, conv_ln_b, w_conv_out, q_norm_g, w_uq, kv_norm_g, w_ukv, w_mla_out, w_out, norm_xattn_g, norm_mem_g, w_xq, w_xkv, w_xo, norm_mlp_g, w_mlp1, w_mlp2, final_norm_g, loss_target, m_norm_mix_g, m_w_in, m_conv_w, m_conv_b, m_conv_ln_g, m_conv_ln_b, m_w_conv_out, m_q_norm_g, m_w_uq, m_kv_norm_g, m_w_ukv, m_w_mla_out, m_w_out, m_norm_xattn_g, m_norm_mem_g, m_w_xq, m_w_xkv, m_w_xo, m_norm_mlp_g, m_w_mlp1, m_w_mlp2, m_final_norm_g, v_norm_mix_g, v_w_in, v_conv_w, v_conv_b, v_conv_ln_g, v_conv_ln_b, v_w_conv_out, v_q_norm_g, v_w_uq, v_kv_norm_g, v_w_ukv, v_w_mla_out, v_w_out, v_norm_xattn_g, v_norm_mem_g, v_w_xq, v_w_xkv, v_w_xo, v_norm_mlp_g, v_w_mlp1, v_w_mlp2, v_final_norm_g):
    args = dict(locals())
    t = x.shape[1]
    n = t // min(ROW_TILE, t)
    xs, mems, tgt = x[0], mem[0], loss_target[0]

    def pack_shards(prefix, group, dtype):
        return _pack_group([args[prefix + k][0].astype(dtype) for k in group])

    def shapes(group):
        return [args[k].shape[1:] for k in group]

    def unpack_full(gathered, group):
        out, off = {}, 0
        for k, shp in zip(group, shapes(group)):
            kp, np_ = _padded(*shp)
            out[k] = _packed_to_full(k, gathered[:, off:off + kp * np_ // 128], shp)
            off += kp * np_ // 128
        return out

    full = unpack_full(_all_gather_packed(pack_shards("", GATHER_FIRST, BF16)), GATHER_FIRST)

    wi = full["w_in"]
    kr_slot = jnp.pad(wi[:, P_KR:P_KR + ROPE], ((0, 0), (NOPE, HP - NOPE - ROPE)))
    wp = jnp.concatenate([wi[:, :P_KR], kr_slot, wi[:, P_KR + ROPE:]], axis=1)
    cw = jnp.pad(full["conv_w"].astype(F32), ((0, HALO - CW), (0, 0)))
    wuq = _head_pad(full["w_uq"], NOPE + ROPE, 1)
    ukv = full["w_ukv"].reshape(KL, NH, NOPE + VD)
    wk = _head_pad(ukv[:, :, :NOPE].reshape(KL, NH * NOPE), NOPE, 1)
    wv = _head_pad(ukv[:, :, NOPE:].reshape(KL, NH * VD), VD, 1)

    inv_freq = THETA ** (-jnp.arange(ROPE // 2, dtype=F32) / (ROPE // 2))
    ang = positions[0].astype(F32)[:, None] * inv_freq
    cs, sn, zr = jnp.cos(ang), jnp.sin(ang), jnp.zeros((t, ROPE // 2), F32)
    tail = jnp.zeros((t, HP - NOPE - ROPE), F32)
    tc = jnp.concatenate([jnp.ones((t, NOPE), F32), cs, cs, tail], axis=1)
    tsa = jnp.concatenate([jnp.zeros((t, NOPE), F32), -sn, zr, tail], axis=1)
    tsb = jnp.concatenate([jnp.zeros((t, NOPE), F32), zr, sn, tail], axis=1)

    u0, a, gt, cq, ckv, krp, gl = _in_proj(xs, norm_mix_g, wp, n)
    zc, conv_out = _conv_branch(a, gt, cw, conv_b, conv_ln_g, conv_ln_b, full["w_conv_out"], n)
    qh, kh, vh, cqn, ckvn = _mla_prep(cq, ckv, krp, tc, tsa, tsb, q_norm_g, wuq, kv_norm_g, wk, wv, n)
    o, lse, gathered = _flash_fwd(qh, kh, vh, exchange=(pack_shards("", GATHER_LATE, BF16), True))
    full.update(unpack_full(gathered, GATHER_LATE))
    wmo = _head_pad(full["w_mla_out"], VD, 0)
    memn, kx, vx = _mem_kv(mems, norm_mem_g, full["w_xkv"])
    mla, merged, h1, u1, qx, ox, h2, ob = _merge_xattn(o, gl, conv_out, xs, wmo, full["w_out"], norm_xattn_g, full["w_xq"],
                                                        kx, vx, full["w_xo"], n)
    gfin = final_norm_g.reshape(1, D)
    u2, rl2, da1, dh3b, dh2, loss_p, dg_fin, dg_mlp = _mlp_loss(h2, tgt, norm_mlp_g, full["w_mlp1"], full["w_mlp2"], gfin, n)

    (dqx, dh1, dgl, dco, dmla, dob, delta, dh2b, dh1b, dkx, dvx, dg_x) = _merge_xattn_bwd(
        dh2, h1, qx, gl, conv_out, mla, o, full["w_xo"], kx, vx, full["w_xq"], norm_xattn_g, full["w_out"], wmo, n)
    gfull = {"w_mlp1": _dw("dw_mlp1", u2, da1), "w_mlp2": _dw("dw_mlp2", rl2, dh3b)}

    def stacked(group):
        return jnp.concatenate([_full_to_packed(k, gfull[k], args[k].shape[1:]) for k in group], axis=1)

    zs, dzc, dg_lng, dg_lnb, dg_cb = _conv_out_bwd(dco, zc, full["w_conv_out"], conv_ln_g, conv_ln_b, n)
    dci, dcw = _conv_glu_bwd(dzc, a, gt, cw, n)
    dkv, dg_mem = _mem_bwd(mems, dkx, dvx, norm_mem_g, full["w_xkv"])
    gfull.update({
        "conv_w": dcw[:CW],
        "w_conv_out": _dw("dw_conv_out", zs, dco),
        "w_mla_out": _head_unpad(_dw("dw_mla_out", ob, dmla), 0, VD, 0),
        "w_out": _dw("dw_out", merged, dh1b),
        "w_xq": _dw("dw_xq", u1, dqx),
        "w_xkv": _dw("dw_xkv", memn, dkv),
        "w_xo": _dw("dw_xo", ox, dh2b),
    })
    dk, dv, dq, parts_early = _flash_bwd(qh, kh, vh, dob, lse.reshape(NH, 1, t), delta.reshape(NH, 1, t),
                                         exchange=(stacked(REDUCE_EARLY), False))
    dqp, dkb, dvb, dcq, dckv, dkrp, dg_q, dg_kv = _mla_prep_bwd(dq, dk, dv, cq, ckv, tc, tsa, tsb, q_norm_g, wuq, kv_norm_g, wk, wv, n)
    pieces = (dci, dcq, dckv, dkrp, dgl)
    dwp = [_dw("dw_in_%d" % k, u0, p) for k, p in enumerate(pieces)]
    gfull["w_in"] = jnp.concatenate([dwp[0], dwp[1], dwp[2], dwp[3][:, NOPE:NOPE + ROPE], dwp[4]], axis=1)
    gfull["w_uq"] = _head_unpad(_dw("dw_uq", cqn, dqp), 0, NOPE + ROPE, 1)
    gk = _head_unpad(_dw("dw_uk", ckvn, dkb), 0, NOPE, 1).reshape(KL, NH, NOPE)
    gv = _head_unpad(_dw("dw_uv", ckvn, dvb), 0, VD, 1).reshape(KL, NH, VD)
    gfull["w_ukv"] = jnp.concatenate([gk, gv], axis=2).reshape(KL, NH * (NOPE + VD))
    grad_x, dg_mix, parts_last = _in_proj_bwd(pieces, dh1, xs, wp, norm_mix_g, n, (stacked(REDUCE_LAST), False))

    big = [{}, {}, {}, {}]
    for name, group, parts in (("grad_sum_early", REDUCE_EARLY, parts_early), ("grad_sum_last", REDUCE_LAST, parts_last)):
        for k, g in zip(group, _unpack_group(_sum_parts(name, parts), shapes(group))):
            res = _adam_call("adam_" + k, args[k][0], g, args["m_" + k][0], args["v_" + k][0])
            for kind, val in enumerate([g] + list(res)):
                big[kind][k] = val[None]

    small_g = {"norm_mix_g": dg_mix, "conv_b": dg_cb, "conv_ln_g": dg_lng, "conv_ln_b": dg_lnb, "q_norm_g": dg_q,
               "kv_norm_g": dg_kv, "norm_xattn_g": dg_x, "norm_mem_g": dg_mem, "norm_mlp_g": dg_mlp, "final_norm_g": dg_fin}
    small_sizes = [int(np.prod(args[k].shape)) for k in SMALL] + [128]
    zero_slot = jnp.zeros((128,), F32)
    small_out = _small_allreduce_adam(
        _pack_small([small_g[k] for k in SMALL] + [jnp.pad(loss_p.reshape(-1), (0, 127))]),
        _pack_small([args[k] for k in SMALL] + [zero_slot]), _pack_small([args["m_" + k] for k in SMALL] + [zero_slot]),
        _pack_small([args["v_" + k] for k in SMALL] + [zero_slot]))

    small = [dict(zip(SMALL, [s.reshape(args[k].shape) for k, s in zip(SMALL, _unpack_small(o_, small_sizes))])) for o_ in small_out]
    loss = _unpack_small(small_out[0], small_sizes)[-1][0]
    order = ("norm_mix_g", "w_in", "conv_w", "conv_b", "conv_ln_g", "conv_ln_b", "w_conv_out", "q_norm_g", "w_uq", "kv_norm_g",
             "w_ukv", "w_mla_out", "w_out", "norm_xattn_g", "norm_mem_g", "w_xq", "w_xkv", "w_xo", "norm_mlp_g", "w_mlp1",
             "w_mlp2", "final_norm_g")
    res = [loss, grad_x[None]]
    for kind in range(4):
        res += [big[kind][k] if k in big[kind] else small[kind][k] for k in order]
    return tuple(res)
```

```python
import functools

import jax
import jax.numpy as jnp
import numpy as np
from jax import lax
from jax.experimental import pallas as pl
from jax.experimental.pallas import tpu as pltpu

F32, BF16 = jnp.float32, jnp.bfloat16
MESH = pl.DeviceIdType.MESH

N_DEV = 8
D = 1024
CC = D // 2
CW = 31
HALO = 32
NH = 8
NOPE, ROPE, VD = D // 16, D // 32, D // 16
QL, KL = 3 * D // 8, D // 4
HP = 128
XH, XD = 4, D // 8
DFF = 4 * D
EPS = 1e-6
THETA = 10000.0
MLA_SCALE = float((NOPE + ROPE) ** -0.5)
X_SCALE = float(XD ** -0.5)
NEG = -1e30
P_A, P_G, P_Q, P_KV, P_KR, P_GL, P_END = 0, CC, 2 * CC, 2 * CC + QL, 2 * CC + QL + KL, 2 * CC + QL + KL + HP, 2 * CC + QL + KL + HP + 2 * D

ADAM_LR, ADAM_B1, ADAM_B2, ADAM_EPS, ADAM_WD, ADAM_STEP = 0.001, 0.9, 0.999, 1e-08, 0.01, 10

ROW_TILE = 256
ATT_TILE = 512
HG = 2
HG_FWD = 4
ATT_CHUNK = 32
CONV_ROWS = 128
DW_TILE = 1024
PACK_TILE = 1536
VMEM_LIMIT = 56 * 1024 * 1024


def _mm(a, w):
    return jnp.dot(a.astype(BF16), w, preferred_element_type=F32)


def _mm_nt(a, w):
    return lax.dot_general(a.astype(BF16), w, (((1,), (1,)), ((), ())), preferred_element_type=F32)


def _mm_tn(a, b):
    return lax.dot_general(a.astype(BF16), b.astype(BF16), (((0,), (0,)), ((), ())), preferred_element_type=F32)


def _rms(x, g):
    r = lax.rsqrt(jnp.mean(x * x, axis=-1, keepdims=True) + EPS)
    xh = x * r
    return xh * g, xh, r


def _rms_bwd(dy, xh, r, g):
    dxh = dy * g
    dx = r * (dxh - xh * jnp.mean(dxh * xh, axis=-1, keepdims=True))
    return dx, jnp.sum(dy * xh, axis=0, keepdims=True)


def _sig(x):
    return 1.0 / (1.0 + jnp.exp(-x))


def _coords():
    return lax.axis_index("x"), lax.axis_index("y"), lax.axis_index("c")


def _exchange_ops(src_ref, mode, dst_ref, send_sems, recv_sems, local_sem):
    x, y, c = _coords()
    me = 4 * x + 2 * y + c
    owned = isinstance(mode, tuple)

    def owner_slot(d):
        return sum(jnp.where(d == o, n, 0) for n, o in enumerate(mode))

    def is_owner(d):
        return functools.reduce(jnp.logical_or, [d == o for o in mode])

    def src(d):
        return src_ref if mode is True else src_ref.at[owner_slot(d) if owned else d]

    local = (pltpu.make_async_copy(src(me), dst_ref.at[me], local_sem), is_owner(me) if owned else True)
    remote = []
    for j in range(1, N_DEV):
        px, py, pc = (1 - x if j & 4 else x), (1 - y if j & 2 else y), (1 - c if j & 1 else c)
        d = 4 * px + 2 * py + pc
        cp = pltpu.make_async_remote_copy(src_ref=src(d), dst_ref=dst_ref.at[me], send_sem=send_sems.at[j - 1],
                                          recv_sem=recv_sems.at[j - 1], device_id=(px, py, pc), device_id_type=MESH)
        remote.append((cp, is_owner(d) if owned else True, is_owner(me) if owned else True))
    return local, remote


def _when(cond, fn):
    if cond is True:
        fn()
    else:
        pl.when(cond)(fn)


def _exchange_hook(first, last, mode, refs):
    @pl.when(first)
    def _():
        (local, here), remote = _exchange_ops(refs[0], mode, *refs[1:])
        _when(here, local.start)
        for cp, sends, _ in remote:
            _when(sends, cp.start)

    @pl.when(last)
    def _():
        (local, here), remote = _exchange_ops(refs[0], mode, *refs[1:])
        for cp, sends, receives in remote:
            _when(sends, cp.wait_send)
            _when(receives, cp.wait_recv)
        _when(here, local.wait)


def _exchange_shapes(exchange):
    if exchange is None:
        return [], [], [], []
    arr, mode = exchange
    shape = (N_DEV,) + (arr.shape if mode is True else arr.shape[1:])
    any_spec = pl.BlockSpec(memory_space=pl.ANY)
    sems = [pltpu.SemaphoreType.DMA((N_DEV - 1,)), pltpu.SemaphoreType.DMA((N_DEV - 1,)), pltpu.SemaphoreType.DMA]
    return [any_spec], [any_spec], [jax.ShapeDtypeStruct(shape, arr.dtype)], sems


def _row_call(name, fn, n, rows, consts, outs, accs=(), scratch=(), exchange=None):
    def row_spec(shape, mode):
        r = shape[-2] // n
        if mode == "cur":
            f = lambda i: i
        elif mode == "prev":
            f = lambda i: jnp.maximum(i - 1, 0)
        else:
            f = lambda i: jnp.minimum(i + 1, n - 1)
        if len(shape) == 2:
            return pl.BlockSpec((r, shape[1]), lambda i: (f(i), 0))
        return pl.BlockSpec((shape[0], r, shape[2]), lambda i: (0, f(i), 0))

    def whole_spec(shape, single):
        nd = len(shape)
        if single:
            return pl.BlockSpec(shape, lambda i: (0,) * nd, pipeline_mode=pl.Buffered(1))
        return pl.BlockSpec(shape, lambda i: (0,) * nd)

    nr, nc, no, na, ns = len(rows), len(consts), len(outs), len(accs), len(scratch)
    exchanges = [] if exchange is None else (exchange if isinstance(exchange, list) else [exchange])
    shapes = [_exchange_shapes(e) for e in exchanges]
    x_in, x_out, x_shape, x_sems = (sum((s[part] for s in shapes), []) for part in range(4))
    nx = len(exchanges)

    def body(*refs):
        i = pl.program_id(0)
        row_refs, const_refs = refs[:nr], refs[nr:nr + nc]
        o0 = nr + nc + nx
        out_refs, acc_refs = refs[o0:o0 + no], refs[o0 + no:o0 + no + na]
        s0 = o0 + no + na + nx
        for e in range(nx):
            sems = refs[s0 + ns + 3 * e:s0 + ns + 3 * e + 3]
            _exchange_hook(i == 0, i == n - 1, exchanges[e][1], (refs[nr + nc + e], refs[o0 + no + na + e]) + tuple(sems))
        if na:
            @pl.when(i == 0)
            def _():
                for a in acc_refs:
                    a[...] = jnp.zeros(a.shape, a.dtype)
        fn(i, row_refs, const_refs, out_refs, acc_refs, refs[s0:s0 + ns])

    res = pl.pallas_call(
        body, name=name, grid=(n,),
        in_specs=[row_spec(a.shape, m) for a, m in rows] + [whole_spec(c.shape, True) for c in consts] + x_in,
        out_specs=[row_spec(s, "cur") for s, _ in outs] + [whole_spec(s, False) for s, _ in accs] + x_out,
        out_shape=[jax.ShapeDtypeStruct(s, d) for s, d in list(outs) + list(accs)] + x_shape,
        scratch_shapes=list(scratch) + x_sems,
        compiler_params=pltpu.CompilerParams(dimension_semantics=("arbitrary",), vmem_limit_bytes=VMEM_LIMIT),
    )(*[a for a, _ in rows], *consts, *[e[0] for e in exchanges])
    return list(res)


def _in_proj(x, g_mix, wp, n):
    t = x.shape[0]

    def fn(i, rows, consts, outs, accs, scr):
        g, w = consts
        u, _, _ = _rms(rows[0][...], g[...])
        ub = u.astype(BF16)
        outs[0][...] = ub
        for k, (lo, hi) in enumerate(((P_A, P_G), (P_G, P_Q), (P_Q, P_KV), (P_KV, P_KR), (P_KR, P_GL), (P_GL, P_END))):
            outs[1 + k][...] = _mm(ub, w[:, lo:hi])

    outs = [((t, D), BF16), ((t, CC), F32), ((t, CC), F32), ((t, QL), F32), ((t, KL), F32), ((t, HP), F32), ((t, 2 * D), F32)]
    return _row_call("in_proj", fn, n, [(x, "cur")], [g_mix, wp], outs)


def _fill_glu_window(i, a, gt, ap, gtp, zz):
    r = a.shape[0]
    zp = ap[r - HALO:, :] * _sig(gtp[r - HALO:, :])
    zz[0:HALO, :] = jnp.where(i > 0, zp, 0.0)
    zz[HALO:, :] = a[...] * _sig(gt[...])


def _shift_copies(buf, sh):
    rows = buf.shape[0]
    for s in range(8):
        sh[s, 0:rows - s, :] = buf[s:rows, :]


def _window(sh, o, base, rb, ln):
    return sh[o % 8, base + o - o % 8:base + o - o % 8 + rb, ln]


def _windows(buf, base, ln, rb, offsets):
    for s in range(8):
        group = [o for o in offsets if o % 8 == s]
        if group:
            shifted = buf[base + s:base + max(group) + rb, ln]
            for o in group:
                yield o, shifted[o - s:o - s + rb]


def _conv_branch(a, gt, cw, cb, lng, lnb, wco, n):
    t = a.shape[0]
    r = t // n

    def fn(i, rows, consts, outs, accs, scr):
        w, b, lg, lb, wo = consts
        zz, zsh = scr
        _fill_glu_window(i, rows[0], rows[1], rows[2], rows[3], zz)
        _shift_copies(zz, zsh)
        rb = min(CONV_ROWS, r)
        for c in range(CC // HP):
            ln = slice(c * HP, (c + 1) * HP)
            for base in range(0, r, rb):
                acc = jnp.zeros((rb, HP), F32)
                for j in range(CW):
                    acc = acc + w[j:j + 1, ln] * _window(zsh, HALO - (CW - 1) + j, base, rb, ln)
                outs[0][base:base + rb, ln] = acc + b[:, ln]
        zc = outs[0][...]
        mu = jnp.mean(zc, axis=-1, keepdims=True)
        dlt = zc - mu
        rs = lax.rsqrt(jnp.mean(dlt * dlt, axis=-1, keepdims=True) + EPS)
        zn = dlt * rs * lg[...] + lb[...]
        outs[1][...] = _mm(zn * _sig(zn), wo[...])

    return _row_call("conv_branch", fn, n, [(a, "cur"), (gt, "cur"), (a, "prev"), (gt, "prev")],
                     [cw, cb, lng, lnb, wco], [((t, CC), F32), ((t, D), F32)],
                     scratch=[pltpu.VMEM((r + HALO, CC), F32), pltpu.VMEM((8, r + HALO, CC), F32)])


def _rope(v, c, sa, sb):
    return v * c + pltpu.roll(v, HP - ROPE // 2, 1) * sa + pltpu.roll(v, ROPE // 2, 1) * sb


def _rope_bwd(dv, c, sa, sb):
    return dv * c + pltpu.roll(dv * sa, ROPE // 2, 1) + pltpu.roll(dv * sb, HP - ROPE // 2, 1)


def _mla_prep(cq, ckv, krp, tc, tsa, tsb, gq, wuq, gkv, wk, wv, n):
    t = cq.shape[0]

    def fn(i, rows, consts, outs, accs, scr):
        g_q, w_q, g_kv, w_k, w_v = consts
        c, sa, sb = rows[3][...], rows[4][...], rows[5][...]
        cqn = _rms(rows[0][...], g_q[...])[0].astype(BF16)
        ckvn = _rms(rows[1][...], g_kv[...])[0].astype(BF16)
        outs[3][...] = cqn
        outs[4][...] = ckvn
        krr = _rope(rows[2][...], c, sa, sb)
        for h in range(NH):
            ln = slice(h * HP, (h + 1) * HP)
            outs[0][:, ln] = (_rope(_mm(cqn, w_q[:, ln]), c, sa, sb) * MLA_SCALE).astype(BF16)
            outs[1][:, ln] = (_mm(ckvn, w_k[:, ln]) + krr).astype(BF16)
        vv = _mm(ckvn, w_v[...])
        lane = lax.broadcasted_iota(jnp.int32, vv.shape, 1)
        outs[2][...] = jnp.where((lane & (HP - 1)) == VD, 1.0, vv).astype(BF16)

    outs = [((t, NH * HP), BF16)] * 3 + [((t, QL), BF16), ((t, KL), BF16)]
    return _row_call("mla_prep", fn, n, [(a, "cur") for a in (cq, ckv, krp, tc, tsa, tsb)], [gq, wuq, gkv, wk, wv], outs)


def _chunk_mask(c, tq, transposed, rows=ATT_CHUNK):
    row = lax.broadcasted_iota(jnp.int32, (rows, tq), 0) + c * rows
    col = lax.broadcasted_iota(jnp.int32, (rows, tq), 1)
    return (row <= col) if transposed else (col <= row)


def _flash_hook(exchange, nh, nq, refs):
    if exchange is not None:
        h, i = pl.program_id(0), pl.program_id(1)
        _exchange_hook((h == 0) & (i == 0), (h == nh - 1) & (i == nq - 1), exchange[1], refs)


def _head_lanes(g):
    return slice(g * HP, (g + 1) * HP)


def _flash_fwd(q, k, v, exchange=None):
    t = q.shape[0]
    tq = min(ATT_TILE, t)
    nq = t // tq
    x_in, x_out, x_shape, x_sems = _exchange_shapes(exchange)

    def body(q_ref, k_ref, v_ref, *rest):
        o_ref, lse_ref = rest[len(x_in):len(x_in) + 2]
        _flash_hook(exchange, NH // HG_FWD, nq, rest[:len(x_in)] + rest[len(x_in) + 2:])
        i = pl.program_id(1)

        def step(j, carry, masked):
            at = pl.ds(pl.multiple_of(j * tq, tq), tq)
            out = []
            for g in range(HG_FWD):
                m, acc = carry[g]
                s = _mm_nt(q_ref[:, _head_lanes(g)], k_ref[at, _head_lanes(g)])
                if masked:
                    s = jnp.where(_chunk_mask(0, tq, False, tq), s, NEG)
                m_new = jnp.maximum(m, jnp.max(s, axis=-1, keepdims=True))
                out.append((m_new, jnp.exp(m - m_new) * acc + _mm(jnp.exp(s - m_new), v_ref[at, _head_lanes(g)])))
            return tuple(out)

        init = tuple((jnp.full((tq, 1), NEG, F32), jnp.zeros((tq, HP), F32)) for _ in range(HG_FWD))
        carry = lax.fori_loop(0, i, lambda j, c: step(j, c, False), init)
        for g, (m, acc) in enumerate(step(i, carry, True)):
            lane = lax.broadcasted_iota(jnp.int32, acc.shape, 1)
            l = jnp.sum(jnp.where(lane == VD, acc, 0.0), axis=-1, keepdims=True)
            o_ref[:, _head_lanes(g)] = acc / l
            lse_ref[g] = m + jnp.log(l)

    wide = HG_FWD * HP
    return pl.pallas_call(
        body, name="flash_fwd", grid=(NH // HG_FWD, nq),
        in_specs=[pl.BlockSpec((tq, wide), lambda h, i: (i, h)), pl.BlockSpec((t, wide), lambda h, i: (0, h)),
                  pl.BlockSpec((t, wide), lambda h, i: (0, h))] + x_in,
        out_specs=[pl.BlockSpec((tq, wide), lambda h, i: (i, h)), pl.BlockSpec((HG_FWD, tq, 1), lambda h, i: (h, i, 0))] + x_out,
        out_shape=[jax.ShapeDtypeStruct((t, NH * HP), F32), jax.ShapeDtypeStruct((NH, t, 1), F32)] + x_shape,
        scratch_shapes=x_sems,
        compiler_params=pltpu.CompilerParams(dimension_semantics=("arbitrary", "arbitrary"), vmem_limit_bytes=VMEM_LIMIT),
    )(q, k, v, *([exchange[0]] if x_in else []))


def _flash_bwd(q, k, v, do, lse_row, delta_row, exchange=None):
    t = q.shape[0]
    tq = min(ATT_TILE, t)
    nq = t // tq
    x_in, x_out, x_shape, x_sems = _exchange_shapes(exchange)

    def body(q_ref, k_ref, v_ref, do_ref, lse_ref, dl_ref, *rest):
        dk_ref, dv_ref, dq_ref = rest[len(x_in):len(x_in) + 3]
        st_scr, dpt_scr, pt_scr, dst_scr = rest[len(rest) - 4:]
        _flash_hook(exchange, NH // HG, nq, rest[:len(x_in)] + rest[len(x_in) + 3:len(rest) - 4])
        j = pl.program_id(1)
        dk_ref[...] = jnp.zeros(dk_ref.shape, F32)
        dv_ref[...] = jnp.zeros(dv_ref.shape, F32)

        @pl.when(j == 0)
        def _():
            dq_ref[...] = jnp.zeros(dq_ref.shape, F32)

        def step(i, masked):
            at = pl.ds(pl.multiple_of(i * tq, tq), tq)
            for g in range(HG):
                st_scr[g] = _mm_nt(k_ref[:, _head_lanes(g)], q_ref[at, _head_lanes(g)])
                dpt_scr[g] = _mm_nt(v_ref[:, _head_lanes(g)], do_ref[at, _head_lanes(g)])
            for g in range(HG):
                lse_i, dl_i = lse_ref[g, :, at], dl_ref[g, :, at]
                for c in range(tq // ATT_CHUNK):
                    rows = slice(c * ATT_CHUNK, (c + 1) * ATT_CHUNK)
                    st = st_scr[g, rows, :]
                    if masked:
                        st = jnp.where(_chunk_mask(c, tq, True), st, NEG)
                    pt = jnp.exp(st - lse_i)
                    pt_scr[g, rows, :] = pt.astype(BF16)
                    dst_scr[g, rows, :] = (pt * (dpt_scr[g, rows, :] - dl_i)).astype(BF16)
            for g in range(HG):
                dv_ref[:, _head_lanes(g)] += _mm(pt_scr[g], do_ref[at, _head_lanes(g)])
                dk_ref[:, _head_lanes(g)] += _mm(dst_scr[g], q_ref[at, _head_lanes(g)])
                dq_ref[at, _head_lanes(g)] += _mm_tn(dst_scr[g], k_ref[:, _head_lanes(g)])

        step(j, True)
        lax.fori_loop(j + 1, nq, lambda i, c: (step(i, False), c)[1], 0)

    wide = HG * HP
    blk = pl.BlockSpec((tq, wide), lambda h, j: (j, h))
    whole = pl.BlockSpec((t, wide), lambda h, j: (0, h))
    row = pl.BlockSpec((HG, 1, t), lambda h, j: (h, 0, 0))
    x_sems = x_sems + [pltpu.VMEM((HG, tq, tq), F32), pltpu.VMEM((HG, tq, tq), F32), pltpu.VMEM((HG, tq, tq), BF16),
                       pltpu.VMEM((HG, tq, tq), BF16)]
    return pl.pallas_call(
        body, name="flash_bwd", grid=(NH // HG, nq), in_specs=[whole, blk, blk, whole, row, row] + x_in,
        out_specs=[blk, blk, whole] + x_out, out_shape=[jax.ShapeDtypeStruct((t, NH * HP), F32)] * 3 + x_shape, scratch_shapes=x_sems,
        compiler_params=pltpu.CompilerParams(dimension_semantics=("arbitrary", "arbitrary"), vmem_limit_bytes=VMEM_LIMIT),
    )(q, k, v, do, lse_row, delta_row, *([exchange[0]] if x_in else []))


def _mem_kv(mem, g_mem, wxkv):
    m = mem.shape[0]

    def fn(i, rows, consts, outs, accs, scr):
        mn = _rms(rows[0][...], consts[0][...])[0].astype(BF16)
        outs[0][...] = mn
        outs[1][...] = _mm(mn, consts[1][:, 0:XH * XD]).astype(BF16)
        outs[2][...] = _mm(mn, consts[1][:, XH * XD:]).astype(BF16)

    return _row_call("mem_kv", fn, 1, [(mem, "cur")], [g_mem, wxkv], [((m, D), BF16), ((m, XH * XD), BF16), ((m, XH * XD), BF16)])


def _merge_xattn(o, gl, conv_out, x, wmo, wo, g_x, wxq, kx, vx, wxo, n):
    t = x.shape[0]

    def fn(i, rows, consts, outs, accs, scr):
        w_mo, w_o, g, w_xq, k_x, v_x, w_xo = consts
        ob = rows[0][...].astype(BF16)
        outs[7][...] = ob
        mla = _mm(ob, w_mo[...])
        outs[0][...] = mla
        merged = (_sig(rows[1][:, 0:D]) * rows[2][...] + _sig(rows[1][:, D:]) * mla).astype(BF16)
        outs[1][...] = merged
        h1 = rows[3][...] + _mm(merged, w_o[...])
        outs[2][...] = h1
        u1 = _rms(h1, g[...])[0].astype(BF16)
        outs[3][...] = u1
        qx = (_mm(u1, w_xq[...]) * X_SCALE).astype(BF16)
        outs[4][...] = qx
        for h in range(XH):
            ln = slice(h * XD, (h + 1) * XD)
            s = _mm_nt(qx[:, ln], k_x[:, ln])
            e = jnp.exp(s - jnp.max(s, axis=-1, keepdims=True))
            p = e / jnp.sum(e, axis=-1, keepdims=True)
            outs[5][:, ln] = _mm(p, v_x[:, ln]).astype(BF16)
        outs[6][...] = h1 + _mm(outs[5][...], w_xo[...])

    outs = [((t, D), F32), ((t, D), BF16), ((t, D), F32), ((t, D), BF16), ((t, XH * XD), BF16), ((t, XH * XD), BF16),
            ((t, D), F32), ((t, NH * HP), BF16)]
    return _row_call("merge_xattn", fn, n, [(a, "cur") for a in (o, gl, conv_out, x)], [wmo, wo, g_x, wxq, kx, vx, wxo], outs)


def _mlp_loss(h2, target, g_mlp, w1, w2, g_fin, n):
    t = h2.shape[0]
    nck = DFF // D

    def fn(i, rows, consts, outs, accs, scr):
        g_m, w_1, w_2, g_f = consts
        h = rows[0][...]
        u2, xh2, r2 = _rms(h, g_m[...])
        ub = u2.astype(BF16)
        outs[0][...] = ub
        h3 = h
        a1 = []
        for c in range(nck):
            ck = slice(c * D, (c + 1) * D)
            a = _mm(ub, w_1[:, ck])
            a1.append(a)
            rl = jnp.maximum(a, 0.0)
            rb = (rl * rl).astype(BF16)
            outs[1][:, ck] = rb
            h3 = h3 + _mm(rb, w_2[ck, :])
        y, xh3, r3 = _rms(h3, g_f[...])
        err = y - rows[1][...]
        accs[0][...] += jnp.sum(jnp.sum(err * err, axis=1, keepdims=True), axis=0, keepdims=True) * (0.5 / D)
        dh3, dgf = _rms_bwd(err * (1.0 / D), xh3, r3, g_f[...])
        accs[1][...] += dgf
        db = dh3.astype(BF16)
        outs[3][...] = db
        du2 = jnp.zeros_like(h)
        for c in range(nck):
            ck = slice(c * D, (c + 1) * D)
            da = (_mm_nt(db, w_2[ck, :]) * (2.0 * jnp.maximum(a1[c], 0.0))).astype(BF16)
            outs[2][:, ck] = da
            du2 = du2 + _mm_nt(da, w_1[:, ck])
        dx2, dgm = _rms_bwd(du2, xh2, r2, g_m[...])
        accs[2][...] += dgm
        outs[4][...] = dh3 + dx2

    outs = [((t, D), BF16), ((t, DFF), BF16), ((t, DFF), BF16), ((t, D), BF16), ((t, D), F32)]
    accs = [((1, 1), F32), ((1, D), F32), ((1, D), F32)]
    return _row_call("mlp_loss", fn, n, [(h2, "cur"), (target, "cur")], [g_mlp, w1, w2, g_fin], outs, accs)


def _merge_xattn_bwd(dh2, h1, qx, gl, conv_out, mla, o, wxo, kx, vx, wxq, g_x, wo, wmo, n):
    t = dh2.shape[0]
    m = kx.shape[0]

    def fn(i, rows, consts, outs, accs, scr):
        w_xo, k_x, v_x, w_xq, g, w_o, w_mo = consts
        d2 = rows[0][...]
        d2b = d2.astype(BF16)
        outs[7][...] = d2b
        dox = _mm_nt(d2b, w_xo[...]).astype(BF16)
        q = rows[2][...]
        dq = []
        for h in range(XH):
            ln = slice(h * XD, (h + 1) * XD)
            qh, kh, vh, doh = q[:, ln], k_x[:, ln], v_x[:, ln], dox[:, ln]
            s = _mm_nt(qh, kh)
            e = jnp.exp(s - jnp.max(s, axis=-1, keepdims=True))
            p = e / jnp.sum(e, axis=-1, keepdims=True)
            dp = _mm_nt(doh, vh)
            ds = p * (dp - jnp.sum(p * dp, axis=-1, keepdims=True))
            dq.append(_mm(ds, kh) * X_SCALE)
            accs[0][:, ln] += _mm_tn(ds, qh)
            accs[1][:, ln] += _mm_tn(p, doh)
        dqx = jnp.concatenate(dq, axis=1).astype(BF16)
        outs[0][...] = dqx
        _, xh1, r1 = _rms(rows[1][...], g[...])
        dx1, dg = _rms_bwd(_mm_nt(dqx, w_xq[...]), xh1, r1, g[...])
        accs[2][...] += dg
        d1 = d2 + dx1
        outs[1][...] = d1
        d1b = d1.astype(BF16)
        outs[8][...] = d1b
        dm = _mm_nt(d1b, w_o[...])
        g0, g1 = _sig(rows[3][:, 0:D]), _sig(rows[3][:, D:])
        outs[2][:, 0:D] = (dm * rows[4][...] * g0 * (1.0 - g0)).astype(BF16)
        outs[2][:, D:] = (dm * rows[5][...] * g1 * (1.0 - g1)).astype(BF16)
        outs[3][...] = (dm * g0).astype(BF16)
        dmla = (dm * g1).astype(BF16)
        outs[4][...] = dmla
        do = _mm_nt(dmla, w_mo[...])
        outs[5][...] = do.astype(BF16)
        prod = do * rows[6][...]
        for h in range(NH):
            outs[6][h] = jnp.sum(prod[:, h * HP:(h + 1) * HP], axis=-1, keepdims=True)

    outs = [((t, XH * XD), BF16), ((t, D), F32), ((t, 2 * D), BF16), ((t, D), BF16), ((t, D), BF16), ((t, NH * HP), BF16),
            ((NH, t, 1), F32), ((t, D), BF16), ((t, D), BF16)]
    accs = [((m, XH * XD), F32), ((m, XH * XD), F32), ((1, D), F32)]
    return _row_call("merge_xattn_bwd", fn, n, [(a, "cur") for a in (dh2, h1, qx, gl, conv_out, mla, o)],
                     [wxo, kx, vx, wxq, g_x, wo, wmo], outs, accs)


def _mla_prep_bwd(dq, dk, dv, cq, ckv, tc, tsa, tsb, gq, wuq, gkv, wk, wv, n):
    t = dq.shape[0]

    def fn(i, rows, consts, outs, accs, scr):
        g_q, w_q, g_kv, w_k, w_v = consts
        c, sa, sb = rows[5][...], rows[6][...], rows[7][...]
        dkr = jnp.zeros((rows[0].shape[0], HP), F32)
        for h in range(NH):
            ln = slice(h * HP, (h + 1) * HP)
            outs[0][:, ln] = _rope_bwd(rows[0][:, ln] * MLA_SCALE, c, sa, sb).astype(BF16)
            dkr = dkr + rows[1][:, ln]
        lane = lax.broadcasted_iota(jnp.int32, dkr.shape, 1)
        outs[5][...] = jnp.where((lane >= NOPE) & (lane < NOPE + ROPE), _rope_bwd(dkr, c, sa, sb), 0.0).astype(BF16)
        dkb, dvb = rows[1][...].astype(BF16), rows[2][...].astype(BF16)
        outs[1][...] = dkb
        outs[2][...] = dvb
        _, xq, rq = _rms(rows[3][...], g_q[...])
        dcq, dgq = _rms_bwd(_mm_nt(outs[0][...], w_q[...]), xq, rq, g_q[...])
        outs[3][...] = dcq.astype(BF16)
        accs[0][...] += dgq
        _, xk, rk = _rms(rows[4][...], g_kv[...])
        dckv, dgk = _rms_bwd(_mm_nt(dkb, w_k[...]) + _mm_nt(dvb, w_v[...]), xk, rk, g_kv[...])
        outs[4][...] = dckv.astype(BF16)
        accs[1][...] += dgk

    outs = [((t, NH * HP), BF16)] * 3 + [((t, QL), BF16), ((t, KL), BF16), ((t, HP), BF16)]
    return _row_call("mla_prep_bwd", fn, n, [(a, "cur") for a in (dq, dk, dv, cq, ckv, tc, tsa, tsb)],
                     [gq, wuq, gkv, wk, wv], outs, [((1, QL), F32), ((1, KL), F32)])


def _conv_out_bwd(dco, zc, wco, lng, lnb, n):
    t = zc.shape[0]

    def fn(i, rows, consts, outs, accs, scr):
        wo, lg, lb = consts
        z = rows[1][...]
        mu = jnp.mean(z, axis=-1, keepdims=True)
        dlt = z - mu
        rs = lax.rsqrt(jnp.mean(dlt * dlt, axis=-1, keepdims=True) + EPS)
        xh = dlt * rs
        zn = xh * lg[...] + lb[...]
        sg = _sig(zn)
        outs[0][...] = (zn * sg).astype(BF16)
        dzn = _mm_nt(rows[0][...], wo[...]) * (sg * (1.0 + zn * (1.0 - sg)))
        accs[0][...] += jnp.sum(dzn * xh, axis=0, keepdims=True)
        accs[1][...] += jnp.sum(dzn, axis=0, keepdims=True)
        dxh = dzn * lg[...]
        dzc = rs * (dxh - jnp.mean(dxh, axis=-1, keepdims=True) - xh * jnp.mean(dxh * xh, axis=-1, keepdims=True))
        outs[1][...] = dzc
        accs[2][...] += jnp.sum(dzc, axis=0, keepdims=True)

    return _row_call("conv_out_bwd", fn, n, [(dco, "cur"), (zc, "cur")], [wco, lng, lnb],
                     [((t, CC), BF16), ((t, CC), F32)], [((1, CC), F32)] * 3)


def _conv_glu_bwd(dzc, a, gt, cw, n):
    t = a.shape[0]
    r = t // n

    def fn(i, rows, consts, outs, accs, scr):
        w = consts[0]
        zz, dd = scr
        _fill_glu_window(i, rows[2], rows[3], rows[4], rows[5], zz)
        dd[0:r, :] = rows[0][...]
        dd[r:, :] = jnp.where(i < n - 1, rows[1][0:HALO, :], 0.0)
        rb = min(CONV_ROWS, r)
        for c in range(CC // HP):
            ln = slice(c * HP, (c + 1) * HP)
            for base in range(0, r, rb):
                here = slice(base, base + rb)
                dcur = dd[here, ln]
                for o, win in _windows(zz, base, ln, rb, [HALO - (CW - 1) + j for j in range(CW)]):
                    j = o - (HALO - (CW - 1))
                    accs[0][j:j + 1, ln] += jnp.sum(dcur * win, axis=0, keepdims=True)
                acc = jnp.zeros((rb, HP), F32)
                for o, win in _windows(dd, base, ln, rb, [CW - 1 - j for j in range(CW)]):
                    j = CW - 1 - o
                    acc = acc + w[j:j + 1, ln] * win
                sg = _sig(rows[3][here, ln])
                outs[0][here, ln] = (acc * sg).astype(BF16)
                outs[0][here, CC + c * HP:CC + (c + 1) * HP] = (acc * rows[2][here, ln] * sg * (1.0 - sg)).astype(BF16)

    return _row_call("conv_glu_bwd", fn, n, [(dzc, "cur"), (dzc, "next"), (a, "cur"), (gt, "cur"), (a, "prev"), (gt, "prev")],
                     [cw], [((t, 2 * CC), BF16)], [((HALO, CC), F32)],
                     scratch=[pltpu.VMEM((r + HALO, CC), F32), pltpu.VMEM((r + HALO, CC), F32)])


def _in_proj_bwd(pieces, dh1, x, wp, g_mix, n, exchange):
    t = x.shape[0]
    offs = (P_A, P_Q, P_KV, P_KR, P_GL, P_END)

    def fn(i, rows, consts, outs, accs, scr):
        w, g = consts
        du = jnp.zeros((rows[0].shape[0], D), F32)
        for k in range(5):
            du = du + _mm_nt(rows[k][...], w[:, offs[k]:offs[k + 1]])
        _, xh, r = _rms(rows[6][...], g[...])
        dx, dg = _rms_bwd(du, xh, r, g[...])
        accs[0][...] += dg
        outs[0][...] = rows[5][...] + dx

    return _row_call("in_proj_bwd", fn, n, [(a, "cur") for a in list(pieces) + [dh1, x]], [wp, g_mix],
                     [((t, D), F32)], [((1, D), F32)], exchange=exchange)


def _mem_bwd(mem, dkx, dvx, g_mem, wxkv):
    m = mem.shape[0]

    def fn(i, rows, consts, outs, accs, scr):
        g, w = consts
        dkv = jnp.concatenate([rows[1][...], rows[2][...]], axis=1).astype(BF16)
        outs[0][...] = dkv
        _, xh, _ = _rms(rows[0][...], g[...])
        accs[0][...] += jnp.sum(_mm_nt(dkv, w[...]) * xh, axis=0, keepdims=True)

    return _row_call("mem_bwd", fn, 1, [(mem, "cur"), (dkx, "cur"), (dvx, "cur")], [g_mem, wxkv],
                     [((m, 2 * XH * XD), BF16)], [((1, D), F32)])


def _dw(name, xs, dy):
    t, k = xs.shape
    nn = dy.shape[1]
    tk, tn, tt = min(k, 1024), min(nn, 1024), min(t, DW_TILE)

    def body(x_ref, dy_ref, o_ref):
        @pl.when(pl.program_id(2) == 0)
        def _():
            o_ref[...] = jnp.zeros(o_ref.shape, F32)
        o_ref[...] += lax.dot_general(x_ref[...], dy_ref[...], (((0,), (0,)), ((), ())), preferred_element_type=F32)

    return pl.pallas_call(
        body, name=name, grid=(k // tk, nn // tn, t // tt),
        in_specs=[pl.BlockSpec((tt, tk), lambda a, b, c: (c, a)), pl.BlockSpec((tt, tn), lambda a, b, c: (c, b))],
        out_specs=pl.BlockSpec((tk, tn), lambda a, b, c: (a, b)),
        out_shape=jax.ShapeDtypeStruct((k, nn), F32),
        compiler_params=pltpu.CompilerParams(dimension_semantics=("arbitrary", "arbitrary", "arbitrary"), vmem_limit_bytes=VMEM_LIMIT),
    )(xs, dy)


def _all_gather_packed(shard):
    rws = shard.shape[0]

    def body(x_ref, out_ref, send_sems, recv_sems, local_sem):
        x, y, c = _coords()
        me, sibling = (x, y, c), (x, y, 1 - c)
        chips = [(1 - x, y), (x, 1 - y), (1 - x, 1 - y)]

        def slot(px, py, pc):
            return out_ref.at[4 * px + 2 * py + pc]

        def copy(k, block, to, src=None):
            return pltpu.make_async_remote_copy(
                src_ref=slot(*block) if src is None else src, dst_ref=slot(*block),
                send_sem=send_sems.at[k], recv_sem=recv_sems.at[k], device_id=to, device_id_type=MESH)

        mine = pltpu.make_async_copy(x_ref, slot(*me), local_sem)
        mine.start()
        first = [copy(0, me, sibling, src=x_ref)] + [copy(1 + j, me, (*chip, c), src=x_ref) for j, chip in enumerate(chips)]
        for cp in first:
            cp.start()
        passed = [copy(4 + j, (*chip, c), sibling) for j, chip in enumerate(chips)]
        for j, chip in enumerate(chips):
            copy(1 + j, (*chip, c), me).wait_recv()
            passed[j].start()
        copy(0, sibling, me).wait_recv()
        for j, chip in enumerate(chips):
            copy(4 + j, (*chip, 1 - c), me).wait_recv()
        for cp in first + passed:
            cp.wait_send()
        mine.wait()

    return pl.pallas_call(
        body, name="all_gather_weights",
        out_shape=jax.ShapeDtypeStruct((N_DEV, rws, 128), shard.dtype),
        in_specs=[pl.BlockSpec(memory_space=pl.ANY)], out_specs=pl.BlockSpec(memory_space=pl.ANY),
        scratch_shapes=[pltpu.SemaphoreType.DMA((7,)), pltpu.SemaphoreType.DMA((7,)), pltpu.SemaphoreType.DMA],
    )(shard)


def _adam(w, g, m, v):
    m = ADAM_B1 * m + (1.0 - ADAM_B1) * g
    v = ADAM_B2 * v + (1.0 - ADAM_B2) * (g * g)
    m_hat = m / (1.0 - ADAM_B1 ** ADAM_STEP)
    v_hat = v / (1.0 - ADAM_B2 ** ADAM_STEP)
    return -ADAM_LR * (m_hat / (jnp.sqrt(v_hat) + ADAM_EPS) + ADAM_WD * w), m, v


def _small_allreduce_adam(part, w, m, v):
    shape = part.shape

    def body(p_ref, w_ref, m_ref, v_ref, g_ref, d_ref, nm_ref, nv_ref, buf, send_sems, recv_sems):
        x, y, c = _coords()
        me = 4 * x + 2 * y + c
        buf[0] = p_ref[...]
        cps = []
        for j in range(1, N_DEV):
            jx, jy, jc = j >> 2, (j >> 1) & 1, j & 1
            peer = (1 - x if jx else x, 1 - y if jy else y, 1 - c if jc else c)
            cps.append(pltpu.make_async_remote_copy(src_ref=p_ref, dst_ref=buf.at[j], send_sem=send_sems.at[j - 1],
                                                    recv_sem=recv_sems.at[j - 1], device_id=peer, device_id_type=MESH))
        for cp in cps:
            cp.start()
        for cp in cps:
            cp.wait()
        g = buf[me]
        for d in range(1, N_DEV):
            g = g + buf[d ^ me]
        g_ref[...] = g
        d_ref[...], nm_ref[...], nv_ref[...] = _adam(w_ref[...], g, m_ref[...], v_ref[...])

    vm = pl.BlockSpec(memory_space=pltpu.VMEM)
    return pl.pallas_call(
        body, name="small_allreduce_adam", out_shape=[jax.ShapeDtypeStruct(shape, F32)] * 4,
        in_specs=[vm] * 4, out_specs=[vm] * 4,
        scratch_shapes=[pltpu.VMEM((N_DEV,) + shape, F32), pltpu.SemaphoreType.DMA((7,)), pltpu.SemaphoreType.DMA((7,))],
    )(part, w, m, v)


def _sum_parts(name, parts):
    rws = parts.shape[1]
    tile = max(d for d in range(16, PACK_TILE + 1, 16) if rws % d == 0)

    def body(p_ref, g_ref):
        g = p_ref[0]
        for d in range(1, N_DEV):
            g = g + p_ref[d]
        g_ref[...] = g

    return pl.pallas_call(
        body, name=name, grid=(rws // tile,), in_specs=[pl.BlockSpec((N_DEV, tile, 128), lambda i: (0, i, 0))],
        out_specs=pl.BlockSpec((tile, 128), lambda i: (i, 0)), out_shape=jax.ShapeDtypeStruct((rws, 128), F32),
        compiler_params=pltpu.CompilerParams(dimension_semantics=("arbitrary",), vmem_limit_bytes=VMEM_LIMIT),
    )(parts)


def _adam_call(name, w, g, m, v):
    def body(w_ref, g_ref, m_ref, v_ref, d_ref, nm_ref, nv_ref):
        d_ref[...], nm_ref[...], nv_ref[...] = _adam(w_ref[...], g_ref[...], m_ref[...], v_ref[...])

    vm = pl.BlockSpec(memory_space=pltpu.VMEM)
    return pl.pallas_call(
        body, name=name, in_specs=[vm] * 4, out_specs=[vm] * 3, out_shape=[jax.ShapeDtypeStruct(w.shape, F32)] * 3,
        compiler_params=pltpu.CompilerParams(vmem_limit_bytes=VMEM_LIMIT),
    )(w, g, m, v)


ROW_SHARDED = ("w_out", "w_xq", "w_xkv", "w_mlp2")
SMALL = ("norm_mix_g", "conv_b", "conv_ln_g", "conv_ln_b", "q_norm_g", "kv_norm_g", "norm_xattn_g", "norm_mem_g", "norm_mlp_g", "final_norm_g")
GATHER_FIRST = ("w_in", "conv_w", "w_conv_out", "w_uq", "w_ukv")
GATHER_LATE = ("w_mla_out", "w_out", "w_xq", "w_xkv", "w_xo", "w_mlp1", "w_mlp2")
REDUCE_EARLY = ("w_mlp1", "w_mlp2", "conv_w", "w_conv_out", "w_mla_out", "w_out", "w_xq", "w_xkv", "w_xo", "w_in")
REDUCE_LAST = ("w_uq", "w_ukv")
ROW_ALIGN = 16


def _padded(k, nn):
    return -(-k // ROW_ALIGN) * ROW_ALIGN, -(-nn // 128) * 128


def _pack_rows(a):
    k, nn = a.shape[-2:]
    kp, np_ = _padded(k, nn)
    if (kp, np_) != (k, nn):
        a = jnp.pad(a, [(0, 0)] * (a.ndim - 2) + [(0, kp - k), (0, np_ - nn)])
    if np_ == 128:
        return a
    lead = a.shape[:-2]
    return jnp.swapaxes(a.reshape(lead + (kp, np_ // 128, 128)), -2, -3).reshape(lead + (kp * np_ // 128, 128))


def _unpack_rows(p, k, nn):
    kp, np_ = _padded(k, nn)
    if np_ != 128:
        lead = p.shape[:-2]
        p = jnp.swapaxes(p.reshape(lead + (np_ // 128, kp, 128)), -2, -3).reshape(lead + (kp, np_))
    return p[..., :k, :nn]


def _pack_group(arrays):
    return jnp.concatenate([_pack_rows(a) for a in arrays], axis=-2)


def _unpack_group(p, shapes):
    out, off = [], 0
    for k, nn in shapes:
        kp, np_ = _padded(k, nn)
        rws = kp * np_ // 128
        out.append(_unpack_rows(p[..., off:off + rws, :], k, nn))
        off += rws
    return out


def _tile_shaped(k, nn):
    return k % ROW_ALIGN == 0 and nn % 128 == 0


def _packed_to_full(name, seg, shard_shape):
    k, nn = shard_shape
    if not _tile_shaped(k, nn):
        stacked = _unpack_rows(seg, k, nn)
        return jnp.transpose(stacked, (1, 0, 2)).reshape(k, N_DEV * nn)
    a = seg.reshape(N_DEV, nn // 128, k, 128)
    if name in ROW_SHARDED:
        return jnp.transpose(a, (0, 2, 1, 3)).reshape(N_DEV * k, nn)
    return jnp.transpose(a, (2, 0, 1, 3)).reshape(k, N_DEV * nn)


def _full_to_packed(name, full, shard_shape):
    k, nn = shard_shape
    if not _tile_shaped(k, nn):
        return _pack_rows(jnp.transpose(full.reshape(k, N_DEV, nn), (1, 0, 2)))
    if name in ROW_SHARDED:
        a = jnp.transpose(full.reshape(N_DEV, k, nn // 128, 128), (0, 2, 1, 3))
    else:
        a = jnp.transpose(full.reshape(k, N_DEV, nn // 128, 128), (1, 2, 0, 3))
    return a.reshape(N_DEV, k * nn // 128, 128)


def _pack_small(vals):
    flat = jnp.concatenate([v.reshape(-1) for v in vals])
    rws = flat.shape[0] // 128
    return jnp.pad(flat, (0, (-(-rws // 8) * 8 - rws) * 128)).reshape(-1, 128)


def _unpack_small(p, sizes):
    flat = p.reshape(-1)
    out, off = [], 0
    for s in sizes:
        out.append(flat[off:off + s])
        off += s
    return out


def _head_pad(w, real, axis):
    shp = list(w.shape)
    shp[axis:axis + 1] = [NH, real]
    w = w.reshape(shp)
    pad = [(0, 0)] * w.ndim
    pad[axis + 1] = (0, HP - real)
    w = jnp.pad(w, pad)
    shp[axis:axis + 2] = [NH * HP]
    return w.reshape(shp)


def _head_unpad(w, lo, real, axis):
    shp = list(w.shape)
    shp[axis:axis + 1] = [NH, HP]
    w = lax.slice_in_dim(w.reshape(shp), lo, lo + real, axis=axis + 1)
    shp[axis:axis + 2] = [NH * real]
    return w.reshape(shp)


def kernel(x, mem, positions, norm_mix_g, w_in, conv_w, conv_b, conv_ln_g, conv_ln_b, w_conv_out, q_norm_g, w_uq, kv_norm_g, w_ukv, w_mla_out, w_out, norm_xattn_g, norm_mem_g, w_xq, w_xkv, w_xo, norm_mlp_g, w_mlp1, w_mlp2, final_norm_g, loss_target, m_norm_mix_g, m_w_in, m_conv_w, m_conv_b, m_conv_ln_g, m_conv_ln_b, m_w_conv_out, m_q_norm_g, m_w_uq, m_kv_norm_g, m_w_ukv, m_w_mla_out, m_w_out, m_norm_xattn_g, m_norm_mem_g, m_w_xq, m_w_xkv, m_w_xo, m_norm_mlp_g, m_w_mlp1, m_w_mlp2, m_final_norm_g, v_norm_mix_g, v_w_in, v_conv_w, v_conv_b, v_conv_ln_g, v_conv_ln_b, v_w_conv_out, v_q_norm_g, v_w_uq, v_kv_norm_g, v_w_ukv, v_w_mla_out, v_w_out, v_norm_xattn_g, v_norm_mem_g, v_w_xq, v_w_xkv, v_w_xo, v_norm_mlp_g, v_w_mlp1, v_w_mlp2, v_final_norm_g):
    args = dict(locals())
    t = x.shape[1]
    n = t // min(ROW_TILE, t)
    xs, mems, tgt = x[0], mem[0], loss_target[0]

    def pack_shards(prefix, group, dtype):
        return _pack_group([args[prefix + k][0].astype(dtype) for k in group])

    def shapes(group):
        return [args[k].shape[1:] for k in group]

    def unpack_full(gathered, group):
        out, off = {}, 0
        for k, shp in zip(group, shapes(group)):
            kp, np_ = _padded(*shp)
            out[k] = _packed_to_full(k, gathered[:, off:off + kp * np_ // 128], shp)
            off += kp * np_ // 128
        return out

    full = unpack_full(_all_gather_packed(pack_shards("", GATHER_FIRST, BF16)), GATHER_FIRST)

    wi = full["w_in"]
    kr_slot = jnp.pad(wi[:, P_KR:P_KR + ROPE], ((0, 0), (NOPE, HP - NOPE - ROPE)))
    wp = jnp.concatenate([wi[:, :P_KR], kr_slot, wi[:, P_KR + ROPE:]], axis=1)
    cw = jnp.pad(full["conv_w"].astype(F32), ((0, HALO - CW), (0, 0)))
    wuq = _head_pad(full["w_uq"], NOPE + ROPE, 1)
    ukv = full["w_ukv"].reshape(KL, NH, NOPE + VD)
    wk = _head_pad(ukv[:, :, :NOPE].reshape(KL, NH * NOPE), NOPE, 1)
    wv = _head_pad(ukv[:, :, NOPE:].reshape(KL, NH * VD), VD, 1)

    inv_freq = THETA ** (-jnp.arange(ROPE // 2, dtype=F32) / (ROPE // 2))
    ang = positions[0].astype(F32)[:, None] * inv_freq
    cs, sn, zr = jnp.cos(ang), jnp.sin(ang), jnp.zeros((t, ROPE // 2), F32)
    tail = jnp.zeros((t, HP - NOPE - ROPE), F32)
    tc = jnp.concatenate([jnp.ones((t, NOPE), F32), cs, cs, tail], axis=1)
    tsa = jnp.concatenate([jnp.zeros((t, NOPE), F32), -sn, zr, tail], axis=1)
    tsb = jnp.concatenate([jnp.zeros((t, NOPE), F32), zr, sn, tail], axis=1)

    u0, a, gt, cq, ckv, krp, gl = _in_proj(xs, norm_mix_g, wp, n)
    zc, conv_out = _conv_branch(a, gt, cw, conv_b, conv_ln_g, conv_ln_b, full["w_conv_out"], n)
    qh, kh, vh, cqn, ckvn = _mla_prep(cq, ckv, krp, tc, tsa, tsb, q_norm_g, wuq, kv_norm_g, wk, wv, n)
    o, lse, gathered = _flash_fwd(qh, kh, vh, exchange=(pack_shards("", GATHER_LATE, BF16), True))
    full.update(unpack_full(gathered, GATHER_LATE))
    wmo = _head_pad(full["w_mla_out"], VD, 0)
    memn, kx, vx = _mem_kv(mems, norm_mem_g, full["w_xkv"])
    mla, merged, h1, u1, qx, ox, h2, ob = _merge_xattn(o, gl, conv_out, xs, wmo, full["w_out"], norm_xattn_g, full["w_xq"],
                                                        kx, vx, full["w_xo"], n)
    gfin = final_norm_g.reshape(1, D)
    u2, rl2, da1, dh3b, dh2, loss_p, dg_fin, dg_mlp = _mlp_loss(h2, tgt, norm_mlp_g, full["w_mlp1"], full["w_mlp2"], gfin, n)

    (dqx, dh1, dgl, dco, dmla, dob, delta, dh2b, dh1b, dkx, dvx, dg_x) = _merge_xattn_bwd(
        dh2, h1, qx, gl, conv_out, mla, o, full["w_xo"], kx, vx, full["w_xq"], norm_xattn_g, full["w_out"], wmo, n)
    gfull = {"w_mlp1": _dw("dw_mlp1", u2, da1), "w_mlp2": _dw("dw_mlp2", rl2, dh3b)}

    def stacked(group):
        return jnp.concatenate([_full_to_packed(k, gfull[k], args[k].shape[1:]) for k in group], axis=1)

    zs, dzc, dg_lng, dg_lnb, dg_cb = _conv_out_bwd(dco, zc, full["w_conv_out"], conv_ln_g, conv_ln_b, n)
    dci, dcw = _conv_glu_bwd(dzc, a, gt, cw, n)
    dkv, dg_mem = _mem_bwd(mems, dkx, dvx, norm_mem_g, full["w_xkv"])
    gfull.update({
        "conv_w": dcw[:CW],
        "w_conv_out": _dw("dw_conv_out", zs, dco),
        "w_mla_out": _head_unpad(_dw("dw_mla_out", ob, dmla), 0, VD, 0),
        "w_out": _dw("dw_out", merged, dh1b),
        "w_xq": _dw("dw_xq", u1, dqx),
        "w_xkv": _dw("dw_xkv", memn, dkv),
        "w_xo": _dw("dw_xo", ox, dh2b),
    })
    late_lo, late_hi, wcols = P_Q, P_KR + ROPE, args["w_in"].shape[2]
    dw_conv_in, dw_gates = _dw("dw_in_0", u0, dci), _dw("dw_in_4", u0, dgl)
    gfull["w_in"] = jnp.concatenate([dw_conv_in, jnp.zeros((D, late_hi - late_lo), F32), dw_gates], axis=1)
    dk, dv, dq, parts_early = _flash_bwd(qh, kh, vh, dob, lse.reshape(NH, 1, t), delta.reshape(NH, 1, t),
                                         exchange=(stacked(REDUCE_EARLY), False))
    dqp, dkb, dvb, dcq, dckv, dkrp, dg_q, dg_kv = _mla_prep_bwd(dq, dk, dv, cq, ckv, tc, tsa, tsb, q_norm_g, wuq, kv_norm_g, wk, wv, n)
    pieces = (dci, dcq, dckv, dkrp, dgl)
    late = jnp.concatenate([_dw("dw_in_1", u0, dcq), _dw("dw_in_2", u0, dckv), _dw("dw_in_3", u0, dkrp)[:, NOPE:NOPE + ROPE]], axis=1)
    gfull["w_uq"] = _head_unpad(_dw("dw_uq", cqn, dqp), 0, NOPE + ROPE, 1)
    gk = _head_unpad(_dw("dw_uk", ckvn, dkb), 0, NOPE, 1).reshape(KL, NH, NOPE)
    gv = _head_unpad(_dw("dw_uv", ckvn, dvb), 0, VD, 1).reshape(KL, NH, VD)
    gfull["w_ukv"] = jnp.concatenate([gk, gv], axis=2).reshape(KL, NH * (NOPE + VD))
    owners = [(d, max(late_lo, d * wcols) - d * wcols, max(late_lo, d * wcols) - late_lo,
               min(late_hi, (d + 1) * wcols) - max(late_lo, d * wcols)) for d in range(N_DEV)
              if min(late_hi, (d + 1) * wcols) > max(late_lo, d * wcols)]
    late_w = -(-max(w for _, _, _, w in owners) // 128) * 128
    late_src = _pack_rows(jnp.stack([jnp.pad(late[:, l0:l0 + w], ((0, 0), (0, late_w - w))) for _, _, l0, w in owners]))
    grad_x, dg_mix, parts_last, parts_late = _in_proj_bwd(
        pieces, dh1, xs, wp, norm_mix_g, n, [(stacked(REDUCE_LAST), False), (late_src, tuple(d for d, _, _, _ in owners))])

    gsum = {}
    for name, group, parts in (("grad_sum_early", REDUCE_EARLY, parts_early), ("grad_sum_last", REDUCE_LAST, parts_last)):
        gsum.update(zip(group, _unpack_group(_sum_parts(name, parts), shapes(group))))
    late_sum = _unpack_rows(_sum_parts("grad_sum_late", parts_late), D, late_w)
    me = 4 * lax.axis_index("x") + 2 * lax.axis_index("y") + lax.axis_index("c")
    placed = jnp.zeros((D, wcols), F32)
    for d, s0, _, w in owners:
        placed = jnp.where(me == d, jnp.pad(late_sum[:, :w], ((0, 0), (s0, wcols - s0 - w))), placed)
    gsum["w_in"] = gsum["w_in"] + placed
    big = [{}, {}, {}, {}]
    for k, g in gsum.items():
        res = _adam_call("adam_" + k, args[k][0], g, args["m_" + k][0], args["v_" + k][0])
        for kind, val in enumerate([g] + list(res)):
            big[kind][k] = val[None]

    small_g = {"norm_mix_g": dg_mix, "conv_b": dg_cb, "conv_ln_g": dg_lng, "conv_ln_b": dg_lnb, "q_norm_g": dg_q,
               "kv_norm_g": dg_kv, "norm_xattn_g": dg_x, "norm_mem_g": dg_mem, "norm_mlp_g": dg_mlp, "final_norm_g": dg_fin}
    small_sizes = [int(np.prod(args[k].shape)) for k in SMALL] + [128]
    zero_slot = jnp.zeros((128,), F32)
    small_out = _small_allreduce_adam(
        _pack_small([small_g[k] for k in SMALL] + [jnp.pad(loss_p.reshape(-1), (0, 127))]),
        _pack_small([args[k] for k in SMALL] + [zero_slot]), _pack_small([args["m_" + k] for k in SMALL] + [zero_slot]),
        _pack_small([args["v_" + k] for k in SMALL] + [zero_slot]))

    small = [dict(zip(SMALL, [s.reshape(args[k].shape) for k, s in zip(SMALL, _unpack_small(o_, small_sizes))])) for o_ in small_out]
    loss = _unpack_small(small_out[0], small_sizes)[-1][0]
    order = ("norm_mix_g", "w_in", "conv_w", "conv_b", "conv_ln_g", "conv_ln_b", "w_conv_out", "q_norm_g", "w_uq", "kv_norm_g",
             "w_ukv", "w_mla_out", "w_out", "norm_xattn_g", "norm_mem_g", "w_xq", "w_xkv", "w_xo", "norm_mlp_g", "w_mlp1",
             "w_mlp2", "final_norm_g")
    res = [loss, grad_x[None]]
    for kind in range(4):
        res += [big[kind][k] if k in big[kind] else small[kind][k] for k in order]
    return tuple(res)
```

```python
import functools

import jax
import jax.numpy as jnp
import numpy as np
from jax import lax
from jax.experimental import pallas as pl
from jax.experimental.pallas import tpu as pltpu

F32, BF16 = jnp.float32, jnp.bfloat16
MESH = pl.DeviceIdType.MESH

N_DEV = 8
D = 1024
CC = D // 2
CW = 31
HALO = 32
NH = 8
NOPE, ROPE, VD = D // 16, D // 32, D // 16
QL, KL = 3 * D // 8, D // 4
HP = 128
XH, XD = 4, D // 8
DFF = 4 * D
EPS = 1e-6
THETA = 10000.0
MLA_SCALE = float((NOPE + ROPE) ** -0.5)
X_SCALE = float(XD ** -0.5)
NEG = -1e30
P_A, P_G, P_Q, P_KV, P_KR, P_GL, P_END = 0, CC, 2 * CC, 2 * CC + QL, 2 * CC + QL + KL, 2 * CC + QL + KL + HP, 2 * CC + QL + KL + HP + 2 * D

ADAM_LR, ADAM_B1, ADAM_B2, ADAM_EPS, ADAM_WD, ADAM_STEP = 0.001, 0.9, 0.999, 1e-08, 0.01, 10

ROW_TILE = 256
LIGHT_ROW_TILE = 512
ATT_TILE = 512
HG = 2
HG_FWD = 4
ATT_CHUNK = 32
CONV_ROWS = 128
DW_TILE = 1024
PACK_TILE = 1536
VMEM_LIMIT = 56 * 1024 * 1024


def _mm(a, w):
    return jnp.dot(a.astype(BF16), w, preferred_element_type=F32)


def _mm_nt(a, w):
    return lax.dot_general(a.astype(BF16), w, (((1,), (1,)), ((), ())), preferred_element_type=F32)


def _mm_tn(a, b):
    return lax.dot_general(a.astype(BF16), b.astype(BF16), (((0,), (0,)), ((), ())), preferred_element_type=F32)


def _rms(x, g):
    r = lax.rsqrt(jnp.mean(x * x, axis=-1, keepdims=True) + EPS)
    xh = x * r
    return xh * g, xh, r


def _rms_bwd(dy, xh, r, g):
    dxh = dy * g
    dx = r * (dxh - xh * jnp.mean(dxh * xh, axis=-1, keepdims=True))
    return dx, jnp.sum(dy * xh, axis=0, keepdims=True)


def _col_to_rows(col):
    return jnp.transpose(jnp.broadcast_to(col, (col.shape[0], HP)))[0:8, :]


def _sig(x):
    return 1.0 / (1.0 + jnp.exp(-x))


def _coords():
    return lax.axis_index("x"), lax.axis_index("y"), lax.axis_index("c")


def _exchange_ops(src_ref, mode, dst_ref, send_sems, recv_sems, local_sem):
    x, y, c = _coords()
    me = 4 * x + 2 * y + c
    owned = isinstance(mode, tuple)

    def owner_slot(d):
        return sum(jnp.where(d == o, n, 0) for n, o in enumerate(mode))

    def is_owner(d):
        return functools.reduce(jnp.logical_or, [d == o for o in mode])

    def src(d):
        return src_ref if mode is True else src_ref.at[owner_slot(d) if owned else d]

    local = (pltpu.make_async_copy(src(me), dst_ref.at[me], local_sem), is_owner(me) if owned else True)
    remote = []
    for j in range(1, N_DEV):
        px, py, pc = (1 - x if j & 4 else x), (1 - y if j & 2 else y), (1 - c if j & 1 else c)
        d = 4 * px + 2 * py + pc
        cp = pltpu.make_async_remote_copy(src_ref=src(d), dst_ref=dst_ref.at[me], send_sem=send_sems.at[j - 1],
                                          recv_sem=recv_sems.at[j - 1], device_id=(px, py, pc), device_id_type=MESH)
        remote.append((cp, is_owner(d) if owned else True, is_owner(me) if owned else True))
    return local, remote


def _when(cond, fn):
    if cond is True:
        fn()
    else:
        pl.when(cond)(fn)


def _exchange_hook(first, last, mode, refs):
    @pl.when(first)
    def _():
        (local, here), remote = _exchange_ops(refs[0], mode, *refs[1:])
        _when(here, local.start)
        for cp, sends, _ in remote:
            _when(sends, cp.start)

    @pl.when(last)
    def _():
        (local, here), remote = _exchange_ops(refs[0], mode, *refs[1:])
        for cp, sends, receives in remote:
            _when(sends, cp.wait_send)
            _when(receives, cp.wait_recv)
        _when(here, local.wait)


def _exchange_shapes(exchange):
    if exchange is None:
        return [], [], [], []
    arr, mode = exchange
    shape = (N_DEV,) + (arr.shape if mode is True else arr.shape[1:])
    any_spec = pl.BlockSpec(memory_space=pl.ANY)
    sems = [pltpu.SemaphoreType.DMA((N_DEV - 1,)), pltpu.SemaphoreType.DMA((N_DEV - 1,)), pltpu.SemaphoreType.DMA]
    return [any_spec], [any_spec], [jax.ShapeDtypeStruct(shape, arr.dtype)], sems


def _row_call(name, fn, n, rows, consts, outs, accs=(), scratch=(), exchange=None):
    def row_spec(shape, mode):
        r = shape[-2] // n
        if mode == "cur":
            f = lambda i: i
        elif mode == "prev":
            f = lambda i: jnp.maximum(i - 1, 0)
        else:
            f = lambda i: jnp.minimum(i + 1, n - 1)
        if len(shape) == 2:
            return pl.BlockSpec((r, shape[1]), lambda i: (f(i), 0))
        return pl.BlockSpec((shape[0], r, shape[2]), lambda i: (0, f(i), 0))

    def lane_spec(shape):
        return pl.BlockSpec((shape[0], shape[1], shape[2] // n), lambda i: (0, 0, i))

    def whole_spec(shape, single):
        nd = len(shape)
        if single:
            return pl.BlockSpec(shape, lambda i: (0,) * nd, pipeline_mode=pl.Buffered(1))
        return pl.BlockSpec(shape, lambda i: (0,) * nd)

    nr, nc, no, na, ns = len(rows), len(consts), len(outs), len(accs), len(scratch)
    exchanges = [] if exchange is None else (exchange if isinstance(exchange, list) else [exchange])
    shapes = [_exchange_shapes(e) for e in exchanges]
    x_in, x_out, x_shape, x_sems = (sum((s[part] for s in shapes), []) for part in range(4))
    nx = len(exchanges)

    def body(*refs):
        i = pl.program_id(0)
        row_refs, const_refs = refs[:nr], refs[nr:nr + nc]
        o0 = nr + nc + nx
        out_refs, acc_refs = refs[o0:o0 + no], refs[o0 + no:o0 + no + na]
        s0 = o0 + no + na + nx
        for e in range(nx):
            sems = refs[s0 + ns + 3 * e:s0 + ns + 3 * e + 3]
            _exchange_hook(i == 0, i == n - 1, exchanges[e][1], (refs[nr + nc + e], refs[o0 + no + na + e]) + tuple(sems))
        if na:
            @pl.when(i == 0)
            def _():
                for a in acc_refs:
                    a[...] = jnp.zeros(a.shape, a.dtype)
        fn(i, row_refs, const_refs, out_refs, acc_refs, refs[s0:s0 + ns])

    res = pl.pallas_call(
        body, name=name, grid=(n,),
        in_specs=[row_spec(a.shape, m) for a, m in rows] + [whole_spec(c.shape, True) for c in consts] + x_in,
        out_specs=[lane_spec(o[0]) if len(o) == 3 else row_spec(o[0], "cur") for o in outs]
        + [whole_spec(s, False) for s, _ in accs] + x_out,
        out_shape=[jax.ShapeDtypeStruct(o[0], o[1]) for o in list(outs) + list(accs)] + x_shape,
        scratch_shapes=list(scratch) + x_sems,
        compiler_params=pltpu.CompilerParams(dimension_semantics=("arbitrary",), vmem_limit_bytes=VMEM_LIMIT),
    )(*[a for a, _ in rows], *consts, *[e[0] for e in exchanges])
    return list(res)


def _in_proj(x, g_mix, wp, n):
    t = x.shape[0]

    def fn(i, rows, consts, outs, accs, scr):
        g, w = consts
        u, _, _ = _rms(rows[0][...], g[...])
        ub = u.astype(BF16)
        outs[0][...] = ub
        for k, (lo, hi) in enumerate(((P_A, P_G), (P_G, P_Q), (P_Q, P_KV), (P_KV, P_KR), (P_KR, P_GL), (P_GL, P_END))):
            outs[1 + k][...] = _mm(ub, w[:, lo:hi])

    outs = [((t, D), BF16), ((t, CC), F32), ((t, CC), F32), ((t, QL), F32), ((t, KL), F32), ((t, HP), F32), ((t, 2 * D), F32)]
    return _row_call("in_proj", fn, n, [(x, "cur")], [g_mix, wp], outs)


def _fill_glu_window(i, a, gt, ap, gtp, zz):
    r = a.shape[0]
    zp = ap[r - HALO:, :] * _sig(gtp[r - HALO:, :])
    zz[0:HALO, :] = jnp.where(i > 0, zp, 0.0)
    zz[HALO:, :] = a[...] * _sig(gt[...])


def _shift_copies(buf, sh):
    rows = buf.shape[0]
    for s in range(8):
        sh[s, 0:rows - s, :] = buf[s:rows, :]


def _window(sh, o, base, rb, ln):
    return sh[o % 8, base + o - o % 8:base + o - o % 8 + rb, ln]


def _windows(buf, base, ln, rb, offsets):
    for s in range(8):
        group = [o for o in offsets if o % 8 == s]
        if group:
            shifted = buf[base + s:base + max(group) + rb, ln]
            for o in group:
                yield o, shifted[o - s:o - s + rb]


def _conv_branch(a, gt, cw, cb, lng, lnb, wco, n):
    t = a.shape[0]
    r = t // n

    def fn(i, rows, consts, outs, accs, scr):
        w, b, lg, lb, wo = consts
        zz, zsh = scr
        _fill_glu_window(i, rows[0], rows[1], rows[2], rows[3], zz)
        _shift_copies(zz, zsh)
        rb = min(CONV_ROWS, r)
        for c in range(CC // HP):
            ln = slice(c * HP, (c + 1) * HP)
            for base in range(0, r, rb):
                acc = jnp.zeros((rb, HP), F32)
                for j in range(CW):
                    acc = acc + w[j:j + 1, ln] * _window(zsh, HALO - (CW - 1) + j, base, rb, ln)
                outs[0][base:base + rb, ln] = acc + b[:, ln]
        zc = outs[0][...]
        mu = jnp.mean(zc, axis=-1, keepdims=True)
        dlt = zc - mu
        rs = lax.rsqrt(jnp.mean(dlt * dlt, axis=-1, keepdims=True) + EPS)
        zn = dlt * rs * lg[...] + lb[...]
        outs[1][...] = _mm(zn * _sig(zn), wo[...])

    return _row_call("conv_branch", fn, n, [(a, "cur"), (gt, "cur"), (a, "prev"), (gt, "prev")],
                     [cw, cb, lng, lnb, wco], [((t, CC), F32), ((t, D), F32)],
                     scratch=[pltpu.VMEM((r + HALO, CC), F32), pltpu.VMEM((8, r + HALO, CC), F32)])


def _rope(v, c, sa, sb):
    return v * c + pltpu.roll(v, HP - ROPE // 2, 1) * sa + pltpu.roll(v, ROPE // 2, 1) * sb


def _rope_bwd(dv, c, sa, sb):
    return dv * c + pltpu.roll(dv * sa, ROPE // 2, 1) + pltpu.roll(dv * sb, HP - ROPE // 2, 1)


def _mla_prep(cq, ckv, krp, tc, tsa, tsb, gq, wuq, gkv, wk, wv, n):
    t = cq.shape[0]

    def fn(i, rows, consts, outs, accs, scr):
        g_q, w_q, g_kv, w_k, w_v = consts
        c, sa, sb = rows[3][...], rows[4][...], rows[5][...]
        cqn = _rms(rows[0][...], g_q[...])[0].astype(BF16)
        ckvn = _rms(rows[1][...], g_kv[...])[0].astype(BF16)
        outs[3][...] = cqn
        outs[4][...] = ckvn
        krr = _rope(rows[2][...], c, sa, sb)
        for h in range(NH):
            ln = slice(h * HP, (h + 1) * HP)
            outs[0][:, ln] = (_rope(_mm(cqn, w_q[:, ln]), c, sa, sb) * MLA_SCALE).astype(BF16)
            outs[1][:, ln] = (_mm(ckvn, w_k[:, ln]) + krr).astype(BF16)
        vv = _mm(ckvn, w_v[...])
        lane = lax.broadcasted_iota(jnp.int32, vv.shape, 1)
        outs[2][...] = jnp.where((lane & (HP - 1)) == VD, 1.0, vv).astype(BF16)

    outs = [((t, NH * HP), BF16)] * 3 + [((t, QL), BF16), ((t, KL), BF16)]
    return _row_call("mla_prep", fn, n, [(a, "cur") for a in (cq, ckv, krp, tc, tsa, tsb)], [gq, wuq, gkv, wk, wv], outs)


def _chunk_mask(c, tq, transposed, rows=ATT_CHUNK):
    row = lax.broadcasted_iota(jnp.int32, (rows, tq), 0) + c * rows
    col = lax.broadcasted_iota(jnp.int32, (rows, tq), 1)
    return (row <= col) if transposed else (col <= row)


def _flash_hook(exchange, nh, nq, refs):
    if exchange is not None:
        h, i = pl.program_id(0), pl.program_id(1)
        _exchange_hook((h == 0) & (i == 0), (h == nh - 1) & (i == nq - 1), exchange[1], refs)


def _head_lanes(g):
    return slice(g * HP, (g + 1) * HP)


def _flash_fwd(q, k, v, exchange=None):
    t = q.shape[0]
    tq = min(ATT_TILE, t)
    nq = t // tq
    x_in, x_out, x_shape, x_sems = _exchange_shapes(exchange)

    def body(q_ref, k_ref, v_ref, *rest):
        o_ref, lse_ref = rest[len(x_in):len(x_in) + 2]
        _flash_hook(exchange, NH // HG_FWD, nq, rest[:len(x_in)] + rest[len(x_in) + 2:])
        i = pl.program_id(1)

        def step(j, carry, masked):
            at = pl.ds(pl.multiple_of(j * tq, tq), tq)
            out = []
            for g in range(HG_FWD):
                m, acc = carry[g]
                s = _mm_nt(q_ref[:, _head_lanes(g)], k_ref[at, _head_lanes(g)])
                if masked:
                    s = jnp.where(_chunk_mask(0, tq, False, tq), s, NEG)
                m_new = jnp.maximum(m, jnp.max(s, axis=-1, keepdims=True))
                out.append((m_new, jnp.exp(m - m_new) * acc + _mm(jnp.exp(s - m_new), v_ref[at, _head_lanes(g)])))
            return tuple(out)

        init = tuple((jnp.full((tq, 1), NEG, F32), jnp.zeros((tq, HP), F32)) for _ in range(HG_FWD))
        carry = lax.fori_loop(0, i, lambda j, c: step(j, c, False), init)
        for g, (m, acc) in enumerate(step(i, carry, True)):
            lane = lax.broadcasted_iota(jnp.int32, acc.shape, 1)
            l = jnp.sum(jnp.where(lane == VD, acc, 0.0), axis=-1, keepdims=True)
            o_ref[:, _head_lanes(g)] = acc / l
            lse_ref[g] = _col_to_rows(m + jnp.log(l))

    wide = HG_FWD * HP
    return pl.pallas_call(
        body, name="flash_fwd", grid=(NH // HG_FWD, nq),
        in_specs=[pl.BlockSpec((tq, wide), lambda h, i: (i, h)), pl.BlockSpec((t, wide), lambda h, i: (0, h)),
                  pl.BlockSpec((t, wide), lambda h, i: (0, h))] + x_in,
        out_specs=[pl.BlockSpec((tq, wide), lambda h, i: (i, h)), pl.BlockSpec((HG_FWD, 8, tq), lambda h, i: (h, 0, i))] + x_out,
        out_shape=[jax.ShapeDtypeStruct((t, NH * HP), F32), jax.ShapeDtypeStruct((NH, 8, t), F32)] + x_shape,
        scratch_shapes=x_sems,
        compiler_params=pltpu.CompilerParams(dimension_semantics=("arbitrary", "arbitrary"), vmem_limit_bytes=VMEM_LIMIT),
    )(q, k, v, *([exchange[0]] if x_in else []))


def _flash_bwd(q, k, v, do, lse_row, delta_row, exchange=None):
    t = q.shape[0]
    tq = min(ATT_TILE, t)
    nq = t // tq
    x_in, x_out, x_shape, x_sems = _exchange_shapes(exchange)

    def body(q_ref, k_ref, v_ref, do_ref, lse_ref, dl_ref, *rest):
        dk_ref, dv_ref, dq_ref = rest[len(x_in):len(x_in) + 3]
        st_scr, dpt_scr, pt_scr, dst_scr = rest[len(rest) - 4:]
        _flash_hook(exchange, NH // HG, nq, rest[:len(x_in)] + rest[len(x_in) + 3:len(rest) - 4])
        j = pl.program_id(1)
        dk_ref[...] = jnp.zeros(dk_ref.shape, F32)
        dv_ref[...] = jnp.zeros(dv_ref.shape, F32)

        @pl.when(j == 0)
        def _():
            dq_ref[...] = jnp.zeros(dq_ref.shape, F32)

        def step(i, masked):
            at = pl.ds(pl.multiple_of(i * tq, tq), tq)
            for g in range(HG):
                st_scr[g] = _mm_nt(k_ref[:, _head_lanes(g)], q_ref[at, _head_lanes(g)])
                dpt_scr[g] = _mm_nt(v_ref[:, _head_lanes(g)], do_ref[at, _head_lanes(g)])
            for g in range(HG):
                lse_i, dl_i = lse_ref[g, 0:1, at], dl_ref[g, 0:1, at]
                for c in range(tq // ATT_CHUNK):
                    rows = slice(c * ATT_CHUNK, (c + 1) * ATT_CHUNK)
                    st = st_scr[g, rows, :]
                    if masked:
                        st = jnp.where(_chunk_mask(c, tq, True), st, NEG)
                    pt = jnp.exp(st - lse_i)
                    pt_scr[g, rows, :] = pt.astype(BF16)
                    dst_scr[g, rows, :] = (pt * (dpt_scr[g, rows, :] - dl_i)).astype(BF16)
            for g in range(HG):
                dv_ref[:, _head_lanes(g)] += _mm(pt_scr[g], do_ref[at, _head_lanes(g)])
                dk_ref[:, _head_lanes(g)] += _mm(dst_scr[g], q_ref[at, _head_lanes(g)])
                dq_ref[at, _head_lanes(g)] += _mm_tn(dst_scr[g], k_ref[:, _head_lanes(g)])

        step(j, True)
        lax.fori_loop(j + 1, nq, lambda i, c: (step(i, False), c)[1], 0)

    wide = HG * HP
    blk = pl.BlockSpec((tq, wide), lambda h, j: (j, h))
    whole = pl.BlockSpec((t, wide), lambda h, j: (0, h))
    row = pl.BlockSpec((HG, 8, t), lambda h, j: (h, 0, 0))
    x_sems = x_sems + [pltpu.VMEM((HG, tq, tq), F32), pltpu.VMEM((HG, tq, tq), F32), pltpu.VMEM((HG, tq, tq), BF16),
                       pltpu.VMEM((HG, tq, tq), BF16)]
    return pl.pallas_call(
        body, name="flash_bwd", grid=(NH // HG, nq), in_specs=[whole, blk, blk, whole, row, row] + x_in,
        out_specs=[blk, blk, whole] + x_out, out_shape=[jax.ShapeDtypeStruct((t, NH * HP), F32)] * 3 + x_shape, scratch_shapes=x_sems,
        compiler_params=pltpu.CompilerParams(dimension_semantics=("arbitrary", "arbitrary"), vmem_limit_bytes=VMEM_LIMIT),
    )(q, k, v, do, lse_row, delta_row, *([exchange[0]] if x_in else []))


def _mem_kv(mem, g_mem, wxkv):
    m = mem.shape[0]

    def fn(i, rows, consts, outs, accs, scr):
        mn = _rms(rows[0][...], consts[0][...])[0].astype(BF16)
        outs[0][...] = mn
        outs[1][...] = _mm(mn, consts[1][:, 0:XH * XD]).astype(BF16)
        outs[2][...] = _mm(mn, consts[1][:, XH * XD:]).astype(BF16)

    return _row_call("mem_kv", fn, 1, [(mem, "cur")], [g_mem, wxkv], [((m, D), BF16), ((m, XH * XD), BF16), ((m, XH * XD), BF16)])


def _merge_xattn(o, gl, conv_out, x, wmo, wo, g_x, wxq, kx, vx, wxo, n):
    t = x.shape[0]

    def fn(i, rows, consts, outs, accs, scr):
        w_mo, w_o, g, w_xq, k_x, v_x, w_xo = consts
        ob = rows[0][...].astype(BF16)
        outs[7][...] = ob
        mla = _mm(ob, w_mo[...])
        outs[0][...] = mla
        merged = (_sig(rows[1][:, 0:D]) * rows[2][...] + _sig(rows[1][:, D:]) * mla).astype(BF16)
        outs[1][...] = merged
        h1 = rows[3][...] + _mm(merged, w_o[...])
        outs[2][...] = h1
        u1 = _rms(h1, g[...])[0].astype(BF16)
        outs[3][...] = u1
        qx = (_mm(u1, w_xq[...]) * X_SCALE).astype(BF16)
        outs[4][...] = qx
        for h in range(XH):
            ln = slice(h * XD, (h + 1) * XD)
            s = _mm_nt(qx[:, ln], k_x[:, ln])
            e = jnp.exp(s - jnp.max(s, axis=-1, keepdims=True))
            p = e / jnp.sum(e, axis=-1, keepdims=True)
            outs[5][:, ln] = _mm(p, v_x[:, ln]).astype(BF16)
        outs[6][...] = h1 + _mm(outs[5][...], w_xo[...])

    outs = [((t, D), F32), ((t, D), BF16), ((t, D), F32), ((t, D), BF16), ((t, XH * XD), BF16), ((t, XH * XD), BF16),
            ((t, D), F32), ((t, NH * HP), BF16)]
    return _row_call("merge_xattn", fn, n, [(a, "cur") for a in (o, gl, conv_out, x)], [wmo, wo, g_x, wxq, kx, vx, wxo], outs)


def _mlp_loss(h2, target, g_mlp, w1, w2, g_fin, n):
    t = h2.shape[0]
    nck = DFF // D

    def fn(i, rows, consts, outs, accs, scr):
        g_m, w_1, w_2, g_f = consts
        h = rows[0][...]
        u2, xh2, r2 = _rms(h, g_m[...])
        ub = u2.astype(BF16)
        outs[0][...] = ub
        h3 = h
        a1 = []
        for c in range(nck):
            ck = slice(c * D, (c + 1) * D)
            a = _mm(ub, w_1[:, ck])
            a1.append(a)
            rl = jnp.maximum(a, 0.0)
            rb = (rl * rl).astype(BF16)
            outs[1][:, ck] = rb
            h3 = h3 + _mm(rb, w_2[ck, :])
        y, xh3, r3 = _rms(h3, g_f[...])
        err = y - rows[1][...]
        accs[0][...] += jnp.sum(jnp.sum(err * err, axis=1, keepdims=True), axis=0, keepdims=True) * (0.5 / D)
        dh3, dgf = _rms_bwd(err * (1.0 / D), xh3, r3, g_f[...])
        accs[1][...] += dgf
        db = dh3.astype(BF16)
        outs[3][...] = db
        du2 = jnp.zeros_like(h)
        for c in range(nck):
            ck = slice(c * D, (c + 1) * D)
            da = (_mm_nt(db, w_2[ck, :]) * (2.0 * jnp.maximum(a1[c], 0.0))).astype(BF16)
            outs[2][:, ck] = da
            du2 = du2 + _mm_nt(da, w_1[:, ck])
        dx2, dgm = _rms_bwd(du2, xh2, r2, g_m[...])
        accs[2][...] += dgm
        outs[4][...] = dh3 + dx2

    outs = [((t, D), BF16), ((t, DFF), BF16), ((t, DFF), BF16), ((t, D), BF16), ((t, D), F32)]
    accs = [((1, 1), F32), ((1, D), F32), ((1, D), F32)]
    return _row_call("mlp_loss", fn, n, [(h2, "cur"), (target, "cur")], [g_mlp, w1, w2, g_fin], outs, accs)


def _merge_xattn_bwd(dh2, h1, qx, gl, conv_out, mla, o, wxo, kx, vx, wxq, g_x, wo, wmo, n):
    t = dh2.shape[0]
    m = kx.shape[0]

    def fn(i, rows, consts, outs, accs, scr):
        w_xo, k_x, v_x, w_xq, g, w_o, w_mo = consts
        d2 = rows[0][...]
        d2b = d2.astype(BF16)
        outs[7][...] = d2b
        dox = _mm_nt(d2b, w_xo[...]).astype(BF16)
        q = rows[2][...]
        dq = []
        for h in range(XH):
            ln = slice(h * XD, (h + 1) * XD)
            qh, kh, vh, doh = q[:, ln], k_x[:, ln], v_x[:, ln], dox[:, ln]
            s = _mm_nt(qh, kh)
            e = jnp.exp(s - jnp.max(s, axis=-1, keepdims=True))
            p = e / jnp.sum(e, axis=-1, keepdims=True)
            dp = _mm_nt(doh, vh)
            ds = p * (dp - jnp.sum(p * dp, axis=-1, keepdims=True))
            dq.append(_mm(ds, kh) * X_SCALE)
            accs[0][:, ln] += _mm_tn(ds, qh)
            accs[1][:, ln] += _mm_tn(p, doh)
        dqx = jnp.concatenate(dq, axis=1).astype(BF16)
        outs[0][...] = dqx
        _, xh1, r1 = _rms(rows[1][...], g[...])
        dx1, dg = _rms_bwd(_mm_nt(dqx, w_xq[...]), xh1, r1, g[...])
        accs[2][...] += dg
        d1 = d2 + dx1
        outs[1][...] = d1
        d1b = d1.astype(BF16)
        outs[8][...] = d1b
        dm = _mm_nt(d1b, w_o[...])
        g0, g1 = _sig(rows[3][:, 0:D]), _sig(rows[3][:, D:])
        outs[2][:, 0:D] = (dm * rows[4][...] * g0 * (1.0 - g0)).astype(BF16)
        outs[2][:, D:] = (dm * rows[5][...] * g1 * (1.0 - g1)).astype(BF16)
        outs[3][...] = (dm * g0).astype(BF16)
        dmla = (dm * g1).astype(BF16)
        outs[4][...] = dmla
        do = _mm_nt(dmla, w_mo[...])
        outs[5][...] = do.astype(BF16)
        prod = do * rows[6][...]
        for h in range(NH):
            outs[6][h] = _col_to_rows(jnp.sum(prod[:, h * HP:(h + 1) * HP], axis=-1, keepdims=True))

    outs = [((t, XH * XD), BF16), ((t, D), F32), ((t, 2 * D), BF16), ((t, D), BF16), ((t, D), BF16), ((t, NH * HP), BF16),
            ((NH, 8, t), F32, "lanes"), ((t, D), BF16), ((t, D), BF16)]
    accs = [((m, XH * XD), F32), ((m, XH * XD), F32), ((1, D), F32)]
    return _row_call("merge_xattn_bwd", fn, n, [(a, "cur") for a in (dh2, h1, qx, gl, conv_out, mla, o)],
                     [wxo, kx, vx, wxq, g_x, wo, wmo], outs, accs)


def _mla_prep_bwd(dq, dk, dv, cq, ckv, tc, tsa, tsb, gq, wuq, gkv, wk, wv, n):
    t = dq.shape[0]

    def fn(i, rows, consts, outs, accs, scr):
        g_q, w_q, g_kv, w_k, w_v = consts
        c, sa, sb = rows[5][...], rows[6][...], rows[7][...]
        dkr = jnp.zeros((rows[0].shape[0], HP), F32)
        for h in range(NH):
            ln = slice(h * HP, (h + 1) * HP)
            outs[0][:, ln] = _rope_bwd(rows[0][:, ln] * MLA_SCALE, c, sa, sb).astype(BF16)
            dkr = dkr + rows[1][:, ln]
        lane = lax.broadcasted_iota(jnp.int32, dkr.shape, 1)
        outs[5][...] = jnp.where((lane >= NOPE) & (lane < NOPE + ROPE), _rope_bwd(dkr, c, sa, sb), 0.0).astype(BF16)
        dkb, dvb = rows[1][...].astype(BF16), rows[2][...].astype(BF16)
        outs[1][...] = dkb
        outs[2][...] = dvb
        _, xq, rq = _rms(rows[3][...], g_q[...])
        dcq, dgq = _rms_bwd(_mm_nt(outs[0][...], w_q[...]), xq, rq, g_q[...])
        outs[3][...] = dcq.astype(BF16)
        accs[0][...] += dgq
        _, xk, rk = _rms(rows[4][...], g_kv[...])
        dckv, dgk = _rms_bwd(_mm_nt(dkb, w_k[...]) + _mm_nt(dvb, w_v[...]), xk, rk, g_kv[...])
        outs[4][...] = dckv.astype(BF16)
        accs[1][...] += dgk

    outs = [((t, NH * HP), BF16)] * 3 + [((t, QL), BF16), ((t, KL), BF16), ((t, HP), BF16)]
    return _row_call("mla_prep_bwd", fn, n, [(a, "cur") for a in (dq, dk, dv, cq, ckv, tc, tsa, tsb)],
                     [gq, wuq, gkv, wk, wv], outs, [((1, QL), F32), ((1, KL), F32)])


def _conv_out_bwd(dco, zc, wco, lng, lnb, n):
    t = zc.shape[0]

    def fn(i, rows, consts, outs, accs, scr):
        wo, lg, lb = consts
        z = rows[1][...]
        mu = jnp.mean(z, axis=-1, keepdims=True)
        dlt = z - mu
        rs = lax.rsqrt(jnp.mean(dlt * dlt, axis=-1, keepdims=True) + EPS)
        xh = dlt * rs
        zn = xh * lg[...] + lb[...]
        sg = _sig(zn)
        outs[0][...] = (zn * sg).astype(BF16)
        dzn = _mm_nt(rows[0][...], wo[...]) * (sg * (1.0 + zn * (1.0 - sg)))
        accs[0][...] += jnp.sum(dzn * xh, axis=0, keepdims=True)
        accs[1][...] += jnp.sum(dzn, axis=0, keepdims=True)
        dxh = dzn * lg[...]
        dzc = rs * (dxh - jnp.mean(dxh, axis=-1, keepdims=True) - xh * jnp.mean(dxh * xh, axis=-1, keepdims=True))
        outs[1][...] = dzc
        accs[2][...] += jnp.sum(dzc, axis=0, keepdims=True)

    return _row_call("conv_out_bwd", fn, n, [(dco, "cur"), (zc, "cur")], [wco, lng, lnb],
                     [((t, CC), BF16), ((t, CC), F32)], [((1, CC), F32)] * 3)


def _conv_glu_bwd(dzc, a, gt, cw, n):
    t = a.shape[0]
    r = t // n

    def fn(i, rows, consts, outs, accs, scr):
        w = consts[0]
        zz, dd = scr
        _fill_glu_window(i, rows[2], rows[3], rows[4], rows[5], zz)
        dd[0:r, :] = rows[0][...]
        dd[r:, :] = jnp.where(i < n - 1, rows[1][0:HALO, :], 0.0)
        rb = min(CONV_ROWS, r)
        for c in range(CC // HP):
            ln = slice(c * HP, (c + 1) * HP)
            for base in range(0, r, rb):
                here = slice(base, base + rb)
                dcur = dd[here, ln]
                for o, win in _windows(zz, base, ln, rb, [HALO - (CW - 1) + j for j in range(CW)]):
                    j = o - (HALO - (CW - 1))
                    accs[0][j:j + 1, ln] += jnp.sum(dcur * win, axis=0, keepdims=True)
                acc = jnp.zeros((rb, HP), F32)
                for o, win in _windows(dd, base, ln, rb, [CW - 1 - j for j in range(CW)]):
                    j = CW - 1 - o
                    acc = acc + w[j:j + 1, ln] * win
                sg = _sig(rows[3][here, ln])
                outs[0][here, ln] = (acc * sg).astype(BF16)
                outs[0][here, CC + c * HP:CC + (c + 1) * HP] = (acc * rows[2][here, ln] * sg * (1.0 - sg)).astype(BF16)

    return _row_call("conv_glu_bwd", fn, n, [(dzc, "cur"), (dzc, "next"), (a, "cur"), (gt, "cur"), (a, "prev"), (gt, "prev")],
                     [cw], [((t, 2 * CC), BF16)], [((HALO, CC), F32)],
                     scratch=[pltpu.VMEM((r + HALO, CC), F32), pltpu.VMEM((r + HALO, CC), F32)])


def _in_proj_bwd(pieces, dh1, x, wp, g_mix, n, exchange):
    t = x.shape[0]
    offs = (P_A, P_Q, P_KV, P_KR, P_GL, P_END)

    def fn(i, rows, consts, outs, accs, scr):
        w, g = consts
        du = jnp.zeros((rows[0].shape[0], D), F32)
        for k in range(5):
            du = du + _mm_nt(rows[k][...], w[:, offs[k]:offs[k + 1]])
        _, xh, r = _rms(rows[6][...], g[...])
        dx, dg = _rms_bwd(du, xh, r, g[...])
        accs[0][...] += dg
        outs[0][...] = rows[5][...] + dx

    return _row_call("in_proj_bwd", fn, n, [(a, "cur") for a in list(pieces) + [dh1, x]], [wp, g_mix],
                     [((t, D), F32)], [((1, D), F32)], exchange=exchange)


def _mem_bwd(mem, dkx, dvx, g_mem, wxkv):
    m = mem.shape[0]

    def fn(i, rows, consts, outs, accs, scr):
        g, w = consts
        dkv = jnp.concatenate([rows[1][...], rows[2][...]], axis=1).astype(BF16)
        outs[0][...] = dkv
        _, xh, _ = _rms(rows[0][...], g[...])
        accs[0][...] += jnp.sum(_mm_nt(dkv, w[...]) * xh, axis=0, keepdims=True)

    return _row_call("mem_bwd", fn, 1, [(mem, "cur"), (dkx, "cur"), (dvx, "cur")], [g_mem, wxkv],
                     [((m, 2 * XH * XD), BF16)], [((1, D), F32)])


def _dw(name, xs, dy):
    t, k = xs.shape
    nn = dy.shape[1]
    tk, tn, tt = min(k, 1024), min(nn, 1024), min(t, DW_TILE)

    def body(x_ref, dy_ref, o_ref):
        @pl.when(pl.program_id(2) == 0)
        def _():
            o_ref[...] = jnp.zeros(o_ref.shape, F32)
        o_ref[...] += lax.dot_general(x_ref[...], dy_ref[...], (((0,), (0,)), ((), ())), preferred_element_type=F32)

    return pl.pallas_call(
        body, name=name, grid=(k // tk, nn // tn, t // tt),
        in_specs=[pl.BlockSpec((tt, tk), lambda a, b, c: (c, a)), pl.BlockSpec((tt, tn), lambda a, b, c: (c, b))],
        out_specs=pl.BlockSpec((tk, tn), lambda a, b, c: (a, b)),
        out_shape=jax.ShapeDtypeStruct((k, nn), F32),
        compiler_params=pltpu.CompilerParams(dimension_semantics=("arbitrary", "arbitrary", "arbitrary"), vmem_limit_bytes=VMEM_LIMIT),
    )(xs, dy)


def _all_gather_packed(shard):
    rws = shard.shape[0]

    def body(x_ref, out_ref, send_sems, recv_sems, local_sem):
        x, y, c = _coords()
        me, sibling = (x, y, c), (x, y, 1 - c)
        chips = [(1 - x, y), (x, 1 - y), (1 - x, 1 - y)]

        def slot(px, py, pc):
            return out_ref.at[4 * px + 2 * py + pc]

        def copy(k, block, to, src=None):
            return pltpu.make_async_remote_copy(
                src_ref=slot(*block) if src is None else src, dst_ref=slot(*block),
                send_sem=send_sems.at[k], recv_sem=recv_sems.at[k], device_id=to, device_id_type=MESH)

        mine = pltpu.make_async_copy(x_ref, slot(*me), local_sem)
        mine.start()
        first = [copy(0, me, sibling, src=x_ref)] + [copy(1 + j, me, (*chip, c), src=x_ref) for j, chip in enumerate(chips)]
        for cp in first:
            cp.start()
        passed = [copy(4 + j, (*chip, c), sibling) for j, chip in enumerate(chips)]
        for j, chip in enumerate(chips):
            copy(1 + j, (*chip, c), me).wait_recv()
            passed[j].start()
        copy(0, sibling, me).wait_recv()
        for j, chip in enumerate(chips):
            copy(4 + j, (*chip, 1 - c), me).wait_recv()
        for cp in first + passed:
            cp.wait_send()
        mine.wait()

    return pl.pallas_call(
        body, name="all_gather_weights",
        out_shape=jax.ShapeDtypeStruct((N_DEV, rws, 128), shard.dtype),
        in_specs=[pl.BlockSpec(memory_space=pl.ANY)], out_specs=pl.BlockSpec(memory_space=pl.ANY),
        scratch_shapes=[pltpu.SemaphoreType.DMA((7,)), pltpu.SemaphoreType.DMA((7,)), pltpu.SemaphoreType.DMA],
    )(shard)


def _adam(w, g, m, v):
    m = ADAM_B1 * m + (1.0 - ADAM_B1) * g
    v = ADAM_B2 * v + (1.0 - ADAM_B2) * (g * g)
    m_hat = m / (1.0 - ADAM_B1 ** ADAM_STEP)
    v_hat = v / (1.0 - ADAM_B2 ** ADAM_STEP)
    return -ADAM_LR * (m_hat / (jnp.sqrt(v_hat) + ADAM_EPS) + ADAM_WD * w), m, v


def _small_allreduce_adam(part, w, m, v):
    shape = part.shape

    def body(p_ref, w_ref, m_ref, v_ref, g_ref, d_ref, nm_ref, nv_ref, buf, send_sems, recv_sems):
        x, y, c = _coords()
        me = 4 * x + 2 * y + c
        buf[0] = p_ref[...]
        cps = []
        for j in range(1, N_DEV):
            jx, jy, jc = j >> 2, (j >> 1) & 1, j & 1
            peer = (1 - x if jx else x, 1 - y if jy else y, 1 - c if jc else c)
            cps.append(pltpu.make_async_remote_copy(src_ref=p_ref, dst_ref=buf.at[j], send_sem=send_sems.at[j - 1],
                                                    recv_sem=recv_sems.at[j - 1], device_id=peer, device_id_type=MESH))
        for cp in cps:
            cp.start()
        for cp in cps:
            cp.wait()
        g = buf[me]
        for d in range(1, N_DEV):
            g = g + buf[d ^ me]
        g_ref[...] = g
        d_ref[...], nm_ref[...], nv_ref[...] = _adam(w_ref[...], g, m_ref[...], v_ref[...])

    vm = pl.BlockSpec(memory_space=pltpu.VMEM)
    return pl.pallas_call(
        body, name="small_allreduce_adam", out_shape=[jax.ShapeDtypeStruct(shape, F32)] * 4,
        in_specs=[vm] * 4, out_specs=[vm] * 4,
        scratch_shapes=[pltpu.VMEM((N_DEV,) + shape, F32), pltpu.SemaphoreType.DMA((7,)), pltpu.SemaphoreType.DMA((7,))],
    )(part, w, m, v)


def _sum_parts(name, parts):
    rws = parts.shape[1]
    tile = max(d for d in range(16, PACK_TILE + 1, 16) if rws % d == 0)

    def body(p_ref, g_ref):
        g = p_ref[0]
        for d in range(1, N_DEV):
            g = g + p_ref[d]
        g_ref[...] = g

    return pl.pallas_call(
        body, name=name, grid=(rws // tile,), in_specs=[pl.BlockSpec((N_DEV, tile, 128), lambda i: (0, i, 0))],
        out_specs=pl.BlockSpec((tile, 128), lambda i: (i, 0)), out_shape=jax.ShapeDtypeStruct((rws, 128), F32),
        compiler_params=pltpu.CompilerParams(dimension_semantics=("arbitrary",), vmem_limit_bytes=VMEM_LIMIT),
    )(parts)


def _adam_call(name, w, g, m, v):
    def body(w_ref, g_ref, m_ref, v_ref, d_ref, nm_ref, nv_ref):
        d_ref[...], nm_ref[...], nv_ref[...] = _adam(w_ref[...], g_ref[...], m_ref[...], v_ref[...])

    vm = pl.BlockSpec(memory_space=pltpu.VMEM)
    return pl.pallas_call(
        body, name=name, in_specs=[vm] * 4, out_specs=[vm] * 3, out_shape=[jax.ShapeDtypeStruct(w.shape, F32)] * 3,
        compiler_params=pltpu.CompilerParams(vmem_limit_bytes=VMEM_LIMIT),
    )(w, g, m, v)


ROW_SHARDED = ("w_out", "w_xq", "w_xkv", "w_mlp2")
SMALL = ("norm_mix_g", "conv_b", "conv_ln_g", "conv_ln_b", "q_norm_g", "kv_norm_g", "norm_xattn_g", "norm_mem_g", "norm_mlp_g", "final_norm_g")
GATHER_FIRST = ("w_in", "conv_w", "w_conv_out", "w_uq", "w_ukv")
GATHER_LATE = ("w_mla_out", "w_out", "w_xq", "w_xkv", "w_xo", "w_mlp1", "w_mlp2")
REDUCE_EARLY = ("w_mlp1", "w_mlp2", "conv_w", "w_conv_out", "w_mla_out", "w_out", "w_xq", "w_xkv", "w_xo", "w_in")
REDUCE_LAST = ("w_uq", "w_ukv")
ROW_ALIGN = 16


def _padded(k, nn):
    return -(-k // ROW_ALIGN) * ROW_ALIGN, -(-nn // 128) * 128


def _pack_rows(a):
    k, nn = a.shape[-2:]
    kp, np_ = _padded(k, nn)
    if (kp, np_) != (k, nn):
        a = jnp.pad(a, [(0, 0)] * (a.ndim - 2) + [(0, kp - k), (0, np_ - nn)])
    if np_ == 128:
        return a
    lead = a.shape[:-2]
    return jnp.swapaxes(a.reshape(lead + (kp, np_ // 128, 128)), -2, -3).reshape(lead + (kp * np_ // 128, 128))


def _unpack_rows(p, k, nn):
    kp, np_ = _padded(k, nn)
    if np_ != 128:
        lead = p.shape[:-2]
        p = jnp.swapaxes(p.reshape(lead + (np_ // 128, kp, 128)), -2, -3).reshape(lead + (kp, np_))
    return p[..., :k, :nn]


def _pack_group(arrays):
    return jnp.concatenate([_pack_rows(a) for a in arrays], axis=-2)


def _unpack_group(p, shapes):
    out, off = [], 0
    for k, nn in shapes:
        kp, np_ = _padded(k, nn)
        rws = kp * np_ // 128
        out.append(_unpack_rows(p[..., off:off + rws, :], k, nn))
        off += rws
    return out


def _tile_shaped(k, nn):
    return k % ROW_ALIGN == 0 and nn % 128 == 0


def _packed_to_full(name, seg, shard_shape):
    k, nn = shard_shape
    if not _tile_shaped(k, nn):
        stacked = _unpack_rows(seg, k, nn)
        return jnp.transpose(stacked, (1, 0, 2)).reshape(k, N_DEV * nn)
    a = seg.reshape(N_DEV, nn // 128, k, 128)
    if name in ROW_SHARDED:
        return jnp.transpose(a, (0, 2, 1, 3)).reshape(N_DEV * k, nn)
    return jnp.transpose(a, (2, 0, 1, 3)).reshape(k, N_DEV * nn)


def _full_to_packed(name, full, shard_shape):
    k, nn = shard_shape
    if not _tile_shaped(k, nn):
        return _pack_rows(jnp.transpose(full.reshape(k, N_DEV, nn), (1, 0, 2)))
    if name in ROW_SHARDED:
        a = jnp.transpose(full.reshape(N_DEV, k, nn // 128, 128), (0, 2, 1, 3))
    else:
        a = jnp.transpose(full.reshape(k, N_DEV, nn // 128, 128), (1, 2, 0, 3))
    return a.reshape(N_DEV, k * nn // 128, 128)


def _pack_small(vals):
    flat = jnp.concatenate([v.reshape(-1) for v in vals])
    rws = flat.shape[0] // 128
    return jnp.pad(flat, (0, (-(-rws // 8) * 8 - rws) * 128)).reshape(-1, 128)


def _unpack_small(p, sizes):
    flat = p.reshape(-1)
    out, off = [], 0
    for s in sizes:
        out.append(flat[off:off + s])
        off += s
    return out


def _head_pad(w, real, axis):
    shp = list(w.shape)
    shp[axis:axis + 1] = [NH, real]
    w = w.reshape(shp)
    pad = [(0, 0)] * w.ndim
    pad[axis + 1] = (0, HP - real)
    w = jnp.pad(w, pad)
    shp[axis:axis + 2] = [NH * HP]
    return w.reshape(shp)


def _head_unpad(w, lo, real, axis):
    shp = list(w.shape)
    shp[axis:axis + 1] = [NH, HP]
    w = lax.slice_in_dim(w.reshape(shp), lo, lo + real, axis=axis + 1)
    shp[axis:axis + 2] = [NH * real]
    return w.reshape(shp)


def kernel(x, mem, positions, norm_mix_g, w_in, conv_w, conv_b, conv_ln_g, conv_ln_b, w_conv_out, q_norm_g, w_uq, kv_norm_g, w_ukv, w_mla_out, w_out, norm_xattn_g, norm_mem_g, w_xq, w_xkv, w_xo, norm_mlp_g, w_mlp1, w_mlp2, final_norm_g, loss_target, m_norm_mix_g, m_w_in, m_conv_w, m_conv_b, m_conv_ln_g, m_conv_ln_b, m_w_conv_out, m_q_norm_g, m_w_uq, m_kv_norm_g, m_w_ukv, m_w_mla_out, m_w_out, m_norm_xattn_g, m_norm_mem_g, m_w_xq, m_w_xkv, m_w_xo, m_norm_mlp_g, m_w_mlp1, m_w_mlp2, m_final_norm_g, v_norm_mix_g, v_w_in, v_conv_w, v_conv_b, v_conv_ln_g, v_conv_ln_b, v_w_conv_out, v_q_norm_g, v_w_uq, v_kv_norm_g, v_w_ukv, v_w_mla_out, v_w_out, v_norm_xattn_g, v_norm_mem_g, v_w_xq, v_w_xkv, v_w_xo, v_norm_mlp_g, v_w_mlp1, v_w_mlp2, v_final_norm_g):
    args = dict(locals())
    t = x.shape[1]
    n = t // min(ROW_TILE, t)
    nl = t // min(LIGHT_ROW_TILE, t)
    xs, mems, tgt = x[0], mem[0], loss_target[0]

    def pack_shards(prefix, group, dtype):
        return _pack_group([args[prefix + k][0].astype(dtype) for k in group])

    def shapes(group):
        return [args[k].shape[1:] for k in group]

    def unpack_full(gathered, group):
        out, off = {}, 0
        for k, shp in zip(group, shapes(group)):
            kp, np_ = _padded(*shp)
            out[k] = _packed_to_full(k, gathered[:, off:off + kp * np_ // 128], shp)
            off += kp * np_ // 128
        return out

    full = unpack_full(_all_gather_packed(pack_shards("", GATHER_FIRST, BF16)), GATHER_FIRST)

    wi = full["w_in"]
    kr_slot = jnp.pad(wi[:, P_KR:P_KR + ROPE], ((0, 0), (NOPE, HP - NOPE - ROPE)))
    wp = jnp.concatenate([wi[:, :P_KR], kr_slot, wi[:, P_KR + ROPE:]], axis=1)
    cw = jnp.pad(full["conv_w"].astype(F32), ((0, HALO - CW), (0, 0)))
    wuq = _head_pad(full["w_uq"], NOPE + ROPE, 1)
    ukv = full["w_ukv"].reshape(KL, NH, NOPE + VD)
    wk = _head_pad(ukv[:, :, :NOPE].reshape(KL, NH * NOPE), NOPE, 1)
    wv = _head_pad(ukv[:, :, NOPE:].reshape(KL, NH * VD), VD, 1)

    half = ROPE // 2
    lane = jnp.arange(HP)
    first, second = (lane >= NOPE) & (lane < NOPE + half), (lane >= NOPE + half) & (lane < NOPE + ROPE)
    inv_freq = THETA ** (-((lane - NOPE) % half).astype(F32) / half)
    ang = positions[0].astype(F32)[:, None] * inv_freq
    cs, sn = jnp.cos(ang), jnp.sin(ang)
    tc = jnp.where(lane < NOPE, 1.0, jnp.where(first | second, cs, 0.0))
    tsa = jnp.where(first, -sn, 0.0)
    tsb = jnp.where(second, sn, 0.0)

    u0, a, gt, cq, ckv, krp, gl = _in_proj(xs, norm_mix_g, wp, nl)
    zc, conv_out = _conv_branch(a, gt, cw, conv_b, conv_ln_g, conv_ln_b, full["w_conv_out"], n)
    qh, kh, vh, cqn, ckvn = _mla_prep(cq, ckv, krp, tc, tsa, tsb, q_norm_g, wuq, kv_norm_g, wk, wv, nl)
    o, lse, gathered = _flash_fwd(qh, kh, vh, exchange=(pack_shards("", GATHER_LATE, BF16), True))
    full.update(unpack_full(gathered, GATHER_LATE))
    wmo = _head_pad(full["w_mla_out"], VD, 0)
    memn, kx, vx = _mem_kv(mems, norm_mem_g, full["w_xkv"])
    mla, merged, h1, u1, qx, ox, h2, ob = _merge_xattn(o, gl, conv_out, xs, wmo, full["w_out"], norm_xattn_g, full["w_xq"],
                                                        kx, vx, full["w_xo"], n)
    gfin = final_norm_g.reshape(1, D)
    u2, rl2, da1, dh3b, dh2, loss_p, dg_fin, dg_mlp = _mlp_loss(h2, tgt, norm_mlp_g, full["w_mlp1"], full["w_mlp2"], gfin, n)

    (dqx, dh1, dgl, dco, dmla, dob, delta, dh2b, dh1b, dkx, dvx, dg_x) = _merge_xattn_bwd(
        dh2, h1, qx, gl, conv_out, mla, o, full["w_xo"], kx, vx, full["w_xq"], norm_xattn_g, full["w_out"], wmo, n)
    gfull = {"w_mlp1": _dw("dw_mlp1", u2, da1), "w_mlp2": _dw("dw_mlp2", rl2, dh3b)}

    def stacked(group):
        return jnp.concatenate([_full_to_packed(k, gfull[k], args[k].shape[1:]) for k in group], axis=1)

    zs, dzc, dg_lng, dg_lnb, dg_cb = _conv_out_bwd(dco, zc, full["w_conv_out"], conv_ln_g, conv_ln_b, nl)
    dci, dcw = _conv_glu_bwd(dzc, a, gt, cw, n)
    dkv, dg_mem = _mem_bwd(mems, dkx, dvx, norm_mem_g, full["w_xkv"])
    gfull.update({
        "conv_w": dcw[:CW],
        "w_conv_out": _dw("dw_conv_out", zs, dco),
        "w_mla_out": _head_unpad(_dw("dw_mla_out", ob, dmla), 0, VD, 0),
        "w_out": _dw("dw_out", merged, dh1b),
        "w_xq": _dw("dw_xq", u1, dqx),
        "w_xkv": _dw("dw_xkv", memn, dkv),
        "w_xo": _dw("dw_xo", ox, dh2b),
    })
    late_lo, late_hi, wcols = P_Q, P_KR + ROPE, args["w_in"].shape[2]
    dw_conv_in, dw_gates = _dw("dw_in_0", u0, dci), _dw("dw_in_4", u0, dgl)
    gfull["w_in"] = jnp.concatenate([dw_conv_in, jnp.zeros((D, late_hi - late_lo), F32), dw_gates], axis=1)
    dk, dv, dq, parts_early = _flash_bwd(qh, kh, vh, dob, lse, delta, exchange=(stacked(REDUCE_EARLY), False))
    dqp, dkb, dvb, dcq, dckv, dkrp, dg_q, dg_kv = _mla_prep_bwd(dq, dk, dv, cq, ckv, tc, tsa, tsb, q_norm_g, wuq, kv_norm_g, wk, wv, nl)
    pieces = (dci, dcq, dckv, dkrp, dgl)
    late = jnp.concatenate([_dw("dw_in_1", u0, dcq), _dw("dw_in_2", u0, dckv), _dw("dw_in_3", u0, dkrp)[:, NOPE:NOPE + ROPE]], axis=1)
    gfull["w_uq"] = _head_unpad(_dw("dw_uq", cqn, dqp), 0, NOPE + ROPE, 1)
    gk = _head_unpad(_dw("dw_uk", ckvn, dkb), 0, NOPE, 1).reshape(KL, NH, NOPE)
    gv = _head_unpad(_dw("dw_uv", ckvn, dvb), 0, VD, 1).reshape(KL, NH, VD)
    gfull["w_ukv"] = jnp.concatenate([gk, gv], axis=2).reshape(KL, NH * (NOPE + VD))
    owners = [(d, max(late_lo, d * wcols) - d * wcols, max(late_lo, d * wcols) - late_lo,
               min(late_hi, (d + 1) * wcols) - max(late_lo, d * wcols)) for d in range(N_DEV)
              if min(late_hi, (d + 1) * wcols) > max(late_lo, d * wcols)]
    late_w = -(-max(w for _, _, _, w in owners) // 128) * 128
    late_src = _pack_rows(jnp.stack([jnp.pad(late[:, l0:l0 + w], ((0, 0), (0, late_w - w))) for _, _, l0, w in owners]))
    grad_x, dg_mix, parts_last, parts_late = _in_proj_bwd(
        pieces, dh1, xs, wp, norm_mix_g, n, [(stacked(REDUCE_LAST), False), (late_src, tuple(d for d, _, _, _ in owners))])

    gsum = {}
    for name, group, parts in (("grad_sum_early", REDUCE_EARLY, parts_early), ("grad_sum_last", REDUCE_LAST, parts_last)):
        gsum.update(zip(group, _unpack_group(_sum_parts(name, parts), shapes(group))))
    late_sum = _unpack_rows(_sum_parts("grad_sum_late", parts_late), D, late_w)
    me = 4 * lax.axis_index("x") + 2 * lax.axis_index("y") + lax.axis_index("c")
    placed = jnp.zeros((D, wcols), F32)
    for d, s0, _, w in owners:
        placed = jnp.where(me == d, jnp.pad(late_sum[:, :w], ((0, 0), (s0, wcols - s0 - w))), placed)
    gsum["w_in"] = gsum["w_in"] + placed
    big = [{}, {}, {}, {}]
    for k, g in gsum.items():
        res = _adam_call("adam_" + k, args[k][0], g, args["m_" + k][0], args["v_" + k][0])
        for kind, val in enumerate([g] + list(res)):
            big[kind][k] = val[None]

    small_g = {"norm_mix_g": dg_mix, "conv_b": dg_cb, "conv_ln_g": dg_lng, "conv_ln_b": dg_lnb, "q_norm_g": dg_q,
               "kv_norm_g": dg_kv, "norm_xattn_g": dg_x, "norm_mem_g": dg_mem, "norm_mlp_g": dg_mlp, "final_norm_g": dg_fin}
    small_sizes = [int(np.prod(args[k].shape)) for k in SMALL] + [128]
    zero_slot = jnp.zeros((128,), F32)
    small_out = _small_allreduce_adam(
        _pack_small([small_g[k] for k in SMALL] + [jnp.pad(loss_p.reshape(-1), (0, 127))]),
        _pack_small([args[k] for k in SMALL] + [zero_slot]), _pack_small([args["m_" + k] for k in SMALL] + [zero_slot]),
        _pack_small([args["v_" + k] for k in SMALL] + [zero_slot]))

    small = [dict(zip(SMALL, [s.reshape(args[k].shape) for k, s in zip(SMALL, _unpack_small(o_, small_sizes))])) for o_ in small_out]
    loss = _unpack_small(small_out[0], small_sizes)[-1][0]
    order = ("norm_mix_g", "w_in", "conv_w", "conv_b", "conv_ln_g", "conv_ln_b", "w_conv_out", "q_norm_g", "w_uq", "kv_norm_g",
             "w_ukv", "w_mla_out", "w_out", "norm_xattn_g", "norm_mem_g", "w_xq", "w_xkv", "w_xo", "norm_mlp_g", "w_mlp1",
             "w_mlp2", "final_norm_g")
    res = [loss, grad_x[None]]
    for kind in range(4):
        res += [big[kind][k] if k in big[kind] else small[kind][k] for k in order]
    return tuple(res)
```

```python
import functools

import jax
import jax.numpy as jnp
import numpy as np
from jax import lax
from jax.experimental import pallas as pl
from jax.experimental.pallas import tpu as pltpu

F32, BF16 = jnp.float32, jnp.bfloat16
MESH = pl.DeviceIdType.MESH

N_DEV = 8
D = 1024
CC = D // 2
CW = 31
HALO = 32
NH = 8
NOPE, ROPE, VD = D // 16, D // 32, D // 16
QL, KL = 3 * D // 8, D // 4
HP = 128
XH, XD = 4, D // 8
DFF = 4 * D
EPS = 1e-6
THETA = 10000.0
MLA_SCALE = float((NOPE + ROPE) ** -0.5)
X_SCALE = float(XD ** -0.5)
NEG = -1e30
P_A, P_G, P_Q, P_KV, P_KR, P_GL, P_END = 0, CC, 2 * CC, 2 * CC + QL, 2 * CC + QL + KL, 2 * CC + QL + KL + HP, 2 * CC + QL + KL + HP + 2 * D

ADAM_LR, ADAM_B1, ADAM_B2, ADAM_EPS, ADAM_WD, ADAM_STEP = 0.001, 0.9, 0.999, 1e-08, 0.01, 10

ROW_TILE = 256
LIGHT_ROW_TILE = 512
ATT_TILE = 512
HG = 2
HG_FWD = 4
ATT_CHUNK = 32
CONV_ROWS = 128
DW_TILE = 1024
PACK_TILE = 1536
VMEM_LIMIT = 56 * 1024 * 1024


def _mm(a, w):
    return jnp.dot(a.astype(BF16), w, preferred_element_type=F32)


def _mm_nt(a, w):
    return lax.dot_general(a.astype(BF16), w, (((1,), (1,)), ((), ())), preferred_element_type=F32)


def _mm_tn(a, b):
    return lax.dot_general(a.astype(BF16), b.astype(BF16), (((0,), (0,)), ((), ())), preferred_element_type=F32)


def _rms(x, g):
    r = lax.rsqrt(jnp.mean(x * x, axis=-1, keepdims=True) + EPS)
    xh = x * r
    return xh * g, xh, r


def _rms_bwd(dy, xh, r, g):
    dxh = dy * g
    dx = r * (dxh - xh * jnp.mean(dxh * xh, axis=-1, keepdims=True))
    return dx, jnp.sum(dy * xh, axis=0, keepdims=True)


def _col_to_rows(col):
    return jnp.transpose(jnp.broadcast_to(col, (col.shape[0], HP)))[0:8, :]


def _sig(x):
    return 1.0 / (1.0 + jnp.exp(-x))


def _coords():
    return lax.axis_index("x"), lax.axis_index("y"), lax.axis_index("c")


def _exchange_ops(src_ref, mode, dst_ref, send_sems, recv_sems, local_sem):
    x, y, c = _coords()
    me = 4 * x + 2 * y + c
    owned = isinstance(mode, tuple)

    def owner_slot(d):
        return sum(jnp.where(d == o, n, 0) for n, o in enumerate(mode))

    def is_owner(d):
        return functools.reduce(jnp.logical_or, [d == o for o in mode])

    def src(d):
        return src_ref if mode is True else src_ref.at[owner_slot(d) if owned else d]

    local = (pltpu.make_async_copy(src(me), dst_ref.at[me], local_sem), is_owner(me) if owned else True)
    remote = []
    for j in range(1, N_DEV):
        px, py, pc = (1 - x if j & 4 else x), (1 - y if j & 2 else y), (1 - c if j & 1 else c)
        d = 4 * px + 2 * py + pc
        cp = pltpu.make_async_remote_copy(src_ref=src(d), dst_ref=dst_ref.at[me], send_sem=send_sems.at[j - 1],
                                          recv_sem=recv_sems.at[j - 1], device_id=(px, py, pc), device_id_type=MESH)
        remote.append((cp, is_owner(d) if owned else True, is_owner(me) if owned else True))
    return local, remote


def _when(cond, fn):
    if cond is True:
        fn()
    else:
        pl.when(cond)(fn)


def _exchange_hook(first, last, mode, refs):
    @pl.when(first)
    def _():
        (local, here), remote = _exchange_ops(refs[0], mode, *refs[1:])
        _when(here, local.start)
        for cp, sends, _ in remote:
            _when(sends, cp.start)

    @pl.when(last)
    def _():
        (local, here), remote = _exchange_ops(refs[0], mode, *refs[1:])
        for cp, sends, receives in remote:
            _when(sends, cp.wait_send)
            _when(receives, cp.wait_recv)
        _when(here, local.wait)


def _exchange_shapes(exchange):
    if exchange is None:
        return [], [], [], []
    arr, mode = exchange
    shape = (N_DEV,) + (arr.shape if mode is True else arr.shape[1:])
    any_spec = pl.BlockSpec(memory_space=pl.ANY)
    sems = [pltpu.SemaphoreType.DMA((N_DEV - 1,)), pltpu.SemaphoreType.DMA((N_DEV - 1,)), pltpu.SemaphoreType.DMA]
    return [any_spec], [any_spec], [jax.ShapeDtypeStruct(shape, arr.dtype)], sems


def _row_call(name, fn, n, rows, consts, outs, accs=(), scratch=(), exchange=None):
    def row_spec(shape, mode):
        r = shape[-2] // n
        if mode == "cur":
            f = lambda i: i
        elif mode == "prev":
            f = lambda i: jnp.maximum(i - 1, 0)
        else:
            f = lambda i: jnp.minimum(i + 1, n - 1)
        if len(shape) == 2:
            return pl.BlockSpec((r, shape[1]), lambda i: (f(i), 0))
        return pl.BlockSpec((shape[0], r, shape[2]), lambda i: (0, f(i), 0))

    def lane_spec(shape):
        return pl.BlockSpec((shape[0], shape[1], shape[2] // n), lambda i: (0, 0, i))

    def whole_spec(shape, single):
        nd = len(shape)
        if single:
            return pl.BlockSpec(shape, lambda i: (0,) * nd, pipeline_mode=pl.Buffered(1))
        return pl.BlockSpec(shape, lambda i: (0,) * nd)

    nr, nc, no, na, ns = len(rows), len(consts), len(outs), len(accs), len(scratch)
    exchanges = [] if exchange is None else (exchange if isinstance(exchange, list) else [exchange])
    shapes = [_exchange_shapes(e) for e in exchanges]
    x_in, x_out, x_shape, x_sems = (sum((s[part] for s in shapes), []) for part in range(4))
    nx = len(exchanges)

    def body(*refs):
        i = pl.program_id(0)
        row_refs, const_refs = refs[:nr], refs[nr:nr + nc]
        o0 = nr + nc + nx
        out_refs, acc_refs = refs[o0:o0 + no], refs[o0 + no:o0 + no + na]
        s0 = o0 + no + na + nx
        for e in range(nx):
            sems = refs[s0 + ns + 3 * e:s0 + ns + 3 * e + 3]
            _exchange_hook(i == 0, i == n - 1, exchanges[e][1], (refs[nr + nc + e], refs[o0 + no + na + e]) + tuple(sems))
        if na:
            @pl.when(i == 0)
            def _():
                for a in acc_refs:
                    a[...] = jnp.zeros(a.shape, a.dtype)
        fn(i, row_refs, const_refs, out_refs, acc_refs, refs[s0:s0 + ns])

    res = pl.pallas_call(
        body, name=name, grid=(n,),
        in_specs=[row_spec(a.shape, m) for a, m in rows] + [whole_spec(c.shape, True) for c in consts] + x_in,
        out_specs=[lane_spec(o[0]) if len(o) == 3 else row_spec(o[0], "cur") for o in outs]
        + [whole_spec(s, False) for s, _ in accs] + x_out,
        out_shape=[jax.ShapeDtypeStruct(o[0], o[1]) for o in list(outs) + list(accs)] + x_shape,
        scratch_shapes=list(scratch) + x_sems,
        compiler_params=pltpu.CompilerParams(dimension_semantics=("arbitrary",), vmem_limit_bytes=VMEM_LIMIT),
    )(*[a for a, _ in rows], *consts, *[e[0] for e in exchanges])
    return list(res)


def _in_proj(x, g_mix, wp, n):
    t = x.shape[0]

    def fn(i, rows, consts, outs, accs, scr):
        g, w = consts
        u, _, _ = _rms(rows[0][...], g[...])
        ub = u.astype(BF16)
        outs[0][...] = ub
        for k, (lo, hi) in enumerate(((P_A, P_G), (P_G, P_Q), (P_Q, P_KV), (P_KV, P_KR), (P_KR, P_GL), (P_GL, P_END))):
            outs[1 + k][...] = _mm(ub, w[:, lo:hi])

    outs = [((t, D), BF16), ((t, CC), F32), ((t, CC), F32), ((t, QL), F32), ((t, KL), F32), ((t, HP), F32), ((t, 2 * D), F32)]
    return _row_call("in_proj", fn, n, [(x, "cur")], [g_mix, wp], outs)


def _fill_glu_window(i, a, gt, ap, gtp, zz):
    r = a.shape[0]
    zp = ap[r - HALO:, :] * _sig(gtp[r - HALO:, :])
    zz[0:HALO, :] = jnp.where(i > 0, zp, 0.0)
    zz[HALO:, :] = a[...] * _sig(gt[...])


def _shift_copies(buf, sh):
    rows = buf.shape[0]
    for s in range(8):
        sh[s, 0:rows - s, :] = buf[s:rows, :]


def _window(sh, o, base, rb, ln):
    return sh[o % 8, base + o - o % 8:base + o - o % 8 + rb, ln]


def _windows(buf, base, ln, rb, offsets):
    for s in range(8):
        group = [o for o in offsets if o % 8 == s]
        if group:
            shifted = buf[base + s:base + max(group) + rb, ln]
            for o in group:
                yield o, shifted[o - s:o - s + rb]


def _conv_branch(a, gt, cw, cb, lng, lnb, wco, n):
    t = a.shape[0]
    r = t // n

    def fn(i, rows, consts, outs, accs, scr):
        w, b, lg, lb, wo = consts
        zz, zsh = scr
        _fill_glu_window(i, rows[0], rows[1], rows[2], rows[3], zz)
        _shift_copies(zz, zsh)
        rb = min(CONV_ROWS, r)
        for c in range(CC // HP):
            ln = slice(c * HP, (c + 1) * HP)
            for base in range(0, r, rb):
                acc = jnp.zeros((rb, HP), F32)
                for j in range(CW):
                    acc = acc + w[j:j + 1, ln] * _window(zsh, HALO - (CW - 1) + j, base, rb, ln)
                outs[0][base:base + rb, ln] = acc + b[:, ln]
        zc = outs[0][...]
        mu = jnp.mean(zc, axis=-1, keepdims=True)
        dlt = zc - mu
        rs = lax.rsqrt(jnp.mean(dlt * dlt, axis=-1, keepdims=True) + EPS)
        zn = dlt * rs * lg[...] + lb[...]
        outs[1][...] = _mm(zn * _sig(zn), wo[...])

    return _row_call("conv_branch", fn, n, [(a, "cur"), (gt, "cur"), (a, "prev"), (gt, "prev")],
                     [cw, cb, lng, lnb, wco], [((t, CC), F32), ((t, D), F32)],
                     scratch=[pltpu.VMEM((r + HALO, CC), F32), pltpu.VMEM((8, r + HALO, CC), F32)])


def _rope(v, c, sa, sb):
    return v * c + pltpu.roll(v, HP - ROPE // 2, 1) * sa + pltpu.roll(v, ROPE // 2, 1) * sb


def _rope_bwd(dv, c, sa, sb):
    return dv * c + pltpu.roll(dv * sa, ROPE // 2, 1) + pltpu.roll(dv * sb, HP - ROPE // 2, 1)


def _mla_prep(cq, ckv, krp, tc, tsa, tsb, gq, wuq, gkv, wk, wv, n):
    t = cq.shape[0]

    def fn(i, rows, consts, outs, accs, scr):
        g_q, w_q, g_kv, w_k, w_v = consts
        c, sa, sb = rows[3][...], rows[4][...], rows[5][...]
        cqn = _rms(rows[0][...], g_q[...])[0].astype(BF16)
        ckvn = _rms(rows[1][...], g_kv[...])[0].astype(BF16)
        outs[3][...] = cqn
        outs[4][...] = ckvn
        krr = _rope(rows[2][...], c, sa, sb)
        for h in range(NH):
            ln = slice(h * HP, (h + 1) * HP)
            outs[0][:, ln] = (_rope(_mm(cqn, w_q[:, ln]), c, sa, sb) * MLA_SCALE).astype(BF16)
            outs[1][:, ln] = (_mm(ckvn, w_k[:, ln]) + krr).astype(BF16)
        vv = _mm(ckvn, w_v[...])
        lane = lax.broadcasted_iota(jnp.int32, vv.shape, 1)
        outs[2][...] = jnp.where((lane & (HP - 1)) == VD, 1.0, vv).astype(BF16)

    outs = [((t, NH * HP), BF16)] * 3 + [((t, QL), BF16), ((t, KL), BF16)]
    return _row_call("mla_prep", fn, n, [(a, "cur") for a in (cq, ckv, krp, tc, tsa, tsb)], [gq, wuq, gkv, wk, wv], outs)


def _chunk_mask(c, tq, transposed, rows=ATT_CHUNK):
    row = lax.broadcasted_iota(jnp.int32, (rows, tq), 0) + c * rows
    col = lax.broadcasted_iota(jnp.int32, (rows, tq), 1)
    return (row <= col) if transposed else (col <= row)


def _flash_hook(exchange, nh, nq, refs):
    if exchange is not None:
        h, i = pl.program_id(0), pl.program_id(1)
        _exchange_hook((h == 0) & (i == 0), (h == nh - 1) & (i == nq - 1), exchange[1], refs)


def _head_lanes(g):
    return slice(g * HP, (g + 1) * HP)


def _flash_fwd(q, k, v, exchange=None):
    t = q.shape[0]
    tq = min(ATT_TILE, t)
    nq = t // tq
    x_in, x_out, x_shape, x_sems = _exchange_shapes(exchange)

    def body(q_ref, k_ref, v_ref, *rest):
        o_ref, lse_ref = rest[len(x_in):len(x_in) + 2]
        _flash_hook(exchange, NH // HG_FWD, nq, rest[:len(x_in)] + rest[len(x_in) + 2:])
        i = pl.program_id(1)

        def step(j, carry, masked):
            at = pl.ds(pl.multiple_of(j * tq, tq), tq)
            out = []
            for g in range(HG_FWD):
                m, acc = carry[g]
                s = _mm_nt(q_ref[:, _head_lanes(g)], k_ref[at, _head_lanes(g)])
                if masked:
                    s = jnp.where(_chunk_mask(0, tq, False, tq), s, NEG)
                m_new = jnp.maximum(m, jnp.max(s, axis=-1, keepdims=True))
                out.append((m_new, jnp.exp(m - m_new) * acc + _mm(jnp.exp(s - m_new), v_ref[at, _head_lanes(g)])))
            return tuple(out)

        init = tuple((jnp.full((tq, 1), NEG, F32), jnp.zeros((tq, HP), F32)) for _ in range(HG_FWD))
        carry = lax.fori_loop(0, i, lambda j, c: step(j, c, False), init)
        for g, (m, acc) in enumerate(step(i, carry, True)):
            lane = lax.broadcasted_iota(jnp.int32, acc.shape, 1)
            l = jnp.sum(jnp.where(lane == VD, acc, 0.0), axis=-1, keepdims=True)
            o_ref[:, _head_lanes(g)] = acc / l
            lse_ref[g] = _col_to_rows(m + jnp.log(l))

    wide = HG_FWD * HP
    return pl.pallas_call(
        body, name="flash_fwd", grid=(NH // HG_FWD, nq),
        in_specs=[pl.BlockSpec((tq, wide), lambda h, i: (i, h)), pl.BlockSpec((t, wide), lambda h, i: (0, h)),
                  pl.BlockSpec((t, wide), lambda h, i: (0, h))] + x_in,
        out_specs=[pl.BlockSpec((tq, wide), lambda h, i: (i, h)), pl.BlockSpec((HG_FWD, 8, tq), lambda h, i: (h, 0, i))] + x_out,
        out_shape=[jax.ShapeDtypeStruct((t, NH * HP), F32), jax.ShapeDtypeStruct((NH, 8, t), F32)] + x_shape,
        scratch_shapes=x_sems,
        compiler_params=pltpu.CompilerParams(dimension_semantics=("arbitrary", "arbitrary"), vmem_limit_bytes=VMEM_LIMIT),
    )(q, k, v, *([exchange[0]] if x_in else []))


def _flash_bwd(q, k, v, do, lse_row, delta_row, exchange=None):
    t = q.shape[0]
    tq = min(ATT_TILE, t)
    nq = t // tq
    x_in, x_out, x_shape, x_sems = _exchange_shapes(exchange)

    def body(q_ref, k_ref, v_ref, do_ref, lse_ref, dl_ref, *rest):
        dk_ref, dv_ref, dq_ref = rest[len(x_in):len(x_in) + 3]
        st_scr, dpt_scr, pt_scr, dst_scr = rest[len(rest) - 4:]
        _flash_hook(exchange, NH // HG, nq, rest[:len(x_in)] + rest[len(x_in) + 3:len(rest) - 4])
        j = pl.program_id(1)
        dk_ref[...] = jnp.zeros(dk_ref.shape, F32)
        dv_ref[...] = jnp.zeros(dv_ref.shape, F32)

        @pl.when(j == 0)
        def _():
            dq_ref[...] = jnp.zeros(dq_ref.shape, F32)

        def step(i, masked):
            at = pl.ds(pl.multiple_of(i * tq, tq), tq)
            for g in range(HG):
                st_scr[g] = _mm_nt(k_ref[:, _head_lanes(g)], q_ref[at, _head_lanes(g)])
                dpt_scr[g] = _mm_nt(v_ref[:, _head_lanes(g)], do_ref[at, _head_lanes(g)])
            for g in range(HG):
                lse_i, dl_i = lse_ref[g, 0:1, at], dl_ref[g, 0:1, at]
                for c in range(tq // ATT_CHUNK):
                    rows = slice(c * ATT_CHUNK, (c + 1) * ATT_CHUNK)
                    st = st_scr[g, rows, :]
                    if masked:
                        st = jnp.where(_chunk_mask(c, tq, True), st, NEG)
                    pt = jnp.exp(st - lse_i)
                    pt_scr[g, rows, :] = pt.astype(BF16)
                    dst_scr[g, rows, :] = (pt * (dpt_scr[g, rows, :] - dl_i)).astype(BF16)
            for g in range(HG):
                dv_ref[:, _head_lanes(g)] += _mm(pt_scr[g], do_ref[at, _head_lanes(g)])
                dk_ref[:, _head_lanes(g)] += _mm(dst_scr[g], q_ref[at, _head_lanes(g)])
                dq_ref[at, _head_lanes(g)] += _mm_tn(dst_scr[g], k_ref[:, _head_lanes(g)])

        step(j, True)
        lax.fori_loop(j + 1, nq, lambda i, c: (step(i, False), c)[1], 0)

    wide = HG * HP
    blk = pl.BlockSpec((tq, wide), lambda h, j: (j, h))
    whole = pl.BlockSpec((t, wide), lambda h, j: (0, h))
    row = pl.BlockSpec((HG, 8, t), lambda h, j: (h, 0, 0))
    x_sems = x_sems + [pltpu.VMEM((HG, tq, tq), F32), pltpu.VMEM((HG, tq, tq), F32), pltpu.VMEM((HG, tq, tq), BF16),
                       pltpu.VMEM((HG, tq, tq), BF16)]
    return pl.pallas_call(
        body, name="flash_bwd", grid=(NH // HG, nq), in_specs=[whole, blk, blk, whole, row, row] + x_in,
        out_specs=[blk, blk, whole] + x_out, out_shape=[jax.ShapeDtypeStruct((t, NH * HP), F32)] * 3 + x_shape, scratch_shapes=x_sems,
        compiler_params=pltpu.CompilerParams(dimension_semantics=("arbitrary", "arbitrary"), vmem_limit_bytes=VMEM_LIMIT),
    )(q, k, v, do, lse_row, delta_row, *([exchange[0]] if x_in else []))


def _mem_kv(mem, g_mem, wxkv):
    m = mem.shape[0]

    def fn(i, rows, consts, outs, accs, scr):
        mn = _rms(rows[0][...], consts[0][...])[0].astype(BF16)
        outs[0][...] = mn
        outs[1][...] = _mm(mn, consts[1][:, 0:XH * XD]).astype(BF16)
        outs[2][...] = _mm(mn, consts[1][:, XH * XD:]).astype(BF16)

    return _row_call("mem_kv", fn, 1, [(mem, "cur")], [g_mem, wxkv], [((m, D), BF16), ((m, XH * XD), BF16), ((m, XH * XD), BF16)])


def _merge_xattn(o, gl, conv_out, x, wmo, wo, g_x, wxq, kx, vx, wxo, n):
    t = x.shape[0]

    def fn(i, rows, consts, outs, accs, scr):
        w_mo, w_o, g, w_xq, k_x, v_x, w_xo = consts
        ob = rows[0][...].astype(BF16)
        outs[7][...] = ob
        mla = _mm(ob, w_mo[...])
        outs[0][...] = mla
        merged = (_sig(rows[1][:, 0:D]) * rows[2][...] + _sig(rows[1][:, D:]) * mla).astype(BF16)
        outs[1][...] = merged
        h1 = rows[3][...] + _mm(merged, w_o[...])
        outs[2][...] = h1
        u1 = _rms(h1, g[...])[0].astype(BF16)
        outs[3][...] = u1
        qx = (_mm(u1, w_xq[...]) * X_SCALE).astype(BF16)
        outs[4][...] = qx
        for h in range(XH):
            ln = slice(h * XD, (h + 1) * XD)
            s = _mm_nt(qx[:, ln], k_x[:, ln])
            e = jnp.exp(s - jnp.max(s, axis=-1, keepdims=True))
            p = e / jnp.sum(e, axis=-1, keepdims=True)
            outs[5][:, ln] = _mm(p, v_x[:, ln]).astype(BF16)
        outs[6][...] = h1 + _mm(outs[5][...], w_xo[...])

    outs = [((t, D), F32), ((t, D), BF16), ((t, D), F32), ((t, D), BF16), ((t, XH * XD), BF16), ((t, XH * XD), BF16),
            ((t, D), F32), ((t, NH * HP), BF16)]
    return _row_call("merge_xattn", fn, n, [(a, "cur") for a in (o, gl, conv_out, x)], [wmo, wo, g_x, wxq, kx, vx, wxo], outs)


def _mlp_loss(h2, target, g_mlp, w1, w2, g_fin, n):
    t = h2.shape[0]
    nck = DFF // D

    def fn(i, rows, consts, outs, accs, scr):
        g_m, w_1, w_2, g_f = consts
        h = rows[0][...]
        u2, xh2, r2 = _rms(h, g_m[...])
        ub = u2.astype(BF16)
        outs[0][...] = ub
        h3 = h
        a1 = []
        for c in range(nck):
            ck = slice(c * D, (c + 1) * D)
            a = _mm(ub, w_1[:, ck])
            a1.append(a)
            rl = jnp.maximum(a, 0.0)
            rb = (rl * rl).astype(BF16)
            outs[1][:, ck] = rb
            h3 = h3 + _mm(rb, w_2[ck, :])
        y, xh3, r3 = _rms(h3, g_f[...])
        err = y - rows[1][...]
        accs[0][...] += jnp.sum(jnp.sum(err * err, axis=1, keepdims=True), axis=0, keepdims=True) * (0.5 / D)
        dh3, dgf = _rms_bwd(err * (1.0 / D), xh3, r3, g_f[...])
        accs[1][...] += dgf
        db = dh3.astype(BF16)
        outs[3][...] = db
        du2 = jnp.zeros_like(h)
        for c in range(nck):
            ck = slice(c * D, (c + 1) * D)
            da = (_mm_nt(db, w_2[ck, :]) * (2.0 * jnp.maximum(a1[c], 0.0))).astype(BF16)
            outs[2][:, ck] = da
            du2 = du2 + _mm_nt(da, w_1[:, ck])
        dx2, dgm = _rms_bwd(du2, xh2, r2, g_m[...])
        accs[2][...] += dgm
        outs[4][...] = dh3 + dx2

    outs = [((t, D), BF16), ((t, DFF), BF16), ((t, DFF), BF16), ((t, D), BF16), ((t, D), F32)]
    accs = [((1, 1), F32), ((1, D), F32), ((1, D), F32)]
    return _row_call("mlp_loss", fn, n, [(h2, "cur"), (target, "cur")], [g_mlp, w1, w2, g_fin], outs, accs)


def _merge_xattn_bwd(dh2, h1, qx, gl, conv_out, mla, o, wxo, kx, vx, wxq, g_x, wo, wmo, n):
    t = dh2.shape[0]
    m = kx.shape[0]

    def fn(i, rows, consts, outs, accs, scr):
        w_xo, k_x, v_x, w_xq, g, w_o, w_mo = consts
        d2 = rows[0][...]
        d2b = d2.astype(BF16)
        outs[7][...] = d2b
        dox = _mm_nt(d2b, w_xo[...]).astype(BF16)
        q = rows[2][...]
        dq = []
        for h in range(XH):
            ln = slice(h * XD, (h + 1) * XD)
            qh, kh, vh, doh = q[:, ln], k_x[:, ln], v_x[:, ln], dox[:, ln]
            s = _mm_nt(qh, kh)
            e = jnp.exp(s - jnp.max(s, axis=-1, keepdims=True))
            p = e / jnp.sum(e, axis=-1, keepdims=True)
            dp = _mm_nt(doh, vh)
            ds = p * (dp - jnp.sum(p * dp, axis=-1, keepdims=True))
            dq.append(_mm(ds, kh) * X_SCALE)
            accs[0][:, ln] += _mm_tn(ds, qh)
            accs[1][:, ln] += _mm_tn(p, doh)
        dqx = jnp.concatenate(dq, axis=1).astype(BF16)
        outs[0][...] = dqx
        _, xh1, r1 = _rms(rows[1][...], g[...])
        dx1, dg = _rms_bwd(_mm_nt(dqx, w_xq[...]), xh1, r1, g[...])
        accs[2][...] += dg
        d1 = d2 + dx1
        outs[1][...] = d1
        d1b = d1.astype(BF16)
        outs[8][...] = d1b
        dm = _mm_nt(d1b, w_o[...])
        g0, g1 = _sig(rows[3][:, 0:D]), _sig(rows[3][:, D:])
        outs[2][:, 0:D] = (dm * rows[4][...] * g0 * (1.0 - g0)).astype(BF16)
        outs[2][:, D:] = (dm * rows[5][...] * g1 * (1.0 - g1)).astype(BF16)
        outs[3][...] = (dm * g0).astype(BF16)
        dmla = (dm * g1).astype(BF16)
        outs[4][...] = dmla
        do = _mm_nt(dmla, w_mo[...])
        outs[5][...] = do.astype(BF16)
        prod = do * rows[6][...]
        for h in range(NH):
            outs[6][h] = _col_to_rows(jnp.sum(prod[:, h * HP:(h + 1) * HP], axis=-1, keepdims=True))

    outs = [((t, XH * XD), BF16), ((t, D), F32), ((t, 2 * D), BF16), ((t, D), BF16), ((t, D), BF16), ((t, NH * HP), BF16),
            ((NH, 8, t), F32, "lanes"), ((t, D), BF16), ((t, D), BF16)]
    accs = [((m, XH * XD), F32), ((m, XH * XD), F32), ((1, D), F32)]
    return _row_call("merge_xattn_bwd", fn, n, [(a, "cur") for a in (dh2, h1, qx, gl, conv_out, mla, o)],
                     [wxo, kx, vx, wxq, g_x, wo, wmo], outs, accs)


def _mla_prep_bwd(dq, dk, dv, cq, ckv, tc, tsa, tsb, gq, wuq, gkv, wk, wv, n):
    t = dq.shape[0]

    def fn(i, rows, consts, outs, accs, scr):
        g_q, w_q, g_kv, w_k, w_v = consts
        c, sa, sb = rows[5][...], rows[6][...], rows[7][...]
        dkr = jnp.zeros((rows[0].shape[0], HP), F32)
        for h in range(NH):
            ln = slice(h * HP, (h + 1) * HP)
            outs[0][:, ln] = _rope_bwd(rows[0][:, ln] * MLA_SCALE, c, sa, sb).astype(BF16)
            dkr = dkr + rows[1][:, ln]
        lane = lax.broadcasted_iota(jnp.int32, dkr.shape, 1)
        outs[5][...] = jnp.where((lane >= NOPE) & (lane < NOPE + ROPE), _rope_bwd(dkr, c, sa, sb), 0.0).astype(BF16)
        dkb, dvb = rows[1][...].astype(BF16), rows[2][...].astype(BF16)
        outs[1][...] = dkb
        outs[2][...] = dvb
        _, xq, rq = _rms(rows[3][...], g_q[...])
        dcq, dgq = _rms_bwd(_mm_nt(outs[0][...], w_q[...]), xq, rq, g_q[...])
        outs[3][...] = dcq.astype(BF16)
        accs[0][...] += dgq
        _, xk, rk = _rms(rows[4][...], g_kv[...])
        dckv, dgk = _rms_bwd(_mm_nt(dkb, w_k[...]) + _mm_nt(dvb, w_v[...]), xk, rk, g_kv[...])
        outs[4][...] = dckv.astype(BF16)
        accs[1][...] += dgk

    outs = [((t, NH * HP), BF16)] * 3 + [((t, QL), BF16), ((t, KL), BF16), ((t, HP), BF16)]
    return _row_call("mla_prep_bwd", fn, n, [(a, "cur") for a in (dq, dk, dv, cq, ckv, tc, tsa, tsb)],
                     [gq, wuq, gkv, wk, wv], outs, [((1, QL), F32), ((1, KL), F32)])


def _conv_out_bwd(dco, zc, wco, lng, lnb, n):
    t = zc.shape[0]

    def fn(i, rows, consts, outs, accs, scr):
        wo, lg, lb = consts
        z = rows[1][...]
        mu = jnp.mean(z, axis=-1, keepdims=True)
        dlt = z - mu
        rs = lax.rsqrt(jnp.mean(dlt * dlt, axis=-1, keepdims=True) + EPS)
        xh = dlt * rs
        zn = xh * lg[...] + lb[...]
        sg = _sig(zn)
        outs[0][...] = (zn * sg).astype(BF16)
        dzn = _mm_nt(rows[0][...], wo[...]) * (sg * (1.0 + zn * (1.0 - sg)))
        accs[0][...] += jnp.sum(dzn * xh, axis=0, keepdims=True)
        accs[1][...] += jnp.sum(dzn, axis=0, keepdims=True)
        dxh = dzn * lg[...]
        dzc = rs * (dxh - jnp.mean(dxh, axis=-1, keepdims=True) - xh * jnp.mean(dxh * xh, axis=-1, keepdims=True))
        outs[1][...] = dzc
        accs[2][...] += jnp.sum(dzc, axis=0, keepdims=True)

    return _row_call("conv_out_bwd", fn, n, [(dco, "cur"), (zc, "cur")], [wco, lng, lnb],
                     [((t, CC), BF16), ((t, CC), F32)], [((1, CC), F32)] * 3)


def _conv_glu_bwd(dzc, a, gt, cw, n):
    t = a.shape[0]
    r = t // n

    def fn(i, rows, consts, outs, accs, scr):
        w = consts[0]
        zz, dd = scr
        _fill_glu_window(i, rows[2], rows[3], rows[4], rows[5], zz)
        dd[0:r, :] = rows[0][...]
        dd[r:, :] = jnp.where(i < n - 1, rows[1][0:HALO, :], 0.0)
        rb = min(CONV_ROWS, r)
        for c in range(CC // HP):
            ln = slice(c * HP, (c + 1) * HP)
            for base in range(0, r, rb):
                here = slice(base, base + rb)
                dcur = dd[here, ln]
                for o, win in _windows(zz, base, ln, rb, [HALO - (CW - 1) + j for j in range(CW)]):
                    j = o - (HALO - (CW - 1))
                    accs[0][j:j + 1, ln] += jnp.sum(dcur * win, axis=0, keepdims=True)
                acc = jnp.zeros((rb, HP), F32)
                for o, win in _windows(dd, base, ln, rb, [CW - 1 - j for j in range(CW)]):
                    j = CW - 1 - o
                    acc = acc + w[j:j + 1, ln] * win
                sg = _sig(rows[3][here, ln])
                outs[0][here, ln] = (acc * sg).astype(BF16)
                outs[0][here, CC + c * HP:CC + (c + 1) * HP] = (acc * rows[2][here, ln] * sg * (1.0 - sg)).astype(BF16)

    return _row_call("conv_glu_bwd", fn, n, [(dzc, "cur"), (dzc, "next"), (a, "cur"), (gt, "cur"), (a, "prev"), (gt, "prev")],
                     [cw], [((t, 2 * CC), BF16)], [((HALO, CC), F32)],
                     scratch=[pltpu.VMEM((r + HALO, CC), F32), pltpu.VMEM((r + HALO, CC), F32)])


def _in_proj_bwd(pieces, dh1, x, wp, g_mix, n, exchange):
    t = x.shape[0]
    offs = (P_A, P_Q, P_KV, P_KR, P_GL, P_END)

    def fn(i, rows, consts, outs, accs, scr):
        w, g = consts
        du = jnp.zeros((rows[0].shape[0], D), F32)
        for k in range(5):
            du = du + _mm_nt(rows[k][...], w[:, offs[k]:offs[k + 1]])
        _, xh, r = _rms(rows[6][...], g[...])
        dx, dg = _rms_bwd(du, xh, r, g[...])
        accs[0][...] += dg
        outs[0][...] = rows[5][...] + dx

    return _row_call("in_proj_bwd", fn, n, [(a, "cur") for a in list(pieces) + [dh1, x]], [wp, g_mix],
                     [((t, D), F32)], [((1, D), F32)], exchange=exchange)


def _mem_bwd(mem, dkx, dvx, g_mem, wxkv):
    m = mem.shape[0]

    def fn(i, rows, consts, outs, accs, scr):
        g, w = consts
        dkv = jnp.concatenate([rows[1][...], rows[2][...]], axis=1).astype(BF16)
        outs[0][...] = dkv
        _, xh, _ = _rms(rows[0][...], g[...])
        accs[0][...] += jnp.sum(_mm_nt(dkv, w[...]) * xh, axis=0, keepdims=True)

    return _row_call("mem_bwd", fn, 1, [(mem, "cur"), (dkx, "cur"), (dvx, "cur")], [g_mem, wxkv],
                     [((m, 2 * XH * XD), BF16)], [((1, D), F32)])


def _dw(name, xs, dy):
    t, k = xs.shape
    nn = dy.shape[1]
    tk, tn, tt = min(k, 1024), min(nn, 1024), min(t, DW_TILE)

    def body(x_ref, dy_ref, o_ref):
        @pl.when(pl.program_id(2) == 0)
        def _():
            o_ref[...] = jnp.zeros(o_ref.shape, F32)
        o_ref[...] += lax.dot_general(x_ref[...], dy_ref[...], (((0,), (0,)), ((), ())), preferred_element_type=F32)

    return pl.pallas_call(
        body, name=name, grid=(k // tk, nn // tn, t // tt),
        in_specs=[pl.BlockSpec((tt, tk), lambda a, b, c: (c, a)), pl.BlockSpec((tt, tn), lambda a, b, c: (c, b))],
        out_specs=pl.BlockSpec((tk, tn), lambda a, b, c: (a, b)),
        out_shape=jax.ShapeDtypeStruct((k, nn), F32),
        compiler_params=pltpu.CompilerParams(dimension_semantics=("arbitrary", "arbitrary", "arbitrary"), vmem_limit_bytes=VMEM_LIMIT),
    )(xs, dy)


def _dw_shared(name, xs, dys):
    t, k = xs.shape
    tt = min(t, DW_TILE)
    nd = len(dys)

    def body(x_ref, *refs):
        @pl.when(pl.program_id(0) == 0)
        def _():
            for o_ref in refs[nd:]:
                o_ref[...] = jnp.zeros(o_ref.shape, F32)
        xb = x_ref[...]
        for dy_ref, o_ref in zip(refs[:nd], refs[nd:]):
            o_ref[...] += lax.dot_general(xb, dy_ref[...], (((0,), (0,)), ((), ())), preferred_element_type=F32)

    return pl.pallas_call(
        body, name=name, grid=(t // tt,),
        in_specs=[pl.BlockSpec((tt, k), lambda c: (c, 0))] + [pl.BlockSpec((tt, d.shape[1]), lambda c: (c, 0)) for d in dys],
        out_specs=[pl.BlockSpec((k, d.shape[1]), lambda c: (0, 0)) for d in dys],
        out_shape=[jax.ShapeDtypeStruct((k, d.shape[1]), F32) for d in dys],
        compiler_params=pltpu.CompilerParams(dimension_semantics=("arbitrary",), vmem_limit_bytes=VMEM_LIMIT),
    )(xs, *dys)


def _all_gather_packed(shard):
    rws = shard.shape[0]

    def body(x_ref, out_ref, send_sems, recv_sems, local_sem):
        x, y, c = _coords()
        me, sibling = (x, y, c), (x, y, 1 - c)
        chips = [(1 - x, y), (x, 1 - y), (1 - x, 1 - y)]

        def slot(px, py, pc):
            return out_ref.at[4 * px + 2 * py + pc]

        def copy(k, block, to, src=None):
            return pltpu.make_async_remote_copy(
                src_ref=slot(*block) if src is None else src, dst_ref=slot(*block),
                send_sem=send_sems.at[k], recv_sem=recv_sems.at[k], device_id=to, device_id_type=MESH)

        mine = pltpu.make_async_copy(x_ref, slot(*me), local_sem)
        mine.start()
        first = [copy(0, me, sibling, src=x_ref)] + [copy(1 + j, me, (*chip, c), src=x_ref) for j, chip in enumerate(chips)]
        for cp in first:
            cp.start()
        passed = [copy(4 + j, (*chip, c), sibling) for j, chip in enumerate(chips)]
        for j, chip in enumerate(chips):
            copy(1 + j, (*chip, c), me).wait_recv()
            passed[j].start()
        copy(0, sibling, me).wait_recv()
        for j, chip in enumerate(chips):
            copy(4 + j, (*chip, 1 - c), me).wait_recv()
        for cp in first + passed:
            cp.wait_send()
        mine.wait()

    return pl.pallas_call(
        body, name="all_gather_weights",
        out_shape=jax.ShapeDtypeStruct((N_DEV, rws, 128), shard.dtype),
        in_specs=[pl.BlockSpec(memory_space=pl.ANY)], out_specs=pl.BlockSpec(memory_space=pl.ANY),
        scratch_shapes=[pltpu.SemaphoreType.DMA((7,)), pltpu.SemaphoreType.DMA((7,)), pltpu.SemaphoreType.DMA],
    )(shard)


def _adam(w, g, m, v):
    m = ADAM_B1 * m + (1.0 - ADAM_B1) * g
    v = ADAM_B2 * v + (1.0 - ADAM_B2) * (g * g)
    m_hat = m / (1.0 - ADAM_B1 ** ADAM_STEP)
    v_hat = v / (1.0 - ADAM_B2 ** ADAM_STEP)
    return -ADAM_LR * (m_hat / (jnp.sqrt(v_hat) + ADAM_EPS) + ADAM_WD * w), m, v


def _small_allreduce_adam(part, w, m, v):
    shape = part.shape

    def body(p_ref, w_ref, m_ref, v_ref, g_ref, d_ref, nm_ref, nv_ref, buf, send_sems, recv_sems):
        x, y, c = _coords()
        me = 4 * x + 2 * y + c
        buf[0] = p_ref[...]
        cps = []
        for j in range(1, N_DEV):
            jx, jy, jc = j >> 2, (j >> 1) & 1, j & 1
            peer = (1 - x if jx else x, 1 - y if jy else y, 1 - c if jc else c)
            cps.append(pltpu.make_async_remote_copy(src_ref=p_ref, dst_ref=buf.at[j], send_sem=send_sems.at[j - 1],
                                                    recv_sem=recv_sems.at[j - 1], device_id=peer, device_id_type=MESH))
        for cp in cps:
            cp.start()
        for cp in cps:
            cp.wait()
        g = buf[me]
        for d in range(1, N_DEV):
            g = g + buf[d ^ me]
        g_ref[...] = g
        d_ref[...], nm_ref[...], nv_ref[...] = _adam(w_ref[...], g, m_ref[...], v_ref[...])

    vm = pl.BlockSpec(memory_space=pltpu.VMEM)
    return pl.pallas_call(
        body, name="small_allreduce_adam", out_shape=[jax.ShapeDtypeStruct(shape, F32)] * 4,
        in_specs=[vm] * 4, out_specs=[vm] * 4,
        scratch_shapes=[pltpu.VMEM((N_DEV,) + shape, F32), pltpu.SemaphoreType.DMA((7,)), pltpu.SemaphoreType.DMA((7,))],
    )(part, w, m, v)


def _sum_parts(name, parts):
    rws = parts.shape[1]
    tile = max(d for d in range(16, PACK_TILE + 1, 16) if rws % d == 0)

    def body(p_ref, g_ref):
        g = p_ref[0]
        for d in range(1, N_DEV):
            g = g + p_ref[d]
        g_ref[...] = g

    return pl.pallas_call(
        body, name=name, grid=(rws // tile,), in_specs=[pl.BlockSpec((N_DEV, tile, 128), lambda i: (0, i, 0))],
        out_specs=pl.BlockSpec((tile, 128), lambda i: (i, 0)), out_shape=jax.ShapeDtypeStruct((rws, 128), F32),
        compiler_params=pltpu.CompilerParams(dimension_semantics=("arbitrary",), vmem_limit_bytes=VMEM_LIMIT),
    )(parts)


def _adam_call(name, w, g, m, v):
    def body(w_ref, g_ref, m_ref, v_ref, d_ref, nm_ref, nv_ref):
        d_ref[...], nm_ref[...], nv_ref[...] = _adam(w_ref[...], g_ref[...], m_ref[...], v_ref[...])

    vm = pl.BlockSpec(memory_space=pltpu.VMEM)
    return pl.pallas_call(
        body, name=name, in_specs=[vm] * 4, out_specs=[vm] * 3, out_shape=[jax.ShapeDtypeStruct(w.shape, F32)] * 3,
        compiler_params=pltpu.CompilerParams(vmem_limit_bytes=VMEM_LIMIT),
    )(w, g, m, v)


ROW_SHARDED = ("w_out", "w_xq", "w_xkv", "w_mlp2")
SMALL = ("norm_mix_g", "conv_b", "conv_ln_g", "conv_ln_b", "q_norm_g", "kv_norm_g", "norm_xattn_g", "norm_mem_g", "norm_mlp_g", "final_norm_g")
GATHER_FIRST = ("w_in", "conv_w", "w_conv_out", "w_uq", "w_ukv")
GATHER_LATE = ("w_mla_out", "w_out", "w_xq", "w_xkv", "w_xo", "w_mlp1", "w_mlp2")
REDUCE_EARLY = ("w_mlp1", "w_mlp2", "conv_w", "w_conv_out", "w_mla_out", "w_out", "w_xq", "w_xkv", "w_xo", "w_in")
REDUCE_LAST = ("w_uq", "w_ukv")
ROW_ALIGN = 16


def _padded(k, nn):
    return -(-k // ROW_ALIGN) * ROW_ALIGN, -(-nn // 128) * 128


def _pack_rows(a):
    k, nn = a.shape[-2:]
    kp, np_ = _padded(k, nn)
    if (kp, np_) != (k, nn):
        a = jnp.pad(a, [(0, 0)] * (a.ndim - 2) + [(0, kp - k), (0, np_ - nn)])
    if np_ == 128:
        return a
    lead = a.shape[:-2]
    return jnp.swapaxes(a.reshape(lead + (kp, np_ // 128, 128)), -2, -3).reshape(lead + (kp * np_ // 128, 128))


def _unpack_rows(p, k, nn):
    kp, np_ = _padded(k, nn)
    if np_ != 128:
        lead = p.shape[:-2]
        p = jnp.swapaxes(p.reshape(lead + (np_ // 128, kp, 128)), -2, -3).reshape(lead + (kp, np_))
    return p[..., :k, :nn]


def _pack_group(arrays):
    return jnp.concatenate([_pack_rows(a) for a in arrays], axis=-2)


def _unpack_group(p, shapes):
    out, off = [], 0
    for k, nn in shapes:
        kp, np_ = _padded(k, nn)
        rws = kp * np_ // 128
        out.append(_unpack_rows(p[..., off:off + rws, :], k, nn))
        off += rws
    return out


def _tile_shaped(k, nn):
    return k % ROW_ALIGN == 0 and nn % 128 == 0


def _packed_to_full(name, seg, shard_shape):
    k, nn = shard_shape
    if not _tile_shaped(k, nn):
        stacked = _unpack_rows(seg, k, nn)
        return jnp.transpose(stacked, (1, 0, 2)).reshape(k, N_DEV * nn)
    a = seg.reshape(N_DEV, nn // 128, k, 128)
    if name in ROW_SHARDED:
        return jnp.transpose(a, (0, 2, 1, 3)).reshape(N_DEV * k, nn)
    return jnp.transpose(a, (2, 0, 1, 3)).reshape(k, N_DEV * nn)


def _full_to_packed(name, full, shard_shape):
    k, nn = shard_shape
    if not _tile_shaped(k, nn):
        return _pack_rows(jnp.transpose(full.reshape(k, N_DEV, nn), (1, 0, 2)))
    if name in ROW_SHARDED:
        a = jnp.transpose(full.reshape(N_DEV, k, nn // 128, 128), (0, 2, 1, 3))
    else:
        a = jnp.transpose(full.reshape(k, N_DEV, nn // 128, 128), (1, 2, 0, 3))
    return a.reshape(N_DEV, k * nn // 128, 128)


def _pack_small(vals):
    flat = jnp.concatenate([v.reshape(-1) for v in vals])
    rws = flat.shape[0] // 128
    return jnp.pad(flat, (0, (-(-rws // 8) * 8 - rws) * 128)).reshape(-1, 128)


def _unpack_small(p, sizes):
    flat = p.reshape(-1)
    out, off = [], 0
    for s in sizes:
        out.append(flat[off:off + s])
        off += s
    return out


def _head_pad(w, real, axis):
    shp = list(w.shape)
    shp[axis:axis + 1] = [NH, real]
    w = w.reshape(shp)
    pad = [(0, 0)] * w.ndim
    pad[axis + 1] = (0, HP - real)
    w = jnp.pad(w, pad)
    shp[axis:axis + 2] = [NH * HP]
    return w.reshape(shp)


def _head_unpad(w, lo, real, axis):
    shp = list(w.shape)
    shp[axis:axis + 1] = [NH, HP]
    w = lax.slice_in_dim(w.reshape(shp), lo, lo + real, axis=axis + 1)
    shp[axis:axis + 2] = [NH * real]
    return w.reshape(shp)


def kernel(x, mem, positions, norm_mix_g, w_in, conv_w, conv_b, conv_ln_g, conv_ln_b, w_conv_out, q_norm_g, w_uq, kv_norm_g, w_ukv, w_mla_out, w_out, norm_xattn_g, norm_mem_g, w_xq, w_xkv, w_xo, norm_mlp_g, w_mlp1, w_mlp2, final_norm_g, loss_target, m_norm_mix_g, m_w_in, m_conv_w, m_conv_b, m_conv_ln_g, m_conv_ln_b, m_w_conv_out, m_q_norm_g, m_w_uq, m_kv_norm_g, m_w_ukv, m_w_mla_out, m_w_out, m_norm_xattn_g, m_norm_mem_g, m_w_xq, m_w_xkv, m_w_xo, m_norm_mlp_g, m_w_mlp1, m_w_mlp2, m_final_norm_g, v_norm_mix_g, v_w_in, v_conv_w, v_conv_b, v_conv_ln_g, v_conv_ln_b, v_w_conv_out, v_q_norm_g, v_w_uq, v_kv_norm_g, v_w_ukv, v_w_mla_out, v_w_out, v_norm_xattn_g, v_norm_mem_g, v_w_xq, v_w_xkv, v_w_xo, v_norm_mlp_g, v_w_mlp1, v_w_mlp2, v_final_norm_g):
    args = dict(locals())
    t = x.shape[1]
    n = t // min(ROW_TILE, t)
    nl = t // min(LIGHT_ROW_TILE, t)
    xs, mems, tgt = x[0], mem[0], loss_target[0]

    def pack_shards(prefix, group, dtype):
        return _pack_group([args[prefix + k][0].astype(dtype) for k in group])

    def shapes(group):
        return [args[k].shape[1:] for k in group]

    def unpack_full(gathered, group):
        out, off = {}, 0
        for k, shp in zip(group, shapes(group)):
            kp, np_ = _padded(*shp)
            out[k] = _packed_to_full(k, gathered[:, off:off + kp * np_ // 128], shp)
            off += kp * np_ // 128
        return out

    full = unpack_full(_all_gather_packed(pack_shards("", GATHER_FIRST, BF16)), GATHER_FIRST)

    wi = full["w_in"]
    kr_slot = jnp.pad(wi[:, P_KR:P_KR + ROPE], ((0, 0), (NOPE, HP - NOPE - ROPE)))
    wp = jnp.concatenate([wi[:, :P_KR], kr_slot, wi[:, P_KR + ROPE:]], axis=1)
    cw = jnp.pad(full["conv_w"].astype(F32), ((0, HALO - CW), (0, 0)))
    wuq = _head_pad(full["w_uq"], NOPE + ROPE, 1)
    ukv = full["w_ukv"].reshape(KL, NH, NOPE + VD)
    wk = _head_pad(ukv[:, :, :NOPE].reshape(KL, NH * NOPE), NOPE, 1)
    wv = _head_pad(ukv[:, :, NOPE:].reshape(KL, NH * VD), VD, 1)

    half = ROPE // 2
    lane = jnp.arange(HP)
    first, second = (lane >= NOPE) & (lane < NOPE + half), (lane >= NOPE + half) & (lane < NOPE + ROPE)
    inv_freq = THETA ** (-((lane - NOPE) % half).astype(F32) / half)
    ang = positions[0].astype(F32)[:, None] * inv_freq
    cs, sn = jnp.cos(ang), jnp.sin(ang)
    tc = jnp.where(lane < NOPE, 1.0, jnp.where(first | second, cs, 0.0))
    tsa = jnp.where(first, -sn, 0.0)
    tsb = jnp.where(second, sn, 0.0)

    u0, a, gt, cq, ckv, krp, gl = _in_proj(xs, norm_mix_g, wp, nl)
    zc, conv_out = _conv_branch(a, gt, cw, conv_b, conv_ln_g, conv_ln_b, full["w_conv_out"], n)
    qh, kh, vh, cqn, ckvn = _mla_prep(cq, ckv, krp, tc, tsa, tsb, q_norm_g, wuq, kv_norm_g, wk, wv, nl)
    o, lse, gathered = _flash_fwd(qh, kh, vh, exchange=(pack_shards("", GATHER_LATE, BF16), True))
    full.update(unpack_full(gathered, GATHER_LATE))
    wmo = _head_pad(full["w_mla_out"], VD, 0)
    memn, kx, vx = _mem_kv(mems, norm_mem_g, full["w_xkv"])
    mla, merged, h1, u1, qx, ox, h2, ob = _merge_xattn(o, gl, conv_out, xs, wmo, full["w_out"], norm_xattn_g, full["w_xq"],
                                                        kx, vx, full["w_xo"], n)
    gfin = final_norm_g.reshape(1, D)
    u2, rl2, da1, dh3b, dh2, loss_p, dg_fin, dg_mlp = _mlp_loss(h2, tgt, norm_mlp_g, full["w_mlp1"], full["w_mlp2"], gfin, n)

    (dqx, dh1, dgl, dco, dmla, dob, delta, dh2b, dh1b, dkx, dvx, dg_x) = _merge_xattn_bwd(
        dh2, h1, qx, gl, conv_out, mla, o, full["w_xo"], kx, vx, full["w_xq"], norm_xattn_g, full["w_out"], wmo, n)
    gfull = {"w_mlp1": _dw("dw_mlp1", u2, da1), "w_mlp2": _dw("dw_mlp2", rl2, dh3b)}

    def stacked(group):
        return jnp.concatenate([_full_to_packed(k, gfull[k], args[k].shape[1:]) for k in group], axis=1)

    zs, dzc, dg_lng, dg_lnb, dg_cb = _conv_out_bwd(dco, zc, full["w_conv_out"], conv_ln_g, conv_ln_b, nl)
    dci, dcw = _conv_glu_bwd(dzc, a, gt, cw, n)
    dkv, dg_mem = _mem_bwd(mems, dkx, dvx, norm_mem_g, full["w_xkv"])
    gfull.update({
        "conv_w": dcw[:CW],
        "w_conv_out": _dw("dw_conv_out", zs, dco),
        "w_mla_out": _head_unpad(_dw("dw_mla_out", ob, dmla), 0, VD, 0),
        "w_out": _dw("dw_out", merged, dh1b),
        "w_xq": _dw("dw_xq", u1, dqx),
        "w_xkv": _dw("dw_xkv", memn, dkv),
        "w_xo": _dw("dw_xo", ox, dh2b),
    })
    late_lo, late_hi, wcols = P_Q, P_KR + ROPE, args["w_in"].shape[2]
    dw_conv_in, dw_gates = _dw("dw_in_0", u0, dci), _dw("dw_in_4", u0, dgl)
    gfull["w_in"] = jnp.concatenate([dw_conv_in, jnp.zeros((D, late_hi - late_lo), F32), dw_gates], axis=1)
    dk, dv, dq, parts_early = _flash_bwd(qh, kh, vh, dob, lse, delta, exchange=(stacked(REDUCE_EARLY), False))
    dqp, dkb, dvb, dcq, dckv, dkrp, dg_q, dg_kv = _mla_prep_bwd(dq, dk, dv, cq, ckv, tc, tsa, tsb, q_norm_g, wuq, kv_norm_g, wk, wv, nl)
    pieces = (dci, dcq, dckv, dkrp, dgl)
    dw_cq, dw_ckv, dw_kr = _dw_shared("dw_in_late", u0, [dcq, dckv, dkrp])
    late = jnp.concatenate([dw_cq, dw_ckv, dw_kr[:, NOPE:NOPE + ROPE]], axis=1)
    gfull["w_uq"] = _head_unpad(_dw("dw_uq", cqn, dqp), 0, NOPE + ROPE, 1)
    dw_k, dw_v = _dw_shared("dw_ukv", ckvn, [dkb, dvb])
    gk = _head_unpad(dw_k, 0, NOPE, 1).reshape(KL, NH, NOPE)
    gv = _head_unpad(dw_v, 0, VD, 1).reshape(KL, NH, VD)
    gfull["w_ukv"] = jnp.concatenate([gk, gv], axis=2).reshape(KL, NH * (NOPE + VD))
    owners = [(d, max(late_lo, d * wcols) - d * wcols, max(late_lo, d * wcols) - late_lo,
               min(late_hi, (d + 1) * wcols) - max(late_lo, d * wcols)) for d in range(N_DEV)
              if min(late_hi, (d + 1) * wcols) > max(late_lo, d * wcols)]
    late_w = -(-max(w for _, _, _, w in owners) // 128) * 128
    late_src = _pack_rows(jnp.stack([jnp.pad(late[:, l0:l0 + w], ((0, 0), (0, late_w - w))) for _, _, l0, w in owners]))
    grad_x, dg_mix, parts_last, parts_late = _in_proj_bwd(
        pieces, dh1, xs, wp, norm_mix_g, n, [(stacked(REDUCE_LAST), False), (late_src, tuple(d for d, _, _, _ in owners))])

    gsum = {}
    for name, group, parts in (("grad_sum_early", REDUCE_EARLY, parts_early), ("grad_sum_last", REDUCE_LAST, parts_last)):
        gsum.update(zip(group, _unpack_group(_sum_parts(name, parts), shapes(group))))
    late_sum = _unpack_rows(_sum_parts("grad_sum_late", parts_late), D, late_w)
    me = 4 * lax.axis_index("x") + 2 * lax.axis_index("y") + lax.axis_index("c")
    placed = jnp.zeros((D, wcols), F32)
    for d, s0, _, w in owners:
        placed = jnp.where(me == d, jnp.pad(late_sum[:, :w], ((0, 0), (s0, wcols - s0 - w))), placed)
    gsum["w_in"] = gsum["w_in"] + placed
    big = [{}, {}, {}, {}]
    for k, g in gsum.items():
        res = _adam_call("adam_" + k, args[k][0], g, args["m_" + k][0], args["v_" + k][0])
        for kind, val in enumerate([g] + list(res)):
            big[kind][k] = val[None]

    small_g = {"norm_mix_g": dg_mix, "conv_b": dg_cb, "conv_ln_g": dg_lng, "conv_ln_b": dg_lnb, "q_norm_g": dg_q,
               "kv_norm_g": dg_kv, "norm_xattn_g": dg_x, "norm_mem_g": dg_mem, "norm_mlp_g": dg_mlp, "final_norm_g": dg_fin}
    small_sizes = [int(np.prod(args[k].shape)) for k in SMALL] + [128]
    zero_slot = jnp.zeros((128,), F32)
    small_out = _small_allreduce_adam(
        _pack_small([small_g[k] for k in SMALL] + [jnp.pad(loss_p.reshape(-1), (0, 127))]),
        _pack_small([args[k] for k in SMALL] + [zero_slot]), _pack_small([args["m_" + k] for k in SMALL] + [zero_slot]),
        _pack_small([args["v_" + k] for k in SMALL] + [zero_slot]))

    small = [dict(zip(SMALL, [s.reshape(args[k].shape) for k, s in zip(SMALL, _unpack_small(o_, small_sizes))])) for o_ in small_out]
    loss = _unpack_small(small_out[0], small_sizes)[-1][0]
    order = ("norm_mix_g", "w_in", "conv_w", "conv_b", "conv_ln_g", "conv_ln_b", "w_conv_out", "q_norm_g", "w_uq", "kv_norm_g",
             "w_ukv", "w_mla_out", "w_out", "norm_xattn_g", "norm_mem_g", "w_xq", "w_xkv", "w_xo", "norm_mlp_g", "w_mlp1",
             "w_mlp2", "final_norm_g")
    res = [loss, grad_x[None]]
    for kind in range(4):
        res += [big[kind][k] if k in big[kind] else small[kind][k] for k in order]
    return tuple(res)
```

```python
import functools

import jax
import jax.numpy as jnp
import numpy as np
from jax import lax
from jax.experimental import pallas as pl
from jax.experimental.pallas import tpu as pltpu

F32, BF16 = jnp.float32, jnp.bfloat16
MESH = pl.DeviceIdType.MESH

N_DEV = 8
D = 1024
CC = D // 2
CW = 31
HALO = 32
NH = 8
NOPE, ROPE, VD = D // 16, D // 32, D // 16
QL, KL = 3 * D // 8, D // 4
HP = 128
XH, XD = 4, D // 8
DFF = 4 * D
EPS = 1e-6
THETA = 10000.0
MLA_SCALE = float((NOPE + ROPE) ** -0.5)
X_SCALE = float(XD ** -0.5)
NEG = -1e30
P_A, P_G, P_Q, P_KV, P_KR, P_GL, P_END = 0, CC, 2 * CC, 2 * CC + QL, 2 * CC + QL + KL, 2 * CC + QL + KL + HP, 2 * CC + QL + KL + HP + 2 * D

ADAM_LR, ADAM_B1, ADAM_B2, ADAM_EPS, ADAM_WD, ADAM_STEP = 0.001, 0.9, 0.999, 1e-08, 0.01, 10

ROW_TILE = 256
LIGHT_ROW_TILE = 512
ATT_TILE = 512
HG = 2
HG_FWD = 4
ATT_CHUNK = 32
CONV_ROWS = 128
DW_TILE = 1024
PACK_TILE = 1536
VMEM_LIMIT = 56 * 1024 * 1024


def _mm(a, w):
    return jnp.dot(a.astype(BF16), w, preferred_element_type=F32)


def _mm_nt(a, w):
    return lax.dot_general(a.astype(BF16), w, (((1,), (1,)), ((), ())), preferred_element_type=F32)


def _mm_tn(a, b):
    return lax.dot_general(a.astype(BF16), b.astype(BF16), (((0,), (0,)), ((), ())), preferred_element_type=F32)


def _rms(x, g):
    r = lax.rsqrt(jnp.mean(x * x, axis=-1, keepdims=True) + EPS)
    xh = x * r
    return xh * g, xh, r


def _rms_bwd(dy, xh, r, g):
    dxh = dy * g
    dx = r * (dxh - xh * jnp.mean(dxh * xh, axis=-1, keepdims=True))
    return dx, jnp.sum(dy * xh, axis=0, keepdims=True)


def _col_to_rows(col):
    return jnp.transpose(jnp.broadcast_to(col, (col.shape[0], HP)))[0:8, :]


def _sig(x):
    return 1.0 / (1.0 + jnp.exp(-x))


def _coords():
    return lax.axis_index("x"), lax.axis_index("y"), lax.axis_index("c")


def _exchange_ops(src_ref, mode, dst_ref, send_sems, recv_sems, local_sem):
    x, y, c = _coords()
    me = 4 * x + 2 * y + c
    owned = isinstance(mode, tuple)

    def owner_slot(d):
        return sum(jnp.where(d == o, n, 0) for n, o in enumerate(mode))

    def is_owner(d):
        return functools.reduce(jnp.logical_or, [d == o for o in mode])

    def src(d):
        return src_ref if mode is True else src_ref.at[owner_slot(d) if owned else d]

    local = (pltpu.make_async_copy(src(me), dst_ref.at[me], local_sem), is_owner(me) if owned else True)
    remote = []
    for j in range(1, N_DEV):
        px, py, pc = (1 - x if j & 4 else x), (1 - y if j & 2 else y), (1 - c if j & 1 else c)
        d = 4 * px + 2 * py + pc
        cp = pltpu.make_async_remote_copy(src_ref=src(d), dst_ref=dst_ref.at[me], send_sem=send_sems.at[j - 1],
                                          recv_sem=recv_sems.at[j - 1], device_id=(px, py, pc), device_id_type=MESH)
        remote.append((cp, is_owner(d) if owned else True, is_owner(me) if owned else True))
    return local, remote


def _when(cond, fn):
    if cond is True:
        fn()
    else:
        pl.when(cond)(fn)


def _exchange_hook(first, last, mode, refs):
    @pl.when(first)
    def _():
        (local, here), remote = _exchange_ops(refs[0], mode, *refs[1:])
        _when(here, local.start)
        for cp, sends, _ in remote:
            _when(sends, cp.start)

    @pl.when(last)
    def _():
        (local, here), remote = _exchange_ops(refs[0], mode, *refs[1:])
        for cp, sends, receives in remote:
            _when(sends, cp.wait_send)
            _when(receives, cp.wait_recv)
        _when(here, local.wait)


def _exchange_shapes(exchange):
    if exchange is None:
        return [], [], [], []
    arr, mode = exchange
    shape = (N_DEV,) + (arr.shape if mode is True else arr.shape[1:])
    any_spec = pl.BlockSpec(memory_space=pl.ANY)
    sems = [pltpu.SemaphoreType.DMA((N_DEV - 1,)), pltpu.SemaphoreType.DMA((N_DEV - 1,)), pltpu.SemaphoreType.DMA]
    return [any_spec], [any_spec], [jax.ShapeDtypeStruct(shape, arr.dtype)], sems


def _row_call(name, fn, n, rows, consts, outs, accs=(), scratch=(), exchange=None):
    def row_spec(shape, mode):
        r = shape[-2] // n
        if mode == "cur":
            f = lambda i: i
        elif mode == "prev":
            f = lambda i: jnp.maximum(i - 1, 0)
        else:
            f = lambda i: jnp.minimum(i + 1, n - 1)
        if len(shape) == 2:
            return pl.BlockSpec((r, shape[1]), lambda i: (f(i), 0))
        return pl.BlockSpec((shape[0], r, shape[2]), lambda i: (0, f(i), 0))

    def lane_spec(shape):
        return pl.BlockSpec((shape[0], shape[1], shape[2] // n), lambda i: (0, 0, i))

    def whole_spec(shape, single):
        nd = len(shape)
        if single:
            return pl.BlockSpec(shape, lambda i: (0,) * nd, pipeline_mode=pl.Buffered(1))
        return pl.BlockSpec(shape, lambda i: (0,) * nd)

    nr, nc, no, na, ns = len(rows), len(consts), len(outs), len(accs), len(scratch)
    exchanges = [] if exchange is None else (exchange if isinstance(exchange, list) else [exchange])
    shapes = [_exchange_shapes(e) for e in exchanges]
    x_in, x_out, x_shape, x_sems = (sum((s[part] for s in shapes), []) for part in range(4))
    nx = len(exchanges)

    def body(*refs):
        i = pl.program_id(0)
        row_refs, const_refs = refs[:nr], refs[nr:nr + nc]
        o0 = nr + nc + nx
        out_refs, acc_refs = refs[o0:o0 + no], refs[o0 + no:o0 + no + na]
        s0 = o0 + no + na + nx
        for e in range(nx):
            sems = refs[s0 + ns + 3 * e:s0 + ns + 3 * e + 3]
            _exchange_hook(i == 0, i == n - 1, exchanges[e][1], (refs[nr + nc + e], refs[o0 + no + na + e]) + tuple(sems))
        if na:
            @pl.when(i == 0)
            def _():
                for a in acc_refs:
                    a[...] = jnp.zeros(a.shape, a.dtype)
        fn(i, row_refs, const_refs, out_refs, acc_refs, refs[s0:s0 + ns])

    res = pl.pallas_call(
        body, name=name, grid=(n,),
        in_specs=[row_spec(a.shape, m) for a, m in rows] + [whole_spec(c.shape, True) for c in consts] + x_in,
        out_specs=[lane_spec(o[0]) if len(o) == 3 else row_spec(o[0], "cur") for o in outs]
        + [whole_spec(s, False) for s, _ in accs] + x_out,
        out_shape=[jax.ShapeDtypeStruct(o[0], o[1]) for o in list(outs) + list(accs)] + x_shape,
        scratch_shapes=list(scratch) + x_sems,
        compiler_params=pltpu.CompilerParams(dimension_semantics=("arbitrary",), vmem_limit_bytes=VMEM_LIMIT),
    )(*[a for a, _ in rows], *consts, *[e[0] for e in exchanges])
    return list(res)


def _in_proj(x, g_mix, wp, n):
    t = x.shape[0]

    def fn(i, rows, consts, outs, accs, scr):
        g, w = consts
        u, _, _ = _rms(rows[0][...], g[...])
        ub = u.astype(BF16)
        outs[0][...] = ub
        for k, (lo, hi) in enumerate(((P_A, P_G), (P_G, P_Q), (P_Q, P_KV), (P_KV, P_KR), (P_KR, P_GL), (P_GL, P_END))):
            outs[1 + k][...] = _mm(ub, w[:, lo:hi])

    outs = [((t, D), BF16), ((t, CC), F32), ((t, CC), F32), ((t, QL), F32), ((t, KL), F32), ((t, HP), F32), ((t, 2 * D), F32)]
    return _row_call("in_proj", fn, n, [(x, "cur")], [g_mix, wp], outs)


def _fill_glu_window(i, a, gt, ap, gtp, zz):
    r = a.shape[0]
    zp = ap[r - HALO:, :] * _sig(gtp[r - HALO:, :])
    zz[0:HALO, :] = jnp.where(i > 0, zp, 0.0)
    zz[HALO:, :] = a[...] * _sig(gt[...])


def _shift_copies(buf, sh):
    rows = buf.shape[0]
    for s in range(8):
        sh[s, 0:rows - s, :] = buf[s:rows, :]


def _window(sh, o, base, rb, ln):
    return sh[o % 8, base + o - o % 8:base + o - o % 8 + rb, ln]


def _windows(buf, base, ln, rb, offsets):
    for s in range(8):
        group = [o for o in offsets if o % 8 == s]
        if group:
            shifted = buf[base + s:base + max(group) + rb, ln]
            for o in group:
                yield o, shifted[o - s:o - s + rb]


def _conv_branch(a, gt, cw, cb, lng, lnb, wco, n):
    t = a.shape[0]
    r = t // n

    def fn(i, rows, consts, outs, accs, scr):
        w, b, lg, lb, wo = consts
        zz, zsh = scr
        _fill_glu_window(i, rows[0], rows[1], rows[2], rows[3], zz)
        _shift_copies(zz, zsh)
        rb = min(CONV_ROWS, r)
        for c in range(CC // HP):
            ln = slice(c * HP, (c + 1) * HP)
            for base in range(0, r, rb):
                acc = jnp.zeros((rb, HP), F32)
                for j in range(CW):
                    acc = acc + w[j:j + 1, ln] * _window(zsh, HALO - (CW - 1) + j, base, rb, ln)
                outs[0][base:base + rb, ln] = acc + b[:, ln]
        zc = outs[0][...]
        mu = jnp.mean(zc, axis=-1, keepdims=True)
        dlt = zc - mu
        rs = lax.rsqrt(jnp.mean(dlt * dlt, axis=-1, keepdims=True) + EPS)
        zn = dlt * rs * lg[...] + lb[...]
        outs[1][...] = _mm(zn * _sig(zn), wo[...])

    return _row_call("conv_branch", fn, n, [(a, "cur"), (gt, "cur"), (a, "prev"), (gt, "prev")],
                     [cw, cb, lng, lnb, wco], [((t, CC), F32), ((t, D), F32)],
                     scratch=[pltpu.VMEM((r + HALO, CC), F32), pltpu.VMEM((8, r + HALO, CC), F32)])


def _rope(v, c, sa, sb):
    return v * c + pltpu.roll(v, HP - ROPE // 2, 1) * sa + pltpu.roll(v, ROPE // 2, 1) * sb


def _rope_bwd(dv, c, sa, sb):
    return dv * c + pltpu.roll(dv * sa, ROPE // 2, 1) + pltpu.roll(dv * sb, HP - ROPE // 2, 1)


def _mla_prep(cq, ckv, krp, tc, tsa, tsb, gq, wuq, gkv, wk, wv, n):
    t = cq.shape[0]

    def fn(i, rows, consts, outs, accs, scr):
        g_q, w_q, g_kv, w_k, w_v = consts
        c, sa, sb = rows[3][...], rows[4][...], rows[5][...]
        cqn = _rms(rows[0][...], g_q[...])[0].astype(BF16)
        ckvn = _rms(rows[1][...], g_kv[...])[0].astype(BF16)
        outs[3][...] = cqn
        outs[4][...] = ckvn
        krr = _rope(rows[2][...], c, sa, sb)
        for h in range(NH):
            ln = slice(h * HP, (h + 1) * HP)
            outs[0][:, ln] = (_rope(_mm(cqn, w_q[:, ln]), c, sa, sb) * MLA_SCALE).astype(BF16)
            outs[1][:, ln] = (_mm(ckvn, w_k[:, ln]) + krr).astype(BF16)
        vv = _mm(ckvn, w_v[...])
        lane = lax.broadcasted_iota(jnp.int32, vv.shape, 1)
        outs[2][...] = jnp.where((lane & (HP - 1)) == VD, 1.0, vv).astype(BF16)

    outs = [((t, NH * HP), BF16)] * 3 + [((t, QL), BF16), ((t, KL), BF16)]
    return _row_call("mla_prep", fn, n, [(a, "cur") for a in (cq, ckv, krp, tc, tsa, tsb)], [gq, wuq, gkv, wk, wv], outs)


def _chunk_mask(c, tq, transposed, rows=ATT_CHUNK):
    row = lax.broadcasted_iota(jnp.int32, (rows, tq), 0) + c * rows
    col = lax.broadcasted_iota(jnp.int32, (rows, tq), 1)
    return (row <= col) if transposed else (col <= row)


def _flash_hook(exchange, nh, nq, refs):
    if exchange is not None:
        h, i = pl.program_id(0), pl.program_id(1)
        _exchange_hook((h == 0) & (i == 0), (h == nh - 1) & (i == nq - 1), exchange[1], refs)


def _head_lanes(g):
    return slice(g * HP, (g + 1) * HP)


def _flash_fwd(q, k, v, exchange=None):
    t = q.shape[0]
    tq = min(ATT_TILE, t)
    nq = t // tq
    x_in, x_out, x_shape, x_sems = _exchange_shapes(exchange)

    def body(q_ref, k_ref, v_ref, *rest):
        o_ref, lse_ref = rest[len(x_in):len(x_in) + 2]
        _flash_hook(exchange, NH // HG_FWD, nq, rest[:len(x_in)] + rest[len(x_in) + 2:])
        i = pl.program_id(1)

        def step(j, carry, masked):
            at = pl.ds(pl.multiple_of(j * tq, tq), tq)
            out = []
            for g in range(HG_FWD):
                m, acc = carry[g]
                s = _mm_nt(q_ref[:, _head_lanes(g)], k_ref[at, _head_lanes(g)])
                if masked:
                    s = jnp.where(_chunk_mask(0, tq, False, tq), s, NEG)
                m_new = jnp.maximum(m, jnp.max(s, axis=-1, keepdims=True))
                out.append((m_new, jnp.exp(m - m_new) * acc + _mm(jnp.exp(s - m_new), v_ref[at, _head_lanes(g)])))
            return tuple(out)

        init = tuple((jnp.full((tq, 1), NEG, F32), jnp.zeros((tq, HP), F32)) for _ in range(HG_FWD))
        carry = lax.fori_loop(0, i, lambda j, c: step(j, c, False), init)
        for g, (m, acc) in enumerate(step(i, carry, True)):
            lane = lax.broadcasted_iota(jnp.int32, acc.shape, 1)
            l = jnp.sum(jnp.where(lane == VD, acc, 0.0), axis=-1, keepdims=True)
            o_ref[:, _head_lanes(g)] = acc / l
            lse_ref[g] = _col_to_rows(m + jnp.log(l))

    wide = HG_FWD * HP
    return pl.pallas_call(
        body, name="flash_fwd", grid=(NH // HG_FWD, nq),
        in_specs=[pl.BlockSpec((tq, wide), lambda h, i: (i, h)), pl.BlockSpec((t, wide), lambda h, i: (0, h)),
                  pl.BlockSpec((t, wide), lambda h, i: (0, h))] + x_in,
        out_specs=[pl.BlockSpec((tq, wide), lambda h, i: (i, h)), pl.BlockSpec((HG_FWD, 8, tq), lambda h, i: (h, 0, i))] + x_out,
        out_shape=[jax.ShapeDtypeStruct((t, NH * HP), F32), jax.ShapeDtypeStruct((NH, 8, t), F32)] + x_shape,
        scratch_shapes=x_sems,
        compiler_params=pltpu.CompilerParams(dimension_semantics=("arbitrary", "arbitrary"), vmem_limit_bytes=VMEM_LIMIT),
    )(q, k, v, *([exchange[0]] if x_in else []))


def _flash_bwd(q, k, v, do, lse_row, delta_row, exchange=None):
    t = q.shape[0]
    tq = min(ATT_TILE, t)
    nq = t // tq
    x_in, x_out, x_shape, x_sems = _exchange_shapes(exchange)

    def body(q_ref, k_ref, v_ref, do_ref, lse_ref, dl_ref, *rest):
        dk_ref, dv_ref, dq_ref = rest[len(x_in):len(x_in) + 3]
        st_scr, dpt_scr, pt_scr, dst_scr = rest[len(rest) - 4:]
        _flash_hook(exchange, NH // HG, nq, rest[:len(x_in)] + rest[len(x_in) + 3:len(rest) - 4])
        j = pl.program_id(1)
        dk_ref[...] = jnp.zeros(dk_ref.shape, F32)
        dv_ref[...] = jnp.zeros(dv_ref.shape, F32)

        @pl.when(j == 0)
        def _():
            dq_ref[...] = jnp.zeros(dq_ref.shape, F32)

        def step(i, masked):
            at = pl.ds(pl.multiple_of(i * tq, tq), tq)
            for g in range(HG):
                st_scr[g] = _mm_nt(k_ref[:, _head_lanes(g)], q_ref[at, _head_lanes(g)])
                dpt_scr[g] = _mm_nt(v_ref[:, _head_lanes(g)], do_ref[at, _head_lanes(g)])
            for g in range(HG):
                lse_i, dl_i = lse_ref[g, 0:1, at], dl_ref[g, 0:1, at]
                for c in range(tq // ATT_CHUNK):
                    rows = slice(c * ATT_CHUNK, (c + 1) * ATT_CHUNK)
                    st = st_scr[g, rows, :]
                    if masked:
                        st = jnp.where(_chunk_mask(c, tq, True), st, NEG)
                    pt = jnp.exp(st - lse_i)
                    pt_scr[g, rows, :] = pt.astype(BF16)
                    dst_scr[g, rows, :] = (pt * (dpt_scr[g, rows, :] - dl_i)).astype(BF16)
            for g in range(HG):
                dv_ref[:, _head_lanes(g)] += _mm(pt_scr[g], do_ref[at, _head_lanes(g)])
                dk_ref[:, _head_lanes(g)] += _mm(dst_scr[g], q_ref[at, _head_lanes(g)])
                dq_ref[at, _head_lanes(g)] += _mm_tn(dst_scr[g], k_ref[:, _head_lanes(g)])

        step(j, True)
        lax.fori_loop(j + 1, nq, lambda i, c: (step(i, False), c)[1], 0)

    wide = HG * HP
    blk = pl.BlockSpec((tq, wide), lambda h, j: (j, h))
    whole = pl.BlockSpec((t, wide), lambda h, j: (0, h))
    row = pl.BlockSpec((HG, 8, t), lambda h, j: (h, 0, 0))
    x_sems = x_sems + [pltpu.VMEM((HG, tq, tq), F32), pltpu.VMEM((HG, tq, tq), F32), pltpu.VMEM((HG, tq, tq), BF16),
                       pltpu.VMEM((HG, tq, tq), BF16)]
    return pl.pallas_call(
        body, name="flash_bwd", grid=(NH // HG, nq), in_specs=[whole, blk, blk, whole, row, row] + x_in,
        out_specs=[blk, blk, whole] + x_out, out_shape=[jax.ShapeDtypeStruct((t, NH * HP), F32)] * 3 + x_shape, scratch_shapes=x_sems,
        compiler_params=pltpu.CompilerParams(dimension_semantics=("arbitrary", "arbitrary"), vmem_limit_bytes=VMEM_LIMIT),
    )(q, k, v, do, lse_row, delta_row, *([exchange[0]] if x_in else []))


def _mem_kv(mem, g_mem, wxkv):
    m = mem.shape[0]

    def fn(i, rows, consts, outs, accs, scr):
        mn = _rms(rows[0][...], consts[0][...])[0].astype(BF16)
        outs[0][...] = mn
        outs[1][...] = _mm(mn, consts[1][:, 0:XH * XD]).astype(BF16)
        outs[2][...] = _mm(mn, consts[1][:, XH * XD:]).astype(BF16)

    return _row_call("mem_kv", fn, 1, [(mem, "cur")], [g_mem, wxkv], [((m, D), BF16), ((m, XH * XD), BF16), ((m, XH * XD), BF16)])


def _merge_xattn(o, gl, conv_out, x, wmo, wo, g_x, wxq, kx, vx, wxo, n):
    t = x.shape[0]

    def fn(i, rows, consts, outs, accs, scr):
        w_mo, w_o, g, w_xq, k_x, v_x, w_xo = consts
        ob = rows[0][...].astype(BF16)
        outs[7][...] = ob
        mla = _mm(ob, w_mo[...])
        outs[0][...] = mla
        merged = (_sig(rows[1][:, 0:D]) * rows[2][...] + _sig(rows[1][:, D:]) * mla).astype(BF16)
        outs[1][...] = merged
        h1 = rows[3][...] + _mm(merged, w_o[...])
        outs[2][...] = h1
        u1 = _rms(h1, g[...])[0].astype(BF16)
        outs[3][...] = u1
        qx = (_mm(u1, w_xq[...]) * X_SCALE).astype(BF16)
        outs[4][...] = qx
        for h in range(XH):
            ln = slice(h * XD, (h + 1) * XD)
            s = _mm_nt(qx[:, ln], k_x[:, ln])
            e = jnp.exp(s - jnp.max(s, axis=-1, keepdims=True))
            p = e / jnp.sum(e, axis=-1, keepdims=True)
            outs[5][:, ln] = _mm(p, v_x[:, ln]).astype(BF16)
        outs[6][...] = h1 + _mm(outs[5][...], w_xo[...])

    outs = [((t, D), F32), ((t, D), BF16), ((t, D), F32), ((t, D), BF16), ((t, XH * XD), BF16), ((t, XH * XD), BF16),
            ((t, D), F32), ((t, NH * HP), BF16)]
    return _row_call("merge_xattn", fn, n, [(a, "cur") for a in (o, gl, conv_out, x)], [wmo, wo, g_x, wxq, kx, vx, wxo], outs)


def _mlp_loss(h2, target, g_mlp, w1, w2, g_fin, n):
    t = h2.shape[0]
    nck = DFF // D

    def fn(i, rows, consts, outs, accs, scr):
        g_m, w_1, w_2, g_f = consts
        h = rows[0][...]
        u2, xh2, r2 = _rms(h, g_m[...])
        ub = u2.astype(BF16)
        outs[0][...] = ub
        h3 = h
        a1 = []
        for c in range(nck):
            ck = slice(c * D, (c + 1) * D)
            a = _mm(ub, w_1[:, ck])
            a1.append(a)
            rl = jnp.maximum(a, 0.0)
            rb = (rl * rl).astype(BF16)
            outs[1][:, ck] = rb
            h3 = h3 + _mm(rb, w_2[ck, :])
        y, xh3, r3 = _rms(h3, g_f[...])
        err = y - rows[1][...]
        accs[0][...] += jnp.sum(jnp.sum(err * err, axis=1, keepdims=True), axis=0, keepdims=True) * (0.5 / D)
        dh3, dgf = _rms_bwd(err * (1.0 / D), xh3, r3, g_f[...])
        accs[1][...] += dgf
        db = dh3.astype(BF16)
        outs[3][...] = db
        du2 = jnp.zeros_like(h)
        for c in range(nck):
            ck = slice(c * D, (c + 1) * D)
            da = (_mm_nt(db, w_2[ck, :]) * (2.0 * jnp.maximum(a1[c], 0.0))).astype(BF16)
            outs[2][:, ck] = da
            du2 = du2 + _mm_nt(da, w_1[:, ck])
        dx2, dgm = _rms_bwd(du2, xh2, r2, g_m[...])
        accs[2][...] += dgm
        outs[4][...] = dh3 + dx2

    outs = [((t, D), BF16), ((t, DFF), BF16), ((t, DFF), BF16), ((t, D), BF16), ((t, D), F32)]
    accs = [((1, 1), F32), ((1, D), F32), ((1, D), F32)]
    return _row_call("mlp_loss", fn, n, [(h2, "cur"), (target, "cur")], [g_mlp, w1, w2, g_fin], outs, accs)


def _merge_xattn_bwd(dh2, h1, qx, gl, conv_out, mla, o, wxo, kx, vx, wxq, g_x, wo, wmo, n):
    t = dh2.shape[0]
    m = kx.shape[0]

    def fn(i, rows, consts, outs, accs, scr):
        w_xo, k_x, v_x, w_xq, g, w_o, w_mo = consts
        d2 = rows[0][...]
        d2b = d2.astype(BF16)
        outs[7][...] = d2b
        dox = _mm_nt(d2b, w_xo[...]).astype(BF16)
        q = rows[2][...]
        dq = []
        for h in range(XH):
            ln = slice(h * XD, (h + 1) * XD)
            qh, kh, vh, doh = q[:, ln], k_x[:, ln], v_x[:, ln], dox[:, ln]
            s = _mm_nt(qh, kh)
            e = jnp.exp(s - jnp.max(s, axis=-1, keepdims=True))
            p = e / jnp.sum(e, axis=-1, keepdims=True)
            dp = _mm_nt(doh, vh)
            ds = p * (dp - jnp.sum(p * dp, axis=-1, keepdims=True))
            dq.append(_mm(ds, kh) * X_SCALE)
            accs[0][:, ln] += _mm_tn(ds, qh)
            accs[1][:, ln] += _mm_tn(p, doh)
        dqx = jnp.concatenate(dq, axis=1).astype(BF16)
        outs[0][...] = dqx
        _, xh1, r1 = _rms(rows[1][...], g[...])
        dx1, dg = _rms_bwd(_mm_nt(dqx, w_xq[...]), xh1, r1, g[...])
        accs[2][...] += dg
        d1 = d2 + dx1
        outs[1][...] = d1
        d1b = d1.astype(BF16)
        outs[8][...] = d1b
        dm = _mm_nt(d1b, w_o[...])
        g0, g1 = _sig(rows[3][:, 0:D]), _sig(rows[3][:, D:])
        outs[2][:, 0:D] = (dm * rows[4][...] * g0 * (1.0 - g0)).astype(BF16)
        outs[2][:, D:] = (dm * rows[5][...] * g1 * (1.0 - g1)).astype(BF16)
        outs[3][...] = (dm * g0).astype(BF16)
        dmla = (dm * g1).astype(BF16)
        outs[4][...] = dmla
        do = _mm_nt(dmla, w_mo[...])
        outs[5][...] = do.astype(BF16)
        prod = do * rows[6][...]
        for h in range(NH):
            outs[6][h] = _col_to_rows(jnp.sum(prod[:, h * HP:(h + 1) * HP], axis=-1, keepdims=True))

    outs = [((t, XH * XD), BF16), ((t, D), F32), ((t, 2 * D), BF16), ((t, D), BF16), ((t, D), BF16), ((t, NH * HP), BF16),
            ((NH, 8, t), F32, "lanes"), ((t, D), BF16), ((t, D), BF16)]
    accs = [((m, XH * XD), F32), ((m, XH * XD), F32), ((1, D), F32)]
    return _row_call("merge_xattn_bwd", fn, n, [(a, "cur") for a in (dh2, h1, qx, gl, conv_out, mla, o)],
                     [wxo, kx, vx, wxq, g_x, wo, wmo], outs, accs)


def _mla_prep_bwd(dq, dk, dv, cq, ckv, tc, tsa, tsb, gq, wuq, gkv, wk, wv, n):
    t = dq.shape[0]

    def fn(i, rows, consts, outs, accs, scr):
        g_q, w_q, g_kv, w_k, w_v = consts
        c, sa, sb = rows[5][...], rows[6][...], rows[7][...]
        dkr = jnp.zeros((rows[0].shape[0], HP), F32)
        for h in range(NH):
            ln = slice(h * HP, (h + 1) * HP)
            outs[0][:, ln] = _rope_bwd(rows[0][:, ln] * MLA_SCALE, c, sa, sb).astype(BF16)
            dkr = dkr + rows[1][:, ln]
        lane = lax.broadcasted_iota(jnp.int32, dkr.shape, 1)
        outs[5][...] = jnp.where((lane >= NOPE) & (lane < NOPE + ROPE), _rope_bwd(dkr, c, sa, sb), 0.0).astype(BF16)
        dkb, dvb = rows[1][...].astype(BF16), rows[2][...].astype(BF16)
        outs[1][...] = dkb
        outs[2][...] = dvb
        _, xq, rq = _rms(rows[3][...], g_q[...])
        dcq, dgq = _rms_bwd(_mm_nt(outs[0][...], w_q[...]), xq, rq, g_q[...])
        outs[3][...] = dcq.astype(BF16)
        accs[0][...] += dgq
        _, xk, rk = _rms(rows[4][...], g_kv[...])
        dckv, dgk = _rms_bwd(_mm_nt(dkb, w_k[...]) + _mm_nt(dvb, w_v[...]), xk, rk, g_kv[...])
        outs[4][...] = dckv.astype(BF16)
        accs[1][...] += dgk

    outs = [((t, NH * HP), BF16)] * 3 + [((t, QL), BF16), ((t, KL), BF16), ((t, HP), BF16)]
    return _row_call("mla_prep_bwd", fn, n, [(a, "cur") for a in (dq, dk, dv, cq, ckv, tc, tsa, tsb)],
                     [gq, wuq, gkv, wk, wv], outs, [((1, QL), F32), ((1, KL), F32)])


def _conv_out_bwd(dco, zc, wco, lng, lnb, n):
    t = zc.shape[0]

    def fn(i, rows, consts, outs, accs, scr):
        wo, lg, lb = consts
        z = rows[1][...]
        mu = jnp.mean(z, axis=-1, keepdims=True)
        dlt = z - mu
        rs = lax.rsqrt(jnp.mean(dlt * dlt, axis=-1, keepdims=True) + EPS)
        xh = dlt * rs
        zn = xh * lg[...] + lb[...]
        sg = _sig(zn)
        outs[0][...] = (zn * sg).astype(BF16)
        dzn = _mm_nt(rows[0][...], wo[...]) * (sg * (1.0 + zn * (1.0 - sg)))
        accs[0][...] += jnp.sum(dzn * xh, axis=0, keepdims=True)
        accs[1][...] += jnp.sum(dzn, axis=0, keepdims=True)
        dxh = dzn * lg[...]
        dzc = rs * (dxh - jnp.mean(dxh, axis=-1, keepdims=True) - xh * jnp.mean(dxh * xh, axis=-1, keepdims=True))
        outs[1][...] = dzc
        accs[2][...] += jnp.sum(dzc, axis=0, keepdims=True)

    return _row_call("conv_out_bwd", fn, n, [(dco, "cur"), (zc, "cur")], [wco, lng, lnb],
                     [((t, CC), BF16), ((t, CC), F32)], [((1, CC), F32)] * 3)


def _conv_glu_bwd(dzc, a, gt, cw, n):
    t = a.shape[0]
    r = t // n

    def fn(i, rows, consts, outs, accs, scr):
        w = consts[0]
        zz, dd = scr
        _fill_glu_window(i, rows[2], rows[3], rows[4], rows[5], zz)
        dd[0:r, :] = rows[0][...]
        dd[r:, :] = jnp.where(i < n - 1, rows[1][0:HALO, :], 0.0)
        rb = min(CONV_ROWS, r)
        for c in range(CC // HP):
            ln = slice(c * HP, (c + 1) * HP)
            for base in range(0, r, rb):
                here = slice(base, base + rb)
                dcur = dd[here, ln]
                for o, win in _windows(zz, base, ln, rb, [HALO - (CW - 1) + j for j in range(CW)]):
                    j = o - (HALO - (CW - 1))
                    accs[0][j:j + 1, ln] += jnp.sum(dcur * win, axis=0, keepdims=True)
                acc = jnp.zeros((rb, HP), F32)
                for o, win in _windows(dd, base, ln, rb, [CW - 1 - j for j in range(CW)]):
                    j = CW - 1 - o
                    acc = acc + w[j:j + 1, ln] * win
                sg = _sig(rows[3][here, ln])
                outs[0][here, ln] = (acc * sg).astype(BF16)
                outs[0][here, CC + c * HP:CC + (c + 1) * HP] = (acc * rows[2][here, ln] * sg * (1.0 - sg)).astype(BF16)

    return _row_call("conv_glu_bwd", fn, n, [(dzc, "cur"), (dzc, "next"), (a, "cur"), (gt, "cur"), (a, "prev"), (gt, "prev")],
                     [cw], [((t, 2 * CC), BF16)], [((HALO, CC), F32)],
                     scratch=[pltpu.VMEM((r + HALO, CC), F32), pltpu.VMEM((r + HALO, CC), F32)])


def _in_proj_bwd(pieces, dh1, x, wp, g_mix, n, exchange):
    t = x.shape[0]
    offs = (P_A, P_Q, P_KV, P_KR, P_GL, P_END)

    def fn(i, rows, consts, outs, accs, scr):
        w, g = consts
        du = jnp.zeros((rows[0].shape[0], D), F32)
        for k in range(5):
            du = du + _mm_nt(rows[k][...], w[:, offs[k]:offs[k + 1]])
        _, xh, r = _rms(rows[6][...], g[...])
        dx, dg = _rms_bwd(du, xh, r, g[...])
        accs[0][...] += dg
        outs[0][...] = rows[5][...] + dx

    return _row_call("in_proj_bwd", fn, n, [(a, "cur") for a in list(pieces) + [dh1, x]], [wp, g_mix],
                     [((t, D), F32)], [((1, D), F32)], exchange=exchange)


def _mem_bwd(mem, dkx, dvx, g_mem, wxkv):
    m = mem.shape[0]

    def fn(i, rows, consts, outs, accs, scr):
        g, w = consts
        dkv = jnp.concatenate([rows[1][...], rows[2][...]], axis=1).astype(BF16)
        outs[0][...] = dkv
        _, xh, _ = _rms(rows[0][...], g[...])
        accs[0][...] += jnp.sum(_mm_nt(dkv, w[...]) * xh, axis=0, keepdims=True)

    return _row_call("mem_bwd", fn, 1, [(mem, "cur"), (dkx, "cur"), (dvx, "cur")], [g_mem, wxkv],
                     [((m, 2 * XH * XD), BF16)], [((1, D), F32)])


def _dw(name, xs, dy):
    t, k = xs.shape
    nn = dy.shape[1]
    tk, tn, tt = min(k, 1024), min(nn, 1024), min(t, DW_TILE)

    def body(x_ref, dy_ref, o_ref):
        @pl.when(pl.program_id(2) == 0)
        def _():
            o_ref[...] = jnp.zeros(o_ref.shape, F32)
        o_ref[...] += lax.dot_general(x_ref[...], dy_ref[...], (((0,), (0,)), ((), ())), preferred_element_type=F32)

    return pl.pallas_call(
        body, name=name, grid=(k // tk, nn // tn, t // tt),
        in_specs=[pl.BlockSpec((tt, tk), lambda a, b, c: (c, a)), pl.BlockSpec((tt, tn), lambda a, b, c: (c, b))],
        out_specs=pl.BlockSpec((tk, tn), lambda a, b, c: (a, b)),
        out_shape=jax.ShapeDtypeStruct((k, nn), F32),
        compiler_params=pltpu.CompilerParams(dimension_semantics=("arbitrary", "arbitrary", "arbitrary"), vmem_limit_bytes=VMEM_LIMIT),
    )(xs, dy)


def _dw_shared(name, xs, dys):
    t, k = xs.shape
    tt = min(t, DW_TILE)
    nd = len(dys)

    def body(x_ref, *refs):
        @pl.when(pl.program_id(0) == 0)
        def _():
            for o_ref in refs[nd:]:
                o_ref[...] = jnp.zeros(o_ref.shape, F32)
        xb = x_ref[...]
        for dy_ref, o_ref in zip(refs[:nd], refs[nd:]):
            o_ref[...] += lax.dot_general(xb, dy_ref[...], (((0,), (0,)), ((), ())), preferred_element_type=F32)

    return pl.pallas_call(
        body, name=name, grid=(t // tt,),
        in_specs=[pl.BlockSpec((tt, k), lambda c: (c, 0))] + [pl.BlockSpec((tt, d.shape[1]), lambda c: (c, 0)) for d in dys],
        out_specs=[pl.BlockSpec((k, d.shape[1]), lambda c: (0, 0)) for d in dys],
        out_shape=[jax.ShapeDtypeStruct((k, d.shape[1]), F32) for d in dys],
        compiler_params=pltpu.CompilerParams(dimension_semantics=("arbitrary",), vmem_limit_bytes=VMEM_LIMIT),
    )(xs, *dys)


def _all_gather_packed(shard):
    rws = shard.shape[0]

    def body(x_ref, out_ref, send_sems, recv_sems, local_sem):
        x, y, c = _coords()
        me, sibling = (x, y, c), (x, y, 1 - c)
        chips = [(1 - x, y), (x, 1 - y), (1 - x, 1 - y)]

        def slot(px, py, pc):
            return out_ref.at[4 * px + 2 * py + pc]

        def copy(k, block, to, src=None):
            return pltpu.make_async_remote_copy(
                src_ref=slot(*block) if src is None else src, dst_ref=slot(*block),
                send_sem=send_sems.at[k], recv_sem=recv_sems.at[k], device_id=to, device_id_type=MESH)

        mine = pltpu.make_async_copy(x_ref, slot(*me), local_sem)
        mine.start()
        first = [copy(0, me, sibling, src=x_ref)] + [copy(1 + j, me, (*chip, c), src=x_ref) for j, chip in enumerate(chips)]
        for cp in first:
            cp.start()
        passed = [copy(4 + j, (*chip, c), sibling) for j, chip in enumerate(chips)]
        for j, chip in enumerate(chips):
            copy(1 + j, (*chip, c), me).wait_recv()
            passed[j].start()
        copy(0, sibling, me).wait_recv()
        for j, chip in enumerate(chips):
            copy(4 + j, (*chip, 1 - c), me).wait_recv()
        for cp in first + passed:
            cp.wait_send()
        mine.wait()

    return pl.pallas_call(
        body, name="all_gather_weights",
        out_shape=jax.ShapeDtypeStruct((N_DEV, rws, 128), shard.dtype),
        in_specs=[pl.BlockSpec(memory_space=pl.ANY)], out_specs=pl.BlockSpec(memory_space=pl.ANY),
        scratch_shapes=[pltpu.SemaphoreType.DMA((7,)), pltpu.SemaphoreType.DMA((7,)), pltpu.SemaphoreType.DMA],
    )(shard)


def _adam(w, g, m, v):
    m = ADAM_B1 * m + (1.0 - ADAM_B1) * g
    v = ADAM_B2 * v + (1.0 - ADAM_B2) * (g * g)
    m_hat = m / (1.0 - ADAM_B1 ** ADAM_STEP)
    v_hat = v / (1.0 - ADAM_B2 ** ADAM_STEP)
    return -ADAM_LR * (m_hat / (jnp.sqrt(v_hat) + ADAM_EPS) + ADAM_WD * w), m, v


def _small_allreduce_adam(part, w, m, v):
    shape = part.shape

    def body(p_ref, w_ref, m_ref, v_ref, g_ref, d_ref, nm_ref, nv_ref, buf, send_sems, recv_sems):
        x, y, c = _coords()
        me = 4 * x + 2 * y + c
        buf[0] = p_ref[...]
        cps = []
        for j in range(1, N_DEV):
            jx, jy, jc = j >> 2, (j >> 1) & 1, j & 1
            peer = (1 - x if jx else x, 1 - y if jy else y, 1 - c if jc else c)
            cps.append(pltpu.make_async_remote_copy(src_ref=p_ref, dst_ref=buf.at[j], send_sem=send_sems.at[j - 1],
                                                    recv_sem=recv_sems.at[j - 1], device_id=peer, device_id_type=MESH))
        for cp in cps:
            cp.start()
        for cp in cps:
            cp.wait()
        g = buf[me]
        for d in range(1, N_DEV):
            g = g + buf[d ^ me]
        g_ref[...] = g
        d_ref[...], nm_ref[...], nv_ref[...] = _adam(w_ref[...], g, m_ref[...], v_ref[...])

    vm = pl.BlockSpec(memory_space=pltpu.VMEM)
    return pl.pallas_call(
        body, name="small_allreduce_adam", out_shape=[jax.ShapeDtypeStruct(shape, F32)] * 4,
        in_specs=[vm] * 4, out_specs=[vm] * 4,
        scratch_shapes=[pltpu.VMEM((N_DEV,) + shape, F32), pltpu.SemaphoreType.DMA((7,)), pltpu.SemaphoreType.DMA((7,))],
    )(part, w, m, v)


def _sum_parts(name, parts):
    rws = parts.shape[1]
    tile = max(d for d in range(16, PACK_TILE + 1, 16) if rws % d == 0)

    def body(p_ref, g_ref):
        g = p_ref[0]
        for d in range(1, N_DEV):
            g = g + p_ref[d]
        g_ref[...] = g

    return pl.pallas_call(
        body, name=name, grid=(rws // tile,), in_specs=[pl.BlockSpec((N_DEV, tile, 128), lambda i: (0, i, 0))],
        out_specs=pl.BlockSpec((tile, 128), lambda i: (i, 0)), out_shape=jax.ShapeDtypeStruct((rws, 128), F32),
        compiler_params=pltpu.CompilerParams(dimension_semantics=("arbitrary",), vmem_limit_bytes=VMEM_LIMIT),
    )(parts)


def _adam_call(name, w, g, m, v):
    def body(w_ref, g_ref, m_ref, v_ref, d_ref, nm_ref, nv_ref):
        d_ref[...], nm_ref[...], nv_ref[...] = _adam(w_ref[...], g_ref[...], m_ref[...], v_ref[...])

    vm = pl.BlockSpec(memory_space=pltpu.VMEM)
    return pl.pallas_call(
        body, name=name, in_specs=[vm] * 4, out_specs=[vm] * 3, out_shape=[jax.ShapeDtypeStruct(w.shape, F32)] * 3,
        compiler_params=pltpu.CompilerParams(vmem_limit_bytes=VMEM_LIMIT),
    )(w, g, m, v)


ROW_SHARDED = ("w_out", "w_xq", "w_xkv", "w_mlp2")
SMALL = ("norm_mix_g", "conv_b", "conv_ln_g", "conv_ln_b", "q_norm_g", "kv_norm_g", "norm_xattn_g", "norm_mem_g", "norm_mlp_g", "final_norm_g")
GATHER_FIRST = ("w_in", "conv_w", "w_conv_out", "w_uq", "w_ukv")
GATHER_LATE = ("w_mla_out", "w_out", "w_xq", "w_xkv", "w_xo", "w_mlp1", "w_mlp2")
REDUCE_EARLY = ("w_mlp1", "w_mlp2", "conv_w", "w_conv_out", "w_mla_out", "w_out", "w_xq", "w_xkv", "w_xo", "w_in")
REDUCE_LAST = ("w_uq", "w_ukv")
ROW_ALIGN = 16


def _padded(k, nn):
    return -(-k // ROW_ALIGN) * ROW_ALIGN, -(-nn // 128) * 128


def _pack_rows(a):
    k, nn = a.shape[-2:]
    kp, np_ = _padded(k, nn)
    if (kp, np_) != (k, nn):
        a = jnp.pad(a, [(0, 0)] * (a.ndim - 2) + [(0, kp - k), (0, np_ - nn)])
    if np_ == 128:
        return a
    lead = a.shape[:-2]
    return jnp.swapaxes(a.reshape(lead + (kp, np_ // 128, 128)), -2, -3).reshape(lead + (kp * np_ // 128, 128))


def _unpack_rows(p, k, nn):
    kp, np_ = _padded(k, nn)
    if np_ != 128:
        lead = p.shape[:-2]
        p = jnp.swapaxes(p.reshape(lead + (np_ // 128, kp, 128)), -2, -3).reshape(lead + (kp, np_))
    return p[..., :k, :nn]


def _pack_group(arrays):
    return jnp.concatenate([_pack_rows(a) for a in arrays], axis=-2)


def _unpack_group(p, shapes):
    out, off = [], 0
    for k, nn in shapes:
        kp, np_ = _padded(k, nn)
        rws = kp * np_ // 128
        out.append(_unpack_rows(p[..., off:off + rws, :], k, nn))
        off += rws
    return out


def _tile_shaped(k, nn):
    return k % ROW_ALIGN == 0 and nn % 128 == 0


def _packed_to_full(name, seg, shard_shape):
    k, nn = shard_shape
    if not _tile_shaped(k, nn):
        stacked = _unpack_rows(seg, k, nn)
        return jnp.transpose(stacked, (1, 0, 2)).reshape(k, N_DEV * nn)
    a = seg.reshape(N_DEV, nn // 128, k, 128)
    if name in ROW_SHARDED:
        return jnp.transpose(a, (0, 2, 1, 3)).reshape(N_DEV * k, nn)
    return jnp.transpose(a, (2, 0, 1, 3)).reshape(k, N_DEV * nn)


def _full_to_packed(name, full, shard_shape):
    k, nn = shard_shape
    if not _tile_shaped(k, nn):
        return _pack_rows(jnp.transpose(full.reshape(k, N_DEV, nn), (1, 0, 2)))
    if name in ROW_SHARDED:
        a = jnp.transpose(full.reshape(N_DEV, k, nn // 128, 128), (0, 2, 1, 3))
    else:
        a = jnp.transpose(full.reshape(k, N_DEV, nn // 128, 128), (1, 2, 0, 3))
    return a.reshape(N_DEV, k * nn // 128, 128)


def _pack_small(vals):
    flat = jnp.concatenate([v.reshape(-1) for v in vals])
    rws = flat.shape[0] // 128
    return jnp.pad(flat, (0, (-(-rws // 8) * 8 - rws) * 128)).reshape(-1, 128)


def _unpack_small(p, sizes):
    flat = p.reshape(-1)
    out, off = [], 0
    for s in sizes:
        out.append(flat[off:off + s])
        off += s
    return out


def _head_pad(w, real, axis):
    shp = list(w.shape)
    shp[axis:axis + 1] = [NH, real]
    w = w.reshape(shp)
    pad = [(0, 0)] * w.ndim
    pad[axis + 1] = (0, HP - real)
    w = jnp.pad(w, pad)
    shp[axis:axis + 2] = [NH * HP]
    return w.reshape(shp)


def _head_unpad(w, lo, real, axis):
    shp = list(w.shape)
    shp[axis:axis + 1] = [NH, HP]
    w = lax.slice_in_dim(w.reshape(shp), lo, lo + real, axis=axis + 1)
    shp[axis:axis + 2] = [NH * real]
    return w.reshape(shp)


def kernel(x, mem, positions, norm_mix_g, w_in, conv_w, conv_b, conv_ln_g, conv_ln_b, w_conv_out, q_norm_g, w_uq, kv_norm_g, w_ukv, w_mla_out, w_out, norm_xattn_g, norm_mem_g, w_xq, w_xkv, w_xo, norm_mlp_g, w_mlp1, w_mlp2, final_norm_g, loss_target, m_norm_mix_g, m_w_in, m_conv_w, m_conv_b, m_conv_ln_g, m_conv_ln_b, m_w_conv_out, m_q_norm_g, m_w_uq, m_kv_norm_g, m_w_ukv, m_w_mla_out, m_w_out, m_norm_xattn_g, m_norm_mem_g, m_w_xq, m_w_xkv, m_w_xo, m_norm_mlp_g, m_w_mlp1, m_w_mlp2, m_final_norm_g, v_norm_mix_g, v_w_in, v_conv_w, v_conv_b, v_conv_ln_g, v_conv_ln_b, v_w_conv_out, v_q_norm_g, v_w_uq, v_kv_norm_g, v_w_ukv, v_w_mla_out, v_w_out, v_norm_xattn_g, v_norm_mem_g, v_w_xq, v_w_xkv, v_w_xo, v_norm_mlp_g, v_w_mlp1, v_w_mlp2, v_final_norm_g):
    args = dict(locals())
    t = x.shape[1]
    n = t // min(ROW_TILE, t)
    nl = t // min(LIGHT_ROW_TILE, t)
    xs, mems, tgt = x[0], mem[0], loss_target[0]

    def pack_shards(prefix, group, dtype):
        return _pack_group([args[prefix + k][0].astype(dtype) for k in group])

    def shapes(group):
        return [args[k].shape[1:] for k in group]

    def unpack_full(gathered, group):
        out, off = {}, 0
        for k, shp in zip(group, shapes(group)):
            kp, np_ = _padded(*shp)
            out[k] = _packed_to_full(k, gathered[:, off:off + kp * np_ // 128], shp)
            off += kp * np_ // 128
        return out

    full = unpack_full(_all_gather_packed(pack_shards("", GATHER_FIRST, BF16)), GATHER_FIRST)

    wi = full["w_in"]
    kr_slot = jnp.pad(wi[:, P_KR:P_KR + ROPE], ((0, 0), (NOPE, HP - NOPE - ROPE)))
    wp = jnp.concatenate([wi[:, :P_KR], kr_slot, wi[:, P_KR + ROPE:]], axis=1)
    cw = jnp.pad(full["conv_w"].astype(F32), ((0, HALO - CW), (0, 0)))
    wuq = _head_pad(full["w_uq"], NOPE + ROPE, 1)
    ukv = full["w_ukv"].reshape(KL, NH, NOPE + VD)
    wk = _head_pad(ukv[:, :, :NOPE].reshape(KL, NH * NOPE), NOPE, 1)
    wv = _head_pad(ukv[:, :, NOPE:].reshape(KL, NH * VD), VD, 1)

    half = ROPE // 2
    lane = jnp.arange(HP)
    first, second = (lane >= NOPE) & (lane < NOPE + half), (lane >= NOPE + half) & (lane < NOPE + ROPE)
    inv_freq = THETA ** (-((lane - NOPE) % half).astype(F32) / half)
    ang = positions[0].astype(F32)[:, None] * inv_freq
    cs, sn = jnp.cos(ang), jnp.sin(ang)
    tc = jnp.where(lane < NOPE, 1.0, jnp.where(first | second, cs, 0.0))
    tsa = jnp.where(first, -sn, 0.0)
    tsb = jnp.where(second, sn, 0.0)

    u0, a, gt, cq, ckv, krp, gl = _in_proj(xs, norm_mix_g, wp, nl)
    zc, conv_out = _conv_branch(a, gt, cw, conv_b, conv_ln_g, conv_ln_b, full["w_conv_out"], n)
    qh, kh, vh, cqn, ckvn = _mla_prep(cq, ckv, krp, tc, tsa, tsb, q_norm_g, wuq, kv_norm_g, wk, wv, nl)
    o, lse, gathered = _flash_fwd(qh, kh, vh, exchange=(pack_shards("", GATHER_LATE, BF16), True))
    full.update(unpack_full(gathered, GATHER_LATE))
    wmo = _head_pad(full["w_mla_out"], VD, 0)
    memn, kx, vx = _mem_kv(mems, norm_mem_g, full["w_xkv"])
    mla, merged, h1, u1, qx, ox, h2, ob = _merge_xattn(o, gl, conv_out, xs, wmo, full["w_out"], norm_xattn_g, full["w_xq"],
                                                        kx, vx, full["w_xo"], nl)
    gfin = final_norm_g.reshape(1, D)
    u2, rl2, da1, dh3b, dh2, loss_p, dg_fin, dg_mlp = _mlp_loss(h2, tgt, norm_mlp_g, full["w_mlp1"], full["w_mlp2"], gfin, n)

    (dqx, dh1, dgl, dco, dmla, dob, delta, dh2b, dh1b, dkx, dvx, dg_x) = _merge_xattn_bwd(
        dh2, h1, qx, gl, conv_out, mla, o, full["w_xo"], kx, vx, full["w_xq"], norm_xattn_g, full["w_out"], wmo, n)
    gfull = {"w_mlp1": _dw("dw_mlp1", u2, da1), "w_mlp2": _dw("dw_mlp2", rl2, dh3b)}

    def stacked(group):
        return jnp.concatenate([_full_to_packed(k, gfull[k], args[k].shape[1:]) for k in group], axis=1)

    zs, dzc, dg_lng, dg_lnb, dg_cb = _conv_out_bwd(dco, zc, full["w_conv_out"], conv_ln_g, conv_ln_b, nl)
    dci, dcw = _conv_glu_bwd(dzc, a, gt, cw, n)
    dkv, dg_mem = _mem_bwd(mems, dkx, dvx, norm_mem_g, full["w_xkv"])
    gfull.update({
        "conv_w": dcw[:CW],
        "w_conv_out": _dw("dw_conv_out", zs, dco),
        "w_mla_out": _head_unpad(_dw("dw_mla_out", ob, dmla), 0, VD, 0),
        "w_out": _dw("dw_out", merged, dh1b),
        "w_xq": _dw("dw_xq", u1, dqx),
        "w_xkv": _dw("dw_xkv", memn, dkv),
        "w_xo": _dw("dw_xo", ox, dh2b),
    })
    late_lo, late_hi, wcols = P_Q, P_KR + ROPE, args["w_in"].shape[2]
    dw_conv_in, dw_gates = _dw("dw_in_0", u0, dci), _dw("dw_in_4", u0, dgl)
    gfull["w_in"] = jnp.concatenate([dw_conv_in, jnp.zeros((D, late_hi - late_lo), F32), dw_gates], axis=1)
    dk, dv, dq, parts_early = _flash_bwd(qh, kh, vh, dob, lse, delta, exchange=(stacked(REDUCE_EARLY), False))
    dqp, dkb, dvb, dcq, dckv, dkrp, dg_q, dg_kv = _mla_prep_bwd(dq, dk, dv, cq, ckv, tc, tsa, tsb, q_norm_g, wuq, kv_norm_g, wk, wv, nl)
    pieces = (dci, dcq, dckv, dkrp, dgl)
    dw_cq, dw_ckv, dw_kr = _dw_shared("dw_in_late", u0, [dcq, dckv, dkrp])
    late = jnp.concatenate([dw_cq, dw_ckv, dw_kr[:, NOPE:NOPE + ROPE]], axis=1)
    gfull["w_uq"] = _head_unpad(_dw("dw_uq", cqn, dqp), 0, NOPE + ROPE, 1)
    dw_k, dw_v = _dw_shared("dw_ukv", ckvn, [dkb, dvb])
    gk = _head_unpad(dw_k, 0, NOPE, 1).reshape(KL, NH, NOPE)
    gv = _head_unpad(dw_v, 0, VD, 1).reshape(KL, NH, VD)
    gfull["w_ukv"] = jnp.concatenate([gk, gv], axis=2).reshape(KL, NH * (NOPE + VD))
    owners = [(d, max(late_lo, d * wcols) - d * wcols, max(late_lo, d * wcols) - late_lo,
               min(late_hi, (d + 1) * wcols) - max(late_lo, d * wcols)) for d in range(N_DEV)
              if min(late_hi, (d + 1) * wcols) > max(late_lo, d * wcols)]
    late_w = -(-max(w for _, _, _, w in owners) // 128) * 128
    late_src = _pack_rows(jnp.stack([jnp.pad(late[:, l0:l0 + w], ((0, 0), (0, late_w - w))) for _, _, l0, w in owners]))
    grad_x, dg_mix, parts_last, parts_late = _in_proj_bwd(
        pieces, dh1, xs, wp, norm_mix_g, n, [(stacked(REDUCE_LAST), False), (late_src, tuple(d for d, _, _, _ in owners))])

    gsum = {}
    for name, group, parts in (("grad_sum_early", REDUCE_EARLY, parts_early), ("grad_sum_last", REDUCE_LAST, parts_last)):
        gsum.update(zip(group, _unpack_group(_sum_parts(name, parts), shapes(group))))
    late_sum = _unpack_rows(_sum_parts("grad_sum_late", parts_late), D, late_w)
    me = 4 * lax.axis_index("x") + 2 * lax.axis_index("y") + lax.axis_index("c")
    placed = jnp.zeros((D, wcols), F32)
    for d, s0, _, w in owners:
        placed = jnp.where(me == d, jnp.pad(late_sum[:, :w], ((0, 0), (s0, wcols - s0 - w))), placed)
    gsum["w_in"] = gsum["w_in"] + placed
    big = [{}, {}, {}, {}]
    for k, g in gsum.items():
        res = _adam_call("adam_" + k, args[k][0], g, args["m_" + k][0], args["v_" + k][0])
        for kind, val in enumerate([g] + list(res)):
            big[kind][k] = val[None]

    small_g = {"norm_mix_g": dg_mix, "conv_b": dg_cb, "conv_ln_g": dg_lng, "conv_ln_b": dg_lnb, "q_norm_g": dg_q,
               "kv_norm_g": dg_kv, "norm_xattn_g": dg_x, "norm_mem_g": dg_mem, "norm_mlp_g": dg_mlp, "final_norm_g": dg_fin}
    small_sizes = [int(np.prod(args[k].shape)) for k in SMALL] + [128]
    zero_slot = jnp.zeros((128,), F32)
    small_out = _small_allreduce_adam(
        _pack_small([small_g[k] for k in SMALL] + [jnp.pad(loss_p.reshape(-1), (0, 127))]),
        _pack_small([args[k] for k in SMALL] + [zero_slot]), _pack_small([args["m_" + k] for k in SMALL] + [zero_slot]),
        _pack_small([args["v_" + k] for k in SMALL] + [zero_slot]))

    small = [dict(zip(SMALL, [s.reshape(args[k].shape) for k, s in zip(SMALL, _unpack_small(o_, small_sizes))])) for o_ in small_out]
    loss = _unpack_small(small_out[0], small_sizes)[-1][0]
    order = ("norm_mix_g", "w_in", "conv_w", "conv_b", "conv_ln_g", "conv_ln_b", "w_conv_out", "q_norm_g", "w_uq", "kv_norm_g",
             "w_ukv", "w_mla_out", "w_out", "norm_xattn_g", "norm_mem_g", "w_xq", "w_xkv", "w_xo", "norm_mlp_g", "w_mlp1",
             "w_mlp2", "final_norm_g")
    res = [loss, grad_x[None]]
    for kind in range(4):
        res += [big[kind][k] if k in big[kind] else small[kind][k] for k in order]
    return tuple(res)
```

```python
import functools

import jax
import jax.numpy as jnp
import numpy as np
from jax import lax
from jax.experimental import pallas as pl
from jax.experimental.pallas import tpu as pltpu

F32, BF16 = jnp.float32, jnp.bfloat16
MESH = pl.DeviceIdType.MESH

N_DEV = 8
D = 1024
CC = D // 2
CW = 31
HALO = 32
NH = 8
NOPE, ROPE, VD = D // 16, D // 32, D // 16
QL, KL = 3 * D // 8, D // 4
HP = 128
XH, XD = 4, D // 8
DFF = 4 * D
EPS = 1e-6
THETA = 10000.0
MLA_SCALE = float((NOPE + ROPE) ** -0.5)
X_SCALE = float(XD ** -0.5)
NEG = -1e30
P_A, P_G, P_Q, P_KV, P_KR, P_GL, P_END = 0, CC, 2 * CC, 2 * CC + QL, 2 * CC + QL + KL, 2 * CC + QL + KL + HP, 2 * CC + QL + KL + HP + 2 * D

ADAM_LR, ADAM_B1, ADAM_B2, ADAM_EPS, ADAM_WD, ADAM_STEP = 0.001, 0.9, 0.999, 1e-08, 0.01, 10

ROW_TILE = 256
LIGHT_ROW_TILE = 512
ATT_TILE = 512
HG = 2
HG_FWD = 4
ATT_CHUNK = 32
CONV_ROWS = 128
DW_TILE = 2048
PACK_TILE = 1536
VMEM_LIMIT = 56 * 1024 * 1024


def _mm(a, w):
    return jnp.dot(a.astype(BF16), w, preferred_element_type=F32)


def _mm_nt(a, w):
    return lax.dot_general(a.astype(BF16), w, (((1,), (1,)), ((), ())), preferred_element_type=F32)


def _mm_tn(a, b):
    return lax.dot_general(a.astype(BF16), b.astype(BF16), (((0,), (0,)), ((), ())), preferred_element_type=F32)


def _rms(x, g):
    r = lax.rsqrt(jnp.mean(x * x, axis=-1, keepdims=True) + EPS)
    xh = x * r
    return xh * g, xh, r


def _rms_bwd(dy, xh, r, g):
    dxh = dy * g
    dx = r * (dxh - xh * jnp.mean(dxh * xh, axis=-1, keepdims=True))
    return dx, jnp.sum(dy * xh, axis=0, keepdims=True)


def _col_to_rows(col):
    return jnp.transpose(jnp.broadcast_to(col, (col.shape[0], HP)))[0:8, :]


def _sig(x):
    return 1.0 / (1.0 + jnp.exp(-x))


def _coords():
    return lax.axis_index("x"), lax.axis_index("y"), lax.axis_index("c")


def _exchange_ops(src_ref, mode, dst_ref, send_sems, recv_sems, local_sem):
    x, y, c = _coords()
    me = 4 * x + 2 * y + c
    owned = isinstance(mode, tuple)

    def owner_slot(d):
        return sum(jnp.where(d == o, n, 0) for n, o in enumerate(mode))

    def is_owner(d):
        return functools.reduce(jnp.logical_or, [d == o for o in mode])

    def src(d):
        return src_ref if mode is True else src_ref.at[owner_slot(d) if owned else d]

    local = (pltpu.make_async_copy(src(me), dst_ref.at[me], local_sem), is_owner(me) if owned else True)
    remote = []
    for j in range(1, N_DEV):
        px, py, pc = (1 - x if j & 4 else x), (1 - y if j & 2 else y), (1 - c if j & 1 else c)
        d = 4 * px + 2 * py + pc
        cp = pltpu.make_async_remote_copy(src_ref=src(d), dst_ref=dst_ref.at[me], send_sem=send_sems.at[j - 1],
                                          recv_sem=recv_sems.at[j - 1], device_id=(px, py, pc), device_id_type=MESH)
        remote.append((cp, is_owner(d) if owned else True, is_owner(me) if owned else True))
    return local, remote


def _when(cond, fn):
    if cond is True:
        fn()
    else:
        pl.when(cond)(fn)


def _exchange_hook(first, last, mode, refs):
    @pl.when(first)
    def _():
        (local, here), remote = _exchange_ops(refs[0], mode, *refs[1:])
        _when(here, local.start)
        for cp, sends, _ in remote:
            _when(sends, cp.start)

    @pl.when(last)
    def _():
        (local, here), remote = _exchange_ops(refs[0], mode, *refs[1:])
        for cp, sends, receives in remote:
            _when(sends, cp.wait_send)
            _when(receives, cp.wait_recv)
        _when(here, local.wait)


def _exchange_shapes(exchange):
    if exchange is None:
        return [], [], [], []
    arr, mode = exchange
    shape = (N_DEV,) + (arr.shape if mode is True else arr.shape[1:])
    any_spec = pl.BlockSpec(memory_space=pl.ANY)
    sems = [pltpu.SemaphoreType.DMA((N_DEV - 1,)), pltpu.SemaphoreType.DMA((N_DEV - 1,)), pltpu.SemaphoreType.DMA]
    return [any_spec], [any_spec], [jax.ShapeDtypeStruct(shape, arr.dtype)], sems


def _row_call(name, fn, n, rows, consts, outs, accs=(), scratch=(), exchange=None):
    def row_spec(shape, mode):
        r = shape[-2] // n
        if mode == "cur":
            f = lambda i: i
        elif mode == "prev":
            f = lambda i: jnp.maximum(i - 1, 0)
        else:
            f = lambda i: jnp.minimum(i + 1, n - 1)
        if len(shape) == 2:
            return pl.BlockSpec((r, shape[1]), lambda i: (f(i), 0))
        return pl.BlockSpec((shape[0], r, shape[2]), lambda i: (0, f(i), 0))

    def lane_spec(shape):
        return pl.BlockSpec((shape[0], shape[1], shape[2] // n), lambda i: (0, 0, i))

    def whole_spec(shape, single):
        nd = len(shape)
        if single:
            return pl.BlockSpec(shape, lambda i: (0,) * nd, pipeline_mode=pl.Buffered(1))
        return pl.BlockSpec(shape, lambda i: (0,) * nd)

    nr, nc, no, na, ns = len(rows), len(consts), len(outs), len(accs), len(scratch)
    exchanges = [] if exchange is None else (exchange if isinstance(exchange, list) else [exchange])
    shapes = [_exchange_shapes(e) for e in exchanges]
    x_in, x_out, x_shape, x_sems = (sum((s[part] for s in shapes), []) for part in range(4))
    nx = len(exchanges)

    def body(*refs):
        i = pl.program_id(0)
        row_refs, const_refs = refs[:nr], refs[nr:nr + nc]
        o0 = nr + nc + nx
        out_refs, acc_refs = refs[o0:o0 + no], refs[o0 + no:o0 + no + na]
        s0 = o0 + no + na + nx
        for e in range(nx):
            sems = refs[s0 + ns + 3 * e:s0 + ns + 3 * e + 3]
            _exchange_hook(i == 0, i == n - 1, exchanges[e][1], (refs[nr + nc + e], refs[o0 + no + na + e]) + tuple(sems))
        if na:
            @pl.when(i == 0)
            def _():
                for a in acc_refs:
                    a[...] = jnp.zeros(a.shape, a.dtype)
        fn(i, row_refs, const_refs, out_refs, acc_refs, refs[s0:s0 + ns])

    res = pl.pallas_call(
        body, name=name, grid=(n,),
        in_specs=[row_spec(a.shape, m) for a, m in rows] + [whole_spec(c.shape, True) for c in consts] + x_in,
        out_specs=[lane_spec(o[0]) if len(o) == 3 else row_spec(o[0], "cur") for o in outs]
        + [whole_spec(s, False) for s, _ in accs] + x_out,
        out_shape=[jax.ShapeDtypeStruct(o[0], o[1]) for o in list(outs) + list(accs)] + x_shape,
        scratch_shapes=list(scratch) + x_sems,
        compiler_params=pltpu.CompilerParams(dimension_semantics=("arbitrary",), vmem_limit_bytes=VMEM_LIMIT),
    )(*[a for a, _ in rows], *consts, *[e[0] for e in exchanges])
    return list(res)


def _in_proj(x, g_mix, wp, n):
    t = x.shape[0]

    def fn(i, rows, consts, outs, accs, scr):
        g, w = consts
        u, _, _ = _rms(rows[0][...], g[...])
        ub = u.astype(BF16)
        outs[0][...] = ub
        for k, (lo, hi) in enumerate(((P_A, P_G), (P_G, P_Q), (P_Q, P_KV), (P_KV, P_KR), (P_KR, P_GL), (P_GL, P_END))):
            outs[1 + k][...] = _mm(ub, w[:, lo:hi])

    outs = [((t, D), BF16), ((t, CC), F32), ((t, CC), F32), ((t, QL), F32), ((t, KL), F32), ((t, HP), F32), ((t, 2 * D), F32)]
    return _row_call("in_proj", fn, n, [(x, "cur")], [g_mix, wp], outs)


def _fill_glu_window(i, a, gt, ap, gtp, zz):
    r = a.shape[0]
    zp = ap[r - HALO:, :] * _sig(gtp[r - HALO:, :])
    zz[0:HALO, :] = jnp.where(i > 0, zp, 0.0)
    zz[HALO:, :] = a[...] * _sig(gt[...])


def _shift_copies(buf, sh):
    rows = buf.shape[0]
    for s in range(8):
        sh[s, 0:rows - s, :] = buf[s:rows, :]


def _window(sh, o, base, rb, ln):
    return sh[o % 8, base + o - o % 8:base + o - o % 8 + rb, ln]


def _windows(buf, base, ln, rb, offsets):
    for s in range(8):
        group = [o for o in offsets if o % 8 == s]
        if group:
            shifted = buf[base + s:base + max(group) + rb, ln]
            for o in group:
                yield o, shifted[o - s:o - s + rb]


def _conv_branch(a, gt, cw, cb, lng, lnb, wco, n):
    t = a.shape[0]
    r = t // n

    def fn(i, rows, consts, outs, accs, scr):
        w, b, lg, lb, wo = consts
        zz, zsh = scr
        _fill_glu_window(i, rows[0], rows[1], rows[2], rows[3], zz)
        _shift_copies(zz, zsh)
        rb = min(CONV_ROWS, r)
        for c in range(CC // HP):
            ln = slice(c * HP, (c + 1) * HP)
            for base in range(0, r, rb):
                acc = jnp.zeros((rb, HP), F32)
                for j in range(CW):
                    acc = acc + w[j:j + 1, ln] * _window(zsh, HALO - (CW - 1) + j, base, rb, ln)
                outs[0][base:base + rb, ln] = acc + b[:, ln]
        zc = outs[0][...]
        mu = jnp.mean(zc, axis=-1, keepdims=True)
        dlt = zc - mu
        rs = lax.rsqrt(jnp.mean(dlt * dlt, axis=-1, keepdims=True) + EPS)
        zn = dlt * rs * lg[...] + lb[...]
        outs[1][...] = _mm(zn * _sig(zn), wo[...])

    return _row_call("conv_branch", fn, n, [(a, "cur"), (gt, "cur"), (a, "prev"), (gt, "prev")],
                     [cw, cb, lng, lnb, wco], [((t, CC), F32), ((t, D), F32)],
                     scratch=[pltpu.VMEM((r + HALO, CC), F32), pltpu.VMEM((8, r + HALO, CC), F32)])


def _rope(v, c, sa, sb):
    return v * c + pltpu.roll(v, HP - ROPE // 2, 1) * sa + pltpu.roll(v, ROPE // 2, 1) * sb


def _rope_bwd(dv, c, sa, sb):
    return dv * c + pltpu.roll(dv * sa, ROPE // 2, 1) + pltpu.roll(dv * sb, HP - ROPE // 2, 1)


def _mla_prep(cq, ckv, krp, tc, tsa, tsb, gq, wuq, gkv, wk, wv, n):
    t = cq.shape[0]

    def fn(i, rows, consts, outs, accs, scr):
        g_q, w_q, g_kv, w_k, w_v = consts
        c, sa, sb = rows[3][...], rows[4][...], rows[5][...]
        cqn = _rms(rows[0][...], g_q[...])[0].astype(BF16)
        ckvn = _rms(rows[1][...], g_kv[...])[0].astype(BF16)
        outs[3][...] = cqn
        outs[4][...] = ckvn
        krr = _rope(rows[2][...], c, sa, sb)
        for h in range(NH):
            ln = slice(h * HP, (h + 1) * HP)
            outs[0][:, ln] = (_rope(_mm(cqn, w_q[:, ln]), c, sa, sb) * MLA_SCALE).astype(BF16)
            outs[1][:, ln] = (_mm(ckvn, w_k[:, ln]) + krr).astype(BF16)
        vv = _mm(ckvn, w_v[...])
        lane = lax.broadcasted_iota(jnp.int32, vv.shape, 1)
        outs[2][...] = jnp.where((lane & (HP - 1)) == VD, 1.0, vv).astype(BF16)

    outs = [((t, NH * HP), BF16)] * 3 + [((t, QL), BF16), ((t, KL), BF16)]
    return _row_call("mla_prep", fn, n, [(a, "cur") for a in (cq, ckv, krp, tc, tsa, tsb)], [gq, wuq, gkv, wk, wv], outs)


def _chunk_mask(c, tq, transposed, rows=ATT_CHUNK):
    row = lax.broadcasted_iota(jnp.int32, (rows, tq), 0) + c * rows
    col = lax.broadcasted_iota(jnp.int32, (rows, tq), 1)
    return (row <= col) if transposed else (col <= row)


def _flash_hook(exchange, nh, nq, refs):
    if exchange is not None:
        h, i = pl.program_id(0), pl.program_id(1)
        _exchange_hook((h == 0) & (i == 0), (h == nh - 1) & (i == nq - 1), exchange[1], refs)


def _head_lanes(g):
    return slice(g * HP, (g + 1) * HP)


def _flash_fwd(q, k, v, exchange=None):
    t = q.shape[0]
    tq = min(ATT_TILE, t)
    nq = t // tq
    x_in, x_out, x_shape, x_sems = _exchange_shapes(exchange)

    def body(q_ref, k_ref, v_ref, *rest):
        o_ref, lse_ref = rest[len(x_in):len(x_in) + 2]
        _flash_hook(exchange, NH // HG_FWD, nq, rest[:len(x_in)] + rest[len(x_in) + 2:])
        i = pl.program_id(1)

        def step(j, carry, masked):
            at = pl.ds(pl.multiple_of(j * tq, tq), tq)
            out = []
            for g in range(HG_FWD):
                m, acc = carry[g]
                s = _mm_nt(q_ref[:, _head_lanes(g)], k_ref[at, _head_lanes(g)])
                if masked:
                    s = jnp.where(_chunk_mask(0, tq, False, tq), s, NEG)
                m_new = jnp.maximum(m, jnp.max(s, axis=-1, keepdims=True))
                out.append((m_new, jnp.exp(m - m_new) * acc + _mm(jnp.exp(s - m_new), v_ref[at, _head_lanes(g)])))
            return tuple(out)

        init = tuple((jnp.full((tq, 1), NEG, F32), jnp.zeros((tq, HP), F32)) for _ in range(HG_FWD))
        carry = lax.fori_loop(0, i, lambda j, c: step(j, c, False), init)
        for g, (m, acc) in enumerate(step(i, carry, True)):
            lane = lax.broadcasted_iota(jnp.int32, acc.shape, 1)
            l = jnp.sum(jnp.where(lane == VD, acc, 0.0), axis=-1, keepdims=True)
            o_ref[:, _head_lanes(g)] = acc / l
            lse_ref[g] = _col_to_rows(m + jnp.log(l))

    wide = HG_FWD * HP
    return pl.pallas_call(
        body, name="flash_fwd", grid=(NH // HG_FWD, nq),
        in_specs=[pl.BlockSpec((tq, wide), lambda h, i: (i, h)), pl.BlockSpec((t, wide), lambda h, i: (0, h)),
                  pl.BlockSpec((t, wide), lambda h, i: (0, h))] + x_in,
        out_specs=[pl.BlockSpec((tq, wide), lambda h, i: (i, h)), pl.BlockSpec((HG_FWD, 8, tq), lambda h, i: (h, 0, i))] + x_out,
        out_shape=[jax.ShapeDtypeStruct((t, NH * HP), F32), jax.ShapeDtypeStruct((NH, 8, t), F32)] + x_shape,
        scratch_shapes=x_sems,
        compiler_params=pltpu.CompilerParams(dimension_semantics=("arbitrary", "arbitrary"), vmem_limit_bytes=VMEM_LIMIT),
    )(q, k, v, *([exchange[0]] if x_in else []))


def _flash_bwd(q, k, v, do, lse_row, delta_row, exchange=None):
    t = q.shape[0]
    tq = min(ATT_TILE, t)
    nq = t // tq
    x_in, x_out, x_shape, x_sems = _exchange_shapes(exchange)

    def body(q_ref, k_ref, v_ref, do_ref, lse_ref, dl_ref, *rest):
        dk_ref, dv_ref, dq_ref = rest[len(x_in):len(x_in) + 3]
        st_scr, dpt_scr, pt_scr, dst_scr = rest[len(rest) - 4:]
        _flash_hook(exchange, NH // HG, nq, rest[:len(x_in)] + rest[len(x_in) + 3:len(rest) - 4])
        j = pl.program_id(1)
        dk_ref[...] = jnp.zeros(dk_ref.shape, F32)
        dv_ref[...] = jnp.zeros(dv_ref.shape, F32)

        @pl.when(j == 0)
        def _():
            dq_ref[...] = jnp.zeros(dq_ref.shape, F32)

        def step(i, masked):
            at = pl.ds(pl.multiple_of(i * tq, tq), tq)
            for g in range(HG):
                st_scr[g] = _mm_nt(k_ref[:, _head_lanes(g)], q_ref[at, _head_lanes(g)])
                dpt_scr[g] = _mm_nt(v_ref[:, _head_lanes(g)], do_ref[at, _head_lanes(g)])
            for g in range(HG):
                lse_i, dl_i = lse_ref[g, 0:1, at], dl_ref[g, 0:1, at]
                for c in range(tq // ATT_CHUNK):
                    rows = slice(c * ATT_CHUNK, (c + 1) * ATT_CHUNK)
                    st = st_scr[g, rows, :]
                    if masked:
                        st = jnp.where(_chunk_mask(c, tq, True), st, NEG)
                    pt = jnp.exp(st - lse_i)
                    pt_scr[g, rows, :] = pt.astype(BF16)
                    dst_scr[g, rows, :] = (pt * (dpt_scr[g, rows, :] - dl_i)).astype(BF16)
            for g in range(HG):
                dv_ref[:, _head_lanes(g)] += _mm(pt_scr[g], do_ref[at, _head_lanes(g)])
                dk_ref[:, _head_lanes(g)] += _mm(dst_scr[g], q_ref[at, _head_lanes(g)])
                dq_ref[at, _head_lanes(g)] += _mm_tn(dst_scr[g], k_ref[:, _head_lanes(g)])

        step(j, True)
        lax.fori_loop(j + 1, nq, lambda i, c: (step(i, False), c)[1], 0)

    wide = HG * HP
    blk = pl.BlockSpec((tq, wide), lambda h, j: (j, h))
    whole = pl.BlockSpec((t, wide), lambda h, j: (0, h))
    row = pl.BlockSpec((HG, 8, t), lambda h, j: (h, 0, 0))
    x_sems = x_sems + [pltpu.VMEM((HG, tq, tq), F32), pltpu.VMEM((HG, tq, tq), F32), pltpu.VMEM((HG, tq, tq), BF16),
                       pltpu.VMEM((HG, tq, tq), BF16)]
    return pl.pallas_call(
        body, name="flash_bwd", grid=(NH // HG, nq), in_specs=[whole, blk, blk, whole, row, row] + x_in,
        out_specs=[blk, blk, whole] + x_out, out_shape=[jax.ShapeDtypeStruct((t, NH * HP), F32)] * 3 + x_shape, scratch_shapes=x_sems,
        compiler_params=pltpu.CompilerParams(dimension_semantics=("arbitrary", "arbitrary"), vmem_limit_bytes=VMEM_LIMIT),
    )(q, k, v, do, lse_row, delta_row, *([exchange[0]] if x_in else []))


def _mem_kv(mem, g_mem, wxkv):
    m = mem.shape[0]

    def fn(i, rows, consts, outs, accs, scr):
        mn = _rms(rows[0][...], consts[0][...])[0].astype(BF16)
        outs[0][...] = mn
        outs[1][...] = _mm(mn, consts[1][:, 0:XH * XD]).astype(BF16)
        outs[2][...] = _mm(mn, consts[1][:, XH * XD:]).astype(BF16)

    return _row_call("mem_kv", fn, 1, [(mem, "cur")], [g_mem, wxkv], [((m, D), BF16), ((m, XH * XD), BF16), ((m, XH * XD), BF16)])


def _merge_xattn(o, gl, conv_out, x, wmo, wo, g_x, wxq, kx, vx, wxo, n):
    t = x.shape[0]

    def fn(i, rows, consts, outs, accs, scr):
        w_mo, w_o, g, w_xq, k_x, v_x, w_xo = consts
        ob = rows[0][...].astype(BF16)
        outs[7][...] = ob
        mla = _mm(ob, w_mo[...])
        outs[0][...] = mla
        merged = (_sig(rows[1][:, 0:D]) * rows[2][...] + _sig(rows[1][:, D:]) * mla).astype(BF16)
        outs[1][...] = merged
        h1 = rows[3][...] + _mm(merged, w_o[...])
        outs[2][...] = h1
        u1 = _rms(h1, g[...])[0].astype(BF16)
        outs[3][...] = u1
        qx = (_mm(u1, w_xq[...]) * X_SCALE).astype(BF16)
        outs[4][...] = qx
        for h in range(XH):
            ln = slice(h * XD, (h + 1) * XD)
            s = _mm_nt(qx[:, ln], k_x[:, ln])
            e = jnp.exp(s - jnp.max(s, axis=-1, keepdims=True))
            p = e / jnp.sum(e, axis=-1, keepdims=True)
            outs[5][:, ln] = _mm(p, v_x[:, ln]).astype(BF16)
        outs[6][...] = h1 + _mm(outs[5][...], w_xo[...])

    outs = [((t, D), F32), ((t, D), BF16), ((t, D), F32), ((t, D), BF16), ((t, XH * XD), BF16), ((t, XH * XD), BF16),
            ((t, D), F32), ((t, NH * HP), BF16)]
    return _row_call("merge_xattn", fn, n, [(a, "cur") for a in (o, gl, conv_out, x)], [wmo, wo, g_x, wxq, kx, vx, wxo], outs)


def _mlp_loss(h2, target, g_mlp, w1, w2, g_fin, n):
    t = h2.shape[0]
    nck = DFF // D

    def fn(i, rows, consts, outs, accs, scr):
        g_m, w_1, w_2, g_f = consts
        h = rows[0][...]
        u2, xh2, r2 = _rms(h, g_m[...])
        ub = u2.astype(BF16)
        outs[0][...] = ub
        h3 = h
        a1 = []
        for c in range(nck):
            ck = slice(c * D, (c + 1) * D)
            a = _mm(ub, w_1[:, ck])
            a1.append(a)
            rl = jnp.maximum(a, 0.0)
            rb = (rl * rl).astype(BF16)
            outs[1][:, ck] = rb
            h3 = h3 + _mm(rb, w_2[ck, :])
        y, xh3, r3 = _rms(h3, g_f[...])
        err = y - rows[1][...]
        accs[0][...] += jnp.sum(jnp.sum(err * err, axis=1, keepdims=True), axis=0, keepdims=True) * (0.5 / D)
        dh3, dgf = _rms_bwd(err * (1.0 / D), xh3, r3, g_f[...])
        accs[1][...] += dgf
        db = dh3.astype(BF16)
        outs[3][...] = db
        du2 = jnp.zeros_like(h)
        for c in range(nck):
            ck = slice(c * D, (c + 1) * D)
            da = (_mm_nt(db, w_2[ck, :]) * (2.0 * jnp.maximum(a1[c], 0.0))).astype(BF16)
            outs[2][:, ck] = da
            du2 = du2 + _mm_nt(da, w_1[:, ck])
        dx2, dgm = _rms_bwd(du2, xh2, r2, g_m[...])
        accs[2][...] += dgm
        outs[4][...] = dh3 + dx2

    outs = [((t, D), BF16), ((t, DFF), BF16), ((t, DFF), BF16), ((t, D), BF16), ((t, D), F32)]
    accs = [((1, 1), F32), ((1, D), F32), ((1, D), F32)]
    return _row_call("mlp_loss", fn, n, [(h2, "cur"), (target, "cur")], [g_mlp, w1, w2, g_fin], outs, accs)


def _merge_xattn_bwd(dh2, h1, qx, gl, conv_out, mla, o, wxo, kx, vx, wxq, g_x, wo, wmo, n):
    t = dh2.shape[0]
    m = kx.shape[0]

    def fn(i, rows, consts, outs, accs, scr):
        w_xo, k_x, v_x, w_xq, g, w_o, w_mo = consts
        d2 = rows[0][...]
        d2b = d2.astype(BF16)
        outs[7][...] = d2b
        dox = _mm_nt(d2b, w_xo[...]).astype(BF16)
        q = rows[2][...]
        dq = []
        for h in range(XH):
            ln = slice(h * XD, (h + 1) * XD)
            qh, kh, vh, doh = q[:, ln], k_x[:, ln], v_x[:, ln], dox[:, ln]
            s = _mm_nt(qh, kh)
            e = jnp.exp(s - jnp.max(s, axis=-1, keepdims=True))
            p = e / jnp.sum(e, axis=-1, keepdims=True)
            dp = _mm_nt(doh, vh)
            ds = p * (dp - jnp.sum(p * dp, axis=-1, keepdims=True))
            dq.append(_mm(ds, kh) * X_SCALE)
            accs[0][:, ln] += _mm_tn(ds, qh)
            accs[1][:, ln] += _mm_tn(p, doh)
        dqx = jnp.concatenate(dq, axis=1).astype(BF16)
        outs[0][...] = dqx
        _, xh1, r1 = _rms(rows[1][...], g[...])
        dx1, dg = _rms_bwd(_mm_nt(dqx, w_xq[...]), xh1, r1, g[...])
        accs[2][...] += dg
        d1 = d2 + dx1
        outs[1][...] = d1
        d1b = d1.astype(BF16)
        outs[8][...] = d1b
        dm = _mm_nt(d1b, w_o[...])
        g0, g1 = _sig(rows[3][:, 0:D]), _sig(rows[3][:, D:])
        outs[2][:, 0:D] = (dm * rows[4][...] * g0 * (1.0 - g0)).astype(BF16)
        outs[2][:, D:] = (dm * rows[5][...] * g1 * (1.0 - g1)).astype(BF16)
        outs[3][...] = (dm * g0).astype(BF16)
        dmla = (dm * g1).astype(BF16)
        outs[4][...] = dmla
        do = _mm_nt(dmla, w_mo[...])
        outs[5][...] = do.astype(BF16)
        prod = do * rows[6][...]
        for h in range(NH):
            outs[6][h] = _col_to_rows(jnp.sum(prod[:, h * HP:(h + 1) * HP], axis=-1, keepdims=True))

    outs = [((t, XH * XD), BF16), ((t, D), F32), ((t, 2 * D), BF16), ((t, D), BF16), ((t, D), BF16), ((t, NH * HP), BF16),
            ((NH, 8, t), F32, "lanes"), ((t, D), BF16), ((t, D), BF16)]
    accs = [((m, XH * XD), F32), ((m, XH * XD), F32), ((1, D), F32)]
    return _row_call("merge_xattn_bwd", fn, n, [(a, "cur") for a in (dh2, h1, qx, gl, conv_out, mla, o)],
                     [wxo, kx, vx, wxq, g_x, wo, wmo], outs, accs)


def _mla_prep_bwd(dq, dk, dv, cq, ckv, tc, tsa, tsb, gq, wuq, gkv, wk, wv, n):
    t = dq.shape[0]

    def fn(i, rows, consts, outs, accs, scr):
        g_q, w_q, g_kv, w_k, w_v = consts
        c, sa, sb = rows[5][...], rows[6][...], rows[7][...]
        dkr = jnp.zeros((rows[0].shape[0], HP), F32)
        for h in range(NH):
            ln = slice(h * HP, (h + 1) * HP)
            outs[0][:, ln] = _rope_bwd(rows[0][:, ln] * MLA_SCALE, c, sa, sb).astype(BF16)
            dkr = dkr + rows[1][:, ln]
        lane = lax.broadcasted_iota(jnp.int32, dkr.shape, 1)
        outs[5][...] = jnp.where((lane >= NOPE) & (lane < NOPE + ROPE), _rope_bwd(dkr, c, sa, sb), 0.0).astype(BF16)
        dkb, dvb = rows[1][...].astype(BF16), rows[2][...].astype(BF16)
        outs[1][...] = dkb
        outs[2][...] = dvb
        _, xq, rq = _rms(rows[3][...], g_q[...])
        dcq, dgq = _rms_bwd(_mm_nt(outs[0][...], w_q[...]), xq, rq, g_q[...])
        outs[3][...] = dcq.astype(BF16)
        accs[0][...] += dgq
        _, xk, rk = _rms(rows[4][...], g_kv[...])
        dckv, dgk = _rms_bwd(_mm_nt(dkb, w_k[...]) + _mm_nt(dvb, w_v[...]), xk, rk, g_kv[...])
        outs[4][...] = dckv.astype(BF16)
        accs[1][...] += dgk

    outs = [((t, NH * HP), BF16)] * 3 + [((t, QL), BF16), ((t, KL), BF16), ((t, HP), BF16)]
    return _row_call("mla_prep_bwd", fn, n, [(a, "cur") for a in (dq, dk, dv, cq, ckv, tc, tsa, tsb)],
                     [gq, wuq, gkv, wk, wv], outs, [((1, QL), F32), ((1, KL), F32)])


def _conv_out_bwd(dco, zc, wco, lng, lnb, n):
    t = zc.shape[0]

    def fn(i, rows, consts, outs, accs, scr):
        wo, lg, lb = consts
        z = rows[1][...]
        mu = jnp.mean(z, axis=-1, keepdims=True)
        dlt = z - mu
        rs = lax.rsqrt(jnp.mean(dlt * dlt, axis=-1, keepdims=True) + EPS)
        xh = dlt * rs
        zn = xh * lg[...] + lb[...]
        sg = _sig(zn)
        outs[0][...] = (zn * sg).astype(BF16)
        dzn = _mm_nt(rows[0][...], wo[...]) * (sg * (1.0 + zn * (1.0 - sg)))
        accs[0][...] += jnp.sum(dzn * xh, axis=0, keepdims=True)
        accs[1][...] += jnp.sum(dzn, axis=0, keepdims=True)
        dxh = dzn * lg[...]
        dzc = rs * (dxh - jnp.mean(dxh, axis=-1, keepdims=True) - xh * jnp.mean(dxh * xh, axis=-1, keepdims=True))
        outs[1][...] = dzc
        accs[2][...] += jnp.sum(dzc, axis=0, keepdims=True)

    return _row_call("conv_out_bwd", fn, n, [(dco, "cur"), (zc, "cur")], [wco, lng, lnb],
                     [((t, CC), BF16), ((t, CC), F32)], [((1, CC), F32)] * 3)


def _conv_glu_bwd(dzc, a, gt, cw, n):
    t = a.shape[0]
    r = t // n

    def fn(i, rows, consts, outs, accs, scr):
        w = consts[0]
        zz, dd = scr
        _fill_glu_window(i, rows[2], rows[3], rows[4], rows[5], zz)
        dd[0:r, :] = rows[0][...]
        dd[r:, :] = jnp.where(i < n - 1, rows[1][0:HALO, :], 0.0)
        rb = min(CONV_ROWS, r)
        for c in range(CC // HP):
            ln = slice(c * HP, (c + 1) * HP)
            for base in range(0, r, rb):
                here = slice(base, base + rb)
                dcur = dd[here, ln]
                for o, win in _windows(zz, base, ln, rb, [HALO - (CW - 1) + j for j in range(CW)]):
                    j = o - (HALO - (CW - 1))
                    accs[0][j:j + 1, ln] += jnp.sum(dcur * win, axis=0, keepdims=True)
                acc = jnp.zeros((rb, HP), F32)
                for o, win in _windows(dd, base, ln, rb, [CW - 1 - j for j in range(CW)]):
                    j = CW - 1 - o
                    acc = acc + w[j:j + 1, ln] * win
                sg = _sig(rows[3][here, ln])
                outs[0][here, ln] = (acc * sg).astype(BF16)
                outs[0][here, CC + c * HP:CC + (c + 1) * HP] = (acc * rows[2][here, ln] * sg * (1.0 - sg)).astype(BF16)

    return _row_call("conv_glu_bwd", fn, n, [(dzc, "cur"), (dzc, "next"), (a, "cur"), (gt, "cur"), (a, "prev"), (gt, "prev")],
                     [cw], [((t, 2 * CC), BF16)], [((HALO, CC), F32)],
                     scratch=[pltpu.VMEM((r + HALO, CC), F32), pltpu.VMEM((r + HALO, CC), F32)])


def _in_proj_bwd(pieces, dh1, x, wp, g_mix, n, exchange):
    t = x.shape[0]
    offs = (P_A, P_Q, P_KV, P_KR, P_GL, P_END)

    def fn(i, rows, consts, outs, accs, scr):
        w, g = consts
        du = jnp.zeros((rows[0].shape[0], D), F32)
        for k in range(5):
            du = du + _mm_nt(rows[k][...], w[:, offs[k]:offs[k + 1]])
        _, xh, r = _rms(rows[6][...], g[...])
        dx, dg = _rms_bwd(du, xh, r, g[...])
        accs[0][...] += dg
        outs[0][...] = rows[5][...] + dx

    return _row_call("in_proj_bwd", fn, n, [(a, "cur") for a in list(pieces) + [dh1, x]], [wp, g_mix],
                     [((t, D), F32)], [((1, D), F32)], exchange=exchange)


def _mem_bwd(mem, dkx, dvx, g_mem, wxkv):
    m = mem.shape[0]

    def fn(i, rows, consts, outs, accs, scr):
        g, w = consts
        dkv = jnp.concatenate([rows[1][...], rows[2][...]], axis=1).astype(BF16)
        outs[0][...] = dkv
        _, xh, _ = _rms(rows[0][...], g[...])
        accs[0][...] += jnp.sum(_mm_nt(dkv, w[...]) * xh, axis=0, keepdims=True)

    return _row_call("mem_bwd", fn, 1, [(mem, "cur"), (dkx, "cur"), (dvx, "cur")], [g_mem, wxkv],
                     [((m, 2 * XH * XD), BF16)], [((1, D), F32)])


def _dw(name, xs, dy):
    t, k = xs.shape
    nn = dy.shape[1]
    tk, tn, tt = min(k, 1024), min(nn, 1024), min(t, DW_TILE)

    def body(x_ref, dy_ref, o_ref):
        @pl.when(pl.program_id(2) == 0)
        def _():
            o_ref[...] = jnp.zeros(o_ref.shape, F32)
        o_ref[...] += lax.dot_general(x_ref[...], dy_ref[...], (((0,), (0,)), ((), ())), preferred_element_type=F32)

    return pl.pallas_call(
        body, name=name, grid=(k // tk, nn // tn, t // tt),
        in_specs=[pl.BlockSpec((tt, tk), lambda a, b, c: (c, a)), pl.BlockSpec((tt, tn), lambda a, b, c: (c, b))],
        out_specs=pl.BlockSpec((tk, tn), lambda a, b, c: (a, b)),
        out_shape=jax.ShapeDtypeStruct((k, nn), F32),
        compiler_params=pltpu.CompilerParams(dimension_semantics=("arbitrary", "arbitrary", "arbitrary"), vmem_limit_bytes=VMEM_LIMIT),
    )(xs, dy)


def _dw_shared(name, xs, dys):
    t, k = xs.shape
    tt = min(t, DW_TILE)
    nd = len(dys)

    def body(x_ref, *refs):
        @pl.when(pl.program_id(0) == 0)
        def _():
            for o_ref in refs[nd:]:
                o_ref[...] = jnp.zeros(o_ref.shape, F32)
        xb = x_ref[...]
        for dy_ref, o_ref in zip(refs[:nd], refs[nd:]):
            o_ref[...] += lax.dot_general(xb, dy_ref[...], (((0,), (0,)), ((), ())), preferred_element_type=F32)

    return pl.pallas_call(
        body, name=name, grid=(t // tt,),
        in_specs=[pl.BlockSpec((tt, k), lambda c: (c, 0))] + [pl.BlockSpec((tt, d.shape[1]), lambda c: (c, 0)) for d in dys],
        out_specs=[pl.BlockSpec((k, d.shape[1]), lambda c: (0, 0)) for d in dys],
        out_shape=[jax.ShapeDtypeStruct((k, d.shape[1]), F32) for d in dys],
        compiler_params=pltpu.CompilerParams(dimension_semantics=("arbitrary",), vmem_limit_bytes=VMEM_LIMIT),
    )(xs, *dys)


def _all_gather_packed(shard):
    rws = shard.shape[0]

    def body(x_ref, out_ref, send_sems, recv_sems, local_sem):
        x, y, c = _coords()
        me, sibling = (x, y, c), (x, y, 1 - c)
        chips = [(1 - x, y), (x, 1 - y), (1 - x, 1 - y)]

        def slot(px, py, pc):
            return out_ref.at[4 * px + 2 * py + pc]

        def copy(k, block, to, src=None):
            return pltpu.make_async_remote_copy(
                src_ref=slot(*block) if src is None else src, dst_ref=slot(*block),
                send_sem=send_sems.at[k], recv_sem=recv_sems.at[k], device_id=to, device_id_type=MESH)

        mine = pltpu.make_async_copy(x_ref, slot(*me), local_sem)
        mine.start()
        first = [copy(0, me, sibling, src=x_ref)] + [copy(1 + j, me, (*chip, c), src=x_ref) for j, chip in enumerate(chips)]
        for cp in first:
            cp.start()
        passed = [copy(4 + j, (*chip, c), sibling) for j, chip in enumerate(chips)]
        for j, chip in enumerate(chips):
            copy(1 + j, (*chip, c), me).wait_recv()
            passed[j].start()
        copy(0, sibling, me).wait_recv()
        for j, chip in enumerate(chips):
            copy(4 + j, (*chip, 1 - c), me).wait_recv()
        for cp in first + passed:
            cp.wait_send()
        mine.wait()

    return pl.pallas_call(
        body, name="all_gather_weights",
        out_shape=jax.ShapeDtypeStruct((N_DEV, rws, 128), shard.dtype),
        in_specs=[pl.BlockSpec(memory_space=pl.ANY)], out_specs=pl.BlockSpec(memory_space=pl.ANY),
        scratch_shapes=[pltpu.SemaphoreType.DMA((7,)), pltpu.SemaphoreType.DMA((7,)), pltpu.SemaphoreType.DMA],
    )(shard)


def _adam(w, g, m, v):
    m = ADAM_B1 * m + (1.0 - ADAM_B1) * g
    v = ADAM_B2 * v + (1.0 - ADAM_B2) * (g * g)
    m_hat = m / (1.0 - ADAM_B1 ** ADAM_STEP)
    v_hat = v / (1.0 - ADAM_B2 ** ADAM_STEP)
    return -ADAM_LR * (m_hat / (jnp.sqrt(v_hat) + ADAM_EPS) + ADAM_WD * w), m, v


def _small_allreduce_adam(part, w, m, v):
    shape = part.shape

    def body(p_ref, w_ref, m_ref, v_ref, g_ref, d_ref, nm_ref, nv_ref, buf, send_sems, recv_sems):
        x, y, c = _coords()
        me = 4 * x + 2 * y + c
        buf[0] = p_ref[...]
        cps = []
        for j in range(1, N_DEV):
            jx, jy, jc = j >> 2, (j >> 1) & 1, j & 1
            peer = (1 - x if jx else x, 1 - y if jy else y, 1 - c if jc else c)
            cps.append(pltpu.make_async_remote_copy(src_ref=p_ref, dst_ref=buf.at[j], send_sem=send_sems.at[j - 1],
                                                    recv_sem=recv_sems.at[j - 1], device_id=peer, device_id_type=MESH))
        for cp in cps:
            cp.start()
        for cp in cps:
            cp.wait()
        g = buf[me]
        for d in range(1, N_DEV):
            g = g + buf[d ^ me]
        g_ref[...] = g
        d_ref[...], nm_ref[...], nv_ref[...] = _adam(w_ref[...], g, m_ref[...], v_ref[...])

    vm = pl.BlockSpec(memory_space=pltpu.VMEM)
    return pl.pallas_call(
        body, name="small_allreduce_adam", out_shape=[jax.ShapeDtypeStruct(shape, F32)] * 4,
        in_specs=[vm] * 4, out_specs=[vm] * 4,
        scratch_shapes=[pltpu.VMEM((N_DEV,) + shape, F32), pltpu.SemaphoreType.DMA((7,)), pltpu.SemaphoreType.DMA((7,))],
    )(part, w, m, v)


def _sum_parts(name, parts):
    rws = parts.shape[1]
    tile = max(d for d in range(16, PACK_TILE + 1, 16) if rws % d == 0)

    def body(p_ref, g_ref):
        g = p_ref[0]
        for d in range(1, N_DEV):
            g = g + p_ref[d]
        g_ref[...] = g

    return pl.pallas_call(
        body, name=name, grid=(rws // tile,), in_specs=[pl.BlockSpec((N_DEV, tile, 128), lambda i: (0, i, 0))],
        out_specs=pl.BlockSpec((tile, 128), lambda i: (i, 0)), out_shape=jax.ShapeDtypeStruct((rws, 128), F32),
        compiler_params=pltpu.CompilerParams(dimension_semantics=("arbitrary",), vmem_limit_bytes=VMEM_LIMIT),
    )(parts)


def _adam_call(name, w, g, m, v):
    def body(w_ref, g_ref, m_ref, v_ref, d_ref, nm_ref, nv_ref):
        d_ref[...], nm_ref[...], nv_ref[...] = _adam(w_ref[...], g_ref[...], m_ref[...], v_ref[...])

    vm = pl.BlockSpec(memory_space=pltpu.VMEM)
    return pl.pallas_call(
        body, name=name, in_specs=[vm] * 4, out_specs=[vm] * 3, out_shape=[jax.ShapeDtypeStruct(w.shape, F32)] * 3,
        compiler_params=pltpu.CompilerParams(vmem_limit_bytes=VMEM_LIMIT),
    )(w, g, m, v)


ROW_SHARDED = ("w_out", "w_xq", "w_xkv", "w_mlp2")
SMALL = ("norm_mix_g", "conv_b", "conv_ln_g", "conv_ln_b", "q_norm_g", "kv_norm_g", "norm_xattn_g", "norm_mem_g", "norm_mlp_g", "final_norm_g")
GATHER_FIRST = ("w_in", "conv_w", "w_conv_out", "w_uq", "w_ukv")
GATHER_LATE = ("w_mla_out", "w_out", "w_xq", "w_xkv", "w_xo", "w_mlp1", "w_mlp2")
REDUCE_EARLY = ("w_mlp1", "w_mlp2", "conv_w", "w_conv_out", "w_mla_out", "w_out", "w_xq", "w_xkv", "w_xo", "w_in")
REDUCE_LAST = ("w_uq", "w_ukv")
ROW_ALIGN = 16


def _padded(k, nn):
    return -(-k // ROW_ALIGN) * ROW_ALIGN, -(-nn // 128) * 128


def _pack_rows(a):
    k, nn = a.shape[-2:]
    kp, np_ = _padded(k, nn)
    if (kp, np_) != (k, nn):
        a = jnp.pad(a, [(0, 0)] * (a.ndim - 2) + [(0, kp - k), (0, np_ - nn)])
    if np_ == 128:
        return a
    lead = a.shape[:-2]
    return jnp.swapaxes(a.reshape(lead + (kp, np_ // 128, 128)), -2, -3).reshape(lead + (kp * np_ // 128, 128))


def _unpack_rows(p, k, nn):
    kp, np_ = _padded(k, nn)
    if np_ != 128:
        lead = p.shape[:-2]
        p = jnp.swapaxes(p.reshape(lead + (np_ // 128, kp, 128)), -2, -3).reshape(lead + (kp, np_))
    return p[..., :k, :nn]


def _pack_group(arrays):
    return jnp.concatenate([_pack_rows(a) for a in arrays], axis=-2)


def _unpack_group(p, shapes):
    out, off = [], 0
    for k, nn in shapes:
        kp, np_ = _padded(k, nn)
        rws = kp * np_ // 128
        out.append(_unpack_rows(p[..., off:off + rws, :], k, nn))
        off += rws
    return out


def _tile_shaped(k, nn):
    return k % ROW_ALIGN == 0 and nn % 128 == 0


def _packed_to_full(name, seg, shard_shape):
    k, nn = shard_shape
    if not _tile_shaped(k, nn):
        stacked = _unpack_rows(seg, k, nn)
        return jnp.transpose(stacked, (1, 0, 2)).reshape(k, N_DEV * nn)
    a = seg.reshape(N_DEV, nn // 128, k, 128)
    if name in ROW_SHARDED:
        return jnp.transpose(a, (0, 2, 1, 3)).reshape(N_DEV * k, nn)
    return jnp.transpose(a, (2, 0, 1, 3)).reshape(k, N_DEV * nn)


def _full_to_packed(name, full, shard_shape):
    k, nn = shard_shape
    if not _tile_shaped(k, nn):
        return _pack_rows(jnp.transpose(full.reshape(k, N_DEV, nn), (1, 0, 2)))
    if name in ROW_SHARDED:
        a = jnp.transpose(full.reshape(N_DEV, k, nn // 128, 128), (0, 2, 1, 3))
    else:
        a = jnp.transpose(full.reshape(k, N_DEV, nn // 128, 128), (1, 2, 0, 3))
    return a.reshape(N_DEV, k * nn // 128, 128)


def _pack_small(vals):
    flat = jnp.concatenate([v.reshape(-1) for v in vals])
    rws = flat.shape[0] // 128
    return jnp.pad(flat, (0, (-(-rws // 8) * 8 - rws) * 128)).reshape(-1, 128)


def _unpack_small(p, sizes):
    flat = p.reshape(-1)
    out, off = [], 0
    for s in sizes:
        out.append(flat[off:off + s])
        off += s
    return out


def _head_pad(w, real, axis):
    shp = list(w.shape)
    shp[axis:axis + 1] = [NH, real]
    w = w.reshape(shp)
    pad = [(0, 0)] * w.ndim
    pad[axis + 1] = (0, HP - real)
    w = jnp.pad(w, pad)
    shp[axis:axis + 2] = [NH * HP]
    return w.reshape(shp)


def _head_unpad(w, lo, real, axis):
    shp = list(w.shape)
    shp[axis:axis + 1] = [NH, HP]
    w = lax.slice_in_dim(w.reshape(shp), lo, lo + real, axis=axis + 1)
    shp[axis:axis + 2] = [NH * real]
    return w.reshape(shp)


def kernel(x, mem, positions, norm_mix_g, w_in, conv_w, conv_b, conv_ln_g, conv_ln_b, w_conv_out, q_norm_g, w_uq, kv_norm_g, w_ukv, w_mla_out, w_out, norm_xattn_g, norm_mem_g, w_xq, w_xkv, w_xo, norm_mlp_g, w_mlp1, w_mlp2, final_norm_g, loss_target, m_norm_mix_g, m_w_in, m_conv_w, m_conv_b, m_conv_ln_g, m_conv_ln_b, m_w_conv_out, m_q_norm_g, m_w_uq, m_kv_norm_g, m_w_ukv, m_w_mla_out, m_w_out, m_norm_xattn_g, m_norm_mem_g, m_w_xq, m_w_xkv, m_w_xo, m_norm_mlp_g, m_w_mlp1, m_w_mlp2, m_final_norm_g, v_norm_mix_g, v_w_in, v_conv_w, v_conv_b, v_conv_ln_g, v_conv_ln_b, v_w_conv_out, v_q_norm_g, v_w_uq, v_kv_norm_g, v_w_ukv, v_w_mla_out, v_w_out, v_norm_xattn_g, v_norm_mem_g, v_w_xq, v_w_xkv, v_w_xo, v_norm_mlp_g, v_w_mlp1, v_w_mlp2, v_final_norm_g):
    args = dict(locals())
    t = x.shape[1]
    n = t // min(ROW_TILE, t)
    nl = t // min(LIGHT_ROW_TILE, t)
    xs, mems, tgt = x[0], mem[0], loss_target[0]

    def pack_shards(prefix, group, dtype):
        return _pack_group([args[prefix + k][0].astype(dtype) for k in group])

    def shapes(group):
        return [args[k].shape[1:] for k in group]

    def unpack_full(gathered, group):
        out, off = {}, 0
        for k, shp in zip(group, shapes(group)):
            kp, np_ = _padded(*shp)
            out[k] = _packed_to_full(k, gathered[:, off:off + kp * np_ // 128], shp)
            off += kp * np_ // 128
        return out

    full = unpack_full(_all_gather_packed(pack_shards("", GATHER_FIRST, BF16)), GATHER_FIRST)

    wi = full["w_in"]
    kr_slot = jnp.pad(wi[:, P_KR:P_KR + ROPE], ((0, 0), (NOPE, HP - NOPE - ROPE)))
    wp = jnp.concatenate([wi[:, :P_KR], kr_slot, wi[:, P_KR + ROPE:]], axis=1)
    cw = jnp.pad(full["conv_w"].astype(F32), ((0, HALO - CW), (0, 0)))
    wuq = _head_pad(full["w_uq"], NOPE + ROPE, 1)
    ukv = full["w_ukv"].reshape(KL, NH, NOPE + VD)
    wk = _head_pad(ukv[:, :, :NOPE].reshape(KL, NH * NOPE), NOPE, 1)
    wv = _head_pad(ukv[:, :, NOPE:].reshape(KL, NH * VD), VD, 1)

    half = ROPE // 2
    lane = jnp.arange(HP)
    first, second = (lane >= NOPE) & (lane < NOPE + half), (lane >= NOPE + half) & (lane < NOPE + ROPE)
    inv_freq = THETA ** (-((lane - NOPE) % half).astype(F32) / half)
    ang = positions[0].astype(F32)[:, None] * inv_freq
    cs, sn = jnp.cos(ang), jnp.sin(ang)
    tc = jnp.where(lane < NOPE, 1.0, jnp.where(first | second, cs, 0.0))
    tsa = jnp.where(first, -sn, 0.0)
    tsb = jnp.where(second, sn, 0.0)

    u0, a, gt, cq, ckv, krp, gl = _in_proj(xs, norm_mix_g, wp, nl)
    zc, conv_out = _conv_branch(a, gt, cw, conv_b, conv_ln_g, conv_ln_b, full["w_conv_out"], n)
    qh, kh, vh, cqn, ckvn = _mla_prep(cq, ckv, krp, tc, tsa, tsb, q_norm_g, wuq, kv_norm_g, wk, wv, nl)
    o, lse, gathered = _flash_fwd(qh, kh, vh, exchange=(pack_shards("", GATHER_LATE, BF16), True))
    full.update(unpack_full(gathered, GATHER_LATE))
    wmo = _head_pad(full["w_mla_out"], VD, 0)
    memn, kx, vx = _mem_kv(mems, norm_mem_g, full["w_xkv"])
    mla, merged, h1, u1, qx, ox, h2, ob = _merge_xattn(o, gl, conv_out, xs, wmo, full["w_out"], norm_xattn_g, full["w_xq"],
                                                        kx, vx, full["w_xo"], nl)
    gfin = final_norm_g.reshape(1, D)
    u2, rl2, da1, dh3b, dh2, loss_p, dg_fin, dg_mlp = _mlp_loss(h2, tgt, norm_mlp_g, full["w_mlp1"], full["w_mlp2"], gfin, n)

    (dqx, dh1, dgl, dco, dmla, dob, delta, dh2b, dh1b, dkx, dvx, dg_x) = _merge_xattn_bwd(
        dh2, h1, qx, gl, conv_out, mla, o, full["w_xo"], kx, vx, full["w_xq"], norm_xattn_g, full["w_out"], wmo, n)
    gfull = {"w_mlp1": _dw("dw_mlp1", u2, da1), "w_mlp2": _dw("dw_mlp2", rl2, dh3b)}

    def stacked(group):
        return jnp.concatenate([_full_to_packed(k, gfull[k], args[k].shape[1:]) for k in group], axis=1)

    zs, dzc, dg_lng, dg_lnb, dg_cb = _conv_out_bwd(dco, zc, full["w_conv_out"], conv_ln_g, conv_ln_b, nl)
    dci, dcw = _conv_glu_bwd(dzc, a, gt, cw, n)
    dkv, dg_mem = _mem_bwd(mems, dkx, dvx, norm_mem_g, full["w_xkv"])
    gfull.update({
        "conv_w": dcw[:CW],
        "w_conv_out": _dw("dw_conv_out", zs, dco),
        "w_mla_out": _head_unpad(_dw("dw_mla_out", ob, dmla), 0, VD, 0),
        "w_out": _dw("dw_out", merged, dh1b),
        "w_xq": _dw("dw_xq", u1, dqx),
        "w_xkv": _dw("dw_xkv", memn, dkv),
        "w_xo": _dw("dw_xo", ox, dh2b),
    })
    late_lo, late_hi, wcols = P_Q, P_KR + ROPE, args["w_in"].shape[2]
    dw_conv_in, dw_gates = _dw("dw_in_0", u0, dci), _dw("dw_in_4", u0, dgl)
    gfull["w_in"] = jnp.concatenate([dw_conv_in, jnp.zeros((D, late_hi - late_lo), F32), dw_gates], axis=1)
    dk, dv, dq, parts_early = _flash_bwd(qh, kh, vh, dob, lse, delta, exchange=(stacked(REDUCE_EARLY), False))
    dqp, dkb, dvb, dcq, dckv, dkrp, dg_q, dg_kv = _mla_prep_bwd(dq, dk, dv, cq, ckv, tc, tsa, tsb, q_norm_g, wuq, kv_norm_g, wk, wv, nl)
    pieces = (dci, dcq, dckv, dkrp, dgl)
    dw_cq, dw_ckv, dw_kr = _dw_shared("dw_in_late", u0, [dcq, dckv, dkrp])
    late = jnp.concatenate([dw_cq, dw_ckv, dw_kr[:, NOPE:NOPE + ROPE]], axis=1)
    gfull["w_uq"] = _head_unpad(_dw("dw_uq", cqn, dqp), 0, NOPE + ROPE, 1)
    dw_k, dw_v = _dw_shared("dw_ukv", ckvn, [dkb, dvb])
    gk = _head_unpad(dw_k, 0, NOPE, 1).reshape(KL, NH, NOPE)
    gv = _head_unpad(dw_v, 0, VD, 1).reshape(KL, NH, VD)
    gfull["w_ukv"] = jnp.concatenate([gk, gv], axis=2).reshape(KL, NH * (NOPE + VD))
    owners = [(d, max(late_lo, d * wcols) - d * wcols, max(late_lo, d * wcols) - late_lo,
               min(late_hi, (d + 1) * wcols) - max(late_lo, d * wcols)) for d in range(N_DEV)
              if min(late_hi, (d + 1) * wcols) > max(late_lo, d * wcols)]
    late_w = -(-max(w for _, _, _, w in owners) // 128) * 128
    late_src = _pack_rows(jnp.stack([jnp.pad(late[:, l0:l0 + w], ((0, 0), (0, late_w - w))) for _, _, l0, w in owners]))
    grad_x, dg_mix, parts_last, parts_late = _in_proj_bwd(
        pieces, dh1, xs, wp, norm_mix_g, n, [(stacked(REDUCE_LAST), False), (late_src, tuple(d for d, _, _, _ in owners))])

    gsum = {}
    for name, group, parts in (("grad_sum_early", REDUCE_EARLY, parts_early), ("grad_sum_last", REDUCE_LAST, parts_last)):
        gsum.update(zip(group, _unpack_group(_sum_parts(name, parts), shapes(group))))
    late_sum = _unpack_rows(_sum_parts("grad_sum_late", parts_late), D, late_w)
    me = 4 * lax.axis_index("x") + 2 * lax.axis_index("y") + lax.axis_index("c")
    placed = jnp.zeros((D, wcols), F32)
    for d, s0, _, w in owners:
        placed = jnp.where(me == d, jnp.pad(late_sum[:, :w], ((0, 0), (s0, wcols - s0 - w))), placed)
    gsum["w_in"] = gsum["w_in"] + placed
    big = [{}, {}, {}, {}]
    for k, g in gsum.items():
        res = _adam_call("adam_" + k, args[k][0], g, args["m_" + k][0], args["v_" + k][0])
        for kind, val in enumerate([g] + list(res)):
            big[kind][k] = val[None]

    small_g = {"norm_mix_g": dg_mix, "conv_b": dg_cb, "conv_ln_g": dg_lng, "conv_ln_b": dg_lnb, "q_norm_g": dg_q,
               "kv_norm_g": dg_kv, "norm_xattn_g": dg_x, "norm_mem_g": dg_mem, "norm_mlp_g": dg_mlp, "final_norm_g": dg_fin}
    small_sizes = [int(np.prod(args[k].shape)) for k in SMALL] + [128]
    zero_slot = jnp.zeros((128,), F32)
    small_out = _small_allreduce_adam(
        _pack_small([small_g[k] for k in SMALL] + [jnp.pad(loss_p.reshape(-1), (0, 127))]),
        _pack_small([args[k] for k in SMALL] + [zero_slot]), _pack_small([args["m_" + k] for k in SMALL] + [zero_slot]),
        _pack_small([args["v_" + k] for k in SMALL] + [zero_slot]))

    small = [dict(zip(SMALL, [s.reshape(args[k].shape) for k, s in zip(SMALL, _unpack_small(o_, small_sizes))])) for o_ in small_out]
    loss = _unpack_small(small_out[0], small_sizes)[-1][0]
    order = ("norm_mix_g", "w_in", "conv_w", "conv_b", "conv_ln_g", "conv_ln_b", "w_conv_out", "q_norm_g", "w_uq", "kv_norm_g",
             "w_ukv", "w_mla_out", "w_out", "norm_xattn_g", "norm_mem_g", "w_xq", "w_xkv", "w_xo", "norm_mlp_g", "w_mlp1",
             "w_mlp2", "final_norm_g")
    res = [loss, grad_x[None]]
    for kind in range(4):
        res += [big[kind][k] if k in big[kind] else small[kind][k] for k in order]
    return tuple(res)
```

```python
import functools

import jax
import jax.numpy as jnp
import numpy as np
from jax import lax
from jax.experimental import pallas as pl
from jax.experimental.pallas import tpu as pltpu

F32, BF16 = jnp.float32, jnp.bfloat16
MESH = pl.DeviceIdType.MESH

N_DEV = 8
D = 1024
CC = D // 2
CW = 31
HALO = 32
NH = 8
NOPE, ROPE, VD = D // 16, D // 32, D // 16
QL, KL = 3 * D // 8, D // 4
HP = 128
XH, XD = 4, D // 8
DFF = 4 * D
EPS = 1e-6
THETA = 10000.0
MLA_SCALE = float((NOPE + ROPE) ** -0.5)
X_SCALE = float(XD ** -0.5)
NEG = -1e30
P_A, P_G, P_Q, P_KV, P_KR, P_GL, P_END = 0, CC, 2 * CC, 2 * CC + QL, 2 * CC + QL + KL, 2 * CC + QL + KL + HP, 2 * CC + QL + KL + HP + 2 * D

ADAM_LR, ADAM_B1, ADAM_B2, ADAM_EPS, ADAM_WD, ADAM_STEP = 0.001, 0.9, 0.999, 1e-08, 0.01, 10

ROW_TILE = 256
LIGHT_ROW_TILE = 512
ATT_TILE = 512
HG = 2
HG_FWD = 4
ATT_CHUNK = 32
CONV_ROWS = 128
DW_TILE = 2048
PACK_TILE = 1536
VMEM_LIMIT = 56 * 1024 * 1024


def _mm(a, w):
    return jnp.dot(a.astype(BF16), w, preferred_element_type=F32)


def _mm_nt(a, w):
    return lax.dot_general(a.astype(BF16), w, (((1,), (1,)), ((), ())), preferred_element_type=F32)


def _mm_tn(a, b):
    return lax.dot_general(a.astype(BF16), b.astype(BF16), (((0,), (0,)), ((), ())), preferred_element_type=F32)


def _rms(x, g):
    r = lax.rsqrt(jnp.mean(x * x, axis=-1, keepdims=True) + EPS)
    xh = x * r
    return xh * g, xh, r


def _rms_bwd(dy, xh, r, g):
    dxh = dy * g
    dx = r * (dxh - xh * jnp.mean(dxh * xh, axis=-1, keepdims=True))
    return dx, jnp.sum(dy * xh, axis=0, keepdims=True)


def _col_to_rows(col):
    return jnp.transpose(jnp.broadcast_to(col, (col.shape[0], HP)))[0:8, :]


def _sig(x):
    return 1.0 / (1.0 + jnp.exp(-x))


def _coords():
    return lax.axis_index("x"), lax.axis_index("y"), lax.axis_index("c")


def _exchange_ops(src_ref, mode, dst_ref, send_sems, recv_sems, local_sem):
    x, y, c = _coords()
    me = 4 * x + 2 * y + c
    owned = isinstance(mode, tuple)

    def owner_slot(d):
        return sum(jnp.where(d == o, n, 0) for n, o in enumerate(mode))

    def is_owner(d):
        return functools.reduce(jnp.logical_or, [d == o for o in mode])

    def src(d):
        return src_ref if mode is True else src_ref.at[owner_slot(d) if owned else d]

    local = (pltpu.make_async_copy(src(me), dst_ref.at[me], local_sem), is_owner(me) if owned else True)
    remote = []
    for j in range(1, N_DEV):
        px, py, pc = (1 - x if j & 4 else x), (1 - y if j & 2 else y), (1 - c if j & 1 else c)
        d = 4 * px + 2 * py + pc
        cp = pltpu.make_async_remote_copy(src_ref=src(d), dst_ref=dst_ref.at[me], send_sem=send_sems.at[j - 1],
                                          recv_sem=recv_sems.at[j - 1], device_id=(px, py, pc), device_id_type=MESH)
        remote.append((cp, is_owner(d) if owned else True, is_owner(me) if owned else True))
    return local, remote


def _when(cond, fn):
    if cond is True:
        fn()
    else:
        pl.when(cond)(fn)


def _exchange_hook(first, last, mode, refs):
    @pl.when(first)
    def _():
        (local, here), remote = _exchange_ops(refs[0], mode, *refs[1:])
        _when(here, local.start)
        for cp, sends, _ in remote:
            _when(sends, cp.start)

    @pl.when(last)
    def _():
        (local, here), remote = _exchange_ops(refs[0], mode, *refs[1:])
        for cp, sends, receives in remote:
            _when(sends, cp.wait_send)
            _when(receives, cp.wait_recv)
        _when(here, local.wait)


def _exchange_shapes(exchange):
    if exchange is None:
        return [], [], [], []
    arr, mode = exchange
    shape = (N_DEV,) + (arr.shape if mode is True else arr.shape[1:])
    any_spec = pl.BlockSpec(memory_space=pl.ANY)
    sems = [pltpu.SemaphoreType.DMA((N_DEV - 1,)), pltpu.SemaphoreType.DMA((N_DEV - 1,)), pltpu.SemaphoreType.DMA]
    return [any_spec], [any_spec], [jax.ShapeDtypeStruct(shape, arr.dtype)], sems


def _row_call(name, fn, n, rows, consts, outs, accs=(), scratch=(), exchange=None):
    def row_spec(shape, mode):
        r = shape[-2] // n
        if mode == "cur":
            f = lambda i: i
        elif mode == "prev":
            f = lambda i: jnp.maximum(i - 1, 0)
        else:
            f = lambda i: jnp.minimum(i + 1, n - 1)
        if len(shape) == 2:
            return pl.BlockSpec((r, shape[1]), lambda i: (f(i), 0))
        return pl.BlockSpec((shape[0], r, shape[2]), lambda i: (0, f(i), 0))

    def lane_spec(shape):
        return pl.BlockSpec((shape[0], shape[1], shape[2] // n), lambda i: (0, 0, i))

    def whole_spec(shape, single):
        nd = len(shape)
        if single:
            return pl.BlockSpec(shape, lambda i: (0,) * nd, pipeline_mode=pl.Buffered(1))
        return pl.BlockSpec(shape, lambda i: (0,) * nd)

    nr, nc, no, na, ns = len(rows), len(consts), len(outs), len(accs), len(scratch)
    exchanges = [] if exchange is None else (exchange if isinstance(exchange, list) else [exchange])
    shapes = [_exchange_shapes(e) for e in exchanges]
    x_in, x_out, x_shape, x_sems = (sum((s[part] for s in shapes), []) for part in range(4))
    nx = len(exchanges)

    def body(*refs):
        i = pl.program_id(0)
        row_refs, const_refs = refs[:nr], refs[nr:nr + nc]
        o0 = nr + nc + nx
        out_refs, acc_refs = refs[o0:o0 + no], refs[o0 + no:o0 + no + na]
        s0 = o0 + no + na + nx
        for e in range(nx):
            sems = refs[s0 + ns + 3 * e:s0 + ns + 3 * e + 3]
            _exchange_hook(i == 0, i == n - 1, exchanges[e][1], (refs[nr + nc + e], refs[o0 + no + na + e]) + tuple(sems))
        if na:
            @pl.when(i == 0)
            def _():
                for a in acc_refs:
                    a[...] = jnp.zeros(a.shape, a.dtype)
        fn(i, row_refs, const_refs, out_refs, acc_refs, refs[s0:s0 + ns])

    res = pl.pallas_call(
        body, name=name, grid=(n,),
        in_specs=[row_spec(a.shape, m) for a, m in rows] + [whole_spec(c.shape, True) for c in consts] + x_in,
        out_specs=[lane_spec(o[0]) if len(o) == 3 else row_spec(o[0], "cur") for o in outs]
        + [whole_spec(s, False) for s, _ in accs] + x_out,
        out_shape=[jax.ShapeDtypeStruct(o[0], o[1]) for o in list(outs) + list(accs)] + x_shape,
        scratch_shapes=list(scratch) + x_sems,
        compiler_params=pltpu.CompilerParams(dimension_semantics=("arbitrary",), vmem_limit_bytes=VMEM_LIMIT),
    )(*[a for a, _ in rows], *consts, *[e[0] for e in exchanges])
    return list(res)


def _in_proj(x, g_mix, wp, n):
    t = x.shape[0]

    def fn(i, rows, consts, outs, accs, scr):
        g, w = consts
        u, _, _ = _rms(rows[0][...], g[...])
        ub = u.astype(BF16)
        outs[0][...] = ub
        for k, (lo, hi) in enumerate(((P_A, P_G), (P_G, P_Q), (P_Q, P_KV), (P_KV, P_KR), (P_KR, P_GL), (P_GL, P_END))):
            outs[1 + k][...] = _mm(ub, w[:, lo:hi])

    outs = [((t, D), BF16), ((t, CC), F32), ((t, CC), F32), ((t, QL), F32), ((t, KL), F32), ((t, HP), F32), ((t, 2 * D), F32)]
    return _row_call("in_proj", fn, n, [(x, "cur")], [g_mix, wp], outs)


def _fill_glu_window(i, a, gt, ap, gtp, zz):
    r = a.shape[0]
    zp = ap[r - HALO:, :] * _sig(gtp[r - HALO:, :])
    zz[0:HALO, :] = jnp.where(i > 0, zp, 0.0)
    zz[HALO:, :] = a[...] * _sig(gt[...])


def _shift_copies(buf, sh):
    rows = buf.shape[0]
    for s in range(8):
        sh[s, 0:rows - s, :] = buf[s:rows, :]


def _window(sh, o, base, rb, ln):
    return sh[o % 8, base + o - o % 8:base + o - o % 8 + rb, ln]


def _windows(buf, base, ln, rb, offsets):
    for s in range(8):
        group = [o for o in offsets if o % 8 == s]
        if group:
            shifted = buf[base + s:base + max(group) + rb, ln]
            for o in group:
                yield o, shifted[o - s:o - s + rb]


def _conv_branch(a, gt, cw, cb, lng, lnb, wco, n):
    t = a.shape[0]
    r = t // n

    def fn(i, rows, consts, outs, accs, scr):
        w, b, lg, lb, wo = consts
        zz, zsh = scr
        _fill_glu_window(i, rows[0], rows[1], rows[2], rows[3], zz)
        _shift_copies(zz, zsh)
        rb = min(CONV_ROWS, r)
        for c in range(CC // HP):
            ln = slice(c * HP, (c + 1) * HP)
            for base in range(0, r, rb):
                acc = jnp.zeros((rb, HP), F32)
                for j in range(CW):
                    acc = acc + w[j:j + 1, ln] * _window(zsh, HALO - (CW - 1) + j, base, rb, ln)
                outs[0][base:base + rb, ln] = acc + b[:, ln]
        zc = outs[0][...]
        mu = jnp.mean(zc, axis=-1, keepdims=True)
        dlt = zc - mu
        rs = lax.rsqrt(jnp.mean(dlt * dlt, axis=-1, keepdims=True) + EPS)
        zn = dlt * rs * lg[...] + lb[...]
        outs[1][...] = _mm(zn * _sig(zn), wo[...])

    return _row_call("conv_branch", fn, n, [(a, "cur"), (gt, "cur"), (a, "prev"), (gt, "prev")],
                     [cw, cb, lng, lnb, wco], [((t, CC), F32), ((t, D), F32)],
                     scratch=[pltpu.VMEM((r + HALO, CC), F32), pltpu.VMEM((8, r + HALO, CC), F32)])


def _rope(v, c, sa, sb):
    return v * c + pltpu.roll(v, HP - ROPE // 2, 1) * sa + pltpu.roll(v, ROPE // 2, 1) * sb


def _rope_bwd(dv, c, sa, sb):
    return dv * c + pltpu.roll(dv * sa, ROPE // 2, 1) + pltpu.roll(dv * sb, HP - ROPE // 2, 1)


def _mla_prep(cq, ckv, krp, tc, tsa, tsb, gq, wuq, gkv, wk, wv, n):
    t = cq.shape[0]

    def fn(i, rows, consts, outs, accs, scr):
        g_q, w_q, g_kv, w_k, w_v = consts
        c, sa, sb = rows[3][...], rows[4][...], rows[5][...]
        cqn = _rms(rows[0][...], g_q[...])[0].astype(BF16)
        ckvn = _rms(rows[1][...], g_kv[...])[0].astype(BF16)
        outs[3][...] = cqn
        outs[4][...] = ckvn
        krr = _rope(rows[2][...], c, sa, sb)
        for h in range(NH):
            ln = slice(h * HP, (h + 1) * HP)
            outs[0][:, ln] = (_rope(_mm(cqn, w_q[:, ln]), c, sa, sb) * MLA_SCALE).astype(BF16)
            outs[1][:, ln] = (_mm(ckvn, w_k[:, ln]) + krr).astype(BF16)
        vv = _mm(ckvn, w_v[...])
        lane = lax.broadcasted_iota(jnp.int32, vv.shape, 1)
        outs[2][...] = jnp.where((lane & (HP - 1)) == VD, 1.0, vv).astype(BF16)

    outs = [((t, NH * HP), BF16)] * 3 + [((t, QL), BF16), ((t, KL), BF16)]
    return _row_call("mla_prep", fn, n, [(a, "cur") for a in (cq, ckv, krp, tc, tsa, tsb)], [gq, wuq, gkv, wk, wv], outs)


def _chunk_mask(c, tq, transposed, rows=ATT_CHUNK):
    row = lax.broadcasted_iota(jnp.int32, (rows, tq), 0) + c * rows
    col = lax.broadcasted_iota(jnp.int32, (rows, tq), 1)
    return (row <= col) if transposed else (col <= row)


def _flash_hook(exchange, nh, nq, refs):
    if exchange is not None:
        h, i = pl.program_id(0), pl.program_id(1)
        _exchange_hook((h == 0) & (i == 0), (h == nh - 1) & (i == nq - 1), exchange[1], refs)


def _head_lanes(g):
    return slice(g * HP, (g + 1) * HP)


def _flash_fwd(q, k, v, exchange=None):
    t = q.shape[0]
    tq = min(ATT_TILE, t)
    nq = t // tq
    x_in, x_out, x_shape, x_sems = _exchange_shapes(exchange)

    def body(q_ref, k_ref, v_ref, *rest):
        o_ref, lse_ref = rest[len(x_in):len(x_in) + 2]
        _flash_hook(exchange, NH // HG_FWD, nq, rest[:len(x_in)] + rest[len(x_in) + 2:])
        i = pl.program_id(1)

        def step(j, carry, masked):
            at = pl.ds(pl.multiple_of(j * tq, tq), tq)
            out = []
            for g in range(HG_FWD):
                m, acc = carry[g]
                s = _mm_nt(q_ref[:, _head_lanes(g)], k_ref[at, _head_lanes(g)])
                if masked:
                    s = jnp.where(_chunk_mask(0, tq, False, tq), s, NEG)
                m_new = jnp.maximum(m, jnp.max(s, axis=-1, keepdims=True))
                out.append((m_new, jnp.exp(m - m_new) * acc + _mm(jnp.exp(s - m_new), v_ref[at, _head_lanes(g)])))
            return tuple(out)

        init = tuple((jnp.full((tq, 1), NEG, F32), jnp.zeros((tq, HP), F32)) for _ in range(HG_FWD))
        carry = lax.fori_loop(0, i, lambda j, c: step(j, c, False), init)
        for g, (m, acc) in enumerate(step(i, carry, True)):
            lane = lax.broadcasted_iota(jnp.int32, acc.shape, 1)
            l = jnp.sum(jnp.where(lane == VD, acc, 0.0), axis=-1, keepdims=True)
            o_ref[:, _head_lanes(g)] = acc / l
            lse_ref[g] = _col_to_rows(m + jnp.log(l))

    wide = HG_FWD * HP
    return pl.pallas_call(
        body, name="flash_fwd", grid=(NH // HG_FWD, nq),
        in_specs=[pl.BlockSpec((tq, wide), lambda h, i: (i, h)), pl.BlockSpec((t, wide), lambda h, i: (0, h)),
                  pl.BlockSpec((t, wide), lambda h, i: (0, h))] + x_in,
        out_specs=[pl.BlockSpec((tq, wide), lambda h, i: (i, h)), pl.BlockSpec((HG_FWD, 8, tq), lambda h, i: (h, 0, i))] + x_out,
        out_shape=[jax.ShapeDtypeStruct((t, NH * HP), F32), jax.ShapeDtypeStruct((NH, 8, t), F32)] + x_shape,
        scratch_shapes=x_sems,
        compiler_params=pltpu.CompilerParams(dimension_semantics=("arbitrary", "arbitrary"), vmem_limit_bytes=VMEM_LIMIT),
    )(q, k, v, *([exchange[0]] if x_in else []))


def _flash_bwd(q, k, v, do, lse_row, delta_row, exchange=None):
    t = q.shape[0]
    tq = min(ATT_TILE, t)
    nq = t // tq
    x_in, x_out, x_shape, x_sems = _exchange_shapes(exchange)

    def body(q_ref, k_ref, v_ref, do_ref, lse_ref, dl_ref, *rest):
        dk_ref, dv_ref, dq_ref = rest[len(x_in):len(x_in) + 3]
        st_scr, dpt_scr, pt_scr, dst_scr = rest[len(rest) - 4:]
        _flash_hook(exchange, NH // HG, nq, rest[:len(x_in)] + rest[len(x_in) + 3:len(rest) - 4])
        j = pl.program_id(1)
        dk_ref[...] = jnp.zeros(dk_ref.shape, F32)
        dv_ref[...] = jnp.zeros(dv_ref.shape, F32)

        @pl.when(j == 0)
        def _():
            dq_ref[...] = jnp.zeros(dq_ref.shape, F32)

        def step(i, masked):
            at = pl.ds(pl.multiple_of(i * tq, tq), tq)
            for g in range(HG):
                st_scr[g] = _mm_nt(k_ref[:, _head_lanes(g)], q_ref[at, _head_lanes(g)])
                dpt_scr[g] = _mm_nt(v_ref[:, _head_lanes(g)], do_ref[at, _head_lanes(g)])
            for g in range(HG):
                lse_i, dl_i = lse_ref[g, 0:1, at], dl_ref[g, 0:1, at]
                for c in range(tq // ATT_CHUNK):
                    rows = slice(c * ATT_CHUNK, (c + 1) * ATT_CHUNK)
                    st = st_scr[g, rows, :]
                    if masked:
                        st = jnp.where(_chunk_mask(c, tq, True), st, NEG)
                    pt = jnp.exp(st - lse_i)
                    pt_scr[g, rows, :] = pt.astype(BF16)
                    dst_scr[g, rows, :] = (pt * (dpt_scr[g, rows, :] - dl_i)).astype(BF16)
            for g in range(HG):
                dv_ref[:, _head_lanes(g)] += _mm(pt_scr[g], do_ref[at, _head_lanes(g)])
                dk_ref[:, _head_lanes(g)] += _mm(dst_scr[g], q_ref[at, _head_lanes(g)])
                dq_ref[at, _head_lanes(g)] += _mm_tn(dst_scr[g], k_ref[:, _head_lanes(g)])

        step(j, True)
        lax.fori_loop(j + 1, nq, lambda i, c: (step(i, False), c)[1], 0)

    wide = HG * HP
    blk = pl.BlockSpec((tq, wide), lambda h, j: (j, h))
    whole = pl.BlockSpec((t, wide), lambda h, j: (0, h))
    row = pl.BlockSpec((HG, 8, t), lambda h, j: (h, 0, 0))
    x_sems = x_sems + [pltpu.VMEM((HG, tq, tq), F32), pltpu.VMEM((HG, tq, tq), F32), pltpu.VMEM((HG, tq, tq), BF16),
                       pltpu.VMEM((HG, tq, tq), BF16)]
    return pl.pallas_call(
        body, name="flash_bwd", grid=(NH // HG, nq), in_specs=[whole, blk, blk, whole, row, row] + x_in,
        out_specs=[blk, blk, whole] + x_out, out_shape=[jax.ShapeDtypeStruct((t, NH * HP), F32)] * 3 + x_shape, scratch_shapes=x_sems,
        compiler_params=pltpu.CompilerParams(dimension_semantics=("arbitrary", "arbitrary"), vmem_limit_bytes=VMEM_LIMIT),
    )(q, k, v, do, lse_row, delta_row, *([exchange[0]] if x_in else []))


def _mem_kv(mem, g_mem, wxkv):
    m = mem.shape[0]

    def fn(i, rows, consts, outs, accs, scr):
        mn = _rms(rows[0][...], consts[0][...])[0].astype(BF16)
        outs[0][...] = mn
        outs[1][...] = _mm(mn, consts[1][:, 0:XH * XD]).astype(BF16)
        outs[2][...] = _mm(mn, consts[1][:, XH * XD:]).astype(BF16)

    return _row_call("mem_kv", fn, 1, [(mem, "cur")], [g_mem, wxkv], [((m, D), BF16), ((m, XH * XD), BF16), ((m, XH * XD), BF16)])


def _merge_xattn(o, gl, conv_out, x, wmo, wo, g_x, wxq, kx, vx, wxo, n):
    t = x.shape[0]

    def fn(i, rows, consts, outs, accs, scr):
        w_mo, w_o, g, w_xq, k_x, v_x, w_xo = consts
        ob = rows[0][...].astype(BF16)
        outs[7][...] = ob
        mla = _mm(ob, w_mo[...])
        outs[0][...] = mla
        merged = (_sig(rows[1][:, 0:D]) * rows[2][...] + _sig(rows[1][:, D:]) * mla).astype(BF16)
        outs[1][...] = merged
        h1 = rows[3][...] + _mm(merged, w_o[...])
        outs[2][...] = h1
        u1 = _rms(h1, g[...])[0].astype(BF16)
        outs[3][...] = u1
        qx = (_mm(u1, w_xq[...]) * X_SCALE).astype(BF16)
        outs[4][...] = qx
        for h in range(XH):
            ln = slice(h * XD, (h + 1) * XD)
            s = _mm_nt(qx[:, ln], k_x[:, ln])
            e = jnp.exp(s - jnp.max(s, axis=-1, keepdims=True))
            p = e / jnp.sum(e, axis=-1, keepdims=True)
            outs[5][:, ln] = _mm(p, v_x[:, ln]).astype(BF16)
        outs[6][...] = h1 + _mm(outs[5][...], w_xo[...])

    outs = [((t, D), F32), ((t, D), BF16), ((t, D), F32), ((t, D), BF16), ((t, XH * XD), BF16), ((t, XH * XD), BF16),
            ((t, D), F32), ((t, NH * HP), BF16)]
    return _row_call("merge_xattn", fn, n, [(a, "cur") for a in (o, gl, conv_out, x)], [wmo, wo, g_x, wxq, kx, vx, wxo], outs)


def _mlp_loss(h2, target, g_mlp, w1, w2, g_fin, n):
    t = h2.shape[0]
    nck = DFF // D

    def fn(i, rows, consts, outs, accs, scr):
        g_m, w_1, w_2, g_f = consts
        h = rows[0][...]
        u2, xh2, r2 = _rms(h, g_m[...])
        ub = u2.astype(BF16)
        outs[0][...] = ub
        h3 = h
        a1 = []
        for c in range(nck):
            ck = slice(c * D, (c + 1) * D)
            a = _mm(ub, w_1[:, ck])
            a1.append(a)
            rl = jnp.maximum(a, 0.0)
            rb = (rl * rl).astype(BF16)
            outs[1][:, ck] = rb
            h3 = h3 + _mm(rb, w_2[ck, :])
        y, xh3, r3 = _rms(h3, g_f[...])
        err = y - rows[1][...]
        accs[0][...] += jnp.sum(jnp.sum(err * err, axis=1, keepdims=True), axis=0, keepdims=True) * (0.5 / D)
        dh3, dgf = _rms_bwd(err * (1.0 / D), xh3, r3, g_f[...])
        accs[1][...] += dgf
        db = dh3.astype(BF16)
        outs[3][...] = db
        du2 = jnp.zeros_like(h)
        for c in range(nck):
            ck = slice(c * D, (c + 1) * D)
            da = (_mm_nt(db, w_2[ck, :]) * (2.0 * jnp.maximum(a1[c], 0.0))).astype(BF16)
            outs[2][:, ck] = da
            du2 = du2 + _mm_nt(da, w_1[:, ck])
        dx2, dgm = _rms_bwd(du2, xh2, r2, g_m[...])
        accs[2][...] += dgm
        outs[4][...] = dh3 + dx2

    outs = [((t, D), BF16), ((t, DFF), BF16), ((t, DFF), BF16), ((t, D), BF16), ((t, D), F32)]
    accs = [((1, 1), F32), ((1, D), F32), ((1, D), F32)]
    return _row_call("mlp_loss", fn, n, [(h2, "cur"), (target, "cur")], [g_mlp, w1, w2, g_fin], outs, accs)


def _merge_xattn_bwd(dh2, h1, qx, gl, conv_out, mla, o, wxo, kx, vx, wxq, g_x, wo, wmo, n):
    t = dh2.shape[0]
    m = kx.shape[0]

    def fn(i, rows, consts, outs, accs, scr):
        w_xo, k_x, v_x, w_xq, g, w_o, w_mo = consts
        d2 = rows[0][...]
        d2b = d2.astype(BF16)
        outs[7][...] = d2b
        dox = _mm_nt(d2b, w_xo[...]).astype(BF16)
        q = rows[2][...]
        dq = []
        for h in range(XH):
            ln = slice(h * XD, (h + 1) * XD)
            qh, kh, vh, doh = q[:, ln], k_x[:, ln], v_x[:, ln], dox[:, ln]
            s = _mm_nt(qh, kh)
            e = jnp.exp(s - jnp.max(s, axis=-1, keepdims=True))
            p = e / jnp.sum(e, axis=-1, keepdims=True)
            dp = _mm_nt(doh, vh)
            ds = p * (dp - jnp.sum(p * dp, axis=-1, keepdims=True))
            dq.append(_mm(ds, kh) * X_SCALE)
            accs[0][:, ln] += _mm_tn(ds, qh)
            accs[1][:, ln] += _mm_tn(p, doh)
        dqx = jnp.concatenate(dq, axis=1).astype(BF16)
        outs[0][...] = dqx
        _, xh1, r1 = _rms(rows[1][...], g[...])
        dx1, dg = _rms_bwd(_mm_nt(dqx, w_xq[...]), xh1, r1, g[...])
        accs[2][...] += dg
        d1 = d2 + dx1
        outs[1][...] = d1
        d1b = d1.astype(BF16)
        outs[8][...] = d1b
        dm = _mm_nt(d1b, w_o[...])
        g0, g1 = _sig(rows[3][:, 0:D]), _sig(rows[3][:, D:])
        outs[2][:, 0:D] = (dm * rows[4][...] * g0 * (1.0 - g0)).astype(BF16)
        outs[2][:, D:] = (dm * rows[5][...] * g1 * (1.0 - g1)).astype(BF16)
        outs[3][...] = (dm * g0).astype(BF16)
        dmla = (dm * g1).astype(BF16)
        outs[4][...] = dmla
        do = _mm_nt(dmla, w_mo[...])
        outs[5][...] = do.astype(BF16)
        prod = do * rows[6][...]
        for h in range(NH):
            outs[6][h] = _col_to_rows(jnp.sum(prod[:, h * HP:(h + 1) * HP], axis=-1, keepdims=True))

    outs = [((t, XH * XD), BF16), ((t, D), F32), ((t, 2 * D), BF16), ((t, D), BF16), ((t, D), BF16), ((t, NH * HP), BF16),
            ((NH, 8, t), F32, "lanes"), ((t, D), BF16), ((t, D), BF16)]
    accs = [((m, XH * XD), F32), ((m, XH * XD), F32), ((1, D), F32)]
    return _row_call("merge_xattn_bwd", fn, n, [(a, "cur") for a in (dh2, h1, qx, gl, conv_out, mla, o)],
                     [wxo, kx, vx, wxq, g_x, wo, wmo], outs, accs)


def _mla_prep_bwd(dq, dk, dv, cq, ckv, tc, tsa, tsb, gq, wuq, gkv, wk, wv, n):
    t = dq.shape[0]

    def fn(i, rows, consts, outs, accs, scr):
        g_q, w_q, g_kv, w_k, w_v = consts
        c, sa, sb = rows[5][...], rows[6][...], rows[7][...]
        dkr = jnp.zeros((rows[0].shape[0], HP), F32)
        for h in range(NH):
            ln = slice(h * HP, (h + 1) * HP)
            outs[0][:, ln] = _rope_bwd(rows[0][:, ln] * MLA_SCALE, c, sa, sb).astype(BF16)
            dkr = dkr + rows[1][:, ln]
        lane = lax.broadcasted_iota(jnp.int32, dkr.shape, 1)
        outs[5][...] = jnp.where((lane >= NOPE) & (lane < NOPE + ROPE), _rope_bwd(dkr, c, sa, sb), 0.0).astype(BF16)
        dkb, dvb = rows[1][...].astype(BF16), rows[2][...].astype(BF16)
        outs[1][...] = dkb
        outs[2][...] = dvb
        _, xq, rq = _rms(rows[3][...], g_q[...])
        dcq, dgq = _rms_bwd(_mm_nt(outs[0][...], w_q[...]), xq, rq, g_q[...])
        outs[3][...] = dcq.astype(BF16)
        accs[0][...] += dgq
        _, xk, rk = _rms(rows[4][...], g_kv[...])
        dckv, dgk = _rms_bwd(_mm_nt(dkb, w_k[...]) + _mm_nt(dvb, w_v[...]), xk, rk, g_kv[...])
        outs[4][...] = dckv.astype(BF16)
        accs[1][...] += dgk

    outs = [((t, NH * HP), BF16)] * 3 + [((t, QL), BF16), ((t, KL), BF16), ((t, HP), BF16)]
    return _row_call("mla_prep_bwd", fn, n, [(a, "cur") for a in (dq, dk, dv, cq, ckv, tc, tsa, tsb)],
                     [gq, wuq, gkv, wk, wv], outs, [((1, QL), F32), ((1, KL), F32)])


def _conv_out_bwd(dco, zc, wco, lng, lnb, n):
    t = zc.shape[0]

    def fn(i, rows, consts, outs, accs, scr):
        wo, lg, lb = consts
        z = rows[1][...]
        mu = jnp.mean(z, axis=-1, keepdims=True)
        dlt = z - mu
        rs = lax.rsqrt(jnp.mean(dlt * dlt, axis=-1, keepdims=True) + EPS)
        xh = dlt * rs
        zn = xh * lg[...] + lb[...]
        sg = _sig(zn)
        outs[0][...] = (zn * sg).astype(BF16)
        dzn = _mm_nt(rows[0][...], wo[...]) * (sg * (1.0 + zn * (1.0 - sg)))
        accs[0][...] += jnp.sum(dzn * xh, axis=0, keepdims=True)
        accs[1][...] += jnp.sum(dzn, axis=0, keepdims=True)
        dxh = dzn * lg[...]
        dzc = rs * (dxh - jnp.mean(dxh, axis=-1, keepdims=True) - xh * jnp.mean(dxh * xh, axis=-1, keepdims=True))
        outs[1][...] = dzc
        accs[2][...] += jnp.sum(dzc, axis=0, keepdims=True)

    return _row_call("conv_out_bwd", fn, n, [(dco, "cur"), (zc, "cur")], [wco, lng, lnb],
                     [((t, CC), BF16), ((t, CC), F32)], [((1, CC), F32)] * 3)


def _conv_glu_bwd(dzc, a, gt, cw, n):
    t = a.shape[0]
    r = t // n

    def fn(i, rows, consts, outs, accs, scr):
        w = consts[0]
        zz, dd = scr
        _fill_glu_window(i, rows[2], rows[3], rows[4], rows[5], zz)
        dd[0:r, :] = rows[0][...]
        dd[r:, :] = jnp.where(i < n - 1, rows[1][0:HALO, :], 0.0)
        rb = min(CONV_ROWS, r)
        for c in range(CC // HP):
            ln = slice(c * HP, (c + 1) * HP)
            for base in range(0, r, rb):
                here = slice(base, base + rb)
                dcur = dd[here, ln]
                for o, win in _windows(zz, base, ln, rb, [HALO - (CW - 1) + j for j in range(CW)]):
                    j = o - (HALO - (CW - 1))
                    accs[0][j:j + 1, ln] += jnp.sum(dcur * win, axis=0, keepdims=True)
                acc = jnp.zeros((rb, HP), F32)
                for o, win in _windows(dd, base, ln, rb, [CW - 1 - j for j in range(CW)]):
                    j = CW - 1 - o
                    acc = acc + w[j:j + 1, ln] * win
                sg = _sig(rows[3][here, ln])
                outs[0][here, ln] = (acc * sg).astype(BF16)
                outs[0][here, CC + c * HP:CC + (c + 1) * HP] = (acc * rows[2][here, ln] * sg * (1.0 - sg)).astype(BF16)

    return _row_call("conv_glu_bwd", fn, n, [(dzc, "cur"), (dzc, "next"), (a, "cur"), (gt, "cur"), (a, "prev"), (gt, "prev")],
                     [cw], [((t, 2 * CC), BF16)], [((HALO, CC), F32)],
                     scratch=[pltpu.VMEM((r + HALO, CC), F32), pltpu.VMEM((r + HALO, CC), F32)])


def _in_proj_bwd(pieces, dh1, x, wp, g_mix, n, exchange):
    t = x.shape[0]
    offs = (P_A, P_Q, P_KV, P_KR, P_GL, P_END)

    def fn(i, rows, consts, outs, accs, scr):
        w, g = consts
        du = jnp.zeros((rows[0].shape[0], D), F32)
        for k in range(5):
            du = du + _mm_nt(rows[k][...], w[:, offs[k]:offs[k + 1]])
        _, xh, r = _rms(rows[6][...], g[...])
        dx, dg = _rms_bwd(du, xh, r, g[...])
        accs[0][...] += dg
        outs[0][...] = rows[5][...] + dx

    return _row_call("in_proj_bwd", fn, n, [(a, "cur") for a in list(pieces) + [dh1, x]], [wp, g_mix],
                     [((t, D), F32)], [((1, D), F32)], exchange=exchange)


def _mem_bwd(mem, dkx, dvx, g_mem, wxkv):
    m = mem.shape[0]

    def fn(i, rows, consts, outs, accs, scr):
        g, w = consts
        dkv = jnp.concatenate([rows[1][...], rows[2][...]], axis=1).astype(BF16)
        outs[0][...] = dkv
        _, xh, _ = _rms(rows[0][...], g[...])
        accs[0][...] += jnp.sum(_mm_nt(dkv, w[...]) * xh, axis=0, keepdims=True)

    return _row_call("mem_bwd", fn, 1, [(mem, "cur"), (dkx, "cur"), (dvx, "cur")], [g_mem, wxkv],
                     [((m, 2 * XH * XD), BF16)], [((1, D), F32)])


def _dw(name, xs, dy):
    t, k = xs.shape
    nn = dy.shape[1]
    tk, tn, tt = min(k, 1024), min(nn, 2048), min(t, DW_TILE)

    def body(x_ref, dy_ref, o_ref):
        @pl.when(pl.program_id(2) == 0)
        def _():
            o_ref[...] = jnp.zeros(o_ref.shape, F32)
        o_ref[...] += lax.dot_general(x_ref[...], dy_ref[...], (((0,), (0,)), ((), ())), preferred_element_type=F32)

    return pl.pallas_call(
        body, name=name, grid=(k // tk, nn // tn, t // tt),
        in_specs=[pl.BlockSpec((tt, tk), lambda a, b, c: (c, a)), pl.BlockSpec((tt, tn), lambda a, b, c: (c, b))],
        out_specs=pl.BlockSpec((tk, tn), lambda a, b, c: (a, b)),
        out_shape=jax.ShapeDtypeStruct((k, nn), F32),
        compiler_params=pltpu.CompilerParams(dimension_semantics=("arbitrary", "arbitrary", "arbitrary"), vmem_limit_bytes=VMEM_LIMIT),
    )(xs, dy)


def _dw_shared(name, xs, dys):
    t, k = xs.shape
    tt = min(t, DW_TILE)
    nd = len(dys)

    def body(x_ref, *refs):
        @pl.when(pl.program_id(0) == 0)
        def _():
            for o_ref in refs[nd:]:
                o_ref[...] = jnp.zeros(o_ref.shape, F32)
        xb = x_ref[...]
        for dy_ref, o_ref in zip(refs[:nd], refs[nd:]):
            o_ref[...] += lax.dot_general(xb, dy_ref[...], (((0,), (0,)), ((), ())), preferred_element_type=F32)

    return pl.pallas_call(
        body, name=name, grid=(t // tt,),
        in_specs=[pl.BlockSpec((tt, k), lambda c: (c, 0))] + [pl.BlockSpec((tt, d.shape[1]), lambda c: (c, 0)) for d in dys],
        out_specs=[pl.BlockSpec((k, d.shape[1]), lambda c: (0, 0)) for d in dys],
        out_shape=[jax.ShapeDtypeStruct((k, d.shape[1]), F32) for d in dys],
        compiler_params=pltpu.CompilerParams(dimension_semantics=("arbitrary",), vmem_limit_bytes=VMEM_LIMIT),
    )(xs, *dys)


def _all_gather_packed(shard):
    rws = shard.shape[0]

    def body(x_ref, out_ref, send_sems, recv_sems, local_sem):
        x, y, c = _coords()
        me, sibling = (x, y, c), (x, y, 1 - c)
        chips = [(1 - x, y), (x, 1 - y), (1 - x, 1 - y)]

        def slot(px, py, pc):
            return out_ref.at[4 * px + 2 * py + pc]

        def copy(k, block, to, src=None):
            return pltpu.make_async_remote_copy(
                src_ref=slot(*block) if src is None else src, dst_ref=slot(*block),
                send_sem=send_sems.at[k], recv_sem=recv_sems.at[k], device_id=to, device_id_type=MESH)

        mine = pltpu.make_async_copy(x_ref, slot(*me), local_sem)
        mine.start()
        first = [copy(0, me, sibling, src=x_ref)] + [copy(1 + j, me, (*chip, c), src=x_ref) for j, chip in enumerate(chips)]
        for cp in first:
            cp.start()
        passed = [copy(4 + j, (*chip, c), sibling) for j, chip in enumerate(chips)]
        for j, chip in enumerate(chips):
            copy(1 + j, (*chip, c), me).wait_recv()
            passed[j].start()
        copy(0, sibling, me).wait_recv()
        for j, chip in enumerate(chips):
            copy(4 + j, (*chip, 1 - c), me).wait_recv()
        for cp in first + passed:
            cp.wait_send()
        mine.wait()

    return pl.pallas_call(
        body, name="all_gather_weights",
        out_shape=jax.ShapeDtypeStruct((N_DEV, rws, 128), shard.dtype),
        in_specs=[pl.BlockSpec(memory_space=pl.ANY)], out_specs=pl.BlockSpec(memory_space=pl.ANY),
        scratch_shapes=[pltpu.SemaphoreType.DMA((7,)), pltpu.SemaphoreType.DMA((7,)), pltpu.SemaphoreType.DMA],
    )(shard)


def _adam(w, g, m, v):
    m = ADAM_B1 * m + (1.0 - ADAM_B1) * g
    v = ADAM_B2 * v + (1.0 - ADAM_B2) * (g * g)
    m_hat = m / (1.0 - ADAM_B1 ** ADAM_STEP)
    v_hat = v / (1.0 - ADAM_B2 ** ADAM_STEP)
    return -ADAM_LR * (m_hat / (jnp.sqrt(v_hat) + ADAM_EPS) + ADAM_WD * w), m, v


def _small_allreduce_adam(part, w, m, v):
    shape = part.shape

    def body(p_ref, w_ref, m_ref, v_ref, g_ref, d_ref, nm_ref, nv_ref, buf, send_sems, recv_sems):
        x, y, c = _coords()
        me = 4 * x + 2 * y + c
        buf[0] = p_ref[...]
        cps = []
        for j in range(1, N_DEV):
            jx, jy, jc = j >> 2, (j >> 1) & 1, j & 1
            peer = (1 - x if jx else x, 1 - y if jy else y, 1 - c if jc else c)
            cps.append(pltpu.make_async_remote_copy(src_ref=p_ref, dst_ref=buf.at[j], send_sem=send_sems.at[j - 1],
                                                    recv_sem=recv_sems.at[j - 1], device_id=peer, device_id_type=MESH))
        for cp in cps:
            cp.start()
        for cp in cps:
            cp.wait()
        g = buf[me]
        for d in range(1, N_DEV):
            g = g + buf[d ^ me]
        g_ref[...] = g
        d_ref[...], nm_ref[...], nv_ref[...] = _adam(w_ref[...], g, m_ref[...], v_ref[...])

    vm = pl.BlockSpec(memory_space=pltpu.VMEM)
    return pl.pallas_call(
        body, name="small_allreduce_adam", out_shape=[jax.ShapeDtypeStruct(shape, F32)] * 4,
        in_specs=[vm] * 4, out_specs=[vm] * 4,
        scratch_shapes=[pltpu.VMEM((N_DEV,) + shape, F32), pltpu.SemaphoreType.DMA((7,)), pltpu.SemaphoreType.DMA((7,))],
    )(part, w, m, v)


def _sum_parts(name, parts):
    rws = parts.shape[1]
    tile = max(d for d in range(16, PACK_TILE + 1, 16) if rws % d == 0)

    def body(p_ref, g_ref):
        g = p_ref[0]
        for d in range(1, N_DEV):
            g = g + p_ref[d]
        g_ref[...] = g

    return pl.pallas_call(
        body, name=name, grid=(rws // tile,), in_specs=[pl.BlockSpec((N_DEV, tile, 128), lambda i: (0, i, 0))],
        out_specs=pl.BlockSpec((tile, 128), lambda i: (i, 0)), out_shape=jax.ShapeDtypeStruct((rws, 128), F32),
        compiler_params=pltpu.CompilerParams(dimension_semantics=("arbitrary",), vmem_limit_bytes=VMEM_LIMIT),
    )(parts)


def _adam_call(name, w, g, m, v):
    def body(w_ref, g_ref, m_ref, v_ref, d_ref, nm_ref, nv_ref):
        d_ref[...], nm_ref[...], nv_ref[...] = _adam(w_ref[...], g_ref[...], m_ref[...], v_ref[...])

    vm = pl.BlockSpec(memory_space=pltpu.VMEM)
    return pl.pallas_call(
        body, name=name, in_specs=[vm] * 4, out_specs=[vm] * 3, out_shape=[jax.ShapeDtypeStruct(w.shape, F32)] * 3,
        compiler_params=pltpu.CompilerParams(vmem_limit_bytes=VMEM_LIMIT),
    )(w, g, m, v)


ROW_SHARDED = ("w_out", "w_xq", "w_xkv", "w_mlp2")
SMALL = ("norm_mix_g", "conv_b", "conv_ln_g", "conv_ln_b", "q_norm_g", "kv_norm_g", "norm_xattn_g", "norm_mem_g", "norm_mlp_g", "final_norm_g")
GATHER_FIRST = ("w_in", "conv_w", "w_conv_out", "w_uq", "w_ukv")
GATHER_LATE = ("w_mla_out", "w_out", "w_xq", "w_xkv", "w_xo", "w_mlp1", "w_mlp2")
REDUCE_EARLY = ("w_mlp1", "w_mlp2", "conv_w", "w_conv_out", "w_mla_out", "w_out", "w_xq", "w_xkv", "w_xo", "w_in")
REDUCE_LAST = ("w_uq", "w_ukv")
ROW_ALIGN = 16


def _padded(k, nn):
    return -(-k // ROW_ALIGN) * ROW_ALIGN, -(-nn // 128) * 128


def _pack_rows(a):
    k, nn = a.shape[-2:]
    kp, np_ = _padded(k, nn)
    if (kp, np_) != (k, nn):
        a = jnp.pad(a, [(0, 0)] * (a.ndim - 2) + [(0, kp - k), (0, np_ - nn)])
    if np_ == 128:
        return a
    lead = a.shape[:-2]
    return jnp.swapaxes(a.reshape(lead + (kp, np_ // 128, 128)), -2, -3).reshape(lead + (kp * np_ // 128, 128))


def _unpack_rows(p, k, nn):
    kp, np_ = _padded(k, nn)
    if np_ != 128:
        lead = p.shape[:-2]
        p = jnp.swapaxes(p.reshape(lead + (np_ // 128, kp, 128)), -2, -3).reshape(lead + (kp, np_))
    return p[..., :k, :nn]


def _pack_group(arrays):
    return jnp.concatenate([_pack_rows(a) for a in arrays], axis=-2)


def _unpack_group(p, shapes):
    out, off = [], 0
    for k, nn in shapes:
        kp, np_ = _padded(k, nn)
        rws = kp * np_ // 128
        out.append(_unpack_rows(p[..., off:off + rws, :], k, nn))
        off += rws
    return out


def _tile_shaped(k, nn):
    return k % ROW_ALIGN == 0 and nn % 128 == 0


def _packed_to_full(name, seg, shard_shape):
    k, nn = shard_shape
    if not _tile_shaped(k, nn):
        stacked = _unpack_rows(seg, k, nn)
        return jnp.transpose(stacked, (1, 0, 2)).reshape(k, N_DEV * nn)
    a = seg.reshape(N_DEV, nn // 128, k, 128)
    if name in ROW_SHARDED:
        return jnp.transpose(a, (0, 2, 1, 3)).reshape(N_DEV * k, nn)
    return jnp.transpose(a, (2, 0, 1, 3)).reshape(k, N_DEV * nn)


def _full_to_packed(name, full, shard_shape):
    k, nn = shard_shape
    if not _tile_shaped(k, nn):
        return _pack_rows(jnp.transpose(full.reshape(k, N_DEV, nn), (1, 0, 2)))
    if name in ROW_SHARDED:
        a = jnp.transpose(full.reshape(N_DEV, k, nn // 128, 128), (0, 2, 1, 3))
    else:
        a = jnp.transpose(full.reshape(k, N_DEV, nn // 128, 128), (1, 2, 0, 3))
    return a.reshape(N_DEV, k * nn // 128, 128)


def _pack_small(vals):
    flat = jnp.concatenate([v.reshape(-1) for v in vals])
    rws = flat.shape[0] // 128
    return jnp.pad(flat, (0, (-(-rws // 8) * 8 - rws) * 128)).reshape(-1, 128)


def _unpack_small(p, sizes):
    flat = p.reshape(-1)
    out, off = [], 0
    for s in sizes:
        out.append(flat[off:off + s])
        off += s
    return out


def _head_pad(w, real, axis):
    shp = list(w.shape)
    shp[axis:axis + 1] = [NH, real]
    w = w.reshape(shp)
    pad = [(0, 0)] * w.ndim
    pad[axis + 1] = (0, HP - real)
    w = jnp.pad(w, pad)
    shp[axis:axis + 2] = [NH * HP]
    return w.reshape(shp)


def _head_unpad(w, lo, real, axis):
    shp = list(w.shape)
    shp[axis:axis + 1] = [NH, HP]
    w = lax.slice_in_dim(w.reshape(shp), lo, lo + real, axis=axis + 1)
    shp[axis:axis + 2] = [NH * real]
    return w.reshape(shp)


def kernel(x, mem, positions, norm_mix_g, w_in, conv_w, conv_b, conv_ln_g, conv_ln_b, w_conv_out, q_norm_g, w_uq, kv_norm_g, w_ukv, w_mla_out, w_out, norm_xattn_g, norm_mem_g, w_xq, w_xkv, w_xo, norm_mlp_g, w_mlp1, w_mlp2, final_norm_g, loss_target, m_norm_mix_g, m_w_in, m_conv_w, m_conv_b, m_conv_ln_g, m_conv_ln_b, m_w_conv_out, m_q_norm_g, m_w_uq, m_kv_norm_g, m_w_ukv, m_w_mla_out, m_w_out, m_norm_xattn_g, m_norm_mem_g, m_w_xq, m_w_xkv, m_w_xo, m_norm_mlp_g, m_w_mlp1, m_w_mlp2, m_final_norm_g, v_norm_mix_g, v_w_in, v_conv_w, v_conv_b, v_conv_ln_g, v_conv_ln_b, v_w_conv_out, v_q_norm_g, v_w_uq, v_kv_norm_g, v_w_ukv, v_w_mla_out, v_w_out, v_norm_xattn_g, v_norm_mem_g, v_w_xq, v_w_xkv, v_w_xo, v_norm_mlp_g, v_w_mlp1, v_w_mlp2, v_final_norm_g):
    args = dict(locals())
    t = x.shape[1]
    n = t // min(ROW_TILE, t)
    nl = t // min(LIGHT_ROW_TILE, t)
    xs, mems, tgt = x[0], mem[0], loss_target[0]

    def pack_shards(prefix, group, dtype):
        return _pack_group([args[prefix + k][0].astype(dtype) for k in group])

    def shapes(group):
        return [args[k].shape[1:] for k in group]

    def unpack_full(gathered, group):
        out, off = {}, 0
        for k, shp in zip(group, shapes(group)):
            kp, np_ = _padded(*shp)
            out[k] = _packed_to_full(k, gathered[:, off:off + kp * np_ // 128], shp)
            off += kp * np_ // 128
        return out

    full = unpack_full(_all_gather_packed(pack_shards("", GATHER_FIRST, BF16)), GATHER_FIRST)

    wi = full["w_in"]
    kr_slot = jnp.pad(wi[:, P_KR:P_KR + ROPE], ((0, 0), (NOPE, HP - NOPE - ROPE)))
    wp = jnp.concatenate([wi[:, :P_KR], kr_slot, wi[:, P_KR + ROPE:]], axis=1)
    cw = jnp.pad(full["conv_w"].astype(F32), ((0, HALO - CW), (0, 0)))
    wuq = _head_pad(full["w_uq"], NOPE + ROPE, 1)
    ukv = full["w_ukv"].reshape(KL, NH, NOPE + VD)
    wk = _head_pad(ukv[:, :, :NOPE].reshape(KL, NH * NOPE), NOPE, 1)
    wv = _head_pad(ukv[:, :, NOPE:].reshape(KL, NH * VD), VD, 1)

    half = ROPE // 2
    lane = jnp.arange(HP)
    first, second = (lane >= NOPE) & (lane < NOPE + half), (lane >= NOPE + half) & (lane < NOPE + ROPE)
    inv_freq = THETA ** (-((lane - NOPE) % half).astype(F32) / half)
    ang = positions[0].astype(F32)[:, None] * inv_freq
    cs, sn = jnp.cos(ang), jnp.sin(ang)
    tc = jnp.where(lane < NOPE, 1.0, jnp.where(first | second, cs, 0.0))
    tsa = jnp.where(first, -sn, 0.0)
    tsb = jnp.where(second, sn, 0.0)

    u0, a, gt, cq, ckv, krp, gl = _in_proj(xs, norm_mix_g, wp, nl)
    zc, conv_out = _conv_branch(a, gt, cw, conv_b, conv_ln_g, conv_ln_b, full["w_conv_out"], n)
    qh, kh, vh, cqn, ckvn = _mla_prep(cq, ckv, krp, tc, tsa, tsb, q_norm_g, wuq, kv_norm_g, wk, wv, nl)
    o, lse, gathered = _flash_fwd(qh, kh, vh, exchange=(pack_shards("", GATHER_LATE, BF16), True))
    full.update(unpack_full(gathered, GATHER_LATE))
    wmo = _head_pad(full["w_mla_out"], VD, 0)
    memn, kx, vx = _mem_kv(mems, norm_mem_g, full["w_xkv"])
    mla, merged, h1, u1, qx, ox, h2, ob = _merge_xattn(o, gl, conv_out, xs, wmo, full["w_out"], norm_xattn_g, full["w_xq"],
                                                        kx, vx, full["w_xo"], nl)
    gfin = final_norm_g.reshape(1, D)
    u2, rl2, da1, dh3b, dh2, loss_p, dg_fin, dg_mlp = _mlp_loss(h2, tgt, norm_mlp_g, full["w_mlp1"], full["w_mlp2"], gfin, n)

    (dqx, dh1, dgl, dco, dmla, dob, delta, dh2b, dh1b, dkx, dvx, dg_x) = _merge_xattn_bwd(
        dh2, h1, qx, gl, conv_out, mla, o, full["w_xo"], kx, vx, full["w_xq"], norm_xattn_g, full["w_out"], wmo, n)
    gfull = {"w_mlp1": _dw("dw_mlp1", u2, da1), "w_mlp2": _dw("dw_mlp2", rl2, dh3b)}

    def stacked(group):
        return jnp.concatenate([_full_to_packed(k, gfull[k], args[k].shape[1:]) for k in group], axis=1)

    zs, dzc, dg_lng, dg_lnb, dg_cb = _conv_out_bwd(dco, zc, full["w_conv_out"], conv_ln_g, conv_ln_b, nl)
    dci, dcw = _conv_glu_bwd(dzc, a, gt, cw, n)
    dkv, dg_mem = _mem_bwd(mems, dkx, dvx, norm_mem_g, full["w_xkv"])
    gfull.update({
        "conv_w": dcw[:CW],
        "w_conv_out": _dw("dw_conv_out", zs, dco),
        "w_mla_out": _head_unpad(_dw("dw_mla_out", ob, dmla), 0, VD, 0),
        "w_out": _dw("dw_out", merged, dh1b),
        "w_xq": _dw("dw_xq", u1, dqx),
        "w_xkv": _dw("dw_xkv", memn, dkv),
        "w_xo": _dw("dw_xo", ox, dh2b),
    })
    late_lo, late_hi, wcols = P_Q, P_KR + ROPE, args["w_in"].shape[2]
    dw_conv_in, dw_gates = _dw("dw_in_0", u0, dci), _dw("dw_in_4", u0, dgl)
    gfull["w_in"] = jnp.concatenate([dw_conv_in, jnp.zeros((D, late_hi - late_lo), F32), dw_gates], axis=1)
    dk, dv, dq, parts_early = _flash_bwd(qh, kh, vh, dob, lse, delta, exchange=(stacked(REDUCE_EARLY), False))
    dqp, dkb, dvb, dcq, dckv, dkrp, dg_q, dg_kv = _mla_prep_bwd(dq, dk, dv, cq, ckv, tc, tsa, tsb, q_norm_g, wuq, kv_norm_g, wk, wv, nl)
    pieces = (dci, dcq, dckv, dkrp, dgl)
    dw_cq, dw_ckv, dw_kr = _dw_shared("dw_in_late", u0, [dcq, dckv, dkrp])
    late = jnp.concatenate([dw_cq, dw_ckv, dw_kr[:, NOPE:NOPE + ROPE]], axis=1)
    gfull["w_uq"] = _head_unpad(_dw("dw_uq", cqn, dqp), 0, NOPE + ROPE, 1)
    dw_k, dw_v = _dw_shared("dw_ukv", ckvn, [dkb, dvb])
    gk = _head_unpad(dw_k, 0, NOPE, 1).reshape(KL, NH, NOPE)
    gv = _head_unpad(dw_v, 0, VD, 1).reshape(KL, NH, VD)
    gfull["w_ukv"] = jnp.concatenate([gk, gv], axis=2).reshape(KL, NH * (NOPE + VD))
    owners = [(d, max(late_lo, d * wcols) - d * wcols, max(late_lo, d * wcols) - late_lo,
               min(late_hi, (d + 1) * wcols) - max(late_lo, d * wcols)) for d in range(N_DEV)
              if min(late_hi, (d + 1) * wcols) > max(late_lo, d * wcols)]
    late_w = -(-max(w for _, _, _, w in owners) // 128) * 128
    late_src = _pack_rows(jnp.stack([jnp.pad(late[:, l0:l0 + w], ((0, 0), (0, late_w - w))) for _, _, l0, w in owners]))
    grad_x, dg_mix, parts_last, parts_late = _in_proj_bwd(
        pieces, dh1, xs, wp, norm_mix_g, n, [(stacked(REDUCE_LAST), False), (late_src, tuple(d for d, _, _, _ in owners))])

    gsum = {}
    for name, group, parts in (("grad_sum_early", REDUCE_EARLY, parts_early), ("grad_sum_last", REDUCE_LAST, parts_last)):
        gsum.update(zip(group, _unpack_group(_sum_parts(name, parts), shapes(group))))
    late_sum = _unpack_rows(_sum_parts("grad_sum_late", parts_late), D, late_w)
    me = 4 * lax.axis_index("x") + 2 * lax.axis_index("y") + lax.axis_index("c")
    placed = jnp.zeros((D, wcols), F32)
    for d, s0, _, w in owners:
        placed = jnp.where(me == d, jnp.pad(late_sum[:, :w], ((0, 0), (s0, wcols - s0 - w))), placed)
    gsum["w_in"] = gsum["w_in"] + placed
    big = [{}, {}, {}, {}]
    for k, g in gsum.items():
        res = _adam_call("adam_" + k, args[k][0], g, args["m_" + k][0], args["v_" + k][0])
        for kind, val in enumerate([g] + list(res)):
            big[kind][k] = val[None]

    small_g = {"norm_mix_g": dg_mix, "conv_b": dg_cb, "conv_ln_g": dg_lng, "conv_ln_b": dg_lnb, "q_norm_g": dg_q,
               "kv_norm_g": dg_kv, "norm_xattn_g": dg_x, "norm_mem_g": dg_mem, "norm_mlp_g": dg_mlp, "final_norm_g": dg_fin}
    small_sizes = [int(np.prod(args[k].shape)) for k in SMALL] + [128]
    zero_slot = jnp.zeros((128,), F32)
    small_out = _small_allreduce_adam(
        _pack_small([small_g[k] for k in SMALL] + [jnp.pad(loss_p.reshape(-1), (0, 127))]),
        _pack_small([args[k] for k in SMALL] + [zero_slot]), _pack_small([args["m_" + k] for k in SMALL] + [zero_slot]),
        _pack_small([args["v_" + k] for k in SMALL] + [zero_slot]))

    small = [dict(zip(SMALL, [s.reshape(args[k].shape) for k, s in zip(SMALL, _unpack_small(o_, small_sizes))])) for o_ in small_out]
    loss = _unpack_small(small_out[0], small_sizes)[-1][0]
    order = ("norm_mix_g", "w_in", "conv_w", "conv_b", "conv_ln_g", "conv_ln_b", "w_conv_out", "q_norm_g", "w_uq", "kv_norm_g",
             "w_ukv", "w_mla_out", "w_out", "norm_xattn_g", "norm_mem_g", "w_xq", "w_xkv", "w_xo", "norm_mlp_g", "w_mlp1",
             "w_mlp2", "final_norm_g")
    res = [loss, grad_x[None]]
    for kind in range(4):
        res += [big[kind][k] if k in big[kind] else small[kind][k] for k in order]
    return tuple(res)
```
